```python
import jax
import jax.numpy as jnp
from jax import lax
import numpy as np

D_MODEL = 1024
BATCH = 32
SEQ = 256
DEPTH = 4
DEC_BATCH = 2
DEC_SEQ = 1024
PAST_LEN = 256

GRID_W = 64
N_EVEN = (DEPTH + 1) // 2
N_ODD = DEPTH // 2
A_HEADS = 8
A_KV_HEADS = 2
A_GROUP = A_HEADS // A_KV_HEADS
A_HEAD_DIM = 64
WINDOW = 128
BLOCK = 128
B_HEADS = 8
B_Q_LORA = 256
B_KV_LORA = 128
B_NOPE = 64
B_ROPE = 32
B_QK_DIM = B_NOPE + B_ROPE
B_V_DIM = 64
IN_SPLITS = (A_HEADS * A_HEAD_DIM, A_KV_HEADS * A_HEAD_DIM, A_KV_HEADS * A_HEAD_DIM, B_Q_LORA, B_KV_LORA, B_ROPE)
IN_DIM = 1184
MIX_OUT = A_HEADS * A_HEAD_DIM + B_HEADS * B_V_DIM
CONV_WIDTH = 31
CONV_PAD = CONV_WIDTH // 2
D_FF = 2816
N_EXPERTS = 8
TOP_K = 2
D_FF_EXPERT = 3584
ROPE_BASE = 10000.0
EPS = 1e-6
Q_BLOCK = 128
N_ADA = 6

kernel_name = "hybrid_prefix_diffusion_trunk_step"


def rms_norm(x, g):
    xf = x.astype(jnp.float32)
    y = xf * lax.rsqrt(jnp.mean(xf * xf, axis=-1, keepdims=True) + EPS)
    return (y * g.astype(jnp.float32)).astype(x.dtype)


def layer_norm(x, g, b):
    xf = x.astype(jnp.float32)
    xc = xf - jnp.mean(xf, axis=-1, keepdims=True)
    y = xc * lax.rsqrt(jnp.mean(xc * xc, axis=-1, keepdims=True) + EPS)
    return (y * g.astype(jnp.float32) + b.astype(jnp.float32)).astype(x.dtype)


def split_columns(x, sizes):
    parts, start = [], 0
    for s in sizes:
        parts.append(x[..., start:start + s])
        start += s
    return parts


def rope_1d(x, pos):
    d = x.shape[-1]
    inv_freq = ROPE_BASE ** (-jnp.arange(0, d, 2, dtype=jnp.float32) / d)
    ang = pos[:, None] * inv_freq[None, :]
    cos = jnp.cos(ang)[:, None, :].astype(x.dtype)
    sin = jnp.sin(ang)[:, None, :].astype(x.dtype)
    x1, x2 = x[..., : d // 2], x[..., d // 2:]
    return jnp.concatenate([x1 * cos - x2 * sin, x1 * sin + x2 * cos], axis=-1)


def axial_rope(x, rows, cols):
    half = x.shape[-1] // 2
    return jnp.concatenate([rope_1d(x[..., :half], rows), rope_1d(x[..., half:], cols)], axis=-1)


def rope_tail(x, rows, cols):
    return jnp.concatenate([x[..., :B_NOPE], axial_rope(x[..., B_NOPE:], rows, cols)], axis=-1)


def ada_params(cond, w, b):
    return jnp.split(jax.nn.silu(cond) @ w + b, N_ADA, axis=-1)


def modulate(x, g, shift, scale):
    return rms_norm(x, g) * (1.0 + scale) + shift


def attn_projections(h, p, e):
    B, L, _ = h.shape
    qa, ka, va, cq, ckv, krope = split_columns(h @ p["w_in_attn"][e], IN_SPLITS)
    qa = rms_norm(qa.reshape(B, L, A_HEADS, A_HEAD_DIM), p["g_qk_a_q"][e])
    ka = rms_norm(ka.reshape(B, L, A_KV_HEADS, A_HEAD_DIM), p["g_qk_a_k"][e])
    va = va.reshape(B, L, A_KV_HEADS, A_HEAD_DIM)
    qb = (rms_norm(cq, p["g_mla_q_a"][e]) @ p["w_mla_q_b"][e]).reshape(B, L, B_HEADS, B_QK_DIM)
    qb = rms_norm(qb, p["g_qk_b_q"][e])
    ckv = rms_norm(ckv, p["g_mla_kv_a"][e])
    return qa, ka, va, qb, ckv, krope


def mla_keys(ckv, krope, w_kv_b, g_k):
    B, L, _ = ckv.shape
    kv = (ckv @ w_kv_b).reshape(B, L, B_HEADS, B_NOPE + B_V_DIM)
    k_rope = jnp.broadcast_to(krope[:, :, None, :], (B, L, B_HEADS, B_ROPE))
    k = rms_norm(jnp.concatenate([kv[..., :B_NOPE], k_rope], axis=-1), g_k)
    return k, kv[..., B_NOPE:]


def block_attention(q, k, v, sink=None):
    B, L, G, R, dq = q.shape
    nb = L // Q_BLOCK
    scale = dq ** -0.5
    qb = jnp.moveaxis(q.reshape(B, nb, Q_BLOCK, G, R, dq), 1, 0)

    def one(qi):
        s = jnp.einsum('bqgrd,bmgd->bgrqm', qi, k).astype(jnp.float32) * scale
        if sink is None:
            pr = jax.nn.softmax(s, axis=-1)
        else:
            sk = jnp.broadcast_to(sink.astype(jnp.float32)[None, :, :, None, None], s.shape[:-1] + (1,))
            pr = jax.nn.softmax(jnp.concatenate([s, sk], axis=-1), axis=-1)[..., :-1]
        return jnp.einsum('bgrqm,bmgd->bqgrd', pr.astype(v.dtype), v)

    o = lax.map(one, qb)
    return jnp.moveaxis(o, 0, 1).reshape(B, L, -1)


def window_attention(q, k, v, k_ctx, v_ctx, sink):
    B, S = q.shape[:2]
    nb = S // BLOCK
    M = k_ctx.shape[1]
    scale = A_HEAD_DIM ** -0.5
    qb = q.reshape(B, nb, BLOCK, A_KV_HEADS, A_GROUP, A_HEAD_DIM)
    pad = ((0, 0), (BLOCK, BLOCK), (0, 0), (0, 0))
    kp = jnp.pad(k, pad).reshape(B, nb + 2, BLOCK, A_KV_HEADS, A_HEAD_DIM)
    vp = jnp.pad(v, pad).reshape(B, nb + 2, BLOCK, A_KV_HEADS, A_HEAD_DIM)
    kb = jnp.concatenate([kp[:, :nb], kp[:, 1:nb + 1], kp[:, 2:]], axis=2)
    vb = jnp.concatenate([vp[:, :nb], vp[:, 1:nb + 1], vp[:, 2:]], axis=2)
    qpos = jnp.arange(nb)[:, None] * BLOCK + jnp.arange(BLOCK)[None, :]
    kpos = (jnp.arange(nb)[:, None] - 1) * BLOCK + jnp.arange(3 * BLOCK)[None, :]
    rel = kpos[:, None, :] - qpos[:, :, None]
    valid = (jnp.abs(rel) <= WINDOW) & (kpos[:, None, :] >= 0) & (kpos[:, None, :] < S)
    s_loc = jnp.einsum('bnqgrd,bnkgd->bgrnqk', qb, kb).astype(jnp.float32) * scale
    s_loc = jnp.where(valid, s_loc, -jnp.inf)
    s_ctx = jnp.einsum('bnqgrd,bmgd->bgrnqm', qb, k_ctx).astype(jnp.float32) * scale
    sk = jnp.broadcast_to(sink.astype(jnp.float32)[None, :, :, None, None, None], s_loc.shape[:-1] + (1,))
    pr = jax.nn.softmax(jnp.concatenate([s_loc, s_ctx, sk], axis=-1), axis=-1)
    kl = 3 * BLOCK
    p_loc = pr[..., :kl].astype(v.dtype)
    p_ctx = pr[..., kl:kl + M].astype(v.dtype)
    o = (jnp.einsum('bgrnqk,bnkgd->bnqgrd', p_loc, vb)
         + jnp.einsum('bgrnqm,bmgd->bnqgrd', p_ctx, v_ctx))
    return o.reshape(B, S, A_HEADS * A_HEAD_DIM)


def conv_module(h, p, o):
    a = h @ p["w_conv_pw1"][o] + p["b_conv_pw1"][o]
    u = a[..., :D_MODEL] * jax.nn.sigmoid(a[..., D_MODEL:])
    w = p["w_conv_dw"][o][:, None, :]
    u = lax.conv_general_dilated(u, w, window_strides=(1,), padding=[(CONV_PAD, CONV_PAD)],
                                 dimension_numbers=('NWC', 'WIO', 'NWC'),
                                 feature_group_count=D_MODEL) + p["b_conv_dw"][o]
    u = jax.nn.silu(layer_norm(u, p["g_conv_ln"][o], p["b_conv_ln"][o]))
    return u @ p["w_conv_pw2"][o] + p["b_conv_pw2"][o]


def swiglu(x, wg, wu, wd):
    return (jax.nn.silu(x @ wg) * (x @ wu)) @ wd


def moe_swiglu(x, w_router, wg, wu, wd):
    B, L, D = x.shape
    xt = x.reshape(B * L, D)
    logits = (xt @ w_router).astype(jnp.float32)
    vals, idx = lax.top_k(logits, TOP_K)
    wts = jax.nn.softmax(vals, axis=-1)
    gates = jnp.sum(jax.nn.one_hot(idx, N_EXPERTS, dtype=jnp.float32) * wts[..., None], axis=1).astype(x.dtype)
    y = jnp.zeros_like(xt)
    for e in range(N_EXPERTS):
        y = y + gates[:, e:e + 1] * swiglu(xt, wg[e], wu[e], wd[e])
    return y.reshape(B, L, D)


def channel_mixer(h, p, l):
    i = l // 2
    if l % 2 == 0:
        return swiglu(h, p["w_ffn_gate"][i], p["w_ffn_up"][i], p["w_ffn_down"][i])
    return moe_swiglu(h, p["w_router"][i], p["w_moe_gate"][i], p["w_moe_up"][i], p["w_moe_down"][i])


def context_trunk(x, c_ctx, p):
    B, L, _ = x.shape
    ks, vs, ckvs, kropes = [], [], [], []
    for l in range(DEPTH):
        sm, cm, gm, sf, cf, gf = ada_params(c_ctx, p["w_ada"][l], p["b_ada"][l])
        h = modulate(x, p["g_norm_mix"][l], sm, cm)
        if l % 2 == 0:
            e = l // 2
            qa, ka, va, qb, ckv, krope = attn_projections(h, p, e)
            oa = block_attention(qa.reshape(B, L, A_KV_HEADS, A_GROUP, A_HEAD_DIM), ka, va, p["sink_a"][e])
            kb, vb = mla_keys(ckv, krope, p["w_mla_kv_b"][e], p["g_qk_b_k"][e])
            ob = block_attention(qb[:, :, :, None, :], kb, vb)
            mix = jnp.concatenate([oa, ob], axis=-1) @ p["w_out_attn"][e]
            ks.append(ka)
            vs.append(va)
            ckvs.append(ckv)
            kropes.append(krope)
        else:
            mix = conv_module(h, p, l // 2)
        x = x + gm * mix
        h = modulate(x, p["g_norm_ffn"][l], sf, cf)
        x = x + gf * channel_mixer(h, p, l)
    return x, jnp.stack(ks, axis=1), jnp.stack(vs, axis=1), jnp.stack(ckvs, axis=1), jnp.stack(kropes, axis=1)


def latent_trunk(x, c, cache_k_win, cache_v_win, cache_ckv, cache_krope, p):
    B, S, _ = x.shape
    n_rows = S // GRID_W
    rows = jnp.repeat(jnp.arange(n_rows, dtype=jnp.float32), GRID_W)
    cols = jnp.tile(jnp.arange(GRID_W, dtype=jnp.float32), n_rows)
    cond = c[:, None, :]
    for l in range(DEPTH):
        sm, cm, gm, sf, cf, gf = ada_params(cond, p["w_ada"][l], p["b_ada"][l])
        h = modulate(x, p["g_norm_mix"][l], sm, cm)
        if l % 2 == 0:
            e = l // 2
            qa, ka, va, qb, ckv, krope = attn_projections(h, p, e)
            oa = window_attention(axial_rope(qa, rows, cols), axial_rope(ka, rows, cols), va,
                                  cache_k_win[:, e], cache_v_win[:, e], p["sink_a"][e])
            kb, vb = mla_keys(ckv, krope, p["w_mla_kv_b"][e], p["g_qk_b_k"][e])
            kc, vc = mla_keys(cache_ckv[:, e], cache_krope[:, e], p["w_mla_kv_b"][e], p["g_qk_b_k"][e])
            k_all = jnp.concatenate([kc, rope_tail(kb, rows, cols)], axis=1)
            v_all = jnp.concatenate([vc, vb], axis=1)
            ob = block_attention(rope_tail(qb, rows, cols)[:, :, :, None, :], k_all, v_all)
            mix = jnp.concatenate([oa, ob], axis=-1) @ p["w_out_attn"][e]
        else:
            mix = conv_module(h, p, l // 2)
        x = x + gm * mix
        h = modulate(x, p["g_norm_ffn"][l], sf, cf)
        x = x + gf * channel_mixer(h, p, l)
    return x


def setup_inputs(seed: int = 0) -> dict:
    key = jax.random.key(seed)
    keys = iter(jax.random.split(key, 48))
    D = D_MODEL

    def nrm(shape, scale=1.0):
        return jax.random.normal(next(keys), shape, jnp.float32) * scale

    def gain(shape):
        return 1.0 + nrm(shape, 0.02)

    return {
        "x_prompt": nrm((BATCH, SEQ, D)),
        "x_sample": nrm((DEC_BATCH, DEC_SEQ, D)),
        "c": nrm((DEC_BATCH, D)),
        "cache_k_win": nrm((DEC_BATCH, N_EVEN, PAST_LEN, A_KV_HEADS, A_HEAD_DIM)),
        "cache_v_win": nrm((DEC_BATCH, N_EVEN, PAST_LEN, A_KV_HEADS, A_HEAD_DIM)),
        "cache_ckv": nrm((DEC_BATCH, N_EVEN, PAST_LEN, B_KV_LORA)),
        "cache_krope": nrm((DEC_BATCH, N_EVEN, PAST_LEN, B_ROPE)),
        "c_ctx": nrm((D,)),
        "w_ada": nrm((DEPTH, D, N_ADA * D), 0.5 * D ** -0.5),
        "b_ada": nrm((DEPTH, N_ADA * D), 0.02),
        "g_norm_mix": gain((DEPTH, D)),
        "g_norm_ffn": gain((DEPTH, D)),
        "w_in_attn": nrm((N_EVEN, D, IN_DIM), D ** -0.5),
        "g_qk_a_q": gain((N_EVEN, A_HEAD_DIM)),
        "g_qk_a_k": gain((N_EVEN, A_HEAD_DIM)),
        "sink_a": nrm((N_EVEN, A_KV_HEADS, A_GROUP), 0.5),
        "g_mla_q_a": gain((N_EVEN, B_Q_LORA)),
        "w_mla_q_b": nrm((N_EVEN, B_Q_LORA, B_HEADS * B_QK_DIM), B_Q_LORA ** -0.5),
        "g_mla_kv_a": gain((N_EVEN, B_KV_LORA)),
        "w_mla_kv_b": nrm((N_EVEN, B_KV_LORA, B_HEADS * (B_NOPE + B_V_DIM)), B_KV_LORA ** -0.5),
        "g_qk_b_q": gain((N_EVEN, B_QK_DIM)),
        "g_qk_b_k": gain((N_EVEN, B_QK_DIM)),
        "w_out_attn": nrm((N_EVEN, MIX_OUT, D), MIX_OUT ** -0.5),
        "w_ffn_gate": nrm((N_EVEN, D, D_FF), D ** -0.5),
        "w_ffn_up": nrm((N_EVEN, D, D_FF), D ** -0.5),
        "w_ffn_down": nrm((N_EVEN, D_FF, D), D_FF ** -0.5),
        "w_conv_pw1": nrm((N_ODD, D, 2 * D), D ** -0.5),
        "b_conv_pw1": nrm((N_ODD, 2 * D), 0.02),
        "w_conv_dw": nrm((N_ODD, CONV_WIDTH, D), CONV_WIDTH ** -0.5),
        "b_conv_dw": nrm((N_ODD, D), 0.02),
        "g_conv_ln": gain((N_ODD, D)),
        "b_conv_ln": nrm((N_ODD, D), 0.02),
        "w_conv_pw2": nrm((N_ODD, D, D), D ** -0.5),
        "b_conv_pw2": nrm((N_ODD, D), 0.02),
        "w_router": nrm((N_ODD, D, N_EXPERTS), D ** -0.5),
        "w_moe_gate": nrm((N_ODD, N_EXPERTS, D, D_FF_EXPERT), D ** -0.5),
        "w_moe_up": nrm((N_ODD, N_EXPERTS, D, D_FF_EXPERT), D ** -0.5),
        "w_moe_down": nrm((N_ODD, N_EXPERTS, D_FF_EXPERT, D), D_FF_EXPERT ** -0.5),
    }


def reference(x_prompt, x_sample, c, cache_k_win, cache_v_win, cache_ckv, cache_krope, c_ctx,
              w_ada, b_ada, g_norm_mix, g_norm_ffn, w_in_attn, g_qk_a_q, g_qk_a_k, sink_a,
              g_mla_q_a, w_mla_q_b, g_mla_kv_a, w_mla_kv_b, g_qk_b_q, g_qk_b_k, w_out_attn,
              w_ffn_gate, w_ffn_up, w_ffn_down, w_conv_pw1, b_conv_pw1, w_conv_dw, b_conv_dw,
              g_conv_ln, b_conv_ln, w_conv_pw2, b_conv_pw2, w_router, w_moe_gate, w_moe_up, w_moe_down):
    p = dict(w_ada=w_ada, b_ada=b_ada, g_norm_mix=g_norm_mix, g_norm_ffn=g_norm_ffn,
             w_in_attn=w_in_attn, g_qk_a_q=g_qk_a_q, g_qk_a_k=g_qk_a_k, sink_a=sink_a,
             g_mla_q_a=g_mla_q_a, w_mla_q_b=w_mla_q_b, g_mla_kv_a=g_mla_kv_a, w_mla_kv_b=w_mla_kv_b,
             g_qk_b_q=g_qk_b_q, g_qk_b_k=g_qk_b_k, w_out_attn=w_out_attn,
             w_ffn_gate=w_ffn_gate, w_ffn_up=w_ffn_up, w_ffn_down=w_ffn_down,
             w_conv_pw1=w_conv_pw1, b_conv_pw1=b_conv_pw1, w_conv_dw=w_conv_dw, b_conv_dw=b_conv_dw,
             g_conv_ln=g_conv_ln, b_conv_ln=b_conv_ln, w_conv_pw2=w_conv_pw2, b_conv_pw2=b_conv_pw2,
             w_router=w_router, w_moe_gate=w_moe_gate, w_moe_up=w_moe_up, w_moe_down=w_moe_down)
    y_prompt, new_k_win, new_v_win, new_ckv, new_krope = context_trunk(x_prompt, c_ctx, p)
    y_sample = latent_trunk(x_sample, c, cache_k_win, cache_v_win, cache_ckv, cache_krope, p)
    return (y_prompt, y_sample, new_k_win, new_v_win, new_ckv, new_krope)
```

```python
import functools

import jax
import jax.numpy as jnp
from jax import lax
from jax.experimental import pallas as pl
from jax.experimental.pallas import tpu as pltpu

F32 = jnp.float32
BF16 = jnp.bfloat16

D = 1024
BATCH = 32
SEQ = 256
DEPTH = 4
DEC_BATCH = 2
DEC_SEQ = 1024
PAST = 256
GRID_W = 64
N_EVEN = 2
A_HEADS = 8
A_KV = 2
A_DH = 64
WINDOW = 128
B_HEADS = 8
B_Q_LORA = 256
B_KV_LORA = 128
B_NOPE = 64
B_ROPE = 32
B_QK = B_NOPE + B_ROPE
B_V = 64
IN_DIM = 1184
IN_PAD = 1280
CONV_W = 31
D_FF = 2816
N_EXP = 8
D_FFE = 3584
ROPE_BASE = 10000.0
EPS = 1e-6
N_ADA = 6

T_CTX = BATCH * SEQ
T_LAT = DEC_BATCH * DEC_SEQ
T_ALL = T_CTX + T_LAT
TM = 256
N_TILES = T_ALL // TM
N_CTX_TILES = T_CTX // TM
LAT_TILES_PER_SEQ = DEC_SEQ // TM
LANE = 128
SLOT = 128

MOE_TR = 1024
MOE_SUB = 256
MOE_TF = 512
MOE_NF = D_FFE // MOE_TF
MOE_ROWS = 2 * T_ALL + N_EXP * MOE_TR
MOE_NT = MOE_ROWS // MOE_TR

FFN_TM = 512
FFN_CH = 1408

VMEM_BIG = 56 * 1024 * 1024


def _dot(a, b):
    return jnp.dot(a, b, preferred_element_type=F32)


def _dot_nt(a, b):
    return lax.dot_general(a, b, (((1,), (1,)), ((), ())), preferred_element_type=F32)


def _split(a):
    hi = a.astype(BF16)
    lo = (a - hi.astype(F32)).astype(BF16)
    return hi, lo


def _dot_x3(a, b):
    ah, al = _split(a)
    bh, bl = _split(b)
    return _dot(ah, bh) + (_dot(ah, bl) + _dot(al, bh))


def _sigmoid(x):
    return 1.0 / (1.0 + jnp.exp(-x))


def _rms(x, n):
    return x * lax.rsqrt(jnp.sum(x * x, axis=-1, keepdims=True) * (1.0 / n) + EPS)


def _modulate(x, g, shift, scale):
    return _rms(x, D) * g * (1.0 + scale) + shift


def _lo_mask(rows):
    return lax.broadcasted_iota(jnp.int32, (rows, LANE), 1) < 64


def _ada_kernel(cond_ref, w_ref, b_ref, o_ref):
    c = cond_ref[...]
    s = c * _sigmoid(c)
    o_ref[...] = _dot_x3(s, w_ref[...]) + b_ref[...]


def _ada_call(cond8, w_ada, b_ada):
    tn = 1024
    return pl.pallas_call(
        _ada_kernel,
        out_shape=jax.ShapeDtypeStruct((DEPTH, 8, N_ADA * D), F32),
        grid=(DEPTH, N_ADA * D // tn),
        in_specs=[
            pl.BlockSpec((8, D), lambda l, j: (0, 0)),
            pl.BlockSpec((None, D, tn), lambda l, j: (l, 0, j)),
            pl.BlockSpec((None, 1, tn), lambda l, j: (l, 0, j)),
        ],
        out_specs=pl.BlockSpec((None, 8, tn), lambda l, j: (l, 0, j)),
        compiler_params=pltpu.CompilerParams(
            dimension_semantics=("arbitrary", "arbitrary"), vmem_limit_bytes=VMEM_BIG),
        name="ada",
    )(cond8, w_ada, b_ada.reshape(DEPTH, 1, N_ADA * D))


def _rope(x, cos, sup, sdn, shift):
    up = pltpu.roll(x, LANE - shift, 1)
    dn = pltpu.roll(x, shift, 1)
    return x * cos + up * sup + dn * sdn


def _norm64_block(blk, lo):
    sq = blk * blk
    s_lo = jnp.sum(jnp.where(lo, sq, 0.0), axis=-1, keepdims=True)
    s_hi = jnp.sum(jnp.where(lo, 0.0, sq), axis=-1, keepdims=True)
    r = jnp.where(lo, lax.rsqrt(s_lo * (1.0 / A_DH) + EPS), lax.rsqrt(s_hi * (1.0 / A_DH) + EPS))
    return blk * r


def _pair_variants(x, lo):
    r = pltpu.roll(x, 64, 1)
    return (jnp.where(lo, x, 0.0), jnp.where(lo, 0.0, r), jnp.where(lo, r, 0.0), jnp.where(lo, 0.0, x))


def _mla_kv(ckv_n, kr, wkv_ref, place_ref, gk_ref, lo, kb_ref, vlo_ref, vhi_ref, rope_tabs):
    kv = _dot(ckv_n.astype(BF16), wkv_ref[...])
    kr_hi, kr_lo = _split(kr)
    place = place_ref[...]
    krs = _dot(kr_hi, place) + _dot(kr_lo, place)
    for h in range(B_HEADS):
        sl = slice(h * SLOT, (h + 1) * SLOT)
        blk = kv[:, sl] + krs[:, sl]
        blk = _rms(blk, B_QK) * gk_ref[:, sl]
        if rope_tabs is not None:
            cos, sup, sdn = rope_tabs
            blk = _rope(blk, cos[:, sl], sup[:, sl], sdn[:, sl], 8)
        kb_ref[:, sl] = blk.astype(BF16)
    for j in range(B_HEADS // 2):
        sl = slice(j * LANE, (j + 1) * LANE)
        v = kv[:, B_HEADS * SLOT + j * LANE: B_HEADS * SLOT + (j + 1) * LANE]
        vlo_ref[:, sl] = jnp.where(lo, v, 0.0).astype(BF16)
        vhi_ref[:, sl] = jnp.where(lo, 0.0, v).astype(BF16)


def _pre_attn_kernel(rope, *refs):
    if rope:
        (x_ref, ada_ref, g_ref, win_ref, wqb_ref, wkv_ref, place_ref,
         gqa_ref, gka_ref, gql_ref, gkl_ref, gqb_ref, gkb_ref,
         ca_ref, sua_ref, sda_ref, cb_ref, sub_ref, sdb_ref,
         qa_o, ka4_o, va4_o, qb_o, kb_o, vlo_o, vhi_o) = refs
    else:
        (x_ref, ada_ref, g_ref, win_ref, wqb_ref, wkv_ref, place_ref,
         gqa_ref, gka_ref, gql_ref, gkl_ref, gqb_ref, gkb_ref,
         qa_o, ka4_o, va4_o, qb_o, kb_o, vlo_o, vhi_o,
         kaf_o, vaf_o, ckvf_o, krf_o) = refs
    lo = _lo_mask(TM)
    h = _modulate(x_ref[...], g_ref[...], ada_ref[0:1, :], ada_ref[1:2, :])
    p = _dot(h.astype(BF16), win_ref[...])

    for j in range(4):
        sl = slice(j * LANE, (j + 1) * LANE)
        blk = _norm64_block(p[:, sl], lo) * gqa_ref[:, sl]
        if rope:
            blk = _rope(blk, ca_ref[:, sl], sua_ref[:, sl], sda_ref[:, sl], 16)
        qa_o[:, sl] = (blk * (A_DH ** -0.5)).astype(BF16)
    ka = _norm64_block(p[:, 512:640], lo) * gka_ref[...]
    va = p[:, 640:768]
    if not rope:
        kaf_o[...] = ka
        vaf_o[...] = va
    else:
        ka = _rope(ka, ca_ref[:, 0:LANE], sua_ref[:, 0:LANE], sda_ref[:, 0:LANE], 16)
    for n, (kk, vv) in enumerate(zip(_pair_variants(ka, lo), _pair_variants(va, lo))):
        sl = slice(n * LANE, (n + 1) * LANE)
        ka4_o[:, sl] = kk.astype(BF16)
        va4_o[:, sl] = vv.astype(BF16)

    cq = _rms(p[:, 768:1024], B_Q_LORA) * gql_ref[...]
    qb = _dot(cq.astype(BF16), wqb_ref[...])
    for hh in range(B_HEADS):
        sl = slice(hh * SLOT, (hh + 1) * SLOT)
        blk = _rms(qb[:, sl], B_QK) * gqb_ref[:, sl]
        if rope:
            blk = _rope(blk, cb_ref[:, sl], sub_ref[:, sl], sdb_ref[:, sl], 8)
        qb_o[:, sl] = (blk * (B_QK ** -0.5)).astype(BF16)

    ckv = _rms(p[:, 1024:1152], B_KV_LORA) * gkl_ref[...]
    kr = p[:, 1152:1280]
    if not rope:
        ckvf_o[...] = ckv
        krf_o[...] = kr
    tabs = (cb_ref, sub_ref, sdb_ref) if rope else None
    _mla_kv(ckv, kr, wkv_ref, place_ref, gkb_ref, lo, kb_o, vlo_o, vhi_o, tabs)


def _const_spec(shape):
    nd = len(shape)
    return pl.BlockSpec(shape, lambda *a: (0,) * nd)


def _pre_attn_call(rope, x, ada, layer, g, lw, tabs):
    n_tiles = (T_LAT if rope else T_CTX) // TM
    off = N_CTX_TILES if rope else 0
    n_rows = n_tiles * TM
    if rope:
        ada_map = lambda i: (layer, 1 + i // LAT_TILES_PER_SEQ, 0, 0)
    else:
        ada_map = lambda i: (layer, 0, 0, 0)
    in_specs = [
        pl.BlockSpec((TM, D), lambda i: (i + off, 0)),
        pl.BlockSpec((None, None, N_ADA, D), ada_map),
        _const_spec((1, D)),
        _const_spec((D, IN_PAD)),
        _const_spec((B_Q_LORA, B_HEADS * SLOT)),
        _const_spec((B_KV_LORA, B_HEADS * SLOT + B_HEADS * B_V)),
        _const_spec((LANE, B_HEADS * SLOT)),
        _const_spec((1, 512)), _const_spec((1, LANE)), _const_spec((1, B_Q_LORA)),
        _const_spec((1, B_KV_LORA)), _const_spec((1, B_HEADS * SLOT)), _const_spec((1, B_HEADS * SLOT)),
    ]
    args = [x, ada, g, lw["w_in"], lw["w_qb"], lw["w_kv"], lw["place"],
            lw["gqa"], lw["gka"], lw["gql"], lw["gkl"], lw["gqb"], lw["gkb"]]
    if rope:
        tmap = lambda i: (i % LAT_TILES_PER_SEQ, 0)
        in_specs += [pl.BlockSpec((TM, 512), tmap)] * 3 + [pl.BlockSpec((TM, B_HEADS * SLOT), tmap)] * 3
        args += list(tabs)
    row = lambda w: pl.BlockSpec((TM, w), lambda i: (i, 0))
    out_shape = [jax.ShapeDtypeStruct((n_rows, w), BF16) for w in (512, 512, 512, 1024, 1024, 512, 512)]
    out_specs = [row(w) for w in (512, 512, 512, 1024, 1024, 512, 512)]
    if not rope:
        out_shape += [jax.ShapeDtypeStruct((n_rows, LANE), F32)] * 4
        out_specs += [row(LANE)] * 4
    return pl.pallas_call(
        functools.partial(_pre_attn_kernel, rope),
        out_shape=out_shape, grid=(n_tiles,), in_specs=in_specs, out_specs=out_specs,
        compiler_params=pltpu.CompilerParams(
            dimension_semantics=("arbitrary",), vmem_limit_bytes=VMEM_BIG),
        name="pre_attn_lat" if rope else "pre_attn_ctx",
    )(*args)


def _cache_kv_kernel(ckv_ref, kr_ref, wkv_ref, place_ref, gkb_ref, kb_o, vlo_o, vhi_o):
    lo = _lo_mask(PAST)
    _mla_kv(ckv_ref[...], kr_ref[...], wkv_ref, place_ref, gkb_ref, lo, kb_o, vlo_o, vhi_o, None)


def _cache_kv_call(ckv, kr, lw):
    rows = DEC_BATCH * PAST
    row = lambda w: pl.BlockSpec((PAST, w), lambda i: (i, 0))
    return pl.pallas_call(
        _cache_kv_kernel,
        out_shape=[jax.ShapeDtypeStruct((rows, 1024), BF16), jax.ShapeDtypeStruct((rows, 512), BF16),
                   jax.ShapeDtypeStruct((rows, 512), BF16)],
        grid=(DEC_BATCH,),
        in_specs=[row(LANE), row(LANE),
                  _const_spec((B_KV_LORA, B_HEADS * SLOT + B_HEADS * B_V)),
                  _const_spec((LANE, B_HEADS * SLOT)), _const_spec((1, B_HEADS * SLOT))],
        out_specs=[row(1024), row(512), row(512)],
        compiler_params=pltpu.CompilerParams(dimension_semantics=("arbitrary",)),
        name="cache_kv",
    )(ckv, kr, lw["w_kv"], lw["place"], lw["gkb"])


def _softmax_pv(scores, values, sink):
    m = jnp.max(scores[0], axis=-1, keepdims=True)
    for s in scores[1:]:
        m = jnp.maximum(m, jnp.max(s, axis=-1, keepdims=True))
    if sink is not None:
        m = jnp.maximum(m, sink)
    ps = [jnp.exp(s - m) for s in scores]
    den = jnp.sum(ps[0], axis=-1, keepdims=True)
    for p in ps[1:]:
        den = den + jnp.sum(p, axis=-1, keepdims=True)
    if sink is not None:
        den = den + jnp.exp(sink - m)
    inv = 1.0 / den
    out = _dot((ps[0] * inv).astype(BF16), values[0])
    for p, v in zip(ps[1:], values[1:]):
        out = out + _dot((p * inv).astype(BF16), v)
    return out


def _attn_ctx_kernel(sink_ref, qa_ref, ka4_ref, va4_ref, qb_ref, kb_ref, vlo_ref, vhi_ref, o_ref):
    for j in range(4):
        q = qa_ref[:, j * LANE:(j + 1) * LANE]
        acc = None
        for par in range(2):
            c = (2 * (j // 2) + par) * LANE
            s = _dot_nt(q, ka4_ref[:, c:c + LANE])
            o = _softmax_pv([s], [va4_ref[:, c:c + LANE]], sink_ref[2 * j + par])
            acc = o if acc is None else acc + o
        o_ref[:, j * LANE:(j + 1) * LANE] = acc.astype(BF16)
    for j in range(4):
        acc = None
        for par, v_ref in enumerate((vlo_ref, vhi_ref)):
            hh = 2 * j + par
            s = _dot_nt(qb_ref[:, hh * SLOT:(hh + 1) * SLOT], kb_ref[:, hh * SLOT:(hh + 1) * SLOT])
            o = _softmax_pv([s], [v_ref[:, j * LANE:(j + 1) * LANE]], None)
            acc = o if acc is None else acc + o
        o_ref[:, 512 + j * LANE:512 + (j + 1) * LANE] = acc.astype(BF16)


def _attn_ctx_call(sink, qa, ka4, va4, qb, kb, vlo, vhi):
    row = lambda w: pl.BlockSpec((SEQ, w), lambda b: (b, 0))
    return pl.pallas_call(
        _attn_ctx_kernel,
        out_shape=jax.ShapeDtypeStruct((T_CTX, D), BF16),
        grid=(BATCH,),
        in_specs=[pl.BlockSpec(memory_space=pltpu.SMEM),
                  row(512), row(512), row(512), row(1024), row(1024), row(512), row(512)],
        out_specs=row(D),
        compiler_params=pltpu.CompilerParams(dimension_semantics=("arbitrary",)),
        name="attn_ctx",
    )(sink, qa, ka4, va4, qb, kb, vlo, vhi)


QB = 128
N_QB = DEC_SEQ // QB
KPAD = DEC_SEQ + 2 * QB
KALL = PAST + DEC_SEQ


def _attn_lat_kernel(sink_ref, qa_ref, ka4p_ref, va4p_ref, ka4c_ref, va4c_ref,
                     qb_ref, kb_ref, vlo_ref, vhi_ref, o_ref):
    n = pl.program_id(1)
    r = lax.broadcasted_iota(jnp.int32, (QB, 3 * QB), 0)
    c = lax.broadcasted_iota(jnp.int32, (QB, 3 * QB), 1)
    rel = c - QB - r
    kpos = (n - 1) * QB + c
    valid = (jnp.abs(rel) <= WINDOW) & (kpos >= 0) & (kpos < DEC_SEQ)
    k0 = pl.multiple_of(n * QB, QB)
    for j in range(4):
        q = qa_ref[:, j * LANE:(j + 1) * LANE]
        acc = None
        for par in range(2):
            cc = (2 * (j // 2) + par) * LANE
            s_loc = _dot_nt(q, ka4p_ref[pl.ds(k0, 3 * QB), cc:cc + LANE])
            s_loc = jnp.where(valid, s_loc, -jnp.inf)
            s_ctx = _dot_nt(q, ka4c_ref[:, cc:cc + LANE])
            o = _softmax_pv([s_ctx, s_loc],
                            [va4c_ref[:, cc:cc + LANE], va4p_ref[pl.ds(k0, 3 * QB), cc:cc + LANE]],
                            sink_ref[2 * j + par])
            acc = o if acc is None else acc + o
        o_ref[:, j * LANE:(j + 1) * LANE] = acc.astype(BF16)
    for j in range(4):
        acc = None
        for par, v_ref in enumerate((vlo_ref, vhi_ref)):
            hh = 2 * j + par
            s = _dot_nt(qb_ref[:, hh * SLOT:(hh + 1) * SLOT], kb_ref[:, hh * SLOT:(hh + 1) * SLOT])
            o = _softmax_pv([s], [v_ref[:, j * LANE:(j + 1) * LANE]], None)
            acc = o if acc is None else acc + o
        o_ref[:, 512 + j * LANE:512 + (j + 1) * LANE] = acc.astype(BF16)


def _attn_lat_call(sink, qa, ka4p, va4p, ka4c, va4c, qb, kball, vlo_all, vhi_all):
    qrow = lambda w: pl.BlockSpec((QB, w), lambda b, n: (b * N_QB + n, 0))
    per_b = lambda rows, w: pl.BlockSpec((None, rows, w), lambda b, n: (b, 0, 0))
    return pl.pallas_call(
        _attn_lat_kernel,
        out_shape=jax.ShapeDtypeStruct((T_LAT, D), BF16),
        grid=(DEC_BATCH, N_QB),
        in_specs=[pl.BlockSpec(memory_space=pltpu.SMEM),
                  qrow(512), per_b(KPAD, 512), per_b(KPAD, 512), per_b(PAST, 512), per_b(PAST, 512),
                  qrow(1024), per_b(KALL, 1024), per_b(KALL, 512), per_b(KALL, 512)],
        out_specs=qrow(D),
        compiler_params=pltpu.CompilerParams(dimension_semantics=("arbitrary", "arbitrary")),
        name="attn_lat",
    )(sink, qa, ka4p, va4p, ka4c, va4c, qb, kball, vlo_all, vhi_all)


def _post_attn_kernel(x_ref, o_ref, ada_ref, g_ref, wo_ref, wg_ref, wu_ref, wd_ref, y_ref):
    mix = _dot(o_ref[...], wo_ref[...])
    x1 = x_ref[...] + ada_ref[2:3, :] * mix
    h = _modulate(x1, g_ref[...], ada_ref[3:4, :], ada_ref[4:5, :]).astype(BF16)
    acc = None
    for c0 in range(0, D_FF, FFN_CH):
        gg = _dot(h, wg_ref[:, c0:c0 + FFN_CH])
        uu = _dot(h, wu_ref[:, c0:c0 + FFN_CH])
        a = (gg * _sigmoid(gg) * uu).astype(BF16)
        part = _dot(a, wd_ref[c0:c0 + FFN_CH, :])
        acc = part if acc is None else acc + part
    y_ref[...] = x1 + ada_ref[5:6, :] * acc


def _tile_cond(i, tiles_per_ctx, tiles_per_lat_seq):
    return jnp.where(i < tiles_per_ctx, 0, 1 + (i - tiles_per_ctx) // tiles_per_lat_seq)


def _post_attn_call(x, o, ada, layer, g, wo, wg, wu, wd):
    n_tiles = T_ALL // FFN_TM
    ada_map = lambda i: (layer, _tile_cond(i, T_CTX // FFN_TM, DEC_SEQ // FFN_TM), 0, 0)
    one = pl.Buffered(1)
    return pl.pallas_call(
        _post_attn_kernel,
        out_shape=jax.ShapeDtypeStruct((T_ALL, D), F32),
        grid=(n_tiles,),
        in_specs=[pl.BlockSpec((FFN_TM, D), lambda i: (i, 0)),
                  pl.BlockSpec((FFN_TM, D), lambda i: (i, 0)),
                  pl.BlockSpec((None, None, N_ADA, D), ada_map),
                  _const_spec((1, D)),
                  pl.BlockSpec((D, D), lambda i: (0, 0), pipeline_mode=one),
                  pl.BlockSpec((D, D_FF), lambda i: (0, 0), pipeline_mode=one),
                  pl.BlockSpec((D, D_FF), lambda i: (0, 0), pipeline_mode=one),
                  pl.BlockSpec((D_FF, D), lambda i: (0, 0), pipeline_mode=one)],
        out_specs=pl.BlockSpec((FFN_TM, D), lambda i: (i, 0)),
        compiler_params=pltpu.CompilerParams(
            dimension_semantics=("arbitrary",), vmem_limit_bytes=VMEM_BIG),
        name="post_attn_ffn",
    )(x, o, ada, g, wo, wg, wu, wd)


HALO = 16
EXT = TM + 2 * HALO
SHROWS = TM + 24
CONV_RC = 32


def _conv_router_kernel(x_ref, xp_ref, xn_ref, ada_ref, gm_ref, gf_ref, w1_ref, b1_ref, wdw_ref, bdw_ref,
                        gln_ref, bln_ref, w2_ref, b2_ref, wr_ref,
                        x1_o, h2_o, ri_o, rw_o, cnt_o, p_ref, sh_ref, v_ref, run_ref):
    i = pl.program_id(0)

    @pl.when(i == 0)
    def _():
        run_ref[...] = jnp.zeros_like(run_ref)

    is_lat = i >= N_CTX_TILES
    j = (i - N_CTX_TILES) % LAT_TILES_PER_SEQ
    lflag = jnp.where(is_lat & (j != 0), 1.0, 0.0)
    rflag = jnp.where(is_lat & (j != LAT_TILES_PER_SEQ - 1), 1.0, 0.0)

    x = x_ref[...]
    xx = jnp.concatenate([xp_ref[...], x, xn_ref[...]], axis=0)
    h = _modulate(xx, gm_ref[...], ada_ref[0:1, :], ada_ref[1:2, :])
    a = _dot(h.astype(BF16), w1_ref[...]) + b1_ref[...]
    u = a[:, :D] * _sigmoid(a[:, D:])
    row = lax.broadcasted_iota(jnp.int32, (EXT, 1), 0)
    u = u * jnp.where(row < HALO, lflag, jnp.where(row >= HALO + TM, rflag, 1.0))

    p_ref[...] = u
    for r in range(8):
        sh_ref[r] = p_ref[pl.ds(r, SHROWS), :]

    def chunk(cidx, carry):
        r0 = pl.multiple_of(cidx * CONV_RC, CONV_RC)
        acc = jnp.zeros((CONV_RC, D), F32) + bdw_ref[...]
        for k in range(CONV_W):
            o = k + 1
            acc = acc + wdw_ref[k:k + 1, :] * sh_ref[o % 8, pl.ds(r0 + 8 * (o // 8), CONV_RC), :]
        mu = jnp.mean(acc, axis=-1, keepdims=True)
        xc = acc - mu
        y = xc * lax.rsqrt(jnp.mean(xc * xc, axis=-1, keepdims=True) + EPS)
        y = y * gln_ref[...] + bln_ref[...]
        v_ref[pl.ds(r0, CONV_RC), :] = (y * _sigmoid(y)).astype(BF16)
        return carry

    lax.fori_loop(0, TM // CONV_RC, chunk, 0)

    mix = _dot(v_ref[...], w2_ref[...]) + b2_ref[...]
    x1 = x + ada_ref[2:3, :] * mix
    x1_o[...] = x1
    h2 = _modulate(x1, gf_ref[...], ada_ref[3:4, :], ada_ref[4:5, :])
    h2_o[...] = h2

    lane = lax.broadcasted_iota(jnp.int32, (TM, LANE), 1)
    lane_f = lane.astype(F32)
    logits = jnp.where(lane < N_EXP, _dot_x3(h2, wr_ref[...]), -jnp.inf)
    m1 = jnp.max(logits, axis=-1, keepdims=True)
    i1 = jnp.min(jnp.where(logits == m1, lane_f, float(LANE)), axis=-1, keepdims=True)
    rest = jnp.where(lane_f == i1, -jnp.inf, logits)
    m2 = jnp.max(rest, axis=-1, keepdims=True)
    i2 = jnp.min(jnp.where(rest == m2, lane_f, float(LANE)), axis=-1, keepdims=True)
    e = jnp.exp(m2 - m1)
    w1 = 1.0 / (1.0 + e)
    w2 = e / (1.0 + e)

    oh1 = jnp.where(lane_f == i1, 1.0, 0.0)
    oh2 = jnp.where(lane_f == i2, 1.0, 0.0)
    tr = lax.broadcasted_iota(jnp.int32, (TM, TM), 0)
    tc = lax.broadcasted_iota(jnp.int32, (TM, TM), 1)
    tri = jnp.where(tr > tc, 1.0, 0.0).astype(BF16)
    run = run_ref[0:1, :]
    cnt1 = jnp.sum(oh1, axis=0, keepdims=True)
    cnt2 = jnp.sum(oh2, axis=0, keepdims=True)
    pre1 = _dot(tri, oh1.astype(BF16)) + run
    pre2 = _dot(tri, oh2.astype(BF16)) + (run + cnt1)
    rank1 = jnp.sum(oh1 * pre1, axis=-1, keepdims=True)
    rank2 = jnp.sum(oh2 * pre2, axis=-1, keepdims=True)
    new_run = run + cnt1 + cnt2
    run_ref[...] = jnp.broadcast_to(new_run, run_ref.shape)
    cnt_o[...] = jnp.broadcast_to(new_run, cnt_o.shape)

    ri = jnp.where(lane == 0, i1, jnp.where(lane == 1, i2, jnp.where(lane == 2, rank1, jnp.where(lane == 3, rank2, 0.0))))
    ri_o[...] = ri.astype(jnp.int32)
    rw_o[...] = jnp.where(lane == 0, w1, jnp.where(lane == 1, w2, 0.0))


def _conv_router_call(x, ada, layer, gm, gf, cw):
    ada_map = lambda i: (layer, _tile_cond(i, N_CTX_TILES, LAT_TILES_PER_SEQ), 0, 0)
    hb = TM // HALO
    row = lambda w: pl.BlockSpec((TM, w), lambda i: (i, 0))
    return pl.pallas_call(
        _conv_router_kernel,
        out_shape=[jax.ShapeDtypeStruct((T_ALL, D), F32), jax.ShapeDtypeStruct((T_ALL, D), F32),
                   jax.ShapeDtypeStruct((T_ALL, LANE), jnp.int32), jax.ShapeDtypeStruct((T_ALL, LANE), F32),
                   jax.ShapeDtypeStruct((8, LANE), F32)],
        grid=(N_TILES,),
        in_specs=[row(D),
                  pl.BlockSpec((HALO, D), lambda i: (jnp.maximum(i * hb - 1, 0), 0)),
                  pl.BlockSpec((HALO, D), lambda i: (jnp.minimum((i + 1) * hb, T_ALL // HALO - 1), 0)),
                  pl.BlockSpec((None, None, N_ADA, D), ada_map),
                  _const_spec((1, D)), _const_spec((1, D)),
                  _const_spec((D, 2 * D)), _const_spec((1, 2 * D)),
                  _const_spec((CONV_W, D)), _const_spec((1, D)),
                  _const_spec((1, D)), _const_spec((1, D)),
                  _const_spec((D, D)), _const_spec((1, D)),
                  _const_spec((D, LANE))],
        out_specs=[row(D), row(D), row(LANE), row(LANE), _const_spec((8, LANE))],
        scratch_shapes=[pltpu.VMEM((EXT, D), F32), pltpu.VMEM((8, SHROWS, D), F32), pltpu.VMEM((TM, D), BF16),
                        pltpu.VMEM((8, LANE), F32)],
        compiler_params=pltpu.CompilerParams(
            dimension_semantics=("arbitrary",), vmem_limit_bytes=VMEM_BIG),
        name="conv_router",
    )(x, x, x, ada, gm, gf, cw["w1"], cw["b1"], cw["wdw"], cw["bdw"], cw["gln"], cw["bln"],
      cw["w2"], cw["b2"], cw["wr"])


def _dispatch_kernel(pos_ref, h_ref, hs_in_ref, hs_ref, sem):
    del hs_in_ref
    base = pl.program_id(0) * TM

    def body(t, carry):
        tok = base + t
        for s in range(2):
            p = pos_ref[2 * tok + s]
            pltpu.make_async_copy(h_ref.at[pl.ds(tok, 1), :], hs_ref.at[pl.ds(p, 1), :], sem).start()
        return carry

    lax.fori_loop(0, TM, body, 0)
    pltpu.make_async_copy(h_ref.at[pl.ds(0, 2 * TM), :], hs_ref.at[pl.ds(0, 2 * TM), :], sem).wait()


def _dispatch_call(pos, h, hs0):
    return pl.pallas_call(
        _dispatch_kernel,
        out_shape=jax.ShapeDtypeStruct((MOE_ROWS, D), F32),
        grid_spec=pltpu.PrefetchScalarGridSpec(
            num_scalar_prefetch=1, grid=(N_TILES,),
            in_specs=[pl.BlockSpec(memory_space=pl.ANY), pl.BlockSpec(memory_space=pl.ANY)],
            out_specs=pl.BlockSpec(memory_space=pl.ANY),
            scratch_shapes=[pltpu.SemaphoreType.DMA(())]),
        input_output_aliases={2: 0},
        compiler_params=pltpu.CompilerParams(dimension_semantics=("arbitrary",), has_side_effects=True),
        name="moe_dispatch",
    )(pos, h, hs0)


def _moe_kernel(te_ref, nv_ref, tb_ref, hs_ref, wg_ref, wu_ref, wd_ref, o_ref, hb_ref, wgb, wub, wdb):
    del te_ref, tb_ref
    i = pl.program_id(0)
    f = pl.program_id(1)
    nv = nv_ref[i]

    @pl.when(f == 0)
    def _():
        o_ref[...] = jnp.zeros_like(o_ref)
        hb_ref[...] = hs_ref[...].astype(BF16)

    @pl.when(nv > 0)
    def _():
        wgb[...] = wg_ref[...].astype(BF16)
        wub[...] = wu_ref[...].astype(BF16)
        wdb[...] = wd_ref[...].astype(BF16)
        for sub in range(MOE_TR // MOE_SUB):
            @pl.when(sub * MOE_SUB < nv)
            def _():
                rows = slice(sub * MOE_SUB, (sub + 1) * MOE_SUB)
                hh = hb_ref[rows, :]
                gg = _dot(hh, wgb[...])
                uu = _dot(hh, wub[...])
                a = (gg * _sigmoid(gg) * uu).astype(BF16)
                o_ref[rows, :] += _dot(a, wdb[...])


def _moe_call(tile_e, tile_nv, tile_blk, hs, wg, wu, wd, lidx):
    f_eff = lambda i, f, nv: jnp.where(nv[i] > 0, f, MOE_NF - 1)
    return pl.pallas_call(
        _moe_kernel,
        out_shape=jax.ShapeDtypeStruct((MOE_ROWS, D), F32),
        grid_spec=pltpu.PrefetchScalarGridSpec(
            num_scalar_prefetch=3, grid=(MOE_NT, MOE_NF),
            in_specs=[
                pl.BlockSpec((MOE_TR, D), lambda i, f, te, nv, tb: (tb[i], 0)),
                pl.BlockSpec((None, None, D, MOE_TF), lambda i, f, te, nv, tb: (lidx, te[i], 0, f_eff(i, f, nv))),
                pl.BlockSpec((None, None, D, MOE_TF), lambda i, f, te, nv, tb: (lidx, te[i], 0, f_eff(i, f, nv))),
                pl.BlockSpec((None, None, MOE_TF, D), lambda i, f, te, nv, tb: (lidx, te[i], f_eff(i, f, nv), 0)),
            ],
            out_specs=pl.BlockSpec((MOE_TR, D), lambda i, f, te, nv, tb: (i, 0)),
            scratch_shapes=[pltpu.VMEM((MOE_TR, D), BF16), pltpu.VMEM((D, MOE_TF), BF16),
                            pltpu.VMEM((D, MOE_TF), BF16), pltpu.VMEM((MOE_TF, D), BF16)]),
        compiler_params=pltpu.CompilerParams(
            dimension_semantics=("arbitrary", "arbitrary"), vmem_limit_bytes=VMEM_BIG),
        name="moe_experts",
    )(tile_e, tile_nv, tile_blk, hs, wg, wu, wd)


def _combine_kernel(pos_ref, x_ref, ada_ref, rw_ref, ys_ref, o_ref, buf, sem):
    base = pl.program_id(0) * TM

    def body(t, carry):
        tok = base + t
        for s in range(2):
            p = pos_ref[2 * tok + s]
            pltpu.make_async_copy(ys_ref.at[pl.ds(p, 1), :], buf.at[s, pl.ds(t, 1), :], sem).start()
        return carry

    lax.fori_loop(0, TM, body, 0)
    for s in range(2):
        pltpu.make_async_copy(ys_ref.at[pl.ds(0, TM), :], buf.at[s], sem).wait()
    w1 = rw_ref[:, 0:1]
    w2 = rw_ref[:, 1:2]
    y = w1 * buf[0] + w2 * buf[1]
    o_ref[...] = x_ref[...] + ada_ref[5:6, :] * y


def _combine_call(pos, x1, ada, layer, rw, ys):
    ada_map = lambda i, p: (layer, _tile_cond(i, N_CTX_TILES, LAT_TILES_PER_SEQ), 0, 0)
    return pl.pallas_call(
        _combine_kernel,
        out_shape=jax.ShapeDtypeStruct((T_ALL, D), F32),
        grid_spec=pltpu.PrefetchScalarGridSpec(
            num_scalar_prefetch=1, grid=(N_TILES,),
            in_specs=[pl.BlockSpec((TM, D), lambda i, p: (i, 0)),
                      pl.BlockSpec((None, None, N_ADA, D), ada_map),
                      pl.BlockSpec((TM, LANE), lambda i, p: (i, 0)),
                      pl.BlockSpec(memory_space=pl.ANY)],
            out_specs=pl.BlockSpec((TM, D), lambda i, p: (i, 0)),
            scratch_shapes=[pltpu.VMEM((2, TM, D), F32), pltpu.SemaphoreType.DMA(())]),
        compiler_params=pltpu.CompilerParams(dimension_semantics=("arbitrary",)),
        name="moe_combine",
    )(pos, x1, ada, rw, ys)


def _moe_layer(x1, h2, ri, rw, cnt, ada, layer, wg, wu, wd, lidx):
    idx = ri[:, 0:2]
    rank = ri[:, 2:4]
    counts = cnt[0, :N_EXP].astype(jnp.int32)
    padded = ((counts + MOE_TR - 1) // MOE_TR) * MOE_TR
    ends = jnp.cumsum(padded)
    starts = ends - padded
    sel = idx[:, :, None] == jnp.arange(N_EXP, dtype=jnp.int32)[None, None, :]
    pos = (jnp.sum(jnp.where(sel, starts[None, None, :], 0), axis=-1) + rank).reshape(-1).astype(jnp.int32)
    tile_start = jnp.arange(MOE_NT, dtype=jnp.int32) * MOE_TR
    te = jnp.sum((tile_start[:, None] >= ends[None, :]).astype(jnp.int32), axis=1)
    used = te < N_EXP
    te_c = jnp.minimum(te, N_EXP - 1)
    nv = jnp.where(used, jnp.clip(starts[te_c] + counts[te_c] - tile_start, 0, MOE_TR), 0).astype(jnp.int32)
    n_used = jnp.sum(used.astype(jnp.int32))
    last = jnp.maximum(n_used - 1, 0)
    te_eff = jnp.where(used, te_c, te_c[last]).astype(jnp.int32)
    tb = jnp.minimum(jnp.arange(MOE_NT, dtype=jnp.int32), last)
    hs = _dispatch_call(pos, h2, jnp.zeros((MOE_ROWS, D), F32))
    ys = _moe_call(te_eff, nv, tb, hs, wg, wu, wd, lidx)
    return _combine_call(pos, x1, ada, layer, rw, ys)


def _rope_tables():
    t = jnp.arange(DEC_SEQ)
    rows = (t // GRID_W).astype(F32)
    cols = (t % GRID_W).astype(F32)

    def axis_tabs(pos, d):
        inv = ROPE_BASE ** (-jnp.arange(0, d, 2, dtype=F32) / d)
        ang = pos[:, None] * inv[None, :]
        cos, sin = jnp.cos(ang), jnp.sin(ang)
        z = jnp.zeros_like(sin)
        return (jnp.concatenate([cos, cos], -1), jnp.concatenate([-sin, z], -1), jnp.concatenate([z, sin], -1))

    def axial(d):
        r = axis_tabs(rows, d // 2)
        c = axis_tabs(cols, d // 2)
        return [jnp.concatenate([a, b], -1) for a, b in zip(r, c)]

    tabs_a = [jnp.tile(tb, (1, A_HEADS)) for tb in axial(A_DH)]
    cb, sub, sdb = axial(B_ROPE)
    ones = jnp.ones((DEC_SEQ, B_NOPE), F32)
    zer = jnp.zeros((DEC_SEQ, B_NOPE), F32)
    pad = jnp.zeros((DEC_SEQ, SLOT - B_QK), F32)
    slot = lambda first, tb: jnp.tile(jnp.concatenate([first, tb, pad], -1), (1, B_HEADS))
    tabs_b = [slot(ones, cb), slot(zer, sub), slot(zer, sdb)]
    return tabs_a + tabs_b


def _attn_weights(e, w_in_attn, g_qk_a_q, g_qk_a_k, g_mla_q_a, w_mla_q_b, g_mla_kv_a, w_mla_kv_b,
                  g_qk_b_q, g_qk_b_k):
    w_in = jnp.pad(w_in_attn[e], ((0, 0), (0, IN_PAD - IN_DIM))).astype(BF16)
    w_qb = jnp.pad(w_mla_q_b[e].reshape(B_Q_LORA, B_HEADS, B_QK), ((0, 0), (0, 0), (0, SLOT - B_QK)))
    w_qb = w_qb.reshape(B_Q_LORA, B_HEADS * SLOT).astype(BF16)
    wkv = w_mla_kv_b[e].reshape(B_KV_LORA, B_HEADS, B_NOPE + B_V)
    wk = jnp.pad(wkv[:, :, :B_NOPE], ((0, 0), (0, 0), (0, SLOT - B_NOPE))).reshape(B_KV_LORA, B_HEADS * SLOT)
    wv = wkv[:, :, B_NOPE:].reshape(B_KV_LORA, B_HEADS * B_V)
    w_kv = jnp.concatenate([wk, wv], axis=1).astype(BF16)
    place = jnp.zeros((LANE, B_HEADS, SLOT), F32)
    ii = jnp.arange(B_ROPE)
    place = place.at[ii, :, B_NOPE + ii].set(1.0).reshape(LANE, B_HEADS * SLOT).astype(BF16)
    slot_gain = lambda g: jnp.tile(jnp.pad(g, (0, SLOT - B_QK)), B_HEADS).reshape(1, B_HEADS * SLOT)
    return dict(
        w_in=w_in, w_qb=w_qb, w_kv=w_kv, place=place,
        gqa=jnp.tile(g_qk_a_q[e], A_HEADS).reshape(1, 512),
        gka=jnp.tile(g_qk_a_k[e], A_KV).reshape(1, LANE),
        gql=g_mla_q_a[e].reshape(1, B_Q_LORA), gkl=g_mla_kv_a[e].reshape(1, B_KV_LORA),
        gqb=slot_gain(g_qk_b_q[e]), gkb=slot_gain(g_qk_b_k[e]))


def _pair_variants_host(x):
    h0, h1 = x[..., :64], x[..., 64:]
    z = jnp.zeros_like(h0)
    return jnp.concatenate([h0, z, z, h0, h1, z, z, h1], axis=-1)


def kernel(x_prompt, x_sample, c, cache_k_win, cache_v_win, cache_ckv, cache_krope, c_ctx, w_ada, b_ada, g_norm_mix, g_norm_ffn, w_in_attn, g_qk_a_q, g_qk_a_k, sink_a, g_mla_q_a, w_mla_q_b, g_mla_kv_a, w_mla_kv_b, g_qk_b_q, g_qk_b_k, w_out_attn, w_ffn_gate, w_ffn_up, w_ffn_down, w_conv_pw1, b_conv_pw1, w_conv_dw, b_conv_dw, g_conv_ln, b_conv_ln, w_conv_pw2, b_conv_pw2, w_router, w_moe_gate, w_moe_up, w_moe_down):
    x = jnp.concatenate([x_prompt.reshape(T_CTX, D), x_sample.reshape(T_LAT, D)], axis=0)
    cond8 = jnp.concatenate([c_ctx[None, :], c, jnp.zeros((8 - 1 - DEC_BATCH, D), F32)], axis=0)
    ada = _ada_call(cond8, w_ada, b_ada).reshape(DEPTH, 8, N_ADA, D)
    tabs = _rope_tables()

    wo_b = w_out_attn.astype(BF16)
    wg_b = w_ffn_gate.astype(BF16)
    wu_b = w_ffn_up.astype(BF16)
    wd_b = w_ffn_down.astype(BF16)
    w1_b = w_conv_pw1.astype(BF16)
    w2_b = w_conv_pw2.astype(BF16)

    new_k, new_v, new_ckv, new_kr = [], [], [], []
    for layer in range(DEPTH):
        gm = g_norm_mix[layer].reshape(1, D)
        gf = g_norm_ffn[layer].reshape(1, D)
        if layer % 2 == 0:
            e = layer // 2
            lw = _attn_weights(e, w_in_attn, g_qk_a_q, g_qk_a_k, g_mla_q_a, w_mla_q_b, g_mla_kv_a,
                               w_mla_kv_b, g_qk_b_q, g_qk_b_k)
            sink = sink_a[e].reshape(A_HEADS)
            (qa, ka4, va4, qb, kb, vlo, vhi, kaf, vaf, ckvf, krf) = _pre_attn_call(False, x, ada, layer, gm, lw, None)
            o_ctx = _attn_ctx_call(sink, qa, ka4, va4, qb, kb, vlo, vhi)
            new_k.append(kaf.reshape(BATCH, SEQ, A_KV, A_DH))
            new_v.append(vaf.reshape(BATCH, SEQ, A_KV, A_DH))
            new_ckv.append(ckvf.reshape(BATCH, SEQ, B_KV_LORA))
            new_kr.append(krf[:, :B_ROPE].reshape(BATCH, SEQ, B_ROPE))
            (qa, ka4, va4, qb, kb, vlo, vhi) = _pre_attn_call(True, x, ada, layer, gm, lw, tabs)
            kr_c = jnp.pad(cache_krope[:, e].reshape(DEC_BATCH * PAST, B_ROPE), ((0, 0), (0, LANE - B_ROPE)))
            kb_c, vlo_c, vhi_c = _cache_kv_call(cache_ckv[:, e].reshape(DEC_BATCH * PAST, B_KV_LORA), kr_c, lw)
            pad_rows = lambda a: jnp.pad(a.reshape(DEC_BATCH, DEC_SEQ, 512), ((0, 0), (QB, QB), (0, 0)))
            ka4c = _pair_variants_host(cache_k_win[:, e].reshape(DEC_BATCH, PAST, LANE)).astype(BF16)
            va4c = _pair_variants_host(cache_v_win[:, e].reshape(DEC_BATCH, PAST, LANE)).astype(BF16)
            cat = lambda a, b, w: jnp.concatenate(
                [a.reshape(DEC_BATCH, PAST, w), b.reshape(DEC_BATCH, DEC_SEQ, w)], axis=1)
            o_lat = _attn_lat_call(sink, qa, pad_rows(ka4), pad_rows(va4), ka4c, va4c, qb,
                                   cat(kb_c, kb, 1024), cat(vlo_c, vlo, 512), cat(vhi_c, vhi, 512))
            o = jnp.concatenate([o_ctx, o_lat], axis=0)
            x = _post_attn_call(x, o, ada, layer, gf, wo_b[e], wg_b[e], wu_b[e], wd_b[e])
        else:
            o_ = layer // 2
            cw = dict(w1=w1_b[o_], b1=b_conv_pw1[o_].reshape(1, 2 * D), wdw=w_conv_dw[o_],
                      bdw=b_conv_dw[o_].reshape(1, D), gln=g_conv_ln[o_].reshape(1, D),
                      bln=b_conv_ln[o_].reshape(1, D), w2=w2_b[o_], b2=b_conv_pw2[o_].reshape(1, D),
                      wr=jnp.pad(w_router[o_], ((0, 0), (0, LANE - N_EXP))))
            x1, h2, ri, rw, cnt = _conv_router_call(x, ada, layer, gm, gf, cw)
            x = _moe_layer(x1, h2, ri, rw, cnt, ada, layer, w_moe_gate, w_moe_up, w_moe_down, o_)

    y_prompt = x[:T_CTX].reshape(BATCH, SEQ, D)
    y_sample = x[T_CTX:].reshape(DEC_BATCH, DEC_SEQ, D)
    return (y_prompt, y_sample, jnp.stack(new_k, axis=1), jnp.stack(new_v, axis=1),
            jnp.stack(new_ckv, axis=1), jnp.stack(new_kr, axis=1))
```

```python
import functools

import jax
import jax.numpy as jnp
from jax import lax
from jax.experimental import pallas as pl
from jax.experimental.pallas import tpu as pltpu

F32 = jnp.float32
BF16 = jnp.bfloat16

D = 1024
BATCH = 32
SEQ = 256
DEPTH = 4
DEC_BATCH = 2
DEC_SEQ = 1024
PAST = 256
GRID_W = 64
N_EVEN = 2
A_HEADS = 8
A_KV = 2
A_DH = 64
WINDOW = 128
B_HEADS = 8
B_Q_LORA = 256
B_KV_LORA = 128
B_NOPE = 64
B_ROPE = 32
B_QK = B_NOPE + B_ROPE
B_V = 64
IN_DIM = 1184
IN_PAD = 1280
CONV_W = 31
D_FF = 2816
N_EXP = 8
D_FFE = 3584
ROPE_BASE = 10000.0
EPS = 1e-6
N_ADA = 6

T_CTX = BATCH * SEQ
T_LAT = DEC_BATCH * DEC_SEQ
T_ALL = T_CTX + T_LAT
TM = 256
N_TILES = T_ALL // TM
N_CTX_TILES = T_CTX // TM
LAT_TILES_PER_SEQ = DEC_SEQ // TM
LANE = 128
SLOT = 128

MOE_TR = 1024
MOE_SUB = 256
MOE_TF = 512
MOE_NF = D_FFE // MOE_TF
MOE_ROWS = 2 * T_ALL + N_EXP * MOE_TR
MOE_NT = MOE_ROWS // MOE_TR

FFN_TM = 512
FFN_CH = 1408

VMEM_BIG = 56 * 1024 * 1024


def _dot(a, b):
    return jnp.dot(a, b, preferred_element_type=F32)


def _dot_nt(a, b):
    return lax.dot_general(a, b, (((1,), (1,)), ((), ())), preferred_element_type=F32)


def _split(a):
    hi = a.astype(BF16)
    lo = (a - hi.astype(F32)).astype(BF16)
    return hi, lo


def _dot_x3(a, b):
    ah, al = _split(a)
    bh, bl = _split(b)
    return _dot(ah, bh) + (_dot(ah, bl) + _dot(al, bh))


def _sigmoid(x):
    return 1.0 / (1.0 + jnp.exp(-x))


def _rms(x, n):
    return x * lax.rsqrt(jnp.sum(x * x, axis=-1, keepdims=True) * (1.0 / n) + EPS)


def _modulate(x, g, shift, scale):
    return _rms(x, D) * g * (1.0 + scale) + shift


def _lo_mask(rows):
    return lax.broadcasted_iota(jnp.int32, (rows, LANE), 1) < 64


def _ada_kernel(cond_ref, w_ref, b_ref, o_ref):
    c = cond_ref[...]
    s = c * _sigmoid(c)
    o_ref[...] = _dot_x3(s, w_ref[...]) + b_ref[...]


def _ada_call(cond8, w_ada, b_ada):
    tn = 1024
    return pl.pallas_call(
        _ada_kernel,
        out_shape=jax.ShapeDtypeStruct((DEPTH, 8, N_ADA * D), F32),
        grid=(DEPTH, N_ADA * D // tn),
        in_specs=[
            pl.BlockSpec((8, D), lambda l, j: (0, 0)),
            pl.BlockSpec((None, D, tn), lambda l, j: (l, 0, j)),
            pl.BlockSpec((None, 1, tn), lambda l, j: (l, 0, j)),
        ],
        out_specs=pl.BlockSpec((None, 8, tn), lambda l, j: (l, 0, j)),
        compiler_params=pltpu.CompilerParams(
            dimension_semantics=("arbitrary", "arbitrary"), vmem_limit_bytes=VMEM_BIG),
        name="ada",
    )(cond8, w_ada, b_ada.reshape(DEPTH, 1, N_ADA * D))


def _rope(x, cos, sup, sdn, shift):
    up = pltpu.roll(x, LANE - shift, 1)
    dn = pltpu.roll(x, shift, 1)
    return x * cos + up * sup + dn * sdn


def _norm64_block(blk, lo):
    sq = blk * blk
    s_lo = jnp.sum(jnp.where(lo, sq, 0.0), axis=-1, keepdims=True)
    s_hi = jnp.sum(jnp.where(lo, 0.0, sq), axis=-1, keepdims=True)
    r = jnp.where(lo, lax.rsqrt(s_lo * (1.0 / A_DH) + EPS), lax.rsqrt(s_hi * (1.0 / A_DH) + EPS))
    return blk * r


def _pair_variants(x, lo):
    r = pltpu.roll(x, 64, 1)
    return (jnp.where(lo, x, 0.0), jnp.where(lo, 0.0, r), jnp.where(lo, r, 0.0), jnp.where(lo, 0.0, x))


def _mla_kv(ckv_n, kr, wkv_ref, place_ref, gk_ref, lo, kb_ref, vlo_ref, vhi_ref, rope_tabs):
    kv = _dot(ckv_n.astype(BF16), wkv_ref[...])
    kr_hi, kr_lo = _split(kr)
    place = place_ref[...]
    krs = _dot(kr_hi, place) + _dot(kr_lo, place)
    for h in range(B_HEADS):
        sl = slice(h * SLOT, (h + 1) * SLOT)
        blk = kv[:, sl] + krs[:, sl]
        blk = _rms(blk, B_QK) * gk_ref[:, sl]
        if rope_tabs is not None:
            cos, sup, sdn = rope_tabs
            blk = _rope(blk, cos[:, sl], sup[:, sl], sdn[:, sl], 8)
        kb_ref[:, sl] = blk.astype(BF16)
    for j in range(B_HEADS // 2):
        sl = slice(j * LANE, (j + 1) * LANE)
        v = kv[:, B_HEADS * SLOT + j * LANE: B_HEADS * SLOT + (j + 1) * LANE]
        vlo_ref[:, sl] = jnp.where(lo, v, 0.0).astype(BF16)
        vhi_ref[:, sl] = jnp.where(lo, 0.0, v).astype(BF16)


def _pre_attn_kernel(rope, *refs):
    if rope:
        (x_ref, ada_ref, g_ref, win_ref, wqb_ref, wkv_ref, place_ref,
         gqa_ref, gka_ref, gql_ref, gkl_ref, gqb_ref, gkb_ref,
         ca_ref, sua_ref, sda_ref, cb_ref, sub_ref, sdb_ref,
         qa_o, ka4_o, va4_o, qb_o, kb_o, vlo_o, vhi_o) = refs
    else:
        (x_ref, ada_ref, g_ref, win_ref, wqb_ref, wkv_ref, place_ref,
         gqa_ref, gka_ref, gql_ref, gkl_ref, gqb_ref, gkb_ref,
         qa_o, ka4_o, va4_o, qb_o, kb_o, vlo_o, vhi_o,
         kaf_o, vaf_o, ckvf_o, krf_o) = refs
    lo = _lo_mask(TM)
    h = _modulate(x_ref[...], g_ref[...], ada_ref[0:1, :], ada_ref[1:2, :])
    p = _dot(h.astype(BF16), win_ref[...])

    for j in range(4):
        sl = slice(j * LANE, (j + 1) * LANE)
        blk = _norm64_block(p[:, sl], lo) * gqa_ref[:, sl]
        if rope:
            blk = _rope(blk, ca_ref[:, sl], sua_ref[:, sl], sda_ref[:, sl], 16)
        qa_o[:, sl] = (blk * (A_DH ** -0.5)).astype(BF16)
    ka = _norm64_block(p[:, 512:640], lo) * gka_ref[...]
    va = p[:, 640:768]
    if not rope:
        kaf_o[...] = ka
        vaf_o[...] = va
    else:
        ka = _rope(ka, ca_ref[:, 0:LANE], sua_ref[:, 0:LANE], sda_ref[:, 0:LANE], 16)
    for n, (kk, vv) in enumerate(zip(_pair_variants(ka, lo), _pair_variants(va, lo))):
        sl = slice(n * LANE, (n + 1) * LANE)
        ka4_o[:, sl] = kk.astype(BF16)
        va4_o[:, sl] = vv.astype(BF16)

    cq = _rms(p[:, 768:1024], B_Q_LORA) * gql_ref[...]
    qb = _dot(cq.astype(BF16), wqb_ref[...])
    for hh in range(B_HEADS):
        sl = slice(hh * SLOT, (hh + 1) * SLOT)
        blk = _rms(qb[:, sl], B_QK) * gqb_ref[:, sl]
        if rope:
            blk = _rope(blk, cb_ref[:, sl], sub_ref[:, sl], sdb_ref[:, sl], 8)
        qb_o[:, sl] = (blk * (B_QK ** -0.5)).astype(BF16)

    ckv = _rms(p[:, 1024:1152], B_KV_LORA) * gkl_ref[...]
    kr = p[:, 1152:1280]
    if not rope:
        ckvf_o[...] = ckv
        krf_o[...] = kr
    tabs = (cb_ref, sub_ref, sdb_ref) if rope else None
    _mla_kv(ckv, kr, wkv_ref, place_ref, gkb_ref, lo, kb_o, vlo_o, vhi_o, tabs)


def _const_spec(shape):
    nd = len(shape)
    return pl.BlockSpec(shape, lambda *a: (0,) * nd)


def _pre_attn_call(rope, x, ada, layer, g, lw, tabs):
    n_tiles = (T_LAT if rope else T_CTX) // TM
    off = N_CTX_TILES if rope else 0
    n_rows = n_tiles * TM
    if rope:
        ada_map = lambda i: (layer, 1 + i // LAT_TILES_PER_SEQ, 0, 0)
    else:
        ada_map = lambda i: (layer, 0, 0, 0)
    in_specs = [
        pl.BlockSpec((TM, D), lambda i: (i + off, 0)),
        pl.BlockSpec((None, None, N_ADA, D), ada_map),
        _const_spec((1, D)),
        _const_spec((D, IN_PAD)),
        _const_spec((B_Q_LORA, B_HEADS * SLOT)),
        _const_spec((B_KV_LORA, B_HEADS * SLOT + B_HEADS * B_V)),
        _const_spec((LANE, B_HEADS * SLOT)),
        _const_spec((1, 512)), _const_spec((1, LANE)), _const_spec((1, B_Q_LORA)),
        _const_spec((1, B_KV_LORA)), _const_spec((1, B_HEADS * SLOT)), _const_spec((1, B_HEADS * SLOT)),
    ]
    args = [x, ada, g, lw["w_in"], lw["w_qb"], lw["w_kv"], lw["place"],
            lw["gqa"], lw["gka"], lw["gql"], lw["gkl"], lw["gqb"], lw["gkb"]]
    if rope:
        tmap = lambda i: (i % LAT_TILES_PER_SEQ, 0)
        in_specs += [pl.BlockSpec((TM, 512), tmap)] * 3 + [pl.BlockSpec((TM, B_HEADS * SLOT), tmap)] * 3
        args += list(tabs)
    row = lambda w: pl.BlockSpec((TM, w), lambda i: (i, 0))
    out_shape = [jax.ShapeDtypeStruct((n_rows, w), BF16) for w in (512, 512, 512, 1024, 1024, 512, 512)]
    out_specs = [row(w) for w in (512, 512, 512, 1024, 1024, 512, 512)]
    if not rope:
        out_shape += [jax.ShapeDtypeStruct((n_rows, LANE), F32)] * 4
        out_specs += [row(LANE)] * 4
    return pl.pallas_call(
        functools.partial(_pre_attn_kernel, rope),
        out_shape=out_shape, grid=(n_tiles,), in_specs=in_specs, out_specs=out_specs,
        compiler_params=pltpu.CompilerParams(
            dimension_semantics=("arbitrary",), vmem_limit_bytes=VMEM_BIG),
        name="pre_attn_lat" if rope else "pre_attn_ctx",
    )(*args)


def _cache_kv_kernel(ckv_ref, kr_ref, wkv_ref, place_ref, gkb_ref, kb_o, vlo_o, vhi_o):
    lo = _lo_mask(PAST)
    _mla_kv(ckv_ref[...], kr_ref[...], wkv_ref, place_ref, gkb_ref, lo, kb_o, vlo_o, vhi_o, None)


def _cache_kv_call(ckv, kr, lw):
    rows = DEC_BATCH * PAST
    row = lambda w: pl.BlockSpec((PAST, w), lambda i: (i, 0))
    return pl.pallas_call(
        _cache_kv_kernel,
        out_shape=[jax.ShapeDtypeStruct((rows, 1024), BF16), jax.ShapeDtypeStruct((rows, 512), BF16),
                   jax.ShapeDtypeStruct((rows, 512), BF16)],
        grid=(DEC_BATCH,),
        in_specs=[row(LANE), row(LANE),
                  _const_spec((B_KV_LORA, B_HEADS * SLOT + B_HEADS * B_V)),
                  _const_spec((LANE, B_HEADS * SLOT)), _const_spec((1, B_HEADS * SLOT))],
        out_specs=[row(1024), row(512), row(512)],
        compiler_params=pltpu.CompilerParams(dimension_semantics=("arbitrary",)),
        name="cache_kv",
    )(ckv, kr, lw["w_kv"], lw["place"], lw["gkb"])


def _softmax_pv(scores, values, sink):
    m = jnp.max(scores[0], axis=-1, keepdims=True)
    for s in scores[1:]:
        m = jnp.maximum(m, jnp.max(s, axis=-1, keepdims=True))
    if sink is not None:
        m = jnp.maximum(m, sink)
    ps = [jnp.exp(s - m) for s in scores]
    den = jnp.sum(ps[0], axis=-1, keepdims=True)
    for p in ps[1:]:
        den = den + jnp.sum(p, axis=-1, keepdims=True)
    if sink is not None:
        den = den + jnp.exp(sink - m)
    inv = 1.0 / den
    out = _dot((ps[0] * inv).astype(BF16), values[0])
    for p, v in zip(ps[1:], values[1:]):
        out = out + _dot((p * inv).astype(BF16), v)
    return out


def _attn_ctx_kernel(sink_ref, qa_ref, ka4_ref, va4_ref, qb_ref, kb_ref, vlo_ref, vhi_ref, o_ref):
    for j in range(4):
        q = qa_ref[:, j * LANE:(j + 1) * LANE]
        acc = None
        for par in range(2):
            c = (2 * (j // 2) + par) * LANE
            s = _dot_nt(q, ka4_ref[:, c:c + LANE])
            o = _softmax_pv([s], [va4_ref[:, c:c + LANE]], sink_ref[2 * j + par])
            acc = o if acc is None else acc + o
        o_ref[:, j * LANE:(j + 1) * LANE] = acc.astype(BF16)
    for j in range(4):
        acc = None
        for par, v_ref in enumerate((vlo_ref, vhi_ref)):
            hh = 2 * j + par
            s = _dot_nt(qb_ref[:, hh * SLOT:(hh + 1) * SLOT], kb_ref[:, hh * SLOT:(hh + 1) * SLOT])
            o = _softmax_pv([s], [v_ref[:, j * LANE:(j + 1) * LANE]], None)
            acc = o if acc is None else acc + o
        o_ref[:, 512 + j * LANE:512 + (j + 1) * LANE] = acc.astype(BF16)


def _attn_ctx_call(sink, qa, ka4, va4, qb, kb, vlo, vhi):
    row = lambda w: pl.BlockSpec((SEQ, w), lambda b: (b, 0))
    return pl.pallas_call(
        _attn_ctx_kernel,
        out_shape=jax.ShapeDtypeStruct((T_CTX, D), BF16),
        grid=(BATCH,),
        in_specs=[pl.BlockSpec(memory_space=pltpu.SMEM),
                  row(512), row(512), row(512), row(1024), row(1024), row(512), row(512)],
        out_specs=row(D),
        compiler_params=pltpu.CompilerParams(dimension_semantics=("arbitrary",)),
        name="attn_ctx",
    )(sink, qa, ka4, va4, qb, kb, vlo, vhi)


QB = 128
N_QB = DEC_SEQ // QB
KPAD = DEC_SEQ + 2 * QB
KALL = PAST + DEC_SEQ


def _attn_lat_kernel(sink_ref, qa_ref, ka4p_ref, va4p_ref, ka4c_ref, va4c_ref,
                     qb_ref, kb_ref, vlo_ref, vhi_ref, o_ref):
    n = pl.program_id(1)
    r = lax.broadcasted_iota(jnp.int32, (QB, 3 * QB), 0)
    c = lax.broadcasted_iota(jnp.int32, (QB, 3 * QB), 1)
    rel = c - QB - r
    kpos = (n - 1) * QB + c
    valid = (jnp.abs(rel) <= WINDOW) & (kpos >= 0) & (kpos < DEC_SEQ)
    k0 = pl.multiple_of(n * QB, QB)
    for j in range(4):
        q = qa_ref[:, j * LANE:(j + 1) * LANE]
        acc = None
        for par in range(2):
            cc = (2 * (j // 2) + par) * LANE
            s_loc = _dot_nt(q, ka4p_ref[pl.ds(k0, 3 * QB), cc:cc + LANE])
            s_loc = jnp.where(valid, s_loc, -jnp.inf)
            s_ctx = _dot_nt(q, ka4c_ref[:, cc:cc + LANE])
            o = _softmax_pv([s_ctx, s_loc],
                            [va4c_ref[:, cc:cc + LANE], va4p_ref[pl.ds(k0, 3 * QB), cc:cc + LANE]],
                            sink_ref[2 * j + par])
            acc = o if acc is None else acc + o
        o_ref[:, j * LANE:(j + 1) * LANE] = acc.astype(BF16)
    for j in range(4):
        acc = None
        for par, v_ref in enumerate((vlo_ref, vhi_ref)):
            hh = 2 * j + par
            s = _dot_nt(qb_ref[:, hh * SLOT:(hh + 1) * SLOT], kb_ref[:, hh * SLOT:(hh + 1) * SLOT])
            o = _softmax_pv([s], [v_ref[:, j * LANE:(j + 1) * LANE]], None)
            acc = o if acc is None else acc + o
        o_ref[:, 512 + j * LANE:512 + (j + 1) * LANE] = acc.astype(BF16)


def _attn_lat_call(sink, qa, ka4p, va4p, ka4c, va4c, qb, kball, vlo_all, vhi_all):
    qrow = lambda w: pl.BlockSpec((QB, w), lambda b, n: (b * N_QB + n, 0))
    per_b = lambda rows, w: pl.BlockSpec((None, rows, w), lambda b, n: (b, 0, 0))
    return pl.pallas_call(
        _attn_lat_kernel,
        out_shape=jax.ShapeDtypeStruct((T_LAT, D), BF16),
        grid=(DEC_BATCH, N_QB),
        in_specs=[pl.BlockSpec(memory_space=pltpu.SMEM),
                  qrow(512), per_b(KPAD, 512), per_b(KPAD, 512), per_b(PAST, 512), per_b(PAST, 512),
                  qrow(1024), per_b(KALL, 1024), per_b(KALL, 512), per_b(KALL, 512)],
        out_specs=qrow(D),
        compiler_params=pltpu.CompilerParams(dimension_semantics=("arbitrary", "arbitrary")),
        name="attn_lat",
    )(sink, qa, ka4p, va4p, ka4c, va4c, qb, kball, vlo_all, vhi_all)


def _post_attn_kernel(x_ref, o_ref, ada_ref, g_ref, wo_ref, wg_ref, wu_ref, wd_ref, y_ref):
    mix = _dot(o_ref[...], wo_ref[...])
    x1 = x_ref[...] + ada_ref[2:3, :] * mix
    h = _modulate(x1, g_ref[...], ada_ref[3:4, :], ada_ref[4:5, :]).astype(BF16)
    acc = None
    for c0 in range(0, D_FF, FFN_CH):
        gg = _dot(h, wg_ref[:, c0:c0 + FFN_CH])
        uu = _dot(h, wu_ref[:, c0:c0 + FFN_CH])
        a = (gg * _sigmoid(gg) * uu).astype(BF16)
        part = _dot(a, wd_ref[c0:c0 + FFN_CH, :])
        acc = part if acc is None else acc + part
    y_ref[...] = x1 + ada_ref[5:6, :] * acc


def _tile_cond(i, tiles_per_ctx, tiles_per_lat_seq):
    return jnp.where(i < tiles_per_ctx, 0, 1 + (i - tiles_per_ctx) // tiles_per_lat_seq)


def _post_attn_call(x, o, ada, layer, g, wo, wg, wu, wd):
    n_tiles = T_ALL // FFN_TM
    ada_map = lambda i: (layer, _tile_cond(i, T_CTX // FFN_TM, DEC_SEQ // FFN_TM), 0, 0)
    one = pl.Buffered(1)
    return pl.pallas_call(
        _post_attn_kernel,
        out_shape=jax.ShapeDtypeStruct((T_ALL, D), F32),
        grid=(n_tiles,),
        in_specs=[pl.BlockSpec((FFN_TM, D), lambda i: (i, 0)),
                  pl.BlockSpec((FFN_TM, D), lambda i: (i, 0)),
                  pl.BlockSpec((None, None, N_ADA, D), ada_map),
                  _const_spec((1, D)),
                  pl.BlockSpec((D, D), lambda i: (0, 0), pipeline_mode=one),
                  pl.BlockSpec((D, D_FF), lambda i: (0, 0), pipeline_mode=one),
                  pl.BlockSpec((D, D_FF), lambda i: (0, 0), pipeline_mode=one),
                  pl.BlockSpec((D_FF, D), lambda i: (0, 0), pipeline_mode=one)],
        out_specs=pl.BlockSpec((FFN_TM, D), lambda i: (i, 0)),
        compiler_params=pltpu.CompilerParams(
            dimension_semantics=("arbitrary",), vmem_limit_bytes=VMEM_BIG),
        name="post_attn_ffn",
    )(x, o, ada, g, wo, wg, wu, wd)


HALO = 16
EXT = TM + 2 * HALO
SHROWS = TM + 24
CONV_RC = 32


def _conv_router_kernel(x_ref, xp_ref, xn_ref, ada_ref, gm_ref, gf_ref, w1_ref, b1_ref, wdw_ref, bdw_ref,
                        gln_ref, bln_ref, w2_ref, b2_ref, wr_ref,
                        x1_o, h2_o, ri_o, rw_o, cnt_o, p_ref, sh_ref, v_ref, run_ref):
    i = pl.program_id(0)

    @pl.when(i == 0)
    def _():
        run_ref[...] = jnp.zeros_like(run_ref)

    is_lat = i >= N_CTX_TILES
    j = (i - N_CTX_TILES) % LAT_TILES_PER_SEQ
    lflag = jnp.where(is_lat & (j != 0), 1.0, 0.0)
    rflag = jnp.where(is_lat & (j != LAT_TILES_PER_SEQ - 1), 1.0, 0.0)

    x = x_ref[...]
    xx = jnp.concatenate([xp_ref[...], x, xn_ref[...]], axis=0)
    h = _modulate(xx, gm_ref[...], ada_ref[0:1, :], ada_ref[1:2, :])
    a = _dot(h.astype(BF16), w1_ref[...]) + b1_ref[...]
    u = a[:, :D] * _sigmoid(a[:, D:])
    row = lax.broadcasted_iota(jnp.int32, (EXT, 1), 0)
    u = u * jnp.where(row < HALO, lflag, jnp.where(row >= HALO + TM, rflag, 1.0))

    p_ref[...] = u
    for r in range(8):
        sh_ref[r] = p_ref[pl.ds(r, SHROWS), :]

    def chunk(cidx, carry):
        r0 = pl.multiple_of(cidx * CONV_RC, CONV_RC)
        acc = jnp.zeros((CONV_RC, D), F32) + bdw_ref[...]
        for k in range(CONV_W):
            o = k + 1
            acc = acc + wdw_ref[k:k + 1, :] * sh_ref[o % 8, pl.ds(r0 + 8 * (o // 8), CONV_RC), :]
        mu = jnp.mean(acc, axis=-1, keepdims=True)
        xc = acc - mu
        y = xc * lax.rsqrt(jnp.mean(xc * xc, axis=-1, keepdims=True) + EPS)
        y = y * gln_ref[...] + bln_ref[...]
        v_ref[pl.ds(r0, CONV_RC), :] = (y * _sigmoid(y)).astype(BF16)
        return carry

    lax.fori_loop(0, TM // CONV_RC, chunk, 0)

    mix = _dot(v_ref[...], w2_ref[...]) + b2_ref[...]
    x1 = x + ada_ref[2:3, :] * mix
    x1_o[...] = x1
    h2 = _modulate(x1, gf_ref[...], ada_ref[3:4, :], ada_ref[4:5, :])
    h2_o[...] = h2

    lane = lax.broadcasted_iota(jnp.int32, (TM, LANE), 1)
    lane_f = lane.astype(F32)
    logits = jnp.where(lane < N_EXP, _dot_x3(h2, wr_ref[...]), -jnp.inf)
    m1 = jnp.max(logits, axis=-1, keepdims=True)
    i1 = jnp.min(jnp.where(logits == m1, lane_f, float(LANE)), axis=-1, keepdims=True)
    rest = jnp.where(lane_f == i1, -jnp.inf, logits)
    m2 = jnp.max(rest, axis=-1, keepdims=True)
    i2 = jnp.min(jnp.where(rest == m2, lane_f, float(LANE)), axis=-1, keepdims=True)
    e = jnp.exp(m2 - m1)
    w1 = 1.0 / (1.0 + e)
    w2 = e / (1.0 + e)

    oh1 = jnp.where(lane_f == i1, 1.0, 0.0)
    oh2 = jnp.where(lane_f == i2, 1.0, 0.0)
    tr = lax.broadcasted_iota(jnp.int32, (TM, TM), 0)
    tc = lax.broadcasted_iota(jnp.int32, (TM, TM), 1)
    tri = jnp.where(tr > tc, 1.0, 0.0).astype(BF16)
    run = run_ref[0:1, :]
    cnt1 = jnp.sum(oh1, axis=0, keepdims=True)
    cnt2 = jnp.sum(oh2, axis=0, keepdims=True)
    pre1 = _dot(tri, oh1.astype(BF16)) + run
    pre2 = _dot(tri, oh2.astype(BF16)) + (run + cnt1)
    rank1 = jnp.sum(oh1 * pre1, axis=-1, keepdims=True)
    rank2 = jnp.sum(oh2 * pre2, axis=-1, keepdims=True)
    new_run = run + cnt1 + cnt2
    run_ref[...] = jnp.broadcast_to(new_run, run_ref.shape)
    cnt_o[...] = jnp.broadcast_to(new_run, cnt_o.shape)

    ri = jnp.where(lane == 0, i1, jnp.where(lane == 1, i2, jnp.where(lane == 2, rank1, jnp.where(lane == 3, rank2, 0.0))))
    ri_o[...] = ri.astype(jnp.int32)
    rw_o[...] = jnp.where(lane == 0, w1, jnp.where(lane == 1, w2, 0.0))


def _conv_router_call(x, ada, layer, gm, gf, cw):
    ada_map = lambda i: (layer, _tile_cond(i, N_CTX_TILES, LAT_TILES_PER_SEQ), 0, 0)
    hb = TM // HALO
    row = lambda w: pl.BlockSpec((TM, w), lambda i: (i, 0))
    return pl.pallas_call(
        _conv_router_kernel,
        out_shape=[jax.ShapeDtypeStruct((T_ALL, D), F32), jax.ShapeDtypeStruct((T_ALL, D), F32),
                   jax.ShapeDtypeStruct((T_ALL, LANE), jnp.int32), jax.ShapeDtypeStruct((T_ALL, LANE), F32),
                   jax.ShapeDtypeStruct((8, LANE), F32)],
        grid=(N_TILES,),
        in_specs=[row(D),
                  pl.BlockSpec((HALO, D), lambda i: (jnp.maximum(i * hb - 1, 0), 0)),
                  pl.BlockSpec((HALO, D), lambda i: (jnp.minimum((i + 1) * hb, T_ALL // HALO - 1), 0)),
                  pl.BlockSpec((None, None, N_ADA, D), ada_map),
                  _const_spec((1, D)), _const_spec((1, D)),
                  _const_spec((D, 2 * D)), _const_spec((1, 2 * D)),
                  _const_spec((CONV_W, D)), _const_spec((1, D)),
                  _const_spec((1, D)), _const_spec((1, D)),
                  _const_spec((D, D)), _const_spec((1, D)),
                  _const_spec((D, LANE))],
        out_specs=[row(D), row(D), row(LANE), row(LANE), _const_spec((8, LANE))],
        scratch_shapes=[pltpu.VMEM((EXT, D), F32), pltpu.VMEM((8, SHROWS, D), F32), pltpu.VMEM((TM, D), BF16),
                        pltpu.VMEM((8, LANE), F32)],
        compiler_params=pltpu.CompilerParams(
            dimension_semantics=("arbitrary",), vmem_limit_bytes=VMEM_BIG),
        name="conv_router",
    )(x, x, x, ada, gm, gf, cw["w1"], cw["b1"], cw["wdw"], cw["bdw"], cw["gln"], cw["bln"],
      cw["w2"], cw["b2"], cw["wr"])


def _dispatch_kernel(pos_ref, h_ref, hs_in_ref, hs_ref, sem):
    del hs_in_ref
    base = pl.program_id(0) * TM

    def body(t, carry):
        for s in range(2):
            p = pos_ref[2 * (base + t) + s]
            pltpu.make_async_copy(h_ref.at[pl.ds(t, 1), :], hs_ref.at[pl.ds(p, 1), :], sem).start()
        return carry

    lax.fori_loop(0, TM, body, 0)
    for s in range(2):
        pltpu.make_async_copy(h_ref, hs_ref.at[pl.ds(0, TM), :], sem).wait()


def _dispatch_call(pos, h, hs0):
    return pl.pallas_call(
        _dispatch_kernel,
        out_shape=jax.ShapeDtypeStruct((MOE_ROWS, D), F32),
        grid_spec=pltpu.PrefetchScalarGridSpec(
            num_scalar_prefetch=1, grid=(N_TILES,),
            in_specs=[pl.BlockSpec((TM, D), lambda i, p: (i, 0)), pl.BlockSpec(memory_space=pl.ANY)],
            out_specs=pl.BlockSpec(memory_space=pl.ANY),
            scratch_shapes=[pltpu.SemaphoreType.DMA(())]),
        input_output_aliases={2: 0},
        compiler_params=pltpu.CompilerParams(dimension_semantics=("arbitrary",), has_side_effects=True),
        name="moe_dispatch",
    )(pos, h, hs0)


def _moe_kernel(te_ref, nv_ref, tb_ref, hs_ref, wg_ref, wu_ref, wd_ref, o_ref, hb_ref, wgb, wub, wdb):
    del te_ref, tb_ref
    i = pl.program_id(0)
    f = pl.program_id(1)
    nv = nv_ref[i]

    @pl.when(f == 0)
    def _():
        o_ref[...] = jnp.zeros_like(o_ref)
        hb_ref[...] = hs_ref[...].astype(BF16)

    @pl.when(nv > 0)
    def _():
        wgb[...] = wg_ref[...].astype(BF16)
        wub[...] = wu_ref[...].astype(BF16)
        wdb[...] = wd_ref[...].astype(BF16)
        for sub in range(MOE_TR // MOE_SUB):
            @pl.when(sub * MOE_SUB < nv)
            def _():
                rows = slice(sub * MOE_SUB, (sub + 1) * MOE_SUB)
                hh = hb_ref[rows, :]
                gg = _dot(hh, wgb[...])
                uu = _dot(hh, wub[...])
                a = (gg * _sigmoid(gg) * uu).astype(BF16)
                o_ref[rows, :] += _dot(a, wdb[...])


def _moe_call(tile_e, tile_nv, tile_blk, hs, wg, wu, wd, lidx):
    f_eff = lambda i, f, nv: jnp.where(nv[i] > 0, f, MOE_NF - 1)
    return pl.pallas_call(
        _moe_kernel,
        out_shape=jax.ShapeDtypeStruct((MOE_ROWS, D), F32),
        grid_spec=pltpu.PrefetchScalarGridSpec(
            num_scalar_prefetch=3, grid=(MOE_NT, MOE_NF),
            in_specs=[
                pl.BlockSpec((MOE_TR, D), lambda i, f, te, nv, tb: (tb[i], 0)),
                pl.BlockSpec((None, None, D, MOE_TF), lambda i, f, te, nv, tb: (lidx, te[i], 0, f_eff(i, f, nv))),
                pl.BlockSpec((None, None, D, MOE_TF), lambda i, f, te, nv, tb: (lidx, te[i], 0, f_eff(i, f, nv))),
                pl.BlockSpec((None, None, MOE_TF, D), lambda i, f, te, nv, tb: (lidx, te[i], f_eff(i, f, nv), 0)),
            ],
            out_specs=pl.BlockSpec((MOE_TR, D), lambda i, f, te, nv, tb: (i, 0)),
            scratch_shapes=[pltpu.VMEM((MOE_TR, D), BF16), pltpu.VMEM((D, MOE_TF), BF16),
                            pltpu.VMEM((D, MOE_TF), BF16), pltpu.VMEM((MOE_TF, D), BF16)]),
        compiler_params=pltpu.CompilerParams(
            dimension_semantics=("arbitrary", "arbitrary"), vmem_limit_bytes=VMEM_BIG),
        name="moe_experts",
    )(tile_e, tile_nv, tile_blk, hs, wg, wu, wd)


def _combine_kernel(pos_ref, x_ref, ada_ref, rw_ref, ys_ref, o_ref, buf, sem):
    base = pl.program_id(0) * TM

    def body(t, carry):
        tok = base + t
        for s in range(2):
            p = pos_ref[2 * tok + s]
            pltpu.make_async_copy(ys_ref.at[pl.ds(p, 1), :], buf.at[s, pl.ds(t, 1), :], sem).start()
        return carry

    lax.fori_loop(0, TM, body, 0)
    for s in range(2):
        pltpu.make_async_copy(ys_ref.at[pl.ds(0, TM), :], buf.at[s], sem).wait()
    w1 = rw_ref[:, 0:1]
    w2 = rw_ref[:, 1:2]
    y = w1 * buf[0] + w2 * buf[1]
    o_ref[...] = x_ref[...] + ada_ref[5:6, :] * y


def _combine_call(pos, x1, ada, layer, rw, ys):
    ada_map = lambda i, p: (layer, _tile_cond(i, N_CTX_TILES, LAT_TILES_PER_SEQ), 0, 0)
    return pl.pallas_call(
        _combine_kernel,
        out_shape=jax.ShapeDtypeStruct((T_ALL, D), F32),
        grid_spec=pltpu.PrefetchScalarGridSpec(
            num_scalar_prefetch=1, grid=(N_TILES,),
            in_specs=[pl.BlockSpec((TM, D), lambda i, p: (i, 0)),
                      pl.BlockSpec((None, None, N_ADA, D), ada_map),
                      pl.BlockSpec((TM, LANE), lambda i, p: (i, 0)),
                      pl.BlockSpec(memory_space=pl.ANY)],
            out_specs=pl.BlockSpec((TM, D), lambda i, p: (i, 0)),
            scratch_shapes=[pltpu.VMEM((2, TM, D), F32), pltpu.SemaphoreType.DMA(())]),
        compiler_params=pltpu.CompilerParams(dimension_semantics=("arbitrary",)),
        name="moe_combine",
    )(pos, x1, ada, rw, ys)


def _moe_layer(x1, h2, ri, rw, cnt, ada, layer, wg, wu, wd, lidx):
    idx = ri[:, 0:2]
    rank = ri[:, 2:4]
    counts = cnt[0, :N_EXP].astype(jnp.int32)
    padded = ((counts + MOE_TR - 1) // MOE_TR) * MOE_TR
    ends = jnp.cumsum(padded)
    starts = ends - padded
    sel = idx[:, :, None] == jnp.arange(N_EXP, dtype=jnp.int32)[None, None, :]
    pos = (jnp.sum(jnp.where(sel, starts[None, None, :], 0), axis=-1) + rank).reshape(-1).astype(jnp.int32)
    tile_start = jnp.arange(MOE_NT, dtype=jnp.int32) * MOE_TR
    te = jnp.sum((tile_start[:, None] >= ends[None, :]).astype(jnp.int32), axis=1)
    used = te < N_EXP
    te_c = jnp.minimum(te, N_EXP - 1)
    nv = jnp.where(used, jnp.clip(starts[te_c] + counts[te_c] - tile_start, 0, MOE_TR), 0).astype(jnp.int32)
    n_used = jnp.sum(used.astype(jnp.int32))
    last = jnp.maximum(n_used - 1, 0)
    te_eff = jnp.where(used, te_c, te_c[last]).astype(jnp.int32)
    tb = jnp.minimum(jnp.arange(MOE_NT, dtype=jnp.int32), last)
    hs = _dispatch_call(pos, h2, jnp.zeros((MOE_ROWS, D), F32))
    ys = _moe_call(te_eff, nv, tb, hs, wg, wu, wd, lidx)
    return _combine_call(pos, x1, ada, layer, rw, ys)


def _rope_tables():
    t = jnp.arange(DEC_SEQ)
    rows = (t // GRID_W).astype(F32)
    cols = (t % GRID_W).astype(F32)

    def axis_tabs(pos, d):
        inv = ROPE_BASE ** (-jnp.arange(0, d, 2, dtype=F32) / d)
        ang = pos[:, None] * inv[None, :]
        cos, sin = jnp.cos(ang), jnp.sin(ang)
        z = jnp.zeros_like(sin)
        return (jnp.concatenate([cos, cos], -1), jnp.concatenate([-sin, z], -1), jnp.concatenate([z, sin], -1))

    def axial(d):
        r = axis_tabs(rows, d // 2)
        c = axis_tabs(cols, d // 2)
        return [jnp.concatenate([a, b], -1) for a, b in zip(r, c)]

    tabs_a = [jnp.tile(tb, (1, A_HEADS)) for tb in axial(A_DH)]
    cb, sub, sdb = axial(B_ROPE)
    ones = jnp.ones((DEC_SEQ, B_NOPE), F32)
    zer = jnp.zeros((DEC_SEQ, B_NOPE), F32)
    pad = jnp.zeros((DEC_SEQ, SLOT - B_QK), F32)
    slot = lambda first, tb: jnp.tile(jnp.concatenate([first, tb, pad], -1), (1, B_HEADS))
    tabs_b = [slot(ones, cb), slot(zer, sub), slot(zer, sdb)]
    return tabs_a + tabs_b


def _attn_weights(e, w_in_attn, g_qk_a_q, g_qk_a_k, g_mla_q_a, w_mla_q_b, g_mla_kv_a, w_mla_kv_b,
                  g_qk_b_q, g_qk_b_k):
    w_in = jnp.pad(w_in_attn[e], ((0, 0), (0, IN_PAD - IN_DIM))).astype(BF16)
    w_qb = jnp.pad(w_mla_q_b[e].reshape(B_Q_LORA, B_HEADS, B_QK), ((0, 0), (0, 0), (0, SLOT - B_QK)))
    w_qb = w_qb.reshape(B_Q_LORA, B_HEADS * SLOT).astype(BF16)
    wkv = w_mla_kv_b[e].reshape(B_KV_LORA, B_HEADS, B_NOPE + B_V)
    wk = jnp.pad(wkv[:, :, :B_NOPE], ((0, 0), (0, 0), (0, SLOT - B_NOPE))).reshape(B_KV_LORA, B_HEADS * SLOT)
    wv = wkv[:, :, B_NOPE:].reshape(B_KV_LORA, B_HEADS * B_V)
    w_kv = jnp.concatenate([wk, wv], axis=1).astype(BF16)
    place = jnp.zeros((LANE, B_HEADS, SLOT), F32)
    ii = jnp.arange(B_ROPE)
    place = place.at[ii, :, B_NOPE + ii].set(1.0).reshape(LANE, B_HEADS * SLOT).astype(BF16)
    slot_gain = lambda g: jnp.tile(jnp.pad(g, (0, SLOT - B_QK)), B_HEADS).reshape(1, B_HEADS * SLOT)
    return dict(
        w_in=w_in, w_qb=w_qb, w_kv=w_kv, place=place,
        gqa=jnp.tile(g_qk_a_q[e], A_HEADS).reshape(1, 512),
        gka=jnp.tile(g_qk_a_k[e], A_KV).reshape(1, LANE),
        gql=g_mla_q_a[e].reshape(1, B_Q_LORA), gkl=g_mla_kv_a[e].reshape(1, B_KV_LORA),
        gqb=slot_gain(g_qk_b_q[e]), gkb=slot_gain(g_qk_b_k[e]))


def _pair_variants_host(x):
    h0, h1 = x[..., :64], x[..., 64:]
    z = jnp.zeros_like(h0)
    return jnp.concatenate([h0, z, z, h0, h1, z, z, h1], axis=-1)


def kernel(x_prompt, x_sample, c, cache_k_win, cache_v_win, cache_ckv, cache_krope, c_ctx, w_ada, b_ada, g_norm_mix, g_norm_ffn, w_in_attn, g_qk_a_q, g_qk_a_k, sink_a, g_mla_q_a, w_mla_q_b, g_mla_kv_a, w_mla_kv_b, g_qk_b_q, g_qk_b_k, w_out_attn, w_ffn_gate, w_ffn_up, w_ffn_down, w_conv_pw1, b_conv_pw1, w_conv_dw, b_conv_dw, g_conv_ln, b_conv_ln, w_conv_pw2, b_conv_pw2, w_router, w_moe_gate, w_moe_up, w_moe_down):
    x = jnp.concatenate([x_prompt.reshape(T_CTX, D), x_sample.reshape(T_LAT, D)], axis=0)
    cond8 = jnp.concatenate([c_ctx[None, :], c, jnp.zeros((8 - 1 - DEC_BATCH, D), F32)], axis=0)
    ada = _ada_call(cond8, w_ada, b_ada).reshape(DEPTH, 8, N_ADA, D)
    tabs = _rope_tables()

    wo_b = w_out_attn.astype(BF16)
    wg_b = w_ffn_gate.astype(BF16)
    wu_b = w_ffn_up.astype(BF16)
    wd_b = w_ffn_down.astype(BF16)
    w1_b = w_conv_pw1.astype(BF16)
    w2_b = w_conv_pw2.astype(BF16)

    new_k, new_v, new_ckv, new_kr = [], [], [], []
    for layer in range(DEPTH):
        gm = g_norm_mix[layer].reshape(1, D)
        gf = g_norm_ffn[layer].reshape(1, D)
        if layer % 2 == 0:
            e = layer // 2
            lw = _attn_weights(e, w_in_attn, g_qk_a_q, g_qk_a_k, g_mla_q_a, w_mla_q_b, g_mla_kv_a,
                               w_mla_kv_b, g_qk_b_q, g_qk_b_k)
            sink = sink_a[e].reshape(A_HEADS)
            (qa, ka4, va4, qb, kb, vlo, vhi, kaf, vaf, ckvf, krf) = _pre_attn_call(False, x, ada, layer, gm, lw, None)
            o_ctx = _attn_ctx_call(sink, qa, ka4, va4, qb, kb, vlo, vhi)
            new_k.append(kaf.reshape(BATCH, SEQ, A_KV, A_DH))
            new_v.append(vaf.reshape(BATCH, SEQ, A_KV, A_DH))
            new_ckv.append(ckvf.reshape(BATCH, SEQ, B_KV_LORA))
            new_kr.append(krf[:, :B_ROPE].reshape(BATCH, SEQ, B_ROPE))
            (qa, ka4, va4, qb, kb, vlo, vhi) = _pre_attn_call(True, x, ada, layer, gm, lw, tabs)
            kr_c = jnp.pad(cache_krope[:, e].reshape(DEC_BATCH * PAST, B_ROPE), ((0, 0), (0, LANE - B_ROPE)))
            kb_c, vlo_c, vhi_c = _cache_kv_call(cache_ckv[:, e].reshape(DEC_BATCH * PAST, B_KV_LORA), kr_c, lw)
            pad_rows = lambda a: jnp.pad(a.reshape(DEC_BATCH, DEC_SEQ, 512), ((0, 0), (QB, QB), (0, 0)))
            ka4c = _pair_variants_host(cache_k_win[:, e].reshape(DEC_BATCH, PAST, LANE)).astype(BF16)
            va4c = _pair_variants_host(cache_v_win[:, e].reshape(DEC_BATCH, PAST, LANE)).astype(BF16)
            cat = lambda a, b, w: jnp.concatenate(
                [a.reshape(DEC_BATCH, PAST, w), b.reshape(DEC_BATCH, DEC_SEQ, w)], axis=1)
            o_lat = _attn_lat_call(sink, qa, pad_rows(ka4), pad_rows(va4), ka4c, va4c, qb,
                                   cat(kb_c, kb, 1024), cat(vlo_c, vlo, 512), cat(vhi_c, vhi, 512))
            o = jnp.concatenate([o_ctx, o_lat], axis=0)
            x = _post_attn_call(x, o, ada, layer, gf, wo_b[e], wg_b[e], wu_b[e], wd_b[e])
        else:
            o_ = layer // 2
            cw = dict(w1=w1_b[o_], b1=b_conv_pw1[o_].reshape(1, 2 * D), wdw=w_conv_dw[o_],
                      bdw=b_conv_dw[o_].reshape(1, D), gln=g_conv_ln[o_].reshape(1, D),
                      bln=b_conv_ln[o_].reshape(1, D), w2=w2_b[o_], b2=b_conv_pw2[o_].reshape(1, D),
                      wr=jnp.pad(w_router[o_], ((0, 0), (0, LANE - N_EXP))))
            x1, h2, ri, rw, cnt = _conv_router_call(x, ada, layer, gm, gf, cw)
            x = _moe_layer(x1, h2, ri, rw, cnt, ada, layer, w_moe_gate, w_moe_up, w_moe_down, o_)

    y_prompt = x[:T_CTX].reshape(BATCH, SEQ, D)
    y_sample = x[T_CTX:].reshape(DEC_BATCH, DEC_SEQ, D)
    return (y_prompt, y_sample, jnp.stack(new_k, axis=1), jnp.stack(new_v, axis=1),
            jnp.stack(new_ckv, axis=1), jnp.stack(new_kr, axis=1))
```

```python
import functools

import jax
import jax.numpy as jnp
from jax import lax
from jax.experimental import pallas as pl
from jax.experimental.pallas import tpu as pltpu

F32 = jnp.float32
BF16 = jnp.bfloat16

D = 1024
BATCH = 32
SEQ = 256
DEPTH = 4
DEC_BATCH = 2
DEC_SEQ = 1024
PAST = 256
GRID_W = 64
N_EVEN = 2
A_HEADS = 8
A_KV = 2
A_DH = 64
WINDOW = 128
B_HEADS = 8
B_Q_LORA = 256
B_KV_LORA = 128
B_NOPE = 64
B_ROPE = 32
B_QK = B_NOPE + B_ROPE
B_V = 64
IN_DIM = 1184
IN_PAD = 1280
CONV_W = 31
D_FF = 2816
N_EXP = 8
D_FFE = 3584
ROPE_BASE = 10000.0
EPS = 1e-6
N_ADA = 6

T_CTX = BATCH * SEQ
T_LAT = DEC_BATCH * DEC_SEQ
T_ALL = T_CTX + T_LAT
TM = 256
N_TILES = T_ALL // TM
N_CTX_TILES = T_CTX // TM
LAT_TILES_PER_SEQ = DEC_SEQ // TM
LANE = 128
SLOT = 128

MOE_SUB = 256
MOE_R = 3072
MOE_TF = 512
MOE_NF = D_FFE // MOE_TF
MOE_ROWS = 2 * T_ALL + N_EXP * MOE_SUB
MOE_NS = MOE_ROWS // MOE_R + N_EXP

FFN_TM = 512
FFN_CH = 1408

VMEM_BIG = 56 * 1024 * 1024


def _dot(a, b):
    return jnp.dot(a, b, preferred_element_type=F32)


def _dot_nt(a, b):
    return lax.dot_general(a, b, (((1,), (1,)), ((), ())), preferred_element_type=F32)


def _split(a):
    hi = a.astype(BF16)
    lo = (a - hi.astype(F32)).astype(BF16)
    return hi, lo


def _dot_x3(a, b):
    ah, al = _split(a)
    bh, bl = _split(b)
    return _dot(ah, bh) + (_dot(ah, bl) + _dot(al, bh))


def _sigmoid(x):
    return 1.0 / (1.0 + jnp.exp(-x))


def _rms(x, n):
    return x * lax.rsqrt(jnp.sum(x * x, axis=-1, keepdims=True) * (1.0 / n) + EPS)


def _modulate(x, g, shift, scale):
    return _rms(x, D) * g * (1.0 + scale) + shift


def _lo_mask(rows):
    return lax.broadcasted_iota(jnp.int32, (rows, LANE), 1) < 64


def _ada_kernel(cond_ref, w_ref, b_ref, o_ref):
    c = cond_ref[...]
    s = c * _sigmoid(c)
    o_ref[...] = _dot_x3(s, w_ref[...]) + b_ref[...]


def _ada_call(cond8, w_ada, b_ada):
    tn = 1024
    return pl.pallas_call(
        _ada_kernel,
        out_shape=jax.ShapeDtypeStruct((DEPTH, 8, N_ADA * D), F32),
        grid=(DEPTH, N_ADA * D // tn),
        in_specs=[
            pl.BlockSpec((8, D), lambda l, j: (0, 0)),
            pl.BlockSpec((None, D, tn), lambda l, j: (l, 0, j)),
            pl.BlockSpec((None, 1, tn), lambda l, j: (l, 0, j)),
        ],
        out_specs=pl.BlockSpec((None, 8, tn), lambda l, j: (l, 0, j)),
        compiler_params=pltpu.CompilerParams(
            dimension_semantics=("arbitrary", "arbitrary"), vmem_limit_bytes=VMEM_BIG),
        name="ada",
    )(cond8, w_ada, b_ada.reshape(DEPTH, 1, N_ADA * D))


def _rope(x, cos, sup, sdn, shift):
    up = pltpu.roll(x, LANE - shift, 1)
    dn = pltpu.roll(x, shift, 1)
    return x * cos + up * sup + dn * sdn


def _norm64_block(blk, lo):
    sq = blk * blk
    s_lo = jnp.sum(jnp.where(lo, sq, 0.0), axis=-1, keepdims=True)
    s_hi = jnp.sum(jnp.where(lo, 0.0, sq), axis=-1, keepdims=True)
    r = jnp.where(lo, lax.rsqrt(s_lo * (1.0 / A_DH) + EPS), lax.rsqrt(s_hi * (1.0 / A_DH) + EPS))
    return blk * r


def _pair_variants(x, lo):
    r = pltpu.roll(x, 64, 1)
    return (jnp.where(lo, x, 0.0), jnp.where(lo, 0.0, r), jnp.where(lo, r, 0.0), jnp.where(lo, 0.0, x))


def _mla_kv(ckv_n, kr, wkv_ref, place_ref, gk_ref, lo, kb_ref, vlo_ref, vhi_ref, rope_tabs):
    kv = _dot(ckv_n.astype(BF16), wkv_ref[...])
    kr_hi, kr_lo = _split(kr)
    place = place_ref[...]
    krs = _dot(kr_hi, place) + _dot(kr_lo, place)
    for h in range(B_HEADS):
        sl = slice(h * SLOT, (h + 1) * SLOT)
        blk = kv[:, sl] + krs[:, sl]
        blk = _rms(blk, B_QK) * gk_ref[:, sl]
        if rope_tabs is not None:
            cos, sup, sdn = rope_tabs
            blk = _rope(blk, cos[:, sl], sup[:, sl], sdn[:, sl], 8)
        kb_ref[:, sl] = blk.astype(BF16)
    for j in range(B_HEADS // 2):
        sl = slice(j * LANE, (j + 1) * LANE)
        v = kv[:, B_HEADS * SLOT + j * LANE: B_HEADS * SLOT + (j + 1) * LANE]
        vlo_ref[:, sl] = jnp.where(lo, v, 0.0).astype(BF16)
        vhi_ref[:, sl] = jnp.where(lo, 0.0, v).astype(BF16)


def _pre_attn_kernel(rope, *refs):
    if rope:
        (x_ref, ada_ref, g_ref, win_ref, wqb_ref, wkv_ref, place_ref,
         gqa_ref, gka_ref, gql_ref, gkl_ref, gqb_ref, gkb_ref,
         ca_ref, sua_ref, sda_ref, cb_ref, sub_ref, sdb_ref,
         qa_o, ka4_o, va4_o, qb_o, kb_o, vlo_o, vhi_o) = refs
    else:
        (x_ref, ada_ref, g_ref, win_ref, wqb_ref, wkv_ref, place_ref,
         gqa_ref, gka_ref, gql_ref, gkl_ref, gqb_ref, gkb_ref,
         qa_o, ka4_o, va4_o, qb_o, kb_o, vlo_o, vhi_o,
         kaf_o, vaf_o, ckvf_o, krf_o) = refs
    lo = _lo_mask(TM)
    h = _modulate(x_ref[...], g_ref[...], ada_ref[0:1, :], ada_ref[1:2, :])
    p = _dot(h.astype(BF16), win_ref[...])

    for j in range(4):
        sl = slice(j * LANE, (j + 1) * LANE)
        blk = _norm64_block(p[:, sl], lo) * gqa_ref[:, sl]
        if rope:
            blk = _rope(blk, ca_ref[:, sl], sua_ref[:, sl], sda_ref[:, sl], 16)
        qa_o[:, sl] = (blk * (A_DH ** -0.5)).astype(BF16)
    ka = _norm64_block(p[:, 512:640], lo) * gka_ref[...]
    va = p[:, 640:768]
    if not rope:
        kaf_o[...] = ka
        vaf_o[...] = va
    else:
        ka = _rope(ka, ca_ref[:, 0:LANE], sua_ref[:, 0:LANE], sda_ref[:, 0:LANE], 16)
    for n, (kk, vv) in enumerate(zip(_pair_variants(ka, lo), _pair_variants(va, lo))):
        sl = slice(n * LANE, (n + 1) * LANE)
        ka4_o[:, sl] = kk.astype(BF16)
        va4_o[:, sl] = vv.astype(BF16)

    cq = _rms(p[:, 768:1024], B_Q_LORA) * gql_ref[...]
    qb = _dot(cq.astype(BF16), wqb_ref[...])
    for hh in range(B_HEADS):
        sl = slice(hh * SLOT, (hh + 1) * SLOT)
        blk = _rms(qb[:, sl], B_QK) * gqb_ref[:, sl]
        if rope:
            blk = _rope(blk, cb_ref[:, sl], sub_ref[:, sl], sdb_ref[:, sl], 8)
        qb_o[:, sl] = (blk * (B_QK ** -0.5)).astype(BF16)

    ckv = _rms(p[:, 1024:1152], B_KV_LORA) * gkl_ref[...]
    kr = p[:, 1152:1280]
    if not rope:
        ckvf_o[...] = ckv
        krf_o[...] = kr
    tabs = (cb_ref, sub_ref, sdb_ref) if rope else None
    _mla_kv(ckv, kr, wkv_ref, place_ref, gkb_ref, lo, kb_o, vlo_o, vhi_o, tabs)


def _const_spec(shape):
    nd = len(shape)
    return pl.BlockSpec(shape, lambda *a: (0,) * nd)


def _pre_attn_call(rope, x, ada, layer, g, lw, tabs):
    n_tiles = (T_LAT if rope else T_CTX) // TM
    off = N_CTX_TILES if rope else 0
    n_rows = n_tiles * TM
    if rope:
        ada_map = lambda i: (layer, 1 + i // LAT_TILES_PER_SEQ, 0, 0)
    else:
        ada_map = lambda i: (layer, 0, 0, 0)
    in_specs = [
        pl.BlockSpec((TM, D), lambda i: (i + off, 0)),
        pl.BlockSpec((None, None, N_ADA, D), ada_map),
        _const_spec((1, D)),
        _const_spec((D, IN_PAD)),
        _const_spec((B_Q_LORA, B_HEADS * SLOT)),
        _const_spec((B_KV_LORA, B_HEADS * SLOT + B_HEADS * B_V)),
        _const_spec((LANE, B_HEADS * SLOT)),
        _const_spec((1, 512)), _const_spec((1, LANE)), _const_spec((1, B_Q_LORA)),
        _const_spec((1, B_KV_LORA)), _const_spec((1, B_HEADS * SLOT)), _const_spec((1, B_HEADS * SLOT)),
    ]
    args = [x, ada, g, lw["w_in"], lw["w_qb"], lw["w_kv"], lw["place"],
            lw["gqa"], lw["gka"], lw["gql"], lw["gkl"], lw["gqb"], lw["gkb"]]
    if rope:
        tmap = lambda i: (i % LAT_TILES_PER_SEQ, 0)
        in_specs += [pl.BlockSpec((TM, 512), tmap)] * 3 + [pl.BlockSpec((TM, B_HEADS * SLOT), tmap)] * 3
        args += list(tabs)
    row = lambda w: pl.BlockSpec((TM, w), lambda i: (i, 0))
    out_shape = [jax.ShapeDtypeStruct((n_rows, w), BF16) for w in (512, 512, 512, 1024, 1024, 512, 512)]
    out_specs = [row(w) for w in (512, 512, 512, 1024, 1024, 512, 512)]
    if not rope:
        out_shape += [jax.ShapeDtypeStruct((n_rows, LANE), F32)] * 4
        out_specs += [row(LANE)] * 4
    return pl.pallas_call(
        functools.partial(_pre_attn_kernel, rope),
        out_shape=out_shape, grid=(n_tiles,), in_specs=in_specs, out_specs=out_specs,
        compiler_params=pltpu.CompilerParams(
            dimension_semantics=("arbitrary",), vmem_limit_bytes=VMEM_BIG),
        name="pre_attn_lat" if rope else "pre_attn_ctx",
    )(*args)


def _cache_kv_kernel(ckv_ref, kr_ref, wkv_ref, place_ref, gkb_ref, kb_o, vlo_o, vhi_o):
    lo = _lo_mask(PAST)
    _mla_kv(ckv_ref[...], kr_ref[...], wkv_ref, place_ref, gkb_ref, lo, kb_o, vlo_o, vhi_o, None)


def _cache_kv_call(ckv, kr, lw):
    rows = DEC_BATCH * PAST
    row = lambda w: pl.BlockSpec((PAST, w), lambda i: (i, 0))
    return pl.pallas_call(
        _cache_kv_kernel,
        out_shape=[jax.ShapeDtypeStruct((rows, 1024), BF16), jax.ShapeDtypeStruct((rows, 512), BF16),
                   jax.ShapeDtypeStruct((rows, 512), BF16)],
        grid=(DEC_BATCH,),
        in_specs=[row(LANE), row(LANE),
                  _const_spec((B_KV_LORA, B_HEADS * SLOT + B_HEADS * B_V)),
                  _const_spec((LANE, B_HEADS * SLOT)), _const_spec((1, B_HEADS * SLOT))],
        out_specs=[row(1024), row(512), row(512)],
        compiler_params=pltpu.CompilerParams(dimension_semantics=("arbitrary",)),
        name="cache_kv",
    )(ckv, kr, lw["w_kv"], lw["place"], lw["gkb"])


def _softmax_pv(scores, values, sink):
    m = jnp.max(scores[0], axis=-1, keepdims=True)
    for s in scores[1:]:
        m = jnp.maximum(m, jnp.max(s, axis=-1, keepdims=True))
    if sink is not None:
        m = jnp.maximum(m, sink)
    ps = [jnp.exp(s - m) for s in scores]
    den = jnp.sum(ps[0], axis=-1, keepdims=True)
    for p in ps[1:]:
        den = den + jnp.sum(p, axis=-1, keepdims=True)
    if sink is not None:
        den = den + jnp.exp(sink - m)
    inv = 1.0 / den
    out = _dot((ps[0] * inv).astype(BF16), values[0])
    for p, v in zip(ps[1:], values[1:]):
        out = out + _dot((p * inv).astype(BF16), v)
    return out


def _attn_ctx_kernel(sink_ref, qa_ref, ka4_ref, va4_ref, qb_ref, kb_ref, vlo_ref, vhi_ref, o_ref):
    for j in range(4):
        q = qa_ref[:, j * LANE:(j + 1) * LANE]
        acc = None
        for par in range(2):
            c = (2 * (j // 2) + par) * LANE
            s = _dot_nt(q, ka4_ref[:, c:c + LANE])
            o = _softmax_pv([s], [va4_ref[:, c:c + LANE]], sink_ref[2 * j + par])
            acc = o if acc is None else acc + o
        o_ref[:, j * LANE:(j + 1) * LANE] = acc.astype(BF16)
    for j in range(4):
        acc = None
        for par, v_ref in enumerate((vlo_ref, vhi_ref)):
            hh = 2 * j + par
            s = _dot_nt(qb_ref[:, hh * SLOT:(hh + 1) * SLOT], kb_ref[:, hh * SLOT:(hh + 1) * SLOT])
            o = _softmax_pv([s], [v_ref[:, j * LANE:(j + 1) * LANE]], None)
            acc = o if acc is None else acc + o
        o_ref[:, 512 + j * LANE:512 + (j + 1) * LANE] = acc.astype(BF16)


def _attn_ctx_call(sink, qa, ka4, va4, qb, kb, vlo, vhi):
    row = lambda w: pl.BlockSpec((SEQ, w), lambda b: (b, 0))
    return pl.pallas_call(
        _attn_ctx_kernel,
        out_shape=jax.ShapeDtypeStruct((T_CTX, D), BF16),
        grid=(BATCH,),
        in_specs=[pl.BlockSpec(memory_space=pltpu.SMEM),
                  row(512), row(512), row(512), row(1024), row(1024), row(512), row(512)],
        out_specs=row(D),
        compiler_params=pltpu.CompilerParams(dimension_semantics=("arbitrary",)),
        name="attn_ctx",
    )(sink, qa, ka4, va4, qb, kb, vlo, vhi)


QB = 128
N_QB = DEC_SEQ // QB
KPAD = DEC_SEQ + 2 * QB
KALL = PAST + DEC_SEQ


def _attn_lat_kernel(sink_ref, qa_ref, ka4p_ref, va4p_ref, ka4c_ref, va4c_ref,
                     qb_ref, kb_ref, vlo_ref, vhi_ref, o_ref):
    n = pl.program_id(1)
    r = lax.broadcasted_iota(jnp.int32, (QB, 3 * QB), 0)
    c = lax.broadcasted_iota(jnp.int32, (QB, 3 * QB), 1)
    rel = c - QB - r
    kpos = (n - 1) * QB + c
    valid = (jnp.abs(rel) <= WINDOW) & (kpos >= 0) & (kpos < DEC_SEQ)
    k0 = pl.multiple_of(n * QB, QB)
    for j in range(4):
        q = qa_ref[:, j * LANE:(j + 1) * LANE]
        acc = None
        for par in range(2):
            cc = (2 * (j // 2) + par) * LANE
            s_loc = _dot_nt(q, ka4p_ref[pl.ds(k0, 3 * QB), cc:cc + LANE])
            s_loc = jnp.where(valid, s_loc, -jnp.inf)
            s_ctx = _dot_nt(q, ka4c_ref[:, cc:cc + LANE])
            o = _softmax_pv([s_ctx, s_loc],
                            [va4c_ref[:, cc:cc + LANE], va4p_ref[pl.ds(k0, 3 * QB), cc:cc + LANE]],
                            sink_ref[2 * j + par])
            acc = o if acc is None else acc + o
        o_ref[:, j * LANE:(j + 1) * LANE] = acc.astype(BF16)
    for j in range(4):
        acc = None
        for par, v_ref in enumerate((vlo_ref, vhi_ref)):
            hh = 2 * j + par
            s = _dot_nt(qb_ref[:, hh * SLOT:(hh + 1) * SLOT], kb_ref[:, hh * SLOT:(hh + 1) * SLOT])
            o = _softmax_pv([s], [v_ref[:, j * LANE:(j + 1) * LANE]], None)
            acc = o if acc is None else acc + o
        o_ref[:, 512 + j * LANE:512 + (j + 1) * LANE] = acc.astype(BF16)


def _attn_lat_call(sink, qa, ka4p, va4p, ka4c, va4c, qb, kball, vlo_all, vhi_all):
    qrow = lambda w: pl.BlockSpec((QB, w), lambda b, n: (b * N_QB + n, 0))
    per_b = lambda rows, w: pl.BlockSpec((None, rows, w), lambda b, n: (b, 0, 0))
    return pl.pallas_call(
        _attn_lat_kernel,
        out_shape=jax.ShapeDtypeStruct((T_LAT, D), BF16),
        grid=(DEC_BATCH, N_QB),
        in_specs=[pl.BlockSpec(memory_space=pltpu.SMEM),
                  qrow(512), per_b(KPAD, 512), per_b(KPAD, 512), per_b(PAST, 512), per_b(PAST, 512),
                  qrow(1024), per_b(KALL, 1024), per_b(KALL, 512), per_b(KALL, 512)],
        out_specs=qrow(D),
        compiler_params=pltpu.CompilerParams(dimension_semantics=("arbitrary", "arbitrary")),
        name="attn_lat",
    )(sink, qa, ka4p, va4p, ka4c, va4c, qb, kball, vlo_all, vhi_all)


def _post_attn_kernel(x_ref, o_ref, ada_ref, g_ref, wo_ref, wg_ref, wu_ref, wd_ref, y_ref):
    mix = _dot(o_ref[...], wo_ref[...])
    x1 = x_ref[...] + ada_ref[2:3, :] * mix
    h = _modulate(x1, g_ref[...], ada_ref[3:4, :], ada_ref[4:5, :]).astype(BF16)
    acc = None
    for c0 in range(0, D_FF, FFN_CH):
        gg = _dot(h, wg_ref[:, c0:c0 + FFN_CH])
        uu = _dot(h, wu_ref[:, c0:c0 + FFN_CH])
        a = (gg * _sigmoid(gg) * uu).astype(BF16)
        part = _dot(a, wd_ref[c0:c0 + FFN_CH, :])
        acc = part if acc is None else acc + part
    y_ref[...] = x1 + ada_ref[5:6, :] * acc


def _tile_cond(i, tiles_per_ctx, tiles_per_lat_seq):
    return jnp.where(i < tiles_per_ctx, 0, 1 + (i - tiles_per_ctx) // tiles_per_lat_seq)


def _post_attn_call(x, o, ada, layer, g, wo, wg, wu, wd):
    n_tiles = T_ALL // FFN_TM
    ada_map = lambda i: (layer, _tile_cond(i, T_CTX // FFN_TM, DEC_SEQ // FFN_TM), 0, 0)
    one = pl.Buffered(1)
    return pl.pallas_call(
        _post_attn_kernel,
        out_shape=jax.ShapeDtypeStruct((T_ALL, D), F32),
        grid=(n_tiles,),
        in_specs=[pl.BlockSpec((FFN_TM, D), lambda i: (i, 0)),
                  pl.BlockSpec((FFN_TM, D), lambda i: (i, 0)),
                  pl.BlockSpec((None, None, N_ADA, D), ada_map),
                  _const_spec((1, D)),
                  pl.BlockSpec((D, D), lambda i: (0, 0), pipeline_mode=one),
                  pl.BlockSpec((D, D_FF), lambda i: (0, 0), pipeline_mode=one),
                  pl.BlockSpec((D, D_FF), lambda i: (0, 0), pipeline_mode=one),
                  pl.BlockSpec((D_FF, D), lambda i: (0, 0), pipeline_mode=one)],
        out_specs=pl.BlockSpec((FFN_TM, D), lambda i: (i, 0)),
        compiler_params=pltpu.CompilerParams(
            dimension_semantics=("arbitrary",), vmem_limit_bytes=VMEM_BIG),
        name="post_attn_ffn",
    )(x, o, ada, g, wo, wg, wu, wd)


HALO = 16
EXT = TM + 2 * HALO
CONV_RC = 64
CONV_LC = 128
CONV_WIN = CONV_RC + 24
CONV_LN_ROWS = 32


def _conv_router_kernel(x_ref, xp_ref, xn_ref, ada_ref, gm_ref, gf_ref, w1_ref, b1_ref, wdw_ref, bdw_ref,
                        gln_ref, bln_ref, w2_ref, b2_ref, wr_ref,
                        x1_o, h2_o, ri_o, rw_o, cnt_o, p_ref, c_ref, v_ref, run_ref):
    i = pl.program_id(0)

    @pl.when(i == 0)
    def _():
        run_ref[...] = jnp.zeros_like(run_ref)

    is_lat = i >= N_CTX_TILES
    j = (i - N_CTX_TILES) % LAT_TILES_PER_SEQ
    lflag = jnp.where(is_lat & (j != 0), 1.0, 0.0)
    rflag = jnp.where(is_lat & (j != LAT_TILES_PER_SEQ - 1), 1.0, 0.0)

    x = x_ref[...]
    xx = jnp.concatenate([xp_ref[...], x, xn_ref[...]], axis=0)
    h = _modulate(xx, gm_ref[...], ada_ref[0:1, :], ada_ref[1:2, :])
    a = _dot(h.astype(BF16), w1_ref[...]) + b1_ref[...]
    u = a[:, :D] * _sigmoid(a[:, D:])
    row = lax.broadcasted_iota(jnp.int32, (EXT, 1), 0)
    u = u * jnp.where(row < HALO, lflag, jnp.where(row >= HALO + TM, rflag, 1.0))

    p_ref[...] = u
    for rc in range(TM // CONV_RC):
        for lc in range(D // CONV_LC):
            ls = slice(lc * CONV_LC, (lc + 1) * CONV_LC)
            acc = jnp.zeros((CONV_RC, CONV_LC), F32) + bdw_ref[:, ls]
            base = p_ref[rc * CONV_RC:rc * CONV_RC + CONV_WIN + 8, ls]
            for r in range(8):
                win = base if r == 0 else pltpu.roll(base, CONV_WIN + 8 - r, 0)
                for q in range(4):
                    o = 8 * q + r
                    if 1 <= o <= CONV_W:
                        acc = acc + wdw_ref[o - 1:o, ls] * win[8 * q:8 * q + CONV_RC]
            c_ref[rc * CONV_RC:(rc + 1) * CONV_RC, ls] = acc

    for c0 in range(0, TM, CONV_LN_ROWS):
        acc = c_ref[c0:c0 + CONV_LN_ROWS, :]
        mu = jnp.mean(acc, axis=-1, keepdims=True)
        xc = acc - mu
        y = xc * lax.rsqrt(jnp.mean(xc * xc, axis=-1, keepdims=True) + EPS)
        y = y * gln_ref[...] + bln_ref[...]
        v_ref[c0:c0 + CONV_LN_ROWS, :] = (y * _sigmoid(y)).astype(BF16)

    mix = _dot(v_ref[...], w2_ref[...]) + b2_ref[...]
    x1 = x + ada_ref[2:3, :] * mix
    x1_o[...] = x1
    h2 = _modulate(x1, gf_ref[...], ada_ref[3:4, :], ada_ref[4:5, :])
    h2_o[...] = h2

    lane = lax.broadcasted_iota(jnp.int32, (TM, LANE), 1)
    lane_f = lane.astype(F32)
    logits = jnp.where(lane < N_EXP, _dot_x3(h2, wr_ref[...]), -jnp.inf)
    m1 = jnp.max(logits, axis=-1, keepdims=True)
    i1 = jnp.min(jnp.where(logits == m1, lane_f, float(LANE)), axis=-1, keepdims=True)
    rest = jnp.where(lane_f == i1, -jnp.inf, logits)
    m2 = jnp.max(rest, axis=-1, keepdims=True)
    i2 = jnp.min(jnp.where(rest == m2, lane_f, float(LANE)), axis=-1, keepdims=True)
    e = jnp.exp(m2 - m1)
    w1 = 1.0 / (1.0 + e)
    w2 = e / (1.0 + e)

    oh1 = jnp.where(lane_f == i1, 1.0, 0.0)
    oh2 = jnp.where(lane_f == i2, 1.0, 0.0)
    tr = lax.broadcasted_iota(jnp.int32, (TM, TM), 0)
    tc = lax.broadcasted_iota(jnp.int32, (TM, TM), 1)
    tri = jnp.where(tr > tc, 1.0, 0.0).astype(BF16)
    run = run_ref[0:1, :]
    cnt1 = jnp.sum(oh1, axis=0, keepdims=True)
    cnt2 = jnp.sum(oh2, axis=0, keepdims=True)
    pre1 = _dot(tri, oh1.astype(BF16)) + run
    pre2 = _dot(tri, oh2.astype(BF16)) + (run + cnt1)
    rank1 = jnp.sum(oh1 * pre1, axis=-1, keepdims=True)
    rank2 = jnp.sum(oh2 * pre2, axis=-1, keepdims=True)
    new_run = run + cnt1 + cnt2
    run_ref[...] = jnp.broadcast_to(new_run, run_ref.shape)
    cnt_o[...] = jnp.broadcast_to(new_run, cnt_o.shape)

    ri = jnp.where(lane == 0, i1, jnp.where(lane == 1, i2, jnp.where(lane == 2, rank1, jnp.where(lane == 3, rank2, 0.0))))
    ri_o[...] = ri.astype(jnp.int32)
    rw_o[...] = jnp.where(lane == 0, w1, jnp.where(lane == 1, w2, 0.0))


def _conv_router_call(x, ada, layer, gm, gf, cw):
    ada_map = lambda i: (layer, _tile_cond(i, N_CTX_TILES, LAT_TILES_PER_SEQ), 0, 0)
    hb = TM // HALO
    row = lambda w: pl.BlockSpec((TM, w), lambda i: (i, 0))
    return pl.pallas_call(
        _conv_router_kernel,
        out_shape=[jax.ShapeDtypeStruct((T_ALL, D), F32), jax.ShapeDtypeStruct((T_ALL, D), F32),
                   jax.ShapeDtypeStruct((T_ALL, LANE), jnp.int32), jax.ShapeDtypeStruct((T_ALL, LANE), F32),
                   jax.ShapeDtypeStruct((8, LANE), F32)],
        grid=(N_TILES,),
        in_specs=[row(D),
                  pl.BlockSpec((HALO, D), lambda i: (jnp.maximum(i * hb - 1, 0), 0)),
                  pl.BlockSpec((HALO, D), lambda i: (jnp.minimum((i + 1) * hb, T_ALL // HALO - 1), 0)),
                  pl.BlockSpec((None, None, N_ADA, D), ada_map),
                  _const_spec((1, D)), _const_spec((1, D)),
                  _const_spec((D, 2 * D)), _const_spec((1, 2 * D)),
                  _const_spec((CONV_W, D)), _const_spec((1, D)),
                  _const_spec((1, D)), _const_spec((1, D)),
                  _const_spec((D, D)), _const_spec((1, D)),
                  _const_spec((D, LANE))],
        out_specs=[row(D), row(D), row(LANE), row(LANE), _const_spec((8, LANE))],
        scratch_shapes=[pltpu.VMEM((EXT, D), F32), pltpu.VMEM((TM, D), F32), pltpu.VMEM((TM, D), BF16),
                        pltpu.VMEM((8, LANE), F32)],
        compiler_params=pltpu.CompilerParams(
            dimension_semantics=("arbitrary",), vmem_limit_bytes=VMEM_BIG),
        name="conv_router",
    )(x, x, x, ada, gm, gf, cw["w1"], cw["b1"], cw["wdw"], cw["bdw"], cw["gln"], cw["bln"],
      cw["w2"], cw["b2"], cw["wr"])


def _dispatch_kernel(pos_ref, h_ref, hs_in_ref, hs_ref, sem):
    del hs_in_ref
    base = pl.program_id(0) * TM

    def body(t, carry):
        for s in range(2):
            p = pos_ref[2 * (base + t) + s]
            pltpu.make_async_copy(h_ref.at[pl.ds(t, 1), :], hs_ref.at[pl.ds(p, 1), :], sem).start()
        return carry

    lax.fori_loop(0, TM, body, 0)
    for s in range(2):
        pltpu.make_async_copy(h_ref, hs_ref.at[pl.ds(0, TM), :], sem).wait()


def _dispatch_call(pos, h, hs0):
    return pl.pallas_call(
        _dispatch_kernel,
        out_shape=jax.ShapeDtypeStruct((MOE_ROWS, D), F32),
        grid_spec=pltpu.PrefetchScalarGridSpec(
            num_scalar_prefetch=1, grid=(N_TILES,),
            in_specs=[pl.BlockSpec((TM, D), lambda i, p: (i, 0)), pl.BlockSpec(memory_space=pl.ANY)],
            out_specs=pl.BlockSpec(memory_space=pl.ANY),
            scratch_shapes=[pltpu.SemaphoreType.DMA(())]),
        input_output_aliases={2: 0},
        compiler_params=pltpu.CompilerParams(dimension_semantics=("arbitrary",), has_side_effects=True),
        name="moe_dispatch",
    )(pos, h, hs0)


def _moe_kernel(se_ref, row0_ref, nsub_ref, hs_in_ref, wg_ref, wu_ref, wd_ref, ys_ref,
                hb_ref, acc_ref, stage, wgb, wub, wdb, sem_in, sem_out):
    del se_ref
    j = pl.program_id(0)
    f = pl.program_id(1)
    nsub = nsub_ref[j]
    row0 = row0_ref[j]

    def load(s, slot):
        src = hs_in_ref.at[pl.ds(pl.multiple_of(row0 + s * MOE_SUB, MOE_SUB), MOE_SUB), :]
        return pltpu.make_async_copy(src, stage.at[slot], sem_in.at[slot])

    def store(s):
        r = pl.multiple_of(s * MOE_SUB, MOE_SUB)
        dst = ys_ref.at[pl.ds(pl.multiple_of(row0 + s * MOE_SUB, MOE_SUB), MOE_SUB), :]
        return pltpu.make_async_copy(acc_ref.at[pl.ds(r, MOE_SUB), :], dst, sem_out)

    def part(r):
        hh = hb_ref[pl.ds(r, MOE_SUB), :]
        gg = _dot(hh, wgb[...])
        uu = _dot(hh, wub[...])
        a = (gg * _sigmoid(gg) * uu).astype(BF16)
        return _dot(a, wdb[...])

    @pl.when(nsub > 0)
    def _():
        wgb[...] = wg_ref[...].astype(BF16)
        wub[...] = wu_ref[...].astype(BF16)
        wdb[...] = wd_ref[...].astype(BF16)

        @pl.when(f == 0)
        def _():
            load(0, 0).start()

            def body(s, carry):
                slot = s % 2
                load(s, slot).wait()

                @pl.when(s + 1 < nsub)
                def _():
                    load(s + 1, 1 - slot).start()

                r = pl.multiple_of(s * MOE_SUB, MOE_SUB)
                hb_ref[pl.ds(r, MOE_SUB), :] = stage[slot].astype(BF16)
                acc_ref[pl.ds(r, MOE_SUB), :] = part(r)
                return carry

            lax.fori_loop(0, nsub, body, 0)

        @pl.when((f > 0) & (f < MOE_NF - 1))
        def _():
            def body(s, carry):
                r = pl.multiple_of(s * MOE_SUB, MOE_SUB)
                acc_ref[pl.ds(r, MOE_SUB), :] += part(r)
                return carry

            lax.fori_loop(0, nsub, body, 0)

        @pl.when(f == MOE_NF - 1)
        def _():
            def body(s, carry):
                r = pl.multiple_of(s * MOE_SUB, MOE_SUB)
                acc_ref[pl.ds(r, MOE_SUB), :] += part(r)
                store(s).start()
                return carry

            lax.fori_loop(0, nsub, body, 0)

            def drain(s, carry):
                store(s).wait()
                return carry

            lax.fori_loop(0, nsub, drain, 0)


def _moe_call(st_e, st_row0, st_nsub, hs, wg, wu, wd, lidx):
    f_eff = lambda j, f, ns: jnp.where(ns[j] > 0, f, MOE_NF - 1)
    return pl.pallas_call(
        _moe_kernel,
        out_shape=jax.ShapeDtypeStruct((MOE_ROWS, D), F32),
        grid_spec=pltpu.PrefetchScalarGridSpec(
            num_scalar_prefetch=3, grid=(MOE_NS, MOE_NF),
            in_specs=[
                pl.BlockSpec(memory_space=pl.ANY),
                pl.BlockSpec((None, None, D, MOE_TF), lambda j, f, se, r0, ns: (lidx, se[j], 0, f_eff(j, f, ns))),
                pl.BlockSpec((None, None, D, MOE_TF), lambda j, f, se, r0, ns: (lidx, se[j], 0, f_eff(j, f, ns))),
                pl.BlockSpec((None, None, MOE_TF, D), lambda j, f, se, r0, ns: (lidx, se[j], f_eff(j, f, ns), 0)),
            ],
            out_specs=pl.BlockSpec(memory_space=pl.ANY),
            scratch_shapes=[pltpu.VMEM((MOE_R, D), BF16), pltpu.VMEM((MOE_R, D), F32),
                            pltpu.VMEM((2, MOE_SUB, D), F32),
                            pltpu.VMEM((D, MOE_TF), BF16), pltpu.VMEM((D, MOE_TF), BF16),
                            pltpu.VMEM((MOE_TF, D), BF16),
                            pltpu.SemaphoreType.DMA((2,)), pltpu.SemaphoreType.DMA(())]),
        input_output_aliases={3: 0},
        compiler_params=pltpu.CompilerParams(
            dimension_semantics=("arbitrary", "arbitrary"), vmem_limit_bytes=VMEM_BIG,
            has_side_effects=True),
        name="moe_experts",
    )(st_e, st_row0, st_nsub, hs, wg, wu, wd)


def _combine_kernel(split, pos_ref, x_ref, ada_ref, rw_ref, ys_ref, *rest):
    if split:
        oc_ref, ol_ref, buf, sem = rest
    else:
        o_ref, buf, sem = rest
    i = pl.program_id(0)
    base = i * TM

    def body(t, carry):
        tok = base + t
        for s in range(2):
            p = pos_ref[2 * tok + s]
            pltpu.make_async_copy(ys_ref.at[pl.ds(p, 1), :], buf.at[s, pl.ds(t, 1), :], sem).start()
        return carry

    lax.fori_loop(0, TM, body, 0)
    for s in range(2):
        pltpu.make_async_copy(ys_ref.at[pl.ds(0, TM), :], buf.at[s], sem).wait()
    w1 = rw_ref[:, 0:1]
    w2 = rw_ref[:, 1:2]
    y = w1 * buf[0] + w2 * buf[1]
    out = x_ref[...] + ada_ref[5:6, :] * y
    if split:
        @pl.when(i < N_CTX_TILES)
        def _():
            oc_ref[...] = out

        @pl.when(i >= N_CTX_TILES)
        def _():
            ol_ref[...] = out
    else:
        o_ref[...] = out


def _combine_call(split, pos, x1, ada, layer, rw, ys):
    ada_map = lambda i, p: (layer, _tile_cond(i, N_CTX_TILES, LAT_TILES_PER_SEQ), 0, 0)
    if split:
        out_shape = [jax.ShapeDtypeStruct((T_CTX, D), F32), jax.ShapeDtypeStruct((T_LAT, D), F32)]
        out_specs = [pl.BlockSpec((TM, D), lambda i, p: (jnp.minimum(i, N_CTX_TILES - 1), 0)),
                     pl.BlockSpec((TM, D), lambda i, p: (jnp.maximum(i - N_CTX_TILES, 0), 0))]
    else:
        out_shape = jax.ShapeDtypeStruct((T_ALL, D), F32)
        out_specs = pl.BlockSpec((TM, D), lambda i, p: (i, 0))
    return pl.pallas_call(
        functools.partial(_combine_kernel, split),
        out_shape=out_shape,
        grid_spec=pltpu.PrefetchScalarGridSpec(
            num_scalar_prefetch=1, grid=(N_TILES,),
            in_specs=[pl.BlockSpec((TM, D), lambda i, p: (i, 0)),
                      pl.BlockSpec((None, None, N_ADA, D), ada_map),
                      pl.BlockSpec((TM, LANE), lambda i, p: (i, 0)),
                      pl.BlockSpec(memory_space=pl.ANY)],
            out_specs=out_specs,
            scratch_shapes=[pltpu.VMEM((2, TM, D), F32), pltpu.SemaphoreType.DMA(())]),
        compiler_params=pltpu.CompilerParams(dimension_semantics=("arbitrary",)),
        name="moe_combine",
    )(pos, x1, ada, rw, ys)


def _moe_layer(x1, h2, ri, rw, cnt, ada, layer, wg, wu, wd, lidx, split):
    idx = ri[:, 0:2]
    rank = ri[:, 2:4]
    counts = cnt[0, :N_EXP].astype(jnp.int32)
    padded = ((counts + MOE_SUB - 1) // MOE_SUB) * MOE_SUB
    ends = jnp.cumsum(padded)
    starts = ends - padded
    sel = idx[:, :, None] == jnp.arange(N_EXP, dtype=jnp.int32)[None, None, :]
    pos = (jnp.sum(jnp.where(sel, starts[None, None, :], 0), axis=-1) + rank).reshape(-1).astype(jnp.int32)
    n_st = (padded + MOE_R - 1) // MOE_R
    st_end = jnp.cumsum(n_st)
    st_begin = st_end - n_st
    j = jnp.arange(MOE_NS, dtype=jnp.int32)
    se = jnp.sum((j[:, None] >= st_end[None, :]).astype(jnp.int32), axis=1)
    used = se < N_EXP
    se_c = jnp.minimum(se, N_EXP - 1)
    k = j - st_begin[se_c]
    row0 = jnp.where(used, starts[se_c] + k * MOE_R, 0).astype(jnp.int32)
    nsub = jnp.where(used, jnp.minimum(MOE_R, padded[se_c] - k * MOE_R) // MOE_SUB, 0).astype(jnp.int32)
    last = jnp.maximum(jnp.sum(used.astype(jnp.int32)) - 1, 0)
    se_eff = jnp.where(used, se_c, se_c[last]).astype(jnp.int32)
    hs = _dispatch_call(pos, h2, jnp.zeros((MOE_ROWS, D), F32))
    ys = _moe_call(se_eff, row0, nsub, hs, wg, wu, wd, lidx)
    return _combine_call(split, pos, x1, ada, layer, rw, ys)


def _rope_tables():
    t = jnp.arange(DEC_SEQ)
    rows = (t // GRID_W).astype(F32)
    cols = (t % GRID_W).astype(F32)

    def axis_tabs(pos, d):
        inv = ROPE_BASE ** (-jnp.arange(0, d, 2, dtype=F32) / d)
        ang = pos[:, None] * inv[None, :]
        cos, sin = jnp.cos(ang), jnp.sin(ang)
        z = jnp.zeros_like(sin)
        return (jnp.concatenate([cos, cos], -1), jnp.concatenate([-sin, z], -1), jnp.concatenate([z, sin], -1))

    def axial(d):
        r = axis_tabs(rows, d // 2)
        c = axis_tabs(cols, d // 2)
        return [jnp.concatenate([a, b], -1) for a, b in zip(r, c)]

    tabs_a = [jnp.tile(tb, (1, A_HEADS)) for tb in axial(A_DH)]
    cb, sub, sdb = axial(B_ROPE)
    ones = jnp.ones((DEC_SEQ, B_NOPE), F32)
    zer = jnp.zeros((DEC_SEQ, B_NOPE), F32)
    pad = jnp.zeros((DEC_SEQ, SLOT - B_QK), F32)
    slot = lambda first, tb: jnp.tile(jnp.concatenate([first, tb, pad], -1), (1, B_HEADS))
    tabs_b = [slot(ones, cb), slot(zer, sub), slot(zer, sdb)]
    return tabs_a + tabs_b


def _attn_weights(e, w_in_attn, g_qk_a_q, g_qk_a_k, g_mla_q_a, w_mla_q_b, g_mla_kv_a, w_mla_kv_b,
                  g_qk_b_q, g_qk_b_k):
    w_in = jnp.pad(w_in_attn[e], ((0, 0), (0, IN_PAD - IN_DIM))).astype(BF16)
    w_qb = jnp.pad(w_mla_q_b[e].reshape(B_Q_LORA, B_HEADS, B_QK), ((0, 0), (0, 0), (0, SLOT - B_QK)))
    w_qb = w_qb.reshape(B_Q_LORA, B_HEADS * SLOT).astype(BF16)
    wkv = w_mla_kv_b[e].reshape(B_KV_LORA, B_HEADS, B_NOPE + B_V)
    wk = jnp.pad(wkv[:, :, :B_NOPE], ((0, 0), (0, 0), (0, SLOT - B_NOPE))).reshape(B_KV_LORA, B_HEADS * SLOT)
    wv = wkv[:, :, B_NOPE:].reshape(B_KV_LORA, B_HEADS * B_V)
    w_kv = jnp.concatenate([wk, wv], axis=1).astype(BF16)
    place = jnp.zeros((LANE, B_HEADS, SLOT), F32)
    ii = jnp.arange(B_ROPE)
    place = place.at[ii, :, B_NOPE + ii].set(1.0).reshape(LANE, B_HEADS * SLOT).astype(BF16)
    slot_gain = lambda g: jnp.tile(jnp.pad(g, (0, SLOT - B_QK)), B_HEADS).reshape(1, B_HEADS * SLOT)
    return dict(
        w_in=w_in, w_qb=w_qb, w_kv=w_kv, place=place,
        gqa=jnp.tile(g_qk_a_q[e], A_HEADS).reshape(1, 512),
        gka=jnp.tile(g_qk_a_k[e], A_KV).reshape(1, LANE),
        gql=g_mla_q_a[e].reshape(1, B_Q_LORA), gkl=g_mla_kv_a[e].reshape(1, B_KV_LORA),
        gqb=slot_gain(g_qk_b_q[e]), gkb=slot_gain(g_qk_b_k[e]))


def _pair_variants_host(x):
    h0, h1 = x[..., :64], x[..., 64:]
    z = jnp.zeros_like(h0)
    return jnp.concatenate([h0, z, z, h0, h1, z, z, h1], axis=-1)


def kernel(x_prompt, x_sample, c, cache_k_win, cache_v_win, cache_ckv, cache_krope, c_ctx, w_ada, b_ada, g_norm_mix, g_norm_ffn, w_in_attn, g_qk_a_q, g_qk_a_k, sink_a, g_mla_q_a, w_mla_q_b, g_mla_kv_a, w_mla_kv_b, g_qk_b_q, g_qk_b_k, w_out_attn, w_ffn_gate, w_ffn_up, w_ffn_down, w_conv_pw1, b_conv_pw1, w_conv_dw, b_conv_dw, g_conv_ln, b_conv_ln, w_conv_pw2, b_conv_pw2, w_router, w_moe_gate, w_moe_up, w_moe_down):
    x = jnp.concatenate([x_prompt.reshape(T_CTX, D), x_sample.reshape(T_LAT, D)], axis=0)
    cond8 = jnp.concatenate([c_ctx[None, :], c, jnp.zeros((8 - 1 - DEC_BATCH, D), F32)], axis=0)
    ada = _ada_call(cond8, w_ada, b_ada).reshape(DEPTH, 8, N_ADA, D)
    tabs = _rope_tables()

    wo_b = w_out_attn.astype(BF16)
    wg_b = w_ffn_gate.astype(BF16)
    wu_b = w_ffn_up.astype(BF16)
    wd_b = w_ffn_down.astype(BF16)
    w1_b = w_conv_pw1.astype(BF16)
    w2_b = w_conv_pw2.astype(BF16)

    new_k, new_v, new_ckv, new_kr = [], [], [], []
    for layer in range(DEPTH):
        gm = g_norm_mix[layer].reshape(1, D)
        gf = g_norm_ffn[layer].reshape(1, D)
        if layer % 2 == 0:
            e = layer // 2
            lw = _attn_weights(e, w_in_attn, g_qk_a_q, g_qk_a_k, g_mla_q_a, w_mla_q_b, g_mla_kv_a,
                               w_mla_kv_b, g_qk_b_q, g_qk_b_k)
            sink = sink_a[e].reshape(A_HEADS)
            (qa, ka4, va4, qb, kb, vlo, vhi, kaf, vaf, ckvf, krf) = _pre_attn_call(False, x, ada, layer, gm, lw, None)
            o_ctx = _attn_ctx_call(sink, qa, ka4, va4, qb, kb, vlo, vhi)
            new_k.append(kaf.reshape(BATCH, SEQ, A_KV, A_DH))
            new_v.append(vaf.reshape(BATCH, SEQ, A_KV, A_DH))
            new_ckv.append(ckvf.reshape(BATCH, SEQ, B_KV_LORA))
            new_kr.append(krf[:, :B_ROPE].reshape(BATCH, SEQ, B_ROPE))
            (qa, ka4, va4, qb, kb, vlo, vhi) = _pre_attn_call(True, x, ada, layer, gm, lw, tabs)
            kr_c = jnp.pad(cache_krope[:, e].reshape(DEC_BATCH * PAST, B_ROPE), ((0, 0), (0, LANE - B_ROPE)))
            kb_c, vlo_c, vhi_c = _cache_kv_call(cache_ckv[:, e].reshape(DEC_BATCH * PAST, B_KV_LORA), kr_c, lw)
            pad_rows = lambda a: jnp.pad(a.reshape(DEC_BATCH, DEC_SEQ, 512), ((0, 0), (QB, QB), (0, 0)))
            ka4c = _pair_variants_host(cache_k_win[:, e].reshape(DEC_BATCH, PAST, LANE)).astype(BF16)
            va4c = _pair_variants_host(cache_v_win[:, e].reshape(DEC_BATCH, PAST, LANE)).astype(BF16)
            cat = lambda a, b, w: jnp.concatenate(
                [a.reshape(DEC_BATCH, PAST, w), b.reshape(DEC_BATCH, DEC_SEQ, w)], axis=1)
            o_lat = _attn_lat_call(sink, qa, pad_rows(ka4), pad_rows(va4), ka4c, va4c, qb,
                                   cat(kb_c, kb, 1024), cat(vlo_c, vlo, 512), cat(vhi_c, vhi, 512))
            o = jnp.concatenate([o_ctx, o_lat], axis=0)
            x = _post_attn_call(x, o, ada, layer, gf, wo_b[e], wg_b[e], wu_b[e], wd_b[e])
        else:
            o_ = layer // 2
            cw = dict(w1=w1_b[o_], b1=b_conv_pw1[o_].reshape(1, 2 * D), wdw=w_conv_dw[o_],
                      bdw=b_conv_dw[o_].reshape(1, D), gln=g_conv_ln[o_].reshape(1, D),
                      bln=b_conv_ln[o_].reshape(1, D), w2=w2_b[o_], b2=b_conv_pw2[o_].reshape(1, D),
                      wr=jnp.pad(w_router[o_], ((0, 0), (0, LANE - N_EXP))))
            x1, h2, ri, rw, cnt = _conv_router_call(x, ada, layer, gm, gf, cw)
            x = _moe_layer(x1, h2, ri, rw, cnt, ada, layer, w_moe_gate, w_moe_up, w_moe_down, o_,
                           split=(layer == DEPTH - 1))

    y_prompt = x[0].reshape(BATCH, SEQ, D)
    y_sample = x[1].reshape(DEC_BATCH, DEC_SEQ, D)
    return (y_prompt, y_sample, jnp.stack(new_k, axis=1), jnp.stack(new_v, axis=1),
            jnp.stack(new_ckv, axis=1), jnp.stack(new_kr, axis=1))
```

```python
import functools

import jax
import jax.numpy as jnp
from jax import lax
from jax.experimental import pallas as pl
from jax.experimental.pallas import tpu as pltpu

F32 = jnp.float32
BF16 = jnp.bfloat16

D = 1024
BATCH = 32
SEQ = 256
DEPTH = 4
DEC_BATCH = 2
DEC_SEQ = 1024
PAST = 256
GRID_W = 64
N_EVEN = 2
A_HEADS = 8
A_KV = 2
A_DH = 64
WINDOW = 128
B_HEADS = 8
B_Q_LORA = 256
B_KV_LORA = 128
B_NOPE = 64
B_ROPE = 32
B_QK = B_NOPE + B_ROPE
B_V = 64
IN_DIM = 1184
IN_PAD = 1280
CONV_W = 31
D_FF = 2816
N_EXP = 8
D_FFE = 3584
ROPE_BASE = 10000.0
EPS = 1e-6
N_ADA = 6

T_CTX = BATCH * SEQ
T_LAT = DEC_BATCH * DEC_SEQ
T_ALL = T_CTX + T_LAT
TM = 256
N_TILES = T_ALL // TM
N_CTX_TILES = T_CTX // TM
LAT_TILES_PER_SEQ = DEC_SEQ // TM
LANE = 128
SLOT = 128

MOE_SUB = 256
MOE_R = 3072
MOE_TF = 512
MOE_NF = D_FFE // MOE_TF
MOE_TC = 512
MOE_DEPTH = 3
MOE_SLOTS = MOE_DEPTH + 1
MOE_ROWS = 2 * T_ALL + N_EXP * MOE_SUB
MOE_NS = MOE_ROWS // MOE_R + N_EXP

FFN_TM = 512
FFN_CH = 1408

VMEM_BIG = 56 * 1024 * 1024


def _dot(a, b):
    return jnp.dot(a, b, preferred_element_type=F32)


def _dot_nt(a, b):
    return lax.dot_general(a, b, (((1,), (1,)), ((), ())), preferred_element_type=F32)


def _split(a):
    hi = a.astype(BF16)
    lo = (a - hi.astype(F32)).astype(BF16)
    return hi, lo


def _dot_x3(a, b):
    ah, al = _split(a)
    bh, bl = _split(b)
    return _dot(ah, bh) + (_dot(ah, bl) + _dot(al, bh))


def _sigmoid(x):
    return 1.0 / (1.0 + jnp.exp(-x))


def _rms(x, n):
    return x * lax.rsqrt(jnp.sum(x * x, axis=-1, keepdims=True) * (1.0 / n) + EPS)


def _modulate(x, g, shift, scale):
    return _rms(x, D) * g * (1.0 + scale) + shift


def _lo_mask(rows):
    return lax.broadcasted_iota(jnp.int32, (rows, LANE), 1) < 64


def _ada_kernel(cond_ref, w_ref, b_ref, o_ref):
    c = cond_ref[...]
    s = c * _sigmoid(c)
    o_ref[...] = _dot_x3(s, w_ref[...]) + b_ref[...]


def _ada_call(cond8, w_ada, b_ada):
    tn = 1024
    return pl.pallas_call(
        _ada_kernel,
        out_shape=jax.ShapeDtypeStruct((DEPTH, 8, N_ADA * D), F32),
        grid=(DEPTH, N_ADA * D // tn),
        in_specs=[
            pl.BlockSpec((8, D), lambda l, j: (0, 0)),
            pl.BlockSpec((None, D, tn), lambda l, j: (l, 0, j)),
            pl.BlockSpec((None, 1, tn), lambda l, j: (l, 0, j)),
        ],
        out_specs=pl.BlockSpec((None, 8, tn), lambda l, j: (l, 0, j)),
        compiler_params=pltpu.CompilerParams(
            dimension_semantics=("arbitrary", "arbitrary"), vmem_limit_bytes=VMEM_BIG),
        name="ada",
    )(cond8, w_ada, b_ada.reshape(DEPTH, 1, N_ADA * D))


def _rope(x, cos, sup, sdn, shift):
    up = pltpu.roll(x, LANE - shift, 1)
    dn = pltpu.roll(x, shift, 1)
    return x * cos + up * sup + dn * sdn


def _norm64_block(blk, lo):
    sq = blk * blk
    s_lo = jnp.sum(jnp.where(lo, sq, 0.0), axis=-1, keepdims=True)
    s_hi = jnp.sum(jnp.where(lo, 0.0, sq), axis=-1, keepdims=True)
    r = jnp.where(lo, lax.rsqrt(s_lo * (1.0 / A_DH) + EPS), lax.rsqrt(s_hi * (1.0 / A_DH) + EPS))
    return blk * r


def _pair_variants(x, lo):
    r = pltpu.roll(x, 64, 1)
    return (jnp.where(lo, x, 0.0), jnp.where(lo, 0.0, r), jnp.where(lo, r, 0.0), jnp.where(lo, 0.0, x))


def _mla_kv(ckv_n, kr, wkv_ref, place_ref, gk_ref, lo, kb_ref, vlo_ref, vhi_ref, rope_tabs):
    kv = _dot(ckv_n.astype(BF16), wkv_ref[...])
    kr_hi, kr_lo = _split(kr)
    place = place_ref[...]
    krs = _dot(kr_hi, place) + _dot(kr_lo, place)
    for h in range(B_HEADS):
        sl = slice(h * SLOT, (h + 1) * SLOT)
        blk = kv[:, sl] + krs[:, sl]
        blk = _rms(blk, B_QK) * gk_ref[:, sl]
        if rope_tabs is not None:
            cos, sup, sdn = rope_tabs
            blk = _rope(blk, cos[:, sl], sup[:, sl], sdn[:, sl], 8)
        kb_ref[:, sl] = blk.astype(BF16)
    for j in range(B_HEADS // 2):
        sl = slice(j * LANE, (j + 1) * LANE)
        v = kv[:, B_HEADS * SLOT + j * LANE: B_HEADS * SLOT + (j + 1) * LANE]
        vlo_ref[:, sl] = jnp.where(lo, v, 0.0).astype(BF16)
        vhi_ref[:, sl] = jnp.where(lo, 0.0, v).astype(BF16)


def _pre_attn_kernel(rope, *refs):
    if rope:
        (x_ref, ada_ref, g_ref, win_ref, wqb_ref, wkv_ref, place_ref,
         gqa_ref, gka_ref, gql_ref, gkl_ref, gqb_ref, gkb_ref,
         ca_ref, sua_ref, sda_ref, cb_ref, sub_ref, sdb_ref,
         qa_o, ka4_o, va4_o, qb_o, kb_o, vlo_o, vhi_o) = refs
    else:
        (x_ref, ada_ref, g_ref, win_ref, wqb_ref, wkv_ref, place_ref,
         gqa_ref, gka_ref, gql_ref, gkl_ref, gqb_ref, gkb_ref,
         qa_o, ka4_o, va4_o, qb_o, kb_o, vlo_o, vhi_o,
         kaf_o, vaf_o, ckvf_o, krf_o) = refs
    lo = _lo_mask(TM)
    h = _modulate(x_ref[...], g_ref[...], ada_ref[0:1, :], ada_ref[1:2, :])
    p = _dot(h.astype(BF16), win_ref[...])

    for j in range(4):
        sl = slice(j * LANE, (j + 1) * LANE)
        blk = _norm64_block(p[:, sl], lo) * gqa_ref[:, sl]
        if rope:
            blk = _rope(blk, ca_ref[:, sl], sua_ref[:, sl], sda_ref[:, sl], 16)
        qa_o[:, sl] = (blk * (A_DH ** -0.5)).astype(BF16)
    ka = _norm64_block(p[:, 512:640], lo) * gka_ref[...]
    va = p[:, 640:768]
    if not rope:
        kaf_o[...] = ka
        vaf_o[...] = va
    else:
        ka = _rope(ka, ca_ref[:, 0:LANE], sua_ref[:, 0:LANE], sda_ref[:, 0:LANE], 16)
    for n, (kk, vv) in enumerate(zip(_pair_variants(ka, lo), _pair_variants(va, lo))):
        sl = slice(n * LANE, (n + 1) * LANE)
        ka4_o[:, sl] = kk.astype(BF16)
        va4_o[:, sl] = vv.astype(BF16)

    cq = _rms(p[:, 768:1024], B_Q_LORA) * gql_ref[...]
    qb = _dot(cq.astype(BF16), wqb_ref[...])
    for hh in range(B_HEADS):
        sl = slice(hh * SLOT, (hh + 1) * SLOT)
        blk = _rms(qb[:, sl], B_QK) * gqb_ref[:, sl]
        if rope:
            blk = _rope(blk, cb_ref[:, sl], sub_ref[:, sl], sdb_ref[:, sl], 8)
        qb_o[:, sl] = (blk * (B_QK ** -0.5)).astype(BF16)

    ckv = _rms(p[:, 1024:1152], B_KV_LORA) * gkl_ref[...]
    kr = p[:, 1152:1280]
    if not rope:
        ckvf_o[...] = ckv
        krf_o[...] = kr
    tabs = (cb_ref, sub_ref, sdb_ref) if rope else None
    _mla_kv(ckv, kr, wkv_ref, place_ref, gkb_ref, lo, kb_o, vlo_o, vhi_o, tabs)


def _const_spec(shape):
    nd = len(shape)
    return pl.BlockSpec(shape, lambda *a: (0,) * nd)


def _pre_attn_call(rope, x, ada, layer, g, lw, tabs):
    n_tiles = (T_LAT if rope else T_CTX) // TM
    off = N_CTX_TILES if rope else 0
    n_rows = n_tiles * TM
    if rope:
        ada_map = lambda i: (layer, 1 + i // LAT_TILES_PER_SEQ, 0, 0)
    else:
        ada_map = lambda i: (layer, 0, 0, 0)
    in_specs = [
        pl.BlockSpec((TM, D), lambda i: (i + off, 0)),
        pl.BlockSpec((None, None, N_ADA, D), ada_map),
        _const_spec((1, D)),
        _const_spec((D, IN_PAD)),
        _const_spec((B_Q_LORA, B_HEADS * SLOT)),
        _const_spec((B_KV_LORA, B_HEADS * SLOT + B_HEADS * B_V)),
        _const_spec((LANE, B_HEADS * SLOT)),
        _const_spec((1, 512)), _const_spec((1, LANE)), _const_spec((1, B_Q_LORA)),
        _const_spec((1, B_KV_LORA)), _const_spec((1, B_HEADS * SLOT)), _const_spec((1, B_HEADS * SLOT)),
    ]
    args = [x, ada, g, lw["w_in"], lw["w_qb"], lw["w_kv"], lw["place"],
            lw["gqa"], lw["gka"], lw["gql"], lw["gkl"], lw["gqb"], lw["gkb"]]
    if rope:
        tmap = lambda i: (i % LAT_TILES_PER_SEQ, 0)
        in_specs += [pl.BlockSpec((TM, 512), tmap)] * 3 + [pl.BlockSpec((TM, B_HEADS * SLOT), tmap)] * 3
        args += list(tabs)
    row = lambda w: pl.BlockSpec((TM, w), lambda i: (i, 0))
    out_shape = [jax.ShapeDtypeStruct((n_rows, w), BF16) for w in (512, 512, 512, 1024, 1024, 512, 512)]
    out_specs = [row(w) for w in (512, 512, 512, 1024, 1024, 512, 512)]
    if not rope:
        out_shape += [jax.ShapeDtypeStruct((n_rows, LANE), F32)] * 4
        out_specs += [row(LANE)] * 4
    return pl.pallas_call(
        functools.partial(_pre_attn_kernel, rope),
        out_shape=out_shape, grid=(n_tiles,), in_specs=in_specs, out_specs=out_specs,
        compiler_params=pltpu.CompilerParams(
            dimension_semantics=("arbitrary",), vmem_limit_bytes=VMEM_BIG),
        name="pre_attn_lat" if rope else "pre_attn_ctx",
    )(*args)


def _cache_kv_kernel(ckv_ref, kr_ref, wkv_ref, place_ref, gkb_ref, kb_o, vlo_o, vhi_o):
    lo = _lo_mask(PAST)
    _mla_kv(ckv_ref[...], kr_ref[...], wkv_ref, place_ref, gkb_ref, lo, kb_o, vlo_o, vhi_o, None)


def _cache_kv_call(ckv, kr, lw):
    rows = DEC_BATCH * PAST
    row = lambda w: pl.BlockSpec((PAST, w), lambda i: (i, 0))
    return pl.pallas_call(
        _cache_kv_kernel,
        out_shape=[jax.ShapeDtypeStruct((rows, 1024), BF16), jax.ShapeDtypeStruct((rows, 512), BF16),
                   jax.ShapeDtypeStruct((rows, 512), BF16)],
        grid=(DEC_BATCH,),
        in_specs=[row(LANE), row(LANE),
                  _const_spec((B_KV_LORA, B_HEADS * SLOT + B_HEADS * B_V)),
                  _const_spec((LANE, B_HEADS * SLOT)), _const_spec((1, B_HEADS * SLOT))],
        out_specs=[row(1024), row(512), row(512)],
        compiler_params=pltpu.CompilerParams(dimension_semantics=("arbitrary",)),
        name="cache_kv",
    )(ckv, kr, lw["w_kv"], lw["place"], lw["gkb"])


def _softmax_pv(scores, values, sink):
    m = jnp.max(scores[0], axis=-1, keepdims=True)
    for s in scores[1:]:
        m = jnp.maximum(m, jnp.max(s, axis=-1, keepdims=True))
    if sink is not None:
        m = jnp.maximum(m, sink)
    ps = [jnp.exp(s - m) for s in scores]
    den = jnp.sum(ps[0], axis=-1, keepdims=True)
    for p in ps[1:]:
        den = den + jnp.sum(p, axis=-1, keepdims=True)
    if sink is not None:
        den = den + jnp.exp(sink - m)
    inv = 1.0 / den
    out = _dot((ps[0] * inv).astype(BF16), values[0])
    for p, v in zip(ps[1:], values[1:]):
        out = out + _dot((p * inv).astype(BF16), v)
    return out


def _attn_ctx_kernel(sink_ref, qa_ref, ka4_ref, va4_ref, qb_ref, kb_ref, vlo_ref, vhi_ref, o_ref):
    for j in range(4):
        q = qa_ref[:, j * LANE:(j + 1) * LANE]
        acc = None
        for par in range(2):
            c = (2 * (j // 2) + par) * LANE
            s = _dot_nt(q, ka4_ref[:, c:c + LANE])
            o = _softmax_pv([s], [va4_ref[:, c:c + LANE]], sink_ref[2 * j + par])
            acc = o if acc is None else acc + o
        o_ref[:, j * LANE:(j + 1) * LANE] = acc.astype(BF16)
    for j in range(4):
        acc = None
        for par, v_ref in enumerate((vlo_ref, vhi_ref)):
            hh = 2 * j + par
            s = _dot_nt(qb_ref[:, hh * SLOT:(hh + 1) * SLOT], kb_ref[:, hh * SLOT:(hh + 1) * SLOT])
            o = _softmax_pv([s], [v_ref[:, j * LANE:(j + 1) * LANE]], None)
            acc = o if acc is None else acc + o
        o_ref[:, 512 + j * LANE:512 + (j + 1) * LANE] = acc.astype(BF16)


def _attn_ctx_call(sink, qa, ka4, va4, qb, kb, vlo, vhi):
    row = lambda w: pl.BlockSpec((SEQ, w), lambda b: (b, 0))
    return pl.pallas_call(
        _attn_ctx_kernel,
        out_shape=jax.ShapeDtypeStruct((T_CTX, D), BF16),
        grid=(BATCH,),
        in_specs=[pl.BlockSpec(memory_space=pltpu.SMEM),
                  row(512), row(512), row(512), row(1024), row(1024), row(512), row(512)],
        out_specs=row(D),
        compiler_params=pltpu.CompilerParams(dimension_semantics=("arbitrary",)),
        name="attn_ctx",
    )(sink, qa, ka4, va4, qb, kb, vlo, vhi)


QB = 128
N_QB = DEC_SEQ // QB
KPAD = DEC_SEQ + 2 * QB
KALL = PAST + DEC_SEQ


def _attn_lat_kernel(sink_ref, qa_ref, ka4p_ref, va4p_ref, ka4c_ref, va4c_ref,
                     qb_ref, kb_ref, vlo_ref, vhi_ref, o_ref):
    n = pl.program_id(1)
    r = lax.broadcasted_iota(jnp.int32, (QB, 3 * QB), 0)
    c = lax.broadcasted_iota(jnp.int32, (QB, 3 * QB), 1)
    rel = c - QB - r
    kpos = (n - 1) * QB + c
    valid = (jnp.abs(rel) <= WINDOW) & (kpos >= 0) & (kpos < DEC_SEQ)
    k0 = pl.multiple_of(n * QB, QB)
    for j in range(4):
        q = qa_ref[:, j * LANE:(j + 1) * LANE]
        acc = None
        for par in range(2):
            cc = (2 * (j // 2) + par) * LANE
            s_loc = _dot_nt(q, ka4p_ref[pl.ds(k0, 3 * QB), cc:cc + LANE])
            s_loc = jnp.where(valid, s_loc, -jnp.inf)
            s_ctx = _dot_nt(q, ka4c_ref[:, cc:cc + LANE])
            o = _softmax_pv([s_ctx, s_loc],
                            [va4c_ref[:, cc:cc + LANE], va4p_ref[pl.ds(k0, 3 * QB), cc:cc + LANE]],
                            sink_ref[2 * j + par])
            acc = o if acc is None else acc + o
        o_ref[:, j * LANE:(j + 1) * LANE] = acc.astype(BF16)
    for j in range(4):
        acc = None
        for par, v_ref in enumerate((vlo_ref, vhi_ref)):
            hh = 2 * j + par
            s = _dot_nt(qb_ref[:, hh * SLOT:(hh + 1) * SLOT], kb_ref[:, hh * SLOT:(hh + 1) * SLOT])
            o = _softmax_pv([s], [v_ref[:, j * LANE:(j + 1) * LANE]], None)
            acc = o if acc is None else acc + o
        o_ref[:, 512 + j * LANE:512 + (j + 1) * LANE] = acc.astype(BF16)


def _attn_lat_call(sink, qa, ka4p, va4p, ka4c, va4c, qb, kball, vlo_all, vhi_all):
    qrow = lambda w: pl.BlockSpec((QB, w), lambda b, n: (b * N_QB + n, 0))
    per_b = lambda rows, w: pl.BlockSpec((None, rows, w), lambda b, n: (b, 0, 0))
    return pl.pallas_call(
        _attn_lat_kernel,
        out_shape=jax.ShapeDtypeStruct((T_LAT, D), BF16),
        grid=(DEC_BATCH, N_QB),
        in_specs=[pl.BlockSpec(memory_space=pltpu.SMEM),
                  qrow(512), per_b(KPAD, 512), per_b(KPAD, 512), per_b(PAST, 512), per_b(PAST, 512),
                  qrow(1024), per_b(KALL, 1024), per_b(KALL, 512), per_b(KALL, 512)],
        out_specs=qrow(D),
        compiler_params=pltpu.CompilerParams(dimension_semantics=("arbitrary", "arbitrary")),
        name="attn_lat",
    )(sink, qa, ka4p, va4p, ka4c, va4c, qb, kball, vlo_all, vhi_all)


def _post_attn_kernel(x_ref, o_ref, ada_ref, g_ref, wo_ref, wg_ref, wu_ref, wd_ref, y_ref):
    mix = _dot(o_ref[...], wo_ref[...])
    x1 = x_ref[...] + ada_ref[2:3, :] * mix
    h = _modulate(x1, g_ref[...], ada_ref[3:4, :], ada_ref[4:5, :]).astype(BF16)
    acc = None
    for c0 in range(0, D_FF, FFN_CH):
        gg = _dot(h, wg_ref[:, c0:c0 + FFN_CH])
        uu = _dot(h, wu_ref[:, c0:c0 + FFN_CH])
        a = (gg * _sigmoid(gg) * uu).astype(BF16)
        part = _dot(a, wd_ref[c0:c0 + FFN_CH, :])
        acc = part if acc is None else acc + part
    y_ref[...] = x1 + ada_ref[5:6, :] * acc


def _tile_cond(i, tiles_per_ctx, tiles_per_lat_seq):
    return jnp.where(i < tiles_per_ctx, 0, 1 + (i - tiles_per_ctx) // tiles_per_lat_seq)


def _post_attn_call(x, o, ada, layer, g, wo, wg, wu, wd):
    n_tiles = T_ALL // FFN_TM
    ada_map = lambda i: (layer, _tile_cond(i, T_CTX // FFN_TM, DEC_SEQ // FFN_TM), 0, 0)
    one = pl.Buffered(1)
    return pl.pallas_call(
        _post_attn_kernel,
        out_shape=jax.ShapeDtypeStruct((T_ALL, D), F32),
        grid=(n_tiles,),
        in_specs=[pl.BlockSpec((FFN_TM, D), lambda i: (i, 0)),
                  pl.BlockSpec((FFN_TM, D), lambda i: (i, 0)),
                  pl.BlockSpec((None, None, N_ADA, D), ada_map),
                  _const_spec((1, D)),
                  pl.BlockSpec((D, D), lambda i: (0, 0), pipeline_mode=one),
                  pl.BlockSpec((D, D_FF), lambda i: (0, 0), pipeline_mode=one),
                  pl.BlockSpec((D, D_FF), lambda i: (0, 0), pipeline_mode=one),
                  pl.BlockSpec((D_FF, D), lambda i: (0, 0), pipeline_mode=one)],
        out_specs=pl.BlockSpec((FFN_TM, D), lambda i: (i, 0)),
        compiler_params=pltpu.CompilerParams(
            dimension_semantics=("arbitrary",), vmem_limit_bytes=VMEM_BIG),
        name="post_attn_ffn",
    )(x, o, ada, g, wo, wg, wu, wd)


HALO = 16
EXT = TM + 2 * HALO
CONV_RC = 64
CONV_LC = 128
CONV_WIN = CONV_RC + 24
CONV_LN_ROWS = 32


def _conv_router_kernel(x_ref, xp_ref, xn_ref, ada_ref, gm_ref, gf_ref, w1_ref, b1_ref, wdw_ref, bdw_ref,
                        gln_ref, bln_ref, w2_ref, b2_ref, wr_ref,
                        x1_o, h2_o, ri_o, rw_o, cnt_o, p_ref, c_ref, v_ref, run_ref):
    i = pl.program_id(0)

    @pl.when(i == 0)
    def _():
        run_ref[...] = jnp.zeros_like(run_ref)

    is_lat = i >= N_CTX_TILES
    j = (i - N_CTX_TILES) % LAT_TILES_PER_SEQ
    lflag = jnp.where(is_lat & (j != 0), 1.0, 0.0)
    rflag = jnp.where(is_lat & (j != LAT_TILES_PER_SEQ - 1), 1.0, 0.0)

    x = x_ref[...]
    xx = jnp.concatenate([xp_ref[...], x, xn_ref[...]], axis=0)
    h = _modulate(xx, gm_ref[...], ada_ref[0:1, :], ada_ref[1:2, :])
    a = _dot(h.astype(BF16), w1_ref[...]) + b1_ref[...]
    u = a[:, :D] * _sigmoid(a[:, D:])
    row = lax.broadcasted_iota(jnp.int32, (EXT, 1), 0)
    u = u * jnp.where(row < HALO, lflag, jnp.where(row >= HALO + TM, rflag, 1.0))

    p_ref[...] = u
    for rc in range(TM // CONV_RC):
        for lc in range(D // CONV_LC):
            ls = slice(lc * CONV_LC, (lc + 1) * CONV_LC)
            acc = jnp.zeros((CONV_RC, CONV_LC), F32) + bdw_ref[:, ls]
            base = p_ref[rc * CONV_RC:rc * CONV_RC + CONV_WIN + 8, ls]
            for r in range(8):
                win = base if r == 0 else pltpu.roll(base, CONV_WIN + 8 - r, 0)
                for q in range(4):
                    o = 8 * q + r
                    if 1 <= o <= CONV_W:
                        acc = acc + wdw_ref[o - 1:o, ls] * win[8 * q:8 * q + CONV_RC]
            c_ref[rc * CONV_RC:(rc + 1) * CONV_RC, ls] = acc

    for c0 in range(0, TM, CONV_LN_ROWS):
        acc = c_ref[c0:c0 + CONV_LN_ROWS, :]
        mu = jnp.mean(acc, axis=-1, keepdims=True)
        xc = acc - mu
        y = xc * lax.rsqrt(jnp.mean(xc * xc, axis=-1, keepdims=True) + EPS)
        y = y * gln_ref[...] + bln_ref[...]
        v_ref[c0:c0 + CONV_LN_ROWS, :] = (y * _sigmoid(y)).astype(BF16)

    mix = _dot(v_ref[...], w2_ref[...]) + b2_ref[...]
    x1 = x + ada_ref[2:3, :] * mix
    x1_o[...] = x1
    h2 = _modulate(x1, gf_ref[...], ada_ref[3:4, :], ada_ref[4:5, :])
    h2_o[...] = h2

    lane = lax.broadcasted_iota(jnp.int32, (TM, LANE), 1)
    lane_f = lane.astype(F32)
    logits = jnp.where(lane < N_EXP, _dot_x3(h2, wr_ref[...]), -jnp.inf)
    m1 = jnp.max(logits, axis=-1, keepdims=True)
    i1 = jnp.min(jnp.where(logits == m1, lane_f, float(LANE)), axis=-1, keepdims=True)
    rest = jnp.where(lane_f == i1, -jnp.inf, logits)
    m2 = jnp.max(rest, axis=-1, keepdims=True)
    i2 = jnp.min(jnp.where(rest == m2, lane_f, float(LANE)), axis=-1, keepdims=True)
    e = jnp.exp(m2 - m1)
    w1 = 1.0 / (1.0 + e)
    w2 = e / (1.0 + e)

    oh1 = jnp.where(lane_f == i1, 1.0, 0.0)
    oh2 = jnp.where(lane_f == i2, 1.0, 0.0)
    tr = lax.broadcasted_iota(jnp.int32, (TM, TM), 0)
    tc = lax.broadcasted_iota(jnp.int32, (TM, TM), 1)
    tri = jnp.where(tr > tc, 1.0, 0.0).astype(BF16)
    run = run_ref[0:1, :]
    cnt1 = jnp.sum(oh1, axis=0, keepdims=True)
    cnt2 = jnp.sum(oh2, axis=0, keepdims=True)
    pre1 = _dot(tri, oh1.astype(BF16)) + run
    pre2 = _dot(tri, oh2.astype(BF16)) + (run + cnt1)
    rank1 = jnp.sum(oh1 * pre1, axis=-1, keepdims=True)
    rank2 = jnp.sum(oh2 * pre2, axis=-1, keepdims=True)
    new_run = run + cnt1 + cnt2
    run_ref[...] = jnp.broadcast_to(new_run, run_ref.shape)
    cnt_o[...] = jnp.broadcast_to(new_run, cnt_o.shape)

    ri = jnp.where(lane == 0, i1, jnp.where(lane == 1, i2, jnp.where(lane == 2, rank1, jnp.where(lane == 3, rank2, 0.0))))
    ri_o[...] = ri.astype(jnp.int32)
    rw_o[...] = jnp.where(lane == 0, w1, jnp.where(lane == 1, w2, 0.0))


def _conv_router_call(x, ada, layer, gm, gf, cw):
    ada_map = lambda i: (layer, _tile_cond(i, N_CTX_TILES, LAT_TILES_PER_SEQ), 0, 0)
    hb = TM // HALO
    row = lambda w: pl.BlockSpec((TM, w), lambda i: (i, 0))
    return pl.pallas_call(
        _conv_router_kernel,
        out_shape=[jax.ShapeDtypeStruct((T_ALL, D), F32), jax.ShapeDtypeStruct((T_ALL, D), F32),
                   jax.ShapeDtypeStruct((T_ALL, LANE), jnp.int32), jax.ShapeDtypeStruct((T_ALL, LANE), F32),
                   jax.ShapeDtypeStruct((8, LANE), F32)],
        grid=(N_TILES,),
        in_specs=[row(D),
                  pl.BlockSpec((HALO, D), lambda i: (jnp.maximum(i * hb - 1, 0), 0)),
                  pl.BlockSpec((HALO, D), lambda i: (jnp.minimum((i + 1) * hb, T_ALL // HALO - 1), 0)),
                  pl.BlockSpec((None, None, N_ADA, D), ada_map),
                  _const_spec((1, D)), _const_spec((1, D)),
                  _const_spec((D, 2 * D)), _const_spec((1, 2 * D)),
                  _const_spec((CONV_W, D)), _const_spec((1, D)),
                  _const_spec((1, D)), _const_spec((1, D)),
                  _const_spec((D, D)), _const_spec((1, D)),
                  _const_spec((D, LANE))],
        out_specs=[row(D), row(D), row(LANE), row(LANE), _const_spec((8, LANE))],
        scratch_shapes=[pltpu.VMEM((EXT, D), F32), pltpu.VMEM((TM, D), F32), pltpu.VMEM((TM, D), BF16),
                        pltpu.VMEM((8, LANE), F32)],
        compiler_params=pltpu.CompilerParams(
            dimension_semantics=("arbitrary",), vmem_limit_bytes=VMEM_BIG),
        name="conv_router",
    )(x, x, x, ada, gm, gf, cw["w1"], cw["b1"], cw["wdw"], cw["bdw"], cw["gln"], cw["bln"],
      cw["w2"], cw["b2"], cw["wr"])


def _dispatch_kernel(pos_ref, h_ref, hs_in_ref, hs_ref, sem):
    del hs_in_ref
    base = pl.program_id(0) * TM

    def body(t, carry):
        for s in range(2):
            p = pos_ref[2 * (base + t) + s]
            pltpu.make_async_copy(h_ref.at[pl.ds(t, 1), :], hs_ref.at[pl.ds(p, 1), :], sem).start()
        return carry

    lax.fori_loop(0, TM, body, 0)
    for s in range(2):
        pltpu.make_async_copy(h_ref, hs_ref.at[pl.ds(0, TM), :], sem).wait()


def _dispatch_call(pos, h, hs0):
    return pl.pallas_call(
        _dispatch_kernel,
        out_shape=jax.ShapeDtypeStruct((MOE_ROWS, D), F32),
        grid_spec=pltpu.PrefetchScalarGridSpec(
            num_scalar_prefetch=1, grid=(N_TILES,),
            in_specs=[pl.BlockSpec((TM, D), lambda i, p: (i, 0)), pl.BlockSpec(memory_space=pl.ANY)],
            out_specs=pl.BlockSpec(memory_space=pl.ANY),
            scratch_shapes=[pltpu.SemaphoreType.DMA(())]),
        input_output_aliases={2: 0},
        compiler_params=pltpu.CompilerParams(dimension_semantics=("arbitrary",), has_side_effects=True),
        name="moe_dispatch",
    )(pos, h, hs0)


def _moe_kernel(se_ref, row0_ref, nsub_ref, hs_in_ref, wg_ref, wu_ref, wd_ref, ys_ref,
                hb_ref, acc_ref, stage, wgb, wub, wdb, sem_in, sem_out):
    del se_ref
    j = pl.program_id(0)
    f = pl.program_id(1)
    nsub = nsub_ref[j]
    row0 = row0_ref[j]

    def load(s, slot):
        src = hs_in_ref.at[pl.ds(pl.multiple_of(row0 + s * MOE_SUB, MOE_SUB), MOE_SUB), :]
        return pltpu.make_async_copy(src, stage.at[slot], sem_in.at[slot])

    def store(s):
        r = pl.multiple_of(s * MOE_SUB, MOE_SUB)
        dst = ys_ref.at[pl.ds(pl.multiple_of(row0 + s * MOE_SUB, MOE_SUB), MOE_SUB), :]
        return pltpu.make_async_copy(acc_ref.at[pl.ds(r, MOE_SUB), :], dst, sem_out)

    def part(r):
        hh = hb_ref[pl.ds(r, MOE_SUB), :]
        out = None
        for c0 in range(0, MOE_TF, MOE_TC):
            gg = _dot(hh, wgb[:, c0:c0 + MOE_TC])
            uu = _dot(hh, wub[:, c0:c0 + MOE_TC])
            a = (gg * _sigmoid(gg) * uu).astype(BF16)
            piece = _dot(a, wdb[c0:c0 + MOE_TC, :])
            out = piece if out is None else out + piece
        return out

    def for_each_subtile(fn):
        def pair(p, carry):
            fn(2 * p)
            fn(2 * p + 1)
            return carry

        lax.fori_loop(0, nsub // 2, pair, 0)

        @pl.when(nsub % 2 == 1)
        def _():
            fn(nsub - 1)

    @pl.when(nsub > 0)
    def _():
        wgb[...] = wg_ref[...].astype(BF16)
        wub[...] = wu_ref[...].astype(BF16)
        wdb[...] = wd_ref[...].astype(BF16)

        @pl.when(f == 0)
        def _():
            for d in range(MOE_DEPTH):
                @pl.when(d < nsub)
                def _():
                    load(d, d).start()

            def first(s):
                slot = s % MOE_SLOTS
                load(s, slot).wait()

                @pl.when(s + MOE_DEPTH < nsub)
                def _():
                    load(s + MOE_DEPTH, (s + MOE_DEPTH) % MOE_SLOTS).start()

                r = pl.multiple_of(s * MOE_SUB, MOE_SUB)
                hb_ref[pl.ds(r, MOE_SUB), :] = stage[slot].astype(BF16)
                acc_ref[pl.ds(r, MOE_SUB), :] = part(r)

            for_each_subtile(first)

        @pl.when((f > 0) & (f < MOE_NF - 1))
        def _():
            def middle(s):
                r = pl.multiple_of(s * MOE_SUB, MOE_SUB)
                acc_ref[pl.ds(r, MOE_SUB), :] += part(r)

            for_each_subtile(middle)

        @pl.when(f == MOE_NF - 1)
        def _():
            def final(s):
                r = pl.multiple_of(s * MOE_SUB, MOE_SUB)
                acc_ref[pl.ds(r, MOE_SUB), :] += part(r)
                store(s).start()

            for_each_subtile(final)

            def drain(s, carry):
                store(s).wait()
                return carry

            lax.fori_loop(0, nsub, drain, 0)


def _moe_call(st_e, st_row0, st_nsub, hs, wg, wu, wd, lidx):
    f_eff = lambda j, f, ns: jnp.where(ns[j] > 0, f, MOE_NF - 1)
    return pl.pallas_call(
        _moe_kernel,
        out_shape=jax.ShapeDtypeStruct((MOE_ROWS, D), F32),
        grid_spec=pltpu.PrefetchScalarGridSpec(
            num_scalar_prefetch=3, grid=(MOE_NS, MOE_NF),
            in_specs=[
                pl.BlockSpec(memory_space=pl.ANY),
                pl.BlockSpec((None, None, D, MOE_TF), lambda j, f, se, r0, ns: (lidx, se[j], 0, f_eff(j, f, ns))),
                pl.BlockSpec((None, None, D, MOE_TF), lambda j, f, se, r0, ns: (lidx, se[j], 0, f_eff(j, f, ns))),
                pl.BlockSpec((None, None, MOE_TF, D), lambda j, f, se, r0, ns: (lidx, se[j], f_eff(j, f, ns), 0)),
            ],
            out_specs=pl.BlockSpec(memory_space=pl.ANY),
            scratch_shapes=[pltpu.VMEM((MOE_R, D), BF16), pltpu.VMEM((MOE_R, D), F32),
                            pltpu.VMEM((MOE_SLOTS, MOE_SUB, D), F32),
                            pltpu.VMEM((D, MOE_TF), BF16), pltpu.VMEM((D, MOE_TF), BF16),
                            pltpu.VMEM((MOE_TF, D), BF16),
                            pltpu.SemaphoreType.DMA((MOE_SLOTS,)), pltpu.SemaphoreType.DMA(())]),
        input_output_aliases={3: 0},
        compiler_params=pltpu.CompilerParams(
            dimension_semantics=("arbitrary", "arbitrary"), vmem_limit_bytes=VMEM_BIG,
            has_side_effects=True),
        name="moe_experts",
    )(st_e, st_row0, st_nsub, hs, wg, wu, wd)


def _combine_kernel(split, pos_ref, x_ref, ada_ref, rw_ref, ys_ref, *rest):
    if split:
        oc_ref, ol_ref, buf, sem = rest
    else:
        o_ref, buf, sem = rest
    g = pl.program_id(0)

    @pl.when(g < N_TILES)
    def _():
        slot = g % 2
        base = g * TM

        def body(t, carry):
            for s in range(2):
                p = pos_ref[2 * (base + t) + s]
                pltpu.make_async_copy(ys_ref.at[pl.ds(p, 1), :], buf.at[slot, s, pl.ds(t, 1), :],
                                      sem.at[slot]).start()
            return carry

        lax.fori_loop(0, TM, body, 0)

    @pl.when(g > 0)
    def _():
        i = g - 1
        slot = i % 2
        for s in range(2):
            pltpu.make_async_copy(ys_ref.at[pl.ds(0, TM), :], buf.at[slot, s], sem.at[slot]).wait()
        w1 = rw_ref[:, 0:1]
        w2 = rw_ref[:, 1:2]
        y = w1 * buf[slot, 0] + w2 * buf[slot, 1]
        out = x_ref[...] + ada_ref[5:6, :] * y
        if split:
            @pl.when(i < N_CTX_TILES)
            def _():
                oc_ref[...] = out

            @pl.when(i >= N_CTX_TILES)
            def _():
                ol_ref[...] = out
        else:
            o_ref[...] = out


def _combine_call(split, pos, x1, ada, layer, rw, ys):
    prev = lambda g: jnp.maximum(g - 1, 0)
    ada_map = lambda g, p: (layer, _tile_cond(prev(g), N_CTX_TILES, LAT_TILES_PER_SEQ), 0, 0)
    if split:
        out_shape = [jax.ShapeDtypeStruct((T_CTX, D), F32), jax.ShapeDtypeStruct((T_LAT, D), F32)]
        out_specs = [pl.BlockSpec((TM, D), lambda g, p: (jnp.minimum(prev(g), N_CTX_TILES - 1), 0)),
                     pl.BlockSpec((TM, D), lambda g, p: (jnp.maximum(prev(g) - N_CTX_TILES, 0), 0))]
    else:
        out_shape = jax.ShapeDtypeStruct((T_ALL, D), F32)
        out_specs = pl.BlockSpec((TM, D), lambda g, p: (prev(g), 0))
    return pl.pallas_call(
        functools.partial(_combine_kernel, split),
        out_shape=out_shape,
        grid_spec=pltpu.PrefetchScalarGridSpec(
            num_scalar_prefetch=1, grid=(N_TILES + 1,),
            in_specs=[pl.BlockSpec((TM, D), lambda g, p: (prev(g), 0)),
                      pl.BlockSpec((None, None, N_ADA, D), ada_map),
                      pl.BlockSpec((TM, LANE), lambda g, p: (prev(g), 0)),
                      pl.BlockSpec(memory_space=pl.ANY)],
            out_specs=out_specs,
            scratch_shapes=[pltpu.VMEM((2, 2, TM, D), F32), pltpu.SemaphoreType.DMA((2,))]),
        compiler_params=pltpu.CompilerParams(dimension_semantics=("arbitrary",)),
        name="moe_combine",
    )(pos, x1, ada, rw, ys)


def _moe_layer(x1, h2, ri, rw, cnt, ada, layer, wg, wu, wd, lidx, split):
    idx = ri[:, 0:2]
    rank = ri[:, 2:4]
    counts = cnt[0, :N_EXP].astype(jnp.int32)
    padded = ((counts + MOE_SUB - 1) // MOE_SUB) * MOE_SUB
    ends = jnp.cumsum(padded)
    starts = ends - padded
    sel = idx[:, :, None] == jnp.arange(N_EXP, dtype=jnp.int32)[None, None, :]
    pos = (jnp.sum(jnp.where(sel, starts[None, None, :], 0), axis=-1) + rank).reshape(-1).astype(jnp.int32)
    n_st = (padded + MOE_R - 1) // MOE_R
    st_end = jnp.cumsum(n_st)
    st_begin = st_end - n_st
    j = jnp.arange(MOE_NS, dtype=jnp.int32)
    se = jnp.sum((j[:, None] >= st_end[None, :]).astype(jnp.int32), axis=1)
    used = se < N_EXP
    se_c = jnp.minimum(se, N_EXP - 1)
    k = j - st_begin[se_c]
    row0 = jnp.where(used, starts[se_c] + k * MOE_R, 0).astype(jnp.int32)
    nsub = jnp.where(used, jnp.minimum(MOE_R, padded[se_c] - k * MOE_R) // MOE_SUB, 0).astype(jnp.int32)
    last = jnp.maximum(jnp.sum(used.astype(jnp.int32)) - 1, 0)
    se_eff = jnp.where(used, se_c, se_c[last]).astype(jnp.int32)
    hs = _dispatch_call(pos, h2, jnp.zeros((MOE_ROWS, D), F32))
    ys = _moe_call(se_eff, row0, nsub, hs, wg, wu, wd, lidx)
    return _combine_call(split, pos, x1, ada, layer, rw, ys)


def _rope_tables():
    t = jnp.arange(DEC_SEQ)
    rows = (t // GRID_W).astype(F32)
    cols = (t % GRID_W).astype(F32)

    def axis_tabs(pos, d):
        inv = ROPE_BASE ** (-jnp.arange(0, d, 2, dtype=F32) / d)
        ang = pos[:, None] * inv[None, :]
        cos, sin = jnp.cos(ang), jnp.sin(ang)
        z = jnp.zeros_like(sin)
        return (jnp.concatenate([cos, cos], -1), jnp.concatenate([-sin, z], -1), jnp.concatenate([z, sin], -1))

    def axial(d):
        r = axis_tabs(rows, d // 2)
        c = axis_tabs(cols, d // 2)
        return [jnp.concatenate([a, b], -1) for a, b in zip(r, c)]

    tabs_a = [jnp.tile(tb, (1, A_HEADS)) for tb in axial(A_DH)]
    cb, sub, sdb = axial(B_ROPE)
    ones = jnp.ones((DEC_SEQ, B_NOPE), F32)
    zer = jnp.zeros((DEC_SEQ, B_NOPE), F32)
    pad = jnp.zeros((DEC_SEQ, SLOT - B_QK), F32)
    slot = lambda first, tb: jnp.tile(jnp.concatenate([first, tb, pad], -1), (1, B_HEADS))
    tabs_b = [slot(ones, cb), slot(zer, sub), slot(zer, sdb)]
    return tabs_a + tabs_b


def _attn_weights(e, w_in_attn, g_qk_a_q, g_qk_a_k, g_mla_q_a, w_mla_q_b, g_mla_kv_a, w_mla_kv_b,
                  g_qk_b_q, g_qk_b_k):
    w_in = jnp.pad(w_in_attn[e], ((0, 0), (0, IN_PAD - IN_DIM))).astype(BF16)
    w_qb = jnp.pad(w_mla_q_b[e].reshape(B_Q_LORA, B_HEADS, B_QK), ((0, 0), (0, 0), (0, SLOT - B_QK)))
    w_qb = w_qb.reshape(B_Q_LORA, B_HEADS * SLOT).astype(BF16)
    wkv = w_mla_kv_b[e].reshape(B_KV_LORA, B_HEADS, B_NOPE + B_V)
    wk = jnp.pad(wkv[:, :, :B_NOPE], ((0, 0), (0, 0), (0, SLOT - B_NOPE))).reshape(B_KV_LORA, B_HEADS * SLOT)
    wv = wkv[:, :, B_NOPE:].reshape(B_KV_LORA, B_HEADS * B_V)
    w_kv = jnp.concatenate([wk, wv], axis=1).astype(BF16)
    place = jnp.zeros((LANE, B_HEADS, SLOT), F32)
    ii = jnp.arange(B_ROPE)
    place = place.at[ii, :, B_NOPE + ii].set(1.0).reshape(LANE, B_HEADS * SLOT).astype(BF16)
    slot_gain = lambda g: jnp.tile(jnp.pad(g, (0, SLOT - B_QK)), B_HEADS).reshape(1, B_HEADS * SLOT)
    return dict(
        w_in=w_in, w_qb=w_qb, w_kv=w_kv, place=place,
        gqa=jnp.tile(g_qk_a_q[e], A_HEADS).reshape(1, 512),
        gka=jnp.tile(g_qk_a_k[e], A_KV).reshape(1, LANE),
        gql=g_mla_q_a[e].reshape(1, B_Q_LORA), gkl=g_mla_kv_a[e].reshape(1, B_KV_LORA),
        gqb=slot_gain(g_qk_b_q[e]), gkb=slot_gain(g_qk_b_k[e]))


def _pair_variants_host(x):
    h0, h1 = x[..., :64], x[..., 64:]
    z = jnp.zeros_like(h0)
    return jnp.concatenate([h0, z, z, h0, h1, z, z, h1], axis=-1)


def kernel(x_prompt, x_sample, c, cache_k_win, cache_v_win, cache_ckv, cache_krope, c_ctx, w_ada, b_ada, g_norm_mix, g_norm_ffn, w_in_attn, g_qk_a_q, g_qk_a_k, sink_a, g_mla_q_a, w_mla_q_b, g_mla_kv_a, w_mla_kv_b, g_qk_b_q, g_qk_b_k, w_out_attn, w_ffn_gate, w_ffn_up, w_ffn_down, w_conv_pw1, b_conv_pw1, w_conv_dw, b_conv_dw, g_conv_ln, b_conv_ln, w_conv_pw2, b_conv_pw2, w_router, w_moe_gate, w_moe_up, w_moe_down):
    x = jnp.concatenate([x_prompt.reshape(T_CTX, D), x_sample.reshape(T_LAT, D)], axis=0)
    cond8 = jnp.concatenate([c_ctx[None, :], c, jnp.zeros((8 - 1 - DEC_BATCH, D), F32)], axis=0)
    ada = _ada_call(cond8, w_ada, b_ada).reshape(DEPTH, 8, N_ADA, D)
    tabs = _rope_tables()

    wo_b = w_out_attn.astype(BF16)
    wg_b = w_ffn_gate.astype(BF16)
    wu_b = w_ffn_up.astype(BF16)
    wd_b = w_ffn_down.astype(BF16)
    w1_b = w_conv_pw1.astype(BF16)
    w2_b = w_conv_pw2.astype(BF16)

    new_k, new_v, new_ckv, new_kr = [], [], [], []
    for layer in range(DEPTH):
        gm = g_norm_mix[layer].reshape(1, D)
        gf = g_norm_ffn[layer].reshape(1, D)
        if layer % 2 == 0:
            e = layer // 2
            lw = _attn_weights(e, w_in_attn, g_qk_a_q, g_qk_a_k, g_mla_q_a, w_mla_q_b, g_mla_kv_a,
                               w_mla_kv_b, g_qk_b_q, g_qk_b_k)
            sink = sink_a[e].reshape(A_HEADS)
            (qa, ka4, va4, qb, kb, vlo, vhi, kaf, vaf, ckvf, krf) = _pre_attn_call(False, x, ada, layer, gm, lw, None)
            o_ctx = _attn_ctx_call(sink, qa, ka4, va4, qb, kb, vlo, vhi)
            new_k.append(kaf.reshape(BATCH, SEQ, A_KV, A_DH))
            new_v.append(vaf.reshape(BATCH, SEQ, A_KV, A_DH))
            new_ckv.append(ckvf.reshape(BATCH, SEQ, B_KV_LORA))
            new_kr.append(krf[:, :B_ROPE].reshape(BATCH, SEQ, B_ROPE))
            (qa, ka4, va4, qb, kb, vlo, vhi) = _pre_attn_call(True, x, ada, layer, gm, lw, tabs)
            kr_c = jnp.pad(cache_krope[:, e].reshape(DEC_BATCH * PAST, B_ROPE), ((0, 0), (0, LANE - B_ROPE)))
            kb_c, vlo_c, vhi_c = _cache_kv_call(cache_ckv[:, e].reshape(DEC_BATCH * PAST, B_KV_LORA), kr_c, lw)
            pad_rows = lambda a: jnp.pad(a.reshape(DEC_BATCH, DEC_SEQ, 512), ((0, 0), (QB, QB), (0, 0)))
            ka4c = _pair_variants_host(cache_k_win[:, e].reshape(DEC_BATCH, PAST, LANE)).astype(BF16)
            va4c = _pair_variants_host(cache_v_win[:, e].reshape(DEC_BATCH, PAST, LANE)).astype(BF16)
            cat = lambda a, b, w: jnp.concatenate(
                [a.reshape(DEC_BATCH, PAST, w), b.reshape(DEC_BATCH, DEC_SEQ, w)], axis=1)
            o_lat = _attn_lat_call(sink, qa, pad_rows(ka4), pad_rows(va4), ka4c, va4c, qb,
                                   cat(kb_c, kb, 1024), cat(vlo_c, vlo, 512), cat(vhi_c, vhi, 512))
            o = jnp.concatenate([o_ctx, o_lat], axis=0)
            x = _post_attn_call(x, o, ada, layer, gf, wo_b[e], wg_b[e], wu_b[e], wd_b[e])
        else:
            o_ = layer // 2
            cw = dict(w1=w1_b[o_], b1=b_conv_pw1[o_].reshape(1, 2 * D), wdw=w_conv_dw[o_],
                      bdw=b_conv_dw[o_].reshape(1, D), gln=g_conv_ln[o_].reshape(1, D),
                      bln=b_conv_ln[o_].reshape(1, D), w2=w2_b[o_], b2=b_conv_pw2[o_].reshape(1, D),
                      wr=jnp.pad(w_router[o_], ((0, 0), (0, LANE - N_EXP))))
            x1, h2, ri, rw, cnt = _conv_router_call(x, ada, layer, gm, gf, cw)
            x = _moe_layer(x1, h2, ri, rw, cnt, ada, layer, w_moe_gate, w_moe_up, w_moe_down, o_,
                           split=(layer == DEPTH - 1))

    y_prompt = x[0].reshape(BATCH, SEQ, D)
    y_sample = x[1].reshape(DEC_BATCH, DEC_SEQ, D)
    return (y_prompt, y_sample, jnp.stack(new_k, axis=1), jnp.stack(new_v, axis=1),
            jnp.stack(new_ckv, axis=1), jnp.stack(new_kr, axis=1))
```

```python
import functools

import jax
import jax.numpy as jnp
from jax import lax
from jax.experimental import pallas as pl
from jax.experimental.pallas import tpu as pltpu

F32 = jnp.float32
BF16 = jnp.bfloat16

D = 1024
BATCH = 32
SEQ = 256
DEPTH = 4
DEC_BATCH = 2
DEC_SEQ = 1024
PAST = 256
GRID_W = 64
N_EVEN = 2
A_HEADS = 8
A_KV = 2
A_DH = 64
WINDOW = 128
B_HEADS = 8
B_Q_LORA = 256
B_KV_LORA = 128
B_NOPE = 64
B_ROPE = 32
B_QK = B_NOPE + B_ROPE
B_V = 64
IN_DIM = 1184
IN_PAD = 1280
CONV_W = 31
D_FF = 2816
N_EXP = 8
D_FFE = 3584
ROPE_BASE = 10000.0
EPS = 1e-6
N_ADA = 6

T_CTX = BATCH * SEQ
T_LAT = DEC_BATCH * DEC_SEQ
T_ALL = T_CTX + T_LAT
TM = 256
N_TILES = T_ALL // TM
N_CTX_TILES = T_CTX // TM
LAT_TILES_PER_SEQ = DEC_SEQ // TM
LANE = 128
SLOT = 128

MOE_SUB = 256
MOE_R = 3072
MOE_TF = 512
MOE_NF = D_FFE // MOE_TF
MOE_TC = 512
MOE_ROWS = 2 * T_ALL + N_EXP * MOE_SUB
MOE_NS = MOE_ROWS // MOE_R + N_EXP

FFN_TM = 512
FFN_CH = 1408

VMEM_BIG = 56 * 1024 * 1024


def _dot(a, b):
    return jnp.dot(a, b, preferred_element_type=F32)


def _dot_nt(a, b):
    return lax.dot_general(a, b, (((1,), (1,)), ((), ())), preferred_element_type=F32)


def _split(a):
    hi = a.astype(BF16)
    lo = (a - hi.astype(F32)).astype(BF16)
    return hi, lo


def _dot_x3(a, b):
    ah, al = _split(a)
    bh, bl = _split(b)
    return _dot(ah, bh) + (_dot(ah, bl) + _dot(al, bh))


def _sigmoid(x):
    return 1.0 / (1.0 + jnp.exp(-x))


def _rms(x, n):
    return x * lax.rsqrt(jnp.sum(x * x, axis=-1, keepdims=True) * (1.0 / n) + EPS)


def _modulate(x, g, shift, scale):
    return _rms(x, D) * g * (1.0 + scale) + shift


def _lo_mask(rows):
    return lax.broadcasted_iota(jnp.int32, (rows, LANE), 1) < 64


def _ada_kernel(cond_ref, w_ref, b_ref, o_ref):
    c = cond_ref[...]
    s = c * _sigmoid(c)
    o_ref[...] = _dot_x3(s, w_ref[...]) + b_ref[...]


def _ada_call(cond8, w_ada, b_ada):
    tn = 1024
    return pl.pallas_call(
        _ada_kernel,
        out_shape=jax.ShapeDtypeStruct((DEPTH, 8, N_ADA * D), F32),
        grid=(DEPTH, N_ADA * D // tn),
        in_specs=[
            pl.BlockSpec((8, D), lambda l, j: (0, 0)),
            pl.BlockSpec((None, D, tn), lambda l, j: (l, 0, j)),
            pl.BlockSpec((None, 1, tn), lambda l, j: (l, 0, j)),
        ],
        out_specs=pl.BlockSpec((None, 8, tn), lambda l, j: (l, 0, j)),
        compiler_params=pltpu.CompilerParams(
            dimension_semantics=("arbitrary", "arbitrary"), vmem_limit_bytes=VMEM_BIG),
        name="ada",
    )(cond8, w_ada, b_ada.reshape(DEPTH, 1, N_ADA * D))


def _rope(x, cos, sup, sdn, shift):
    up = pltpu.roll(x, LANE - shift, 1)
    dn = pltpu.roll(x, shift, 1)
    return x * cos + up * sup + dn * sdn


def _norm64_block(blk, lo):
    sq = blk * blk
    s_lo = jnp.sum(jnp.where(lo, sq, 0.0), axis=-1, keepdims=True)
    s_hi = jnp.sum(jnp.where(lo, 0.0, sq), axis=-1, keepdims=True)
    r = jnp.where(lo, lax.rsqrt(s_lo * (1.0 / A_DH) + EPS), lax.rsqrt(s_hi * (1.0 / A_DH) + EPS))
    return blk * r


def _pair_variants(x, lo):
    r = pltpu.roll(x, 64, 1)
    return (jnp.where(lo, x, 0.0), jnp.where(lo, 0.0, r), jnp.where(lo, r, 0.0), jnp.where(lo, 0.0, x))


def _mla_kv(ckv_n, kr, wkv_ref, place_ref, gk_ref, lo, kb_ref, vlo_ref, vhi_ref, rope_tabs):
    kv = _dot(ckv_n.astype(BF16), wkv_ref[...])
    kr_hi, kr_lo = _split(kr)
    place = place_ref[...]
    krs = _dot(kr_hi, place) + _dot(kr_lo, place)
    for h in range(B_HEADS):
        sl = slice(h * SLOT, (h + 1) * SLOT)
        blk = kv[:, sl] + krs[:, sl]
        blk = _rms(blk, B_QK) * gk_ref[:, sl]
        if rope_tabs is not None:
            cos, sup, sdn = rope_tabs
            blk = _rope(blk, cos[:, sl], sup[:, sl], sdn[:, sl], 8)
        kb_ref[:, sl] = blk.astype(BF16)
    for j in range(B_HEADS // 2):
        sl = slice(j * LANE, (j + 1) * LANE)
        v = kv[:, B_HEADS * SLOT + j * LANE: B_HEADS * SLOT + (j + 1) * LANE]
        vlo_ref[:, sl] = jnp.where(lo, v, 0.0).astype(BF16)
        vhi_ref[:, sl] = jnp.where(lo, 0.0, v).astype(BF16)


def _pre_attn_kernel(rope, *refs):
    if rope:
        (x_ref, ada_ref, g_ref, win_ref, wqb_ref, wkv_ref, place_ref,
         gqa_ref, gka_ref, gql_ref, gkl_ref, gqb_ref, gkb_ref,
         ca_ref, sua_ref, sda_ref, cb_ref, sub_ref, sdb_ref,
         qa_o, ka4_o, va4_o, qb_o, kb_o, vlo_o, vhi_o) = refs
    else:
        (x_ref, ada_ref, g_ref, win_ref, wqb_ref, wkv_ref, place_ref,
         gqa_ref, gka_ref, gql_ref, gkl_ref, gqb_ref, gkb_ref,
         qa_o, ka4_o, va4_o, qb_o, kb_o, vlo_o, vhi_o,
         kaf_o, vaf_o, ckvf_o, krf_o) = refs
    lo = _lo_mask(TM)
    h = _modulate(x_ref[...], g_ref[...], ada_ref[0:1, :], ada_ref[1:2, :])
    p = _dot(h.astype(BF16), win_ref[...])

    for j in range(4):
        sl = slice(j * LANE, (j + 1) * LANE)
        blk = _norm64_block(p[:, sl], lo) * gqa_ref[:, sl]
        if rope:
            blk = _rope(blk, ca_ref[:, sl], sua_ref[:, sl], sda_ref[:, sl], 16)
        qa_o[:, sl] = (blk * (A_DH ** -0.5)).astype(BF16)
    ka = _norm64_block(p[:, 512:640], lo) * gka_ref[...]
    va = p[:, 640:768]
    if not rope:
        kaf_o[...] = ka
        vaf_o[...] = va
    else:
        ka = _rope(ka, ca_ref[:, 0:LANE], sua_ref[:, 0:LANE], sda_ref[:, 0:LANE], 16)
    for n, (kk, vv) in enumerate(zip(_pair_variants(ka, lo), _pair_variants(va, lo))):
        sl = slice(n * LANE, (n + 1) * LANE)
        ka4_o[:, sl] = kk.astype(BF16)
        va4_o[:, sl] = vv.astype(BF16)

    cq = _rms(p[:, 768:1024], B_Q_LORA) * gql_ref[...]
    qb = _dot(cq.astype(BF16), wqb_ref[...])
    for hh in range(B_HEADS):
        sl = slice(hh * SLOT, (hh + 1) * SLOT)
        blk = _rms(qb[:, sl], B_QK) * gqb_ref[:, sl]
        if rope:
            blk = _rope(blk, cb_ref[:, sl], sub_ref[:, sl], sdb_ref[:, sl], 8)
        qb_o[:, sl] = (blk * (B_QK ** -0.5)).astype(BF16)

    ckv = _rms(p[:, 1024:1152], B_KV_LORA) * gkl_ref[...]
    kr = p[:, 1152:1280]
    if not rope:
        ckvf_o[...] = ckv
        krf_o[...] = kr
    tabs = (cb_ref, sub_ref, sdb_ref) if rope else None
    _mla_kv(ckv, kr, wkv_ref, place_ref, gkb_ref, lo, kb_o, vlo_o, vhi_o, tabs)


def _const_spec(shape):
    nd = len(shape)
    return pl.BlockSpec(shape, lambda *a: (0,) * nd)


def _pre_attn_call(rope, x, ada, layer, g, lw, tabs):
    n_tiles = (T_LAT if rope else T_CTX) // TM
    off = N_CTX_TILES if rope else 0
    n_rows = n_tiles * TM
    if rope:
        ada_map = lambda i: (layer, 1 + i // LAT_TILES_PER_SEQ, 0, 0)
    else:
        ada_map = lambda i: (layer, 0, 0, 0)
    in_specs = [
        pl.BlockSpec((TM, D), lambda i: (i + off, 0)),
        pl.BlockSpec((None, None, N_ADA, D), ada_map),
        _const_spec((1, D)),
        _const_spec((D, IN_PAD)),
        _const_spec((B_Q_LORA, B_HEADS * SLOT)),
        _const_spec((B_KV_LORA, B_HEADS * SLOT + B_HEADS * B_V)),
        _const_spec((LANE, B_HEADS * SLOT)),
        _const_spec((1, 512)), _const_spec((1, LANE)), _const_spec((1, B_Q_LORA)),
        _const_spec((1, B_KV_LORA)), _const_spec((1, B_HEADS * SLOT)), _const_spec((1, B_HEADS * SLOT)),
    ]
    args = [x, ada, g, lw["w_in"], lw["w_qb"], lw["w_kv"], lw["place"],
            lw["gqa"], lw["gka"], lw["gql"], lw["gkl"], lw["gqb"], lw["gkb"]]
    if rope:
        tmap = lambda i: (i % LAT_TILES_PER_SEQ, 0)
        in_specs += [pl.BlockSpec((TM, 512), tmap)] * 3 + [pl.BlockSpec((TM, B_HEADS * SLOT), tmap)] * 3
        args += list(tabs)
    row = lambda w: pl.BlockSpec((TM, w), lambda i: (i, 0))
    out_shape = [jax.ShapeDtypeStruct((n_rows, w), BF16) for w in (512, 512, 512, 1024, 1024, 512, 512)]
    out_specs = [row(w) for w in (512, 512, 512, 1024, 1024, 512, 512)]
    if not rope:
        out_shape += [jax.ShapeDtypeStruct((n_rows, LANE), F32)] * 4
        out_specs += [row(LANE)] * 4
    return pl.pallas_call(
        functools.partial(_pre_attn_kernel, rope),
        out_shape=out_shape, grid=(n_tiles,), in_specs=in_specs, out_specs=out_specs,
        compiler_params=pltpu.CompilerParams(
            dimension_semantics=("arbitrary",), vmem_limit_bytes=VMEM_BIG),
        name="pre_attn_lat" if rope else "pre_attn_ctx",
    )(*args)


def _cache_kv_kernel(ckv_ref, kr_ref, wkv_ref, place_ref, gkb_ref, kb_o, vlo_o, vhi_o):
    lo = _lo_mask(PAST)
    _mla_kv(ckv_ref[...], kr_ref[...], wkv_ref, place_ref, gkb_ref, lo, kb_o, vlo_o, vhi_o, None)


def _cache_kv_call(ckv, kr, lw):
    rows = DEC_BATCH * PAST
    row = lambda w: pl.BlockSpec((PAST, w), lambda i: (i, 0))
    return pl.pallas_call(
        _cache_kv_kernel,
        out_shape=[jax.ShapeDtypeStruct((rows, 1024), BF16), jax.ShapeDtypeStruct((rows, 512), BF16),
                   jax.ShapeDtypeStruct((rows, 512), BF16)],
        grid=(DEC_BATCH,),
        in_specs=[row(LANE), row(LANE),
                  _const_spec((B_KV_LORA, B_HEADS * SLOT + B_HEADS * B_V)),
                  _const_spec((LANE, B_HEADS * SLOT)), _const_spec((1, B_HEADS * SLOT))],
        out_specs=[row(1024), row(512), row(512)],
        compiler_params=pltpu.CompilerParams(dimension_semantics=("arbitrary",)),
        name="cache_kv",
    )(ckv, kr, lw["w_kv"], lw["place"], lw["gkb"])


def _softmax_pv(scores, values, sink):
    m = jnp.max(scores[0], axis=-1, keepdims=True)
    for s in scores[1:]:
        m = jnp.maximum(m, jnp.max(s, axis=-1, keepdims=True))
    if sink is not None:
        m = jnp.maximum(m, sink)
    ps = [jnp.exp(s - m) for s in scores]
    den = jnp.sum(ps[0], axis=-1, keepdims=True)
    for p in ps[1:]:
        den = den + jnp.sum(p, axis=-1, keepdims=True)
    if sink is not None:
        den = den + jnp.exp(sink - m)
    inv = 1.0 / den
    out = _dot((ps[0] * inv).astype(BF16), values[0])
    for p, v in zip(ps[1:], values[1:]):
        out = out + _dot((p * inv).astype(BF16), v)
    return out


def _attn_ctx_kernel(sink_ref, qa_ref, ka4_ref, va4_ref, qb_ref, kb_ref, vlo_ref, vhi_ref, o_ref):
    for j in range(4):
        q = qa_ref[:, j * LANE:(j + 1) * LANE]
        acc = None
        for par in range(2):
            c = (2 * (j // 2) + par) * LANE
            s = _dot_nt(q, ka4_ref[:, c:c + LANE])
            o = _softmax_pv([s], [va4_ref[:, c:c + LANE]], sink_ref[2 * j + par])
            acc = o if acc is None else acc + o
        o_ref[:, j * LANE:(j + 1) * LANE] = acc.astype(BF16)
    for j in range(4):
        acc = None
        for par, v_ref in enumerate((vlo_ref, vhi_ref)):
            hh = 2 * j + par
            s = _dot_nt(qb_ref[:, hh * SLOT:(hh + 1) * SLOT], kb_ref[:, hh * SLOT:(hh + 1) * SLOT])
            o = _softmax_pv([s], [v_ref[:, j * LANE:(j + 1) * LANE]], None)
            acc = o if acc is None else acc + o
        o_ref[:, 512 + j * LANE:512 + (j + 1) * LANE] = acc.astype(BF16)


def _attn_ctx_call(sink, qa, ka4, va4, qb, kb, vlo, vhi):
    row = lambda w: pl.BlockSpec((SEQ, w), lambda b: (b, 0))
    return pl.pallas_call(
        _attn_ctx_kernel,
        out_shape=jax.ShapeDtypeStruct((T_CTX, D), BF16),
        grid=(BATCH,),
        in_specs=[pl.BlockSpec(memory_space=pltpu.SMEM),
                  row(512), row(512), row(512), row(1024), row(1024), row(512), row(512)],
        out_specs=row(D),
        compiler_params=pltpu.CompilerParams(dimension_semantics=("arbitrary",)),
        name="attn_ctx",
    )(sink, qa, ka4, va4, qb, kb, vlo, vhi)


QB = 128
N_QB = DEC_SEQ // QB
KPAD = DEC_SEQ + 2 * QB
KALL = PAST + DEC_SEQ


def _attn_lat_kernel(sink_ref, qa_ref, ka4p_ref, va4p_ref, ka4c_ref, va4c_ref,
                     qb_ref, kb_ref, vlo_ref, vhi_ref, o_ref):
    n = pl.program_id(1)
    r = lax.broadcasted_iota(jnp.int32, (QB, 3 * QB), 0)
    c = lax.broadcasted_iota(jnp.int32, (QB, 3 * QB), 1)
    rel = c - QB - r
    kpos = (n - 1) * QB + c
    valid = (jnp.abs(rel) <= WINDOW) & (kpos >= 0) & (kpos < DEC_SEQ)
    k0 = pl.multiple_of(n * QB, QB)
    for j in range(4):
        q = qa_ref[:, j * LANE:(j + 1) * LANE]
        acc = None
        for par in range(2):
            cc = (2 * (j // 2) + par) * LANE
            s_loc = _dot_nt(q, ka4p_ref[pl.ds(k0, 3 * QB), cc:cc + LANE])
            s_loc = jnp.where(valid, s_loc, -jnp.inf)
            s_ctx = _dot_nt(q, ka4c_ref[:, cc:cc + LANE])
            o = _softmax_pv([s_ctx, s_loc],
                            [va4c_ref[:, cc:cc + LANE], va4p_ref[pl.ds(k0, 3 * QB), cc:cc + LANE]],
                            sink_ref[2 * j + par])
            acc = o if acc is None else acc + o
        o_ref[:, j * LANE:(j + 1) * LANE] = acc.astype(BF16)
    for j in range(4):
        acc = None
        for par, v_ref in enumerate((vlo_ref, vhi_ref)):
            hh = 2 * j + par
            s = _dot_nt(qb_ref[:, hh * SLOT:(hh + 1) * SLOT], kb_ref[:, hh * SLOT:(hh + 1) * SLOT])
            o = _softmax_pv([s], [v_ref[:, j * LANE:(j + 1) * LANE]], None)
            acc = o if acc is None else acc + o
        o_ref[:, 512 + j * LANE:512 + (j + 1) * LANE] = acc.astype(BF16)


def _attn_lat_call(sink, qa, ka4p, va4p, ka4c, va4c, qb, kball, vlo_all, vhi_all):
    qrow = lambda w: pl.BlockSpec((QB, w), lambda b, n: (b * N_QB + n, 0))
    per_b = lambda rows, w: pl.BlockSpec((None, rows, w), lambda b, n: (b, 0, 0))
    return pl.pallas_call(
        _attn_lat_kernel,
        out_shape=jax.ShapeDtypeStruct((T_LAT, D), BF16),
        grid=(DEC_BATCH, N_QB),
        in_specs=[pl.BlockSpec(memory_space=pltpu.SMEM),
                  qrow(512), per_b(KPAD, 512), per_b(KPAD, 512), per_b(PAST, 512), per_b(PAST, 512),
                  qrow(1024), per_b(KALL, 1024), per_b(KALL, 512), per_b(KALL, 512)],
        out_specs=qrow(D),
        compiler_params=pltpu.CompilerParams(dimension_semantics=("arbitrary", "arbitrary")),
        name="attn_lat",
    )(sink, qa, ka4p, va4p, ka4c, va4c, qb, kball, vlo_all, vhi_all)


def _post_attn_kernel(x_ref, o_ref, ada_ref, g_ref, wo_ref, wg_ref, wu_ref, wd_ref, y_ref):
    mix = _dot(o_ref[...], wo_ref[...])
    x1 = x_ref[...] + ada_ref[2:3, :] * mix
    h = _modulate(x1, g_ref[...], ada_ref[3:4, :], ada_ref[4:5, :]).astype(BF16)
    acc = None
    for c0 in range(0, D_FF, FFN_CH):
        gg = _dot(h, wg_ref[:, c0:c0 + FFN_CH])
        uu = _dot(h, wu_ref[:, c0:c0 + FFN_CH])
        a = (gg * _sigmoid(gg) * uu).astype(BF16)
        part = _dot(a, wd_ref[c0:c0 + FFN_CH, :])
        acc = part if acc is None else acc + part
    y_ref[...] = x1 + ada_ref[5:6, :] * acc


def _tile_cond(i, tiles_per_ctx, tiles_per_lat_seq):
    return jnp.where(i < tiles_per_ctx, 0, 1 + (i - tiles_per_ctx) // tiles_per_lat_seq)


def _post_attn_call(x, o, ada, layer, g, wo, wg, wu, wd):
    n_tiles = T_ALL // FFN_TM
    ada_map = lambda i: (layer, _tile_cond(i, T_CTX // FFN_TM, DEC_SEQ // FFN_TM), 0, 0)
    one = pl.Buffered(1)
    return pl.pallas_call(
        _post_attn_kernel,
        out_shape=jax.ShapeDtypeStruct((T_ALL, D), F32),
        grid=(n_tiles,),
        in_specs=[pl.BlockSpec((FFN_TM, D), lambda i: (i, 0)),
                  pl.BlockSpec((FFN_TM, D), lambda i: (i, 0)),
                  pl.BlockSpec((None, None, N_ADA, D), ada_map),
                  _const_spec((1, D)),
                  pl.BlockSpec((D, D), lambda i: (0, 0), pipeline_mode=one),
                  pl.BlockSpec((D, D_FF), lambda i: (0, 0), pipeline_mode=one),
                  pl.BlockSpec((D, D_FF), lambda i: (0, 0), pipeline_mode=one),
                  pl.BlockSpec((D_FF, D), lambda i: (0, 0), pipeline_mode=one)],
        out_specs=pl.BlockSpec((FFN_TM, D), lambda i: (i, 0)),
        compiler_params=pltpu.CompilerParams(
            dimension_semantics=("arbitrary",), vmem_limit_bytes=VMEM_BIG),
        name="post_attn_ffn",
    )(x, o, ada, g, wo, wg, wu, wd)


HALO = 16
EXT = TM + 2 * HALO
CONV_RC = 64
CONV_LC = 128
CONV_WIN = CONV_RC + 24
CONV_LN_ROWS = 32


def _conv_router_kernel(x_ref, xp_ref, xn_ref, ada_ref, gm_ref, gf_ref, w1_ref, b1_ref, wdw_ref, bdw_ref,
                        gln_ref, bln_ref, w2_ref, b2_ref, wr_ref,
                        x1_o, h2_o, ri_o, gx_o, cnt_o, p_ref, c_ref, v_ref, run_ref):
    i = pl.program_id(0)

    @pl.when(i == 0)
    def _():
        run_ref[...] = jnp.zeros_like(run_ref)

    is_lat = i >= N_CTX_TILES
    j = (i - N_CTX_TILES) % LAT_TILES_PER_SEQ
    lflag = jnp.where(is_lat & (j != 0), 1.0, 0.0)
    rflag = jnp.where(is_lat & (j != LAT_TILES_PER_SEQ - 1), 1.0, 0.0)

    x = x_ref[...]
    xx = jnp.concatenate([xp_ref[...], x, xn_ref[...]], axis=0)
    h = _modulate(xx, gm_ref[...], ada_ref[0:1, :], ada_ref[1:2, :])
    a = _dot(h.astype(BF16), w1_ref[...]) + b1_ref[...]
    u = a[:, :D] * _sigmoid(a[:, D:])
    row = lax.broadcasted_iota(jnp.int32, (EXT, 1), 0)
    u = u * jnp.where(row < HALO, lflag, jnp.where(row >= HALO + TM, rflag, 1.0))

    p_ref[...] = u
    for rc in range(TM // CONV_RC):
        for lc in range(D // CONV_LC):
            ls = slice(lc * CONV_LC, (lc + 1) * CONV_LC)
            acc = jnp.zeros((CONV_RC, CONV_LC), F32) + bdw_ref[:, ls]
            base = p_ref[rc * CONV_RC:rc * CONV_RC + CONV_WIN + 8, ls]
            for r in range(8):
                win = base if r == 0 else pltpu.roll(base, CONV_WIN + 8 - r, 0)
                for q in range(4):
                    o = 8 * q + r
                    if 1 <= o <= CONV_W:
                        acc = acc + wdw_ref[o - 1:o, ls] * win[8 * q:8 * q + CONV_RC]
            c_ref[rc * CONV_RC:(rc + 1) * CONV_RC, ls] = acc

    for c0 in range(0, TM, CONV_LN_ROWS):
        acc = c_ref[c0:c0 + CONV_LN_ROWS, :]
        mu = jnp.mean(acc, axis=-1, keepdims=True)
        xc = acc - mu
        y = xc * lax.rsqrt(jnp.mean(xc * xc, axis=-1, keepdims=True) + EPS)
        y = y * gln_ref[...] + bln_ref[...]
        v_ref[c0:c0 + CONV_LN_ROWS, :] = (y * _sigmoid(y)).astype(BF16)

    mix = _dot(v_ref[...], w2_ref[...]) + b2_ref[...]
    x1 = x + ada_ref[2:3, :] * mix
    x1_o[...] = x1
    h2 = _modulate(x1, gf_ref[...], ada_ref[3:4, :], ada_ref[4:5, :])
    h2_o[...] = h2.astype(BF16)

    lane = lax.broadcasted_iota(jnp.int32, (TM, LANE), 1)
    lane_f = lane.astype(F32)
    logits = jnp.where(lane < N_EXP, _dot_x3(h2, wr_ref[...]), -jnp.inf)
    m1 = jnp.max(logits, axis=-1, keepdims=True)
    i1 = jnp.min(jnp.where(logits == m1, lane_f, float(LANE)), axis=-1, keepdims=True)
    rest = jnp.where(lane_f == i1, -jnp.inf, logits)
    m2 = jnp.max(rest, axis=-1, keepdims=True)
    i2 = jnp.min(jnp.where(rest == m2, lane_f, float(LANE)), axis=-1, keepdims=True)
    e = jnp.exp(m2 - m1)
    w1 = 1.0 / (1.0 + e)
    w2 = e / (1.0 + e)

    oh1 = jnp.where(lane_f == i1, 1.0, 0.0)
    oh2 = jnp.where(lane_f == i2, 1.0, 0.0)
    tr = lax.broadcasted_iota(jnp.int32, (TM, TM), 0)
    tc = lax.broadcasted_iota(jnp.int32, (TM, TM), 1)
    tri = jnp.where(tr > tc, 1.0, 0.0).astype(BF16)
    run = run_ref[0:1, :]
    cnt1 = jnp.sum(oh1, axis=0, keepdims=True)
    cnt2 = jnp.sum(oh2, axis=0, keepdims=True)
    pre1 = _dot(tri, oh1.astype(BF16)) + run
    pre2 = _dot(tri, oh2.astype(BF16)) + (run + cnt1)
    rank1 = jnp.sum(oh1 * pre1, axis=-1, keepdims=True)
    rank2 = jnp.sum(oh2 * pre2, axis=-1, keepdims=True)
    new_run = run + cnt1 + cnt2
    run_ref[...] = jnp.broadcast_to(new_run, run_ref.shape)
    srow = lax.broadcasted_iota(jnp.int32, cnt_o.shape, 0)
    cnt_o[...] = jnp.where(srow == 0, run, jnp.where(srow == 1, cnt1 + cnt2, 0.0))

    ri = jnp.where(lane == 0, i1, jnp.where(lane == 1, i2, jnp.where(lane == 2, rank1, jnp.where(lane == 3, rank2, 0.0))))
    ri_o[...] = ri.astype(jnp.int32)

    def pieces(w):
        hi = w.astype(BF16).astype(F32)
        mid = (w - hi).astype(BF16).astype(F32)
        return hi, mid, (w - hi) - mid

    gx = jnp.zeros((TM, LANE), F32)
    for n, piece in enumerate(pieces(w1) + pieces(w2)):
        gx = jnp.where(lane == n, piece, gx)
    gx_o[...] = gx.astype(BF16)


def _conv_router_call(x, ada, layer, gm, gf, cw):
    ada_map = lambda i: (layer, _tile_cond(i, N_CTX_TILES, LAT_TILES_PER_SEQ), 0, 0)
    hb = TM // HALO
    row = lambda w: pl.BlockSpec((TM, w), lambda i: (i, 0))
    return pl.pallas_call(
        _conv_router_kernel,
        out_shape=[jax.ShapeDtypeStruct((T_ALL, D), F32), jax.ShapeDtypeStruct((T_ALL, D), BF16),
                   jax.ShapeDtypeStruct((T_ALL, LANE), jnp.int32), jax.ShapeDtypeStruct((T_ALL, LANE), BF16),
                   jax.ShapeDtypeStruct((N_TILES, 8, LANE), F32)],
        grid=(N_TILES,),
        in_specs=[row(D),
                  pl.BlockSpec((HALO, D), lambda i: (jnp.maximum(i * hb - 1, 0), 0)),
                  pl.BlockSpec((HALO, D), lambda i: (jnp.minimum((i + 1) * hb, T_ALL // HALO - 1), 0)),
                  pl.BlockSpec((None, None, N_ADA, D), ada_map),
                  _const_spec((1, D)), _const_spec((1, D)),
                  _const_spec((D, 2 * D)), _const_spec((1, 2 * D)),
                  _const_spec((CONV_W, D)), _const_spec((1, D)),
                  _const_spec((1, D)), _const_spec((1, D)),
                  _const_spec((D, D)), _const_spec((1, D)),
                  _const_spec((D, LANE))],
        out_specs=[row(D), row(D), row(LANE), row(LANE), pl.BlockSpec((None, 8, LANE), lambda i: (i, 0, 0))],
        scratch_shapes=[pltpu.VMEM((EXT, D), F32), pltpu.VMEM((TM, D), F32), pltpu.VMEM((TM, D), BF16),
                        pltpu.VMEM((8, LANE), F32)],
        compiler_params=pltpu.CompilerParams(
            dimension_semantics=("arbitrary",), vmem_limit_bytes=VMEM_BIG),
        name="conv_router",
    )(x, x, x, ada, gm, gf, cw["w1"], cw["b1"], cw["wdw"], cw["bdw"], cw["gln"], cw["bln"],
      cw["w2"], cw["b2"], cw["wr"])


def _onehot_rows(pp_row, base):
    riota = lax.broadcasted_iota(jnp.int32, (MOE_SUB, TM), 0)
    return (pp_row - base) == riota


def _moe_kernel(se_ref, row0_ref, nsub_ref, ta_ref, tn_ref, h_ref, gx_ref, post_ref, wg_ref, wu_ref, wd_ref,
                ys_in_ref, ys_ref, hb_ref, acc_ref, gs_ref, hstage, gstage, wgb, wub, wdb, sem_in, sem_out):
    del ys_in_ref
    j = pl.program_id(0)
    f = pl.program_id(1)
    nsub = nsub_ref[j]
    row0 = row0_ref[j]
    e = se_ref[j]

    def tile_loads(i, slot):
        r = pl.multiple_of(i * TM, TM)
        return (pltpu.make_async_copy(h_ref.at[pl.ds(r, TM), :], hstage.at[slot], sem_in.at[slot]),
                pltpu.make_async_copy(gx_ref.at[pl.ds(r, TM), :], gstage.at[slot], sem_in.at[slot]))

    def store(s):
        r = pl.multiple_of(s * MOE_SUB, MOE_SUB)
        dst = ys_ref.at[pl.ds(pl.multiple_of(row0 + s * MOE_SUB, MOE_SUB), MOE_SUB), :]
        return pltpu.make_async_copy(hb_ref.at[pl.ds(r, MOE_SUB), :], dst, sem_out)

    def gather_tile(i, carry):
        slot = i % 2
        for cp in tile_loads(i, slot):
            cp.wait()

        @pl.when(i + 1 < N_TILES)
        def _():
            for cp in tile_loads(i + 1, 1 - slot):
                cp.start()

        a = ta_ref[i * N_EXP + e] - row0
        n = tn_ref[i * N_EXP + e]
        last = a + n - 1

        @pl.when((n > 0) & (last >= 0) & (a < nsub * MOE_SUB))
        def _():
            lo = jnp.maximum(a, 0) // MOE_SUB
            hi = jnp.minimum(last // MOE_SUB, nsub - 1)
            pp = post_ref[pl.ds(pl.multiple_of(i * 8, 8), 8), :]
            lane = lax.broadcasted_iota(jnp.int32, (MOE_SUB, LANE), 1)

            def block(b):
                base = row0 + b * MOE_SUB
                m0 = _onehot_rows(pp[0:1, :], base)
                m1 = _onehot_rows(pp[1:2, :], base)
                rr = pl.multiple_of(b * MOE_SUB, MOE_SUB)
                both = jnp.where(m0 | m1, 1.0, 0.0).astype(BF16)
                acc_ref[pl.ds(rr, MOE_SUB), :] += _dot(both, hstage[slot])
                g0 = _dot(jnp.where(m0, 1.0, 0.0).astype(BF16), gstage[slot])
                g1 = _dot(jnp.where(m1, 1.0, 0.0).astype(BF16), gstage[slot])
                gs_ref[pl.ds(rr, MOE_SUB), :] += (jnp.where(lane < 3, g0, 0.0)
                                                  + jnp.where((lane >= 3) & (lane < 6), g1, 0.0))

            block(lo)

            @pl.when(hi > lo)
            def _():
                block(lo + 1)

        return carry

    def part(r):
        hh = hb_ref[pl.ds(r, MOE_SUB), :]
        out = None
        for c0 in range(0, MOE_TF, MOE_TC):
            gg = _dot(hh, wgb[:, c0:c0 + MOE_TC])
            uu = _dot(hh, wub[:, c0:c0 + MOE_TC])
            a = (gg * _sigmoid(gg) * uu).astype(BF16)
            piece = _dot(a, wdb[c0:c0 + MOE_TC, :])
            out = piece if out is None else out + piece
        return out

    def for_each_subtile(fn):
        def pair(p, carry):
            fn(2 * p)
            fn(2 * p + 1)
            return carry

        lax.fori_loop(0, nsub // 2, pair, 0)

        @pl.when(nsub % 2 == 1)
        def _():
            fn(nsub - 1)

    @pl.when(nsub > 0)
    def _():
        wgb[...] = wg_ref[...].astype(BF16)
        wub[...] = wu_ref[...].astype(BF16)
        wdb[...] = wd_ref[...].astype(BF16)

        @pl.when(f == 0)
        def _():
            for cp in tile_loads(0, 0):
                cp.start()

            def clear(s, carry):
                r = pl.multiple_of(s * MOE_SUB, MOE_SUB)
                acc_ref[pl.ds(r, MOE_SUB), :] = jnp.zeros((MOE_SUB, D), F32)
                gs_ref[pl.ds(r, MOE_SUB), :] = jnp.zeros((MOE_SUB, LANE), F32)
                return carry

            lax.fori_loop(0, nsub, clear, 0)
            lax.fori_loop(0, N_TILES, gather_tile, 0)

            def first(s):
                r = pl.multiple_of(s * MOE_SUB, MOE_SUB)
                hb_ref[pl.ds(r, MOE_SUB), :] = acc_ref[pl.ds(r, MOE_SUB), :].astype(BF16)
                acc_ref[pl.ds(r, MOE_SUB), :] = part(r)

            for_each_subtile(first)

        @pl.when((f > 0) & (f < MOE_NF - 1))
        def _():
            def middle(s):
                r = pl.multiple_of(s * MOE_SUB, MOE_SUB)
                acc_ref[pl.ds(r, MOE_SUB), :] += part(r)

            for_each_subtile(middle)

        @pl.when(f == MOE_NF - 1)
        def _():
            def final(s):
                r = pl.multiple_of(s * MOE_SUB, MOE_SUB)
                y = acc_ref[pl.ds(r, MOE_SUB), :] + part(r)
                gate = jnp.sum(gs_ref[pl.ds(r, MOE_SUB), :], axis=-1, keepdims=True)
                hb_ref[pl.ds(r, MOE_SUB), :] = (y * gate).astype(BF16)
                store(s).start()

            for_each_subtile(final)

            def drain(s, carry):
                store(s).wait()
                return carry

            lax.fori_loop(0, nsub, drain, 0)


def _moe_call(st_e, st_row0, st_nsub, t_a, t_n, h2b, gx, pos_t, wg, wu, wd, ys0, lidx):
    f_eff = lambda j, f, ns: jnp.where(ns[j] > 0, f, MOE_NF - 1)
    wmap_in = lambda j, f, se, r0, ns, ta, tn: (lidx, se[j], 0, f_eff(j, f, ns))
    wmap_out = lambda j, f, se, r0, ns, ta, tn: (lidx, se[j], f_eff(j, f, ns), 0)
    return pl.pallas_call(
        _moe_kernel,
        out_shape=jax.ShapeDtypeStruct((MOE_ROWS, D), BF16),
        grid_spec=pltpu.PrefetchScalarGridSpec(
            num_scalar_prefetch=5, grid=(MOE_NS, MOE_NF),
            in_specs=[
                pl.BlockSpec(memory_space=pl.ANY),
                pl.BlockSpec(memory_space=pl.ANY),
                pl.BlockSpec((N_TILES * 8, TM), lambda j, f, se, r0, ns, ta, tn: (0, 0)),
                pl.BlockSpec((None, None, D, MOE_TF), wmap_in),
                pl.BlockSpec((None, None, D, MOE_TF), wmap_in),
                pl.BlockSpec((None, None, MOE_TF, D), wmap_out),
                pl.BlockSpec(memory_space=pl.ANY),
            ],
            out_specs=pl.BlockSpec(memory_space=pl.ANY),
            scratch_shapes=[pltpu.VMEM((MOE_R, D), BF16), pltpu.VMEM((MOE_R, D), F32),
                            pltpu.VMEM((MOE_R, LANE), F32),
                            pltpu.VMEM((2, TM, D), BF16), pltpu.VMEM((2, TM, LANE), BF16),
                            pltpu.VMEM((D, MOE_TF), BF16), pltpu.VMEM((D, MOE_TF), BF16),
                            pltpu.VMEM((MOE_TF, D), BF16),
                            pltpu.SemaphoreType.DMA((2,)), pltpu.SemaphoreType.DMA(())]),
        input_output_aliases={11: 0},
        compiler_params=pltpu.CompilerParams(
            dimension_semantics=("arbitrary", "arbitrary"), vmem_limit_bytes=VMEM_BIG,
            has_side_effects=True),
        name="moe_experts",
    )(st_e, st_row0, st_nsub, t_a, t_n, h2b, gx, pos_t, wg, wu, wd, ys0)


def _combine_kernel(split, ta_ref, tn_ref, x_ref, ada_ref, posc_ref, ys_ref, *rest):
    if split:
        oc_ref, ol_ref, ybuf, oacc, sem = rest
    else:
        o_ref, ybuf, oacc, sem = rest
    g = pl.program_id(0)

    def blocks(i, e):
        a = ta_ref[i * N_EXP + e]
        n = tn_ref[i * N_EXP + e]
        lo = a // MOE_SUB
        return n > 0, lo, (n > 0) & ((a + n - 1) // MOE_SUB > lo)

    def block_copy(slot, e, k, blk):
        src = ys_ref.at[pl.ds(pl.multiple_of(blk * MOE_SUB, MOE_SUB), MOE_SUB), :]
        return pltpu.make_async_copy(src, ybuf.at[slot, 2 * e + k], sem.at[slot, 2 * e + k])

    @pl.when(g < N_TILES)
    def _():
        for e in range(N_EXP):
            has, lo, two = blocks(g, e)
            for k, cond in ((0, has), (1, two)):
                @pl.when(cond)
                def _():
                    block_copy(g % 2, e, k, lo + k).start()

    @pl.when(g > 0)
    def _():
        i = g - 1
        slot = i % 2
        p0 = posc_ref[:, 0:1]
        p1 = posc_ref[:, 1:2]
        liota = lax.broadcasted_iota(jnp.int32, (TM, MOE_SUB), 1)
        oacc[...] = jnp.zeros_like(oacc)
        for e in range(N_EXP):
            has, lo, two = blocks(i, e)
            for k, cond in ((0, has), (1, two)):
                @pl.when(cond)
                def _():
                    block_copy(slot, e, k, lo + k).wait()
                    base = (lo + k) * MOE_SUB
                    hit = ((p0 - base) == liota) | ((p1 - base) == liota)
                    oacc[...] += _dot(jnp.where(hit, 1.0, 0.0).astype(BF16), ybuf[slot, 2 * e + k])
        out = x_ref[...] + ada_ref[5:6, :] * oacc[...]
        if split:
            @pl.when(i < N_CTX_TILES)
            def _():
                oc_ref[...] = out

            @pl.when(i >= N_CTX_TILES)
            def _():
                ol_ref[...] = out
        else:
            o_ref[...] = out


def _combine_call(split, t_a, t_n, x1, ada, layer, posc, ys):
    prev = lambda g: jnp.maximum(g - 1, 0)
    ada_map = lambda g, a, n: (layer, _tile_cond(prev(g), N_CTX_TILES, LAT_TILES_PER_SEQ), 0, 0)
    if split:
        out_shape = [jax.ShapeDtypeStruct((T_CTX, D), F32), jax.ShapeDtypeStruct((T_LAT, D), F32)]
        out_specs = [pl.BlockSpec((TM, D), lambda g, a, n: (jnp.minimum(prev(g), N_CTX_TILES - 1), 0)),
                     pl.BlockSpec((TM, D), lambda g, a, n: (jnp.maximum(prev(g) - N_CTX_TILES, 0), 0))]
    else:
        out_shape = jax.ShapeDtypeStruct((T_ALL, D), F32)
        out_specs = pl.BlockSpec((TM, D), lambda g, a, n: (prev(g), 0))
    return pl.pallas_call(
        functools.partial(_combine_kernel, split),
        out_shape=out_shape,
        grid_spec=pltpu.PrefetchScalarGridSpec(
            num_scalar_prefetch=2, grid=(N_TILES + 1,),
            in_specs=[pl.BlockSpec((TM, D), lambda g, a, n: (prev(g), 0)),
                      pl.BlockSpec((None, None, N_ADA, D), ada_map),
                      pl.BlockSpec((TM, LANE), lambda g, a, n: (prev(g), 0)),
                      pl.BlockSpec(memory_space=pl.ANY)],
            out_specs=out_specs,
            scratch_shapes=[pltpu.VMEM((2, 2 * N_EXP, MOE_SUB, D), BF16), pltpu.VMEM((TM, D), F32),
                            pltpu.SemaphoreType.DMA((2, 2 * N_EXP))]),
        compiler_params=pltpu.CompilerParams(
            dimension_semantics=("arbitrary",), vmem_limit_bytes=VMEM_BIG),
        name="moe_combine",
    )(t_a, t_n, x1, ada, posc, ys)


def _moe_layer(x1, h2b, ri, gx, runs, ada, layer, wg, wu, wd, lidx, split):
    idx = ri[:, 0:2]
    rank = ri[:, 2:4]
    before = runs[:, 0, :N_EXP].astype(jnp.int32)
    t_n = runs[:, 1, :N_EXP].astype(jnp.int32)
    counts = before[-1] + t_n[-1]
    padded = ((counts + MOE_SUB - 1) // MOE_SUB) * MOE_SUB
    ends = jnp.cumsum(padded)
    starts = ends - padded
    sel = idx[:, :, None] == jnp.arange(N_EXP, dtype=jnp.int32)[None, None, :]
    pos = (jnp.sum(jnp.where(sel, starts[None, None, :], 0), axis=-1) + rank).astype(jnp.int32)
    t_a = (starts[None, :] + before).reshape(-1).astype(jnp.int32)
    t_n = t_n.reshape(-1)
    pos_t = jnp.pad(pos.reshape(N_TILES, TM, 2).transpose(0, 2, 1), ((0, 0), (0, 6), (0, 0)))
    pos_t = pos_t.reshape(N_TILES * 8, TM)
    posc = jnp.pad(pos, ((0, 0), (0, LANE - 2)))
    n_st = (padded + MOE_R - 1) // MOE_R
    st_end = jnp.cumsum(n_st)
    st_begin = st_end - n_st
    j = jnp.arange(MOE_NS, dtype=jnp.int32)
    se = jnp.sum((j[:, None] >= st_end[None, :]).astype(jnp.int32), axis=1)
    used = se < N_EXP
    se_c = jnp.minimum(se, N_EXP - 1)
    k = j - st_begin[se_c]
    row0 = jnp.where(used, starts[se_c] + k * MOE_R, 0).astype(jnp.int32)
    nsub = jnp.where(used, jnp.minimum(MOE_R, padded[se_c] - k * MOE_R) // MOE_SUB, 0).astype(jnp.int32)
    last = jnp.maximum(jnp.sum(used.astype(jnp.int32)) - 1, 0)
    se_eff = jnp.where(used, se_c, se_c[last]).astype(jnp.int32)
    ys = _moe_call(se_eff, row0, nsub, t_a, t_n, h2b, gx, pos_t, wg, wu, wd,
                   jnp.zeros((MOE_ROWS, D), BF16), lidx)
    return _combine_call(split, t_a, t_n, x1, ada, layer, posc, ys)


def _rope_tables():
    t = jnp.arange(DEC_SEQ)
    rows = (t // GRID_W).astype(F32)
    cols = (t % GRID_W).astype(F32)

    def axis_tabs(pos, d):
        inv = ROPE_BASE ** (-jnp.arange(0, d, 2, dtype=F32) / d)
        ang = pos[:, None] * inv[None, :]
        cos, sin = jnp.cos(ang), jnp.sin(ang)
        z = jnp.zeros_like(sin)
        return (jnp.concatenate([cos, cos], -1), jnp.concatenate([-sin, z], -1), jnp.concatenate([z, sin], -1))

    def axial(d):
        r = axis_tabs(rows, d // 2)
        c = axis_tabs(cols, d // 2)
        return [jnp.concatenate([a, b], -1) for a, b in zip(r, c)]

    tabs_a = [jnp.tile(tb, (1, A_HEADS)) for tb in axial(A_DH)]
    cb, sub, sdb = axial(B_ROPE)
    ones = jnp.ones((DEC_SEQ, B_NOPE), F32)
    zer = jnp.zeros((DEC_SEQ, B_NOPE), F32)
    pad = jnp.zeros((DEC_SEQ, SLOT - B_QK), F32)
    slot = lambda first, tb: jnp.tile(jnp.concatenate([first, tb, pad], -1), (1, B_HEADS))
    tabs_b = [slot(ones, cb), slot(zer, sub), slot(zer, sdb)]
    return tabs_a + tabs_b


def _attn_weights(e, w_in_attn, g_qk_a_q, g_qk_a_k, g_mla_q_a, w_mla_q_b, g_mla_kv_a, w_mla_kv_b,
                  g_qk_b_q, g_qk_b_k):
    w_in = jnp.pad(w_in_attn[e], ((0, 0), (0, IN_PAD - IN_DIM))).astype(BF16)
    w_qb = jnp.pad(w_mla_q_b[e].reshape(B_Q_LORA, B_HEADS, B_QK), ((0, 0), (0, 0), (0, SLOT - B_QK)))
    w_qb = w_qb.reshape(B_Q_LORA, B_HEADS * SLOT).astype(BF16)
    wkv = w_mla_kv_b[e].reshape(B_KV_LORA, B_HEADS, B_NOPE + B_V)
    wk = jnp.pad(wkv[:, :, :B_NOPE], ((0, 0), (0, 0), (0, SLOT - B_NOPE))).reshape(B_KV_LORA, B_HEADS * SLOT)
    wv = wkv[:, :, B_NOPE:].reshape(B_KV_LORA, B_HEADS * B_V)
    w_kv = jnp.concatenate([wk, wv], axis=1).astype(BF16)
    place = jnp.zeros((LANE, B_HEADS, SLOT), F32)
    ii = jnp.arange(B_ROPE)
    place = place.at[ii, :, B_NOPE + ii].set(1.0).reshape(LANE, B_HEADS * SLOT).astype(BF16)
    slot_gain = lambda g: jnp.tile(jnp.pad(g, (0, SLOT - B_QK)), B_HEADS).reshape(1, B_HEADS * SLOT)
    return dict(
        w_in=w_in, w_qb=w_qb, w_kv=w_kv, place=place,
        gqa=jnp.tile(g_qk_a_q[e], A_HEADS).reshape(1, 512),
        gka=jnp.tile(g_qk_a_k[e], A_KV).reshape(1, LANE),
        gql=g_mla_q_a[e].reshape(1, B_Q_LORA), gkl=g_mla_kv_a[e].reshape(1, B_KV_LORA),
        gqb=slot_gain(g_qk_b_q[e]), gkb=slot_gain(g_qk_b_k[e]))


def _pair_variants_host(x):
    h0, h1 = x[..., :64], x[..., 64:]
    z = jnp.zeros_like(h0)
    return jnp.concatenate([h0, z, z, h0, h1, z, z, h1], axis=-1)


def kernel(x_prompt, x_sample, c, cache_k_win, cache_v_win, cache_ckv, cache_krope, c_ctx, w_ada, b_ada, g_norm_mix, g_norm_ffn, w_in_attn, g_qk_a_q, g_qk_a_k, sink_a, g_mla_q_a, w_mla_q_b, g_mla_kv_a, w_mla_kv_b, g_qk_b_q, g_qk_b_k, w_out_attn, w_ffn_gate, w_ffn_up, w_ffn_down, w_conv_pw1, b_conv_pw1, w_conv_dw, b_conv_dw, g_conv_ln, b_conv_ln, w_conv_pw2, b_conv_pw2, w_router, w_moe_gate, w_moe_up, w_moe_down):
    x = jnp.concatenate([x_prompt.reshape(T_CTX, D), x_sample.reshape(T_LAT, D)], axis=0)
    cond8 = jnp.concatenate([c_ctx[None, :], c, jnp.zeros((8 - 1 - DEC_BATCH, D), F32)], axis=0)
    ada = _ada_call(cond8, w_ada, b_ada).reshape(DEPTH, 8, N_ADA, D)
    tabs = _rope_tables()

    wo_b = w_out_attn.astype(BF16)
    wg_b = w_ffn_gate.astype(BF16)
    wu_b = w_ffn_up.astype(BF16)
    wd_b = w_ffn_down.astype(BF16)
    w1_b = w_conv_pw1.astype(BF16)
    w2_b = w_conv_pw2.astype(BF16)

    new_k, new_v, new_ckv, new_kr = [], [], [], []
    for layer in range(DEPTH):
        gm = g_norm_mix[layer].reshape(1, D)
        gf = g_norm_ffn[layer].reshape(1, D)
        if layer % 2 == 0:
            e = layer // 2
            lw = _attn_weights(e, w_in_attn, g_qk_a_q, g_qk_a_k, g_mla_q_a, w_mla_q_b, g_mla_kv_a,
                               w_mla_kv_b, g_qk_b_q, g_qk_b_k)
            sink = sink_a[e].reshape(A_HEADS)
            (qa, ka4, va4, qb, kb, vlo, vhi, kaf, vaf, ckvf, krf) = _pre_attn_call(False, x, ada, layer, gm, lw, None)
            o_ctx = _attn_ctx_call(sink, qa, ka4, va4, qb, kb, vlo, vhi)
            new_k.append(kaf.reshape(BATCH, SEQ, A_KV, A_DH))
            new_v.append(vaf.reshape(BATCH, SEQ, A_KV, A_DH))
            new_ckv.append(ckvf.reshape(BATCH, SEQ, B_KV_LORA))
            new_kr.append(krf[:, :B_ROPE].reshape(BATCH, SEQ, B_ROPE))
            (qa, ka4, va4, qb, kb, vlo, vhi) = _pre_attn_call(True, x, ada, layer, gm, lw, tabs)
            kr_c = jnp.pad(cache_krope[:, e].reshape(DEC_BATCH * PAST, B_ROPE), ((0, 0), (0, LANE - B_ROPE)))
            kb_c, vlo_c, vhi_c = _cache_kv_call(cache_ckv[:, e].reshape(DEC_BATCH * PAST, B_KV_LORA), kr_c, lw)
            pad_rows = lambda a: jnp.pad(a.reshape(DEC_BATCH, DEC_SEQ, 512), ((0, 0), (QB, QB), (0, 0)))
            ka4c = _pair_variants_host(cache_k_win[:, e].reshape(DEC_BATCH, PAST, LANE)).astype(BF16)
            va4c = _pair_variants_host(cache_v_win[:, e].reshape(DEC_BATCH, PAST, LANE)).astype(BF16)
            cat = lambda a, b, w: jnp.concatenate(
                [a.reshape(DEC_BATCH, PAST, w), b.reshape(DEC_BATCH, DEC_SEQ, w)], axis=1)
            o_lat = _attn_lat_call(sink, qa, pad_rows(ka4), pad_rows(va4), ka4c, va4c, qb,
                                   cat(kb_c, kb, 1024), cat(vlo_c, vlo, 512), cat(vhi_c, vhi, 512))
            o = jnp.concatenate([o_ctx, o_lat], axis=0)
            x = _post_attn_call(x, o, ada, layer, gf, wo_b[e], wg_b[e], wu_b[e], wd_b[e])
        else:
            o_ = layer // 2
            cw = dict(w1=w1_b[o_], b1=b_conv_pw1[o_].reshape(1, 2 * D), wdw=w_conv_dw[o_],
                      bdw=b_conv_dw[o_].reshape(1, D), gln=g_conv_ln[o_].reshape(1, D),
                      bln=b_conv_ln[o_].reshape(1, D), w2=w2_b[o_], b2=b_conv_pw2[o_].reshape(1, D),
                      wr=jnp.pad(w_router[o_], ((0, 0), (0, LANE - N_EXP))))
            x1, h2b, ri, gx, runs = _conv_router_call(x, ada, layer, gm, gf, cw)
            x = _moe_layer(x1, h2b, ri, gx, runs, ada, layer, w_moe_gate, w_moe_up, w_moe_down, o_,
                           split=(layer == DEPTH - 1))

    y_prompt = x[0].reshape(BATCH, SEQ, D)
    y_sample = x[1].reshape(DEC_BATCH, DEC_SEQ, D)
    return (y_prompt, y_sample, jnp.stack(new_k, axis=1), jnp.stack(new_v, axis=1),
            jnp.stack(new_ckv, axis=1), jnp.stack(new_kr, axis=1))
```

```python
import functools

import jax
import jax.numpy as jnp
from jax import lax
from jax.experimental import pallas as pl
from jax.experimental.pallas import tpu as pltpu

F32 = jnp.float32
BF16 = jnp.bfloat16

D = 1024
BATCH = 32
SEQ = 256
DEPTH = 4
DEC_BATCH = 2
DEC_SEQ = 1024
PAST = 256
GRID_W = 64
N_EVEN = 2
A_HEADS = 8
A_KV = 2
A_DH = 64
WINDOW = 128
B_HEADS = 8
B_Q_LORA = 256
B_KV_LORA = 128
B_NOPE = 64
B_ROPE = 32
B_QK = B_NOPE + B_ROPE
B_V = 64
IN_DIM = 1184
IN_PAD = 1280
CONV_W = 31
D_FF = 2816
N_EXP = 8
D_FFE = 3584
ROPE_BASE = 10000.0
EPS = 1e-6
N_ADA = 6

T_CTX = BATCH * SEQ
T_LAT = DEC_BATCH * DEC_SEQ
T_ALL = T_CTX + T_LAT
TM = 256
N_TILES = T_ALL // TM
N_CTX_TILES = T_CTX // TM
LAT_TILES_PER_SEQ = DEC_SEQ // TM
LANE = 128
SLOT = 128

MOE_SUB = 256
MOE_R = 3072
MOE_TF = 512
MOE_NF = D_FFE // MOE_TF
MOE_TC = 512
MOE_GB = 128
MOE_GDEPTH = 3
MOE_GSLOTS = MOE_GDEPTH + 1
MOE_ROWS = 2 * T_ALL + N_EXP * MOE_SUB
MOE_NS = MOE_ROWS // MOE_R + N_EXP

FFN_TM = 512
FFN_CH = 1408

VMEM_BIG = 56 * 1024 * 1024


def _dot(a, b):
    return jnp.dot(a, b, preferred_element_type=F32)


def _dot_nt(a, b):
    return lax.dot_general(a, b, (((1,), (1,)), ((), ())), preferred_element_type=F32)


def _split(a):
    hi = a.astype(BF16)
    lo = (a - hi.astype(F32)).astype(BF16)
    return hi, lo


def _dot_x3(a, b):
    ah, al = _split(a)
    bh, bl = _split(b)
    return _dot(ah, bh) + (_dot(ah, bl) + _dot(al, bh))


def _sigmoid(x):
    return 1.0 / (1.0 + jnp.exp(-x))


def _rms(x, n):
    return x * lax.rsqrt(jnp.sum(x * x, axis=-1, keepdims=True) * (1.0 / n) + EPS)


def _modulate(x, g, shift, scale):
    return _rms(x, D) * g * (1.0 + scale) + shift


def _lo_mask(rows):
    return lax.broadcasted_iota(jnp.int32, (rows, LANE), 1) < 64


def _ada_kernel(cond_ref, w_ref, b_ref, o_ref):
    c = cond_ref[...]
    s = c * _sigmoid(c)
    o_ref[...] = _dot_x3(s, w_ref[...]) + b_ref[...]


def _ada_call(cond8, w_ada, b_ada):
    tn = 1024
    return pl.pallas_call(
        _ada_kernel,
        out_shape=jax.ShapeDtypeStruct((DEPTH, 8, N_ADA * D), F32),
        grid=(DEPTH, N_ADA * D // tn),
        in_specs=[
            pl.BlockSpec((8, D), lambda l, j: (0, 0)),
            pl.BlockSpec((None, D, tn), lambda l, j: (l, 0, j)),
            pl.BlockSpec((None, 1, tn), lambda l, j: (l, 0, j)),
        ],
        out_specs=pl.BlockSpec((None, 8, tn), lambda l, j: (l, 0, j)),
        compiler_params=pltpu.CompilerParams(
            dimension_semantics=("arbitrary", "arbitrary"), vmem_limit_bytes=VMEM_BIG),
        name="ada",
    )(cond8, w_ada, b_ada.reshape(DEPTH, 1, N_ADA * D))


def _rope(x, cos, sup, sdn, shift):
    up = pltpu.roll(x, LANE - shift, 1)
    dn = pltpu.roll(x, shift, 1)
    return x * cos + up * sup + dn * sdn


def _norm64_block(blk, lo):
    sq = blk * blk
    s_lo = jnp.sum(jnp.where(lo, sq, 0.0), axis=-1, keepdims=True)
    s_hi = jnp.sum(jnp.where(lo, 0.0, sq), axis=-1, keepdims=True)
    r = jnp.where(lo, lax.rsqrt(s_lo * (1.0 / A_DH) + EPS), lax.rsqrt(s_hi * (1.0 / A_DH) + EPS))
    return blk * r


def _pair_variants(x, lo):
    r = pltpu.roll(x, 64, 1)
    return (jnp.where(lo, x, 0.0), jnp.where(lo, 0.0, r), jnp.where(lo, r, 0.0), jnp.where(lo, 0.0, x))


def _mla_kv(ckv_n, kr, wkv_ref, place_ref, gk_ref, lo, kb_ref, vlo_ref, vhi_ref, rope_tabs):
    kv = _dot(ckv_n.astype(BF16), wkv_ref[...])
    kr_hi, kr_lo = _split(kr)
    place = place_ref[...]
    krs = _dot(kr_hi, place) + _dot(kr_lo, place)
    for h in range(B_HEADS):
        sl = slice(h * SLOT, (h + 1) * SLOT)
        blk = kv[:, sl] + krs[:, sl]
        blk = _rms(blk, B_QK) * gk_ref[:, sl]
        if rope_tabs is not None:
            cos, sup, sdn = rope_tabs
            blk = _rope(blk, cos[:, sl], sup[:, sl], sdn[:, sl], 8)
        kb_ref[:, sl] = blk.astype(BF16)
    for j in range(B_HEADS // 2):
        sl = slice(j * LANE, (j + 1) * LANE)
        v = kv[:, B_HEADS * SLOT + j * LANE: B_HEADS * SLOT + (j + 1) * LANE]
        vlo_ref[:, sl] = jnp.where(lo, v, 0.0).astype(BF16)
        vhi_ref[:, sl] = jnp.where(lo, 0.0, v).astype(BF16)


def _pre_attn_kernel(rope, *refs):
    if rope:
        (x_ref, ada_ref, g_ref, win_ref, wqb_ref, wkv_ref, place_ref,
         gqa_ref, gka_ref, gql_ref, gkl_ref, gqb_ref, gkb_ref,
         ca_ref, sua_ref, sda_ref, cb_ref, sub_ref, sdb_ref,
         qa_o, ka4_o, va4_o, qb_o, kb_o, vlo_o, vhi_o) = refs
    else:
        (x_ref, ada_ref, g_ref, win_ref, wqb_ref, wkv_ref, place_ref,
         gqa_ref, gka_ref, gql_ref, gkl_ref, gqb_ref, gkb_ref,
         qa_o, ka4_o, va4_o, qb_o, kb_o, vlo_o, vhi_o,
         kaf_o, vaf_o, ckvf_o, krf_o) = refs
    lo = _lo_mask(TM)
    h = _modulate(x_ref[...], g_ref[...], ada_ref[0:1, :], ada_ref[1:2, :])
    p = _dot(h.astype(BF16), win_ref[...])

    for j in range(4):
        sl = slice(j * LANE, (j + 1) * LANE)
        blk = _norm64_block(p[:, sl], lo) * gqa_ref[:, sl]
        if rope:
            blk = _rope(blk, ca_ref[:, sl], sua_ref[:, sl], sda_ref[:, sl], 16)
        qa_o[:, sl] = (blk * (A_DH ** -0.5)).astype(BF16)
    ka = _norm64_block(p[:, 512:640], lo) * gka_ref[...]
    va = p[:, 640:768]
    if not rope:
        kaf_o[...] = ka
        vaf_o[...] = va
    else:
        ka = _rope(ka, ca_ref[:, 0:LANE], sua_ref[:, 0:LANE], sda_ref[:, 0:LANE], 16)
    for n, (kk, vv) in enumerate(zip(_pair_variants(ka, lo), _pair_variants(va, lo))):
        sl = slice(n * LANE, (n + 1) * LANE)
        ka4_o[:, sl] = kk.astype(BF16)
        va4_o[:, sl] = vv.astype(BF16)

    cq = _rms(p[:, 768:1024], B_Q_LORA) * gql_ref[...]
    qb = _dot(cq.astype(BF16), wqb_ref[...])
    for hh in range(B_HEADS):
        sl = slice(hh * SLOT, (hh + 1) * SLOT)
        blk = _rms(qb[:, sl], B_QK) * gqb_ref[:, sl]
        if rope:
            blk = _rope(blk, cb_ref[:, sl], sub_ref[:, sl], sdb_ref[:, sl], 8)
        qb_o[:, sl] = (blk * (B_QK ** -0.5)).astype(BF16)

    ckv = _rms(p[:, 1024:1152], B_KV_LORA) * gkl_ref[...]
    kr = p[:, 1152:1280]
    if not rope:
        ckvf_o[...] = ckv
        krf_o[...] = kr
    tabs = (cb_ref, sub_ref, sdb_ref) if rope else None
    _mla_kv(ckv, kr, wkv_ref, place_ref, gkb_ref, lo, kb_o, vlo_o, vhi_o, tabs)


def _const_spec(shape):
    nd = len(shape)
    return pl.BlockSpec(shape, lambda *a: (0,) * nd)


def _pre_attn_call(rope, x, ada, layer, g, lw, tabs):
    n_tiles = (T_LAT if rope else T_CTX) // TM
    off = N_CTX_TILES if rope else 0
    n_rows = n_tiles * TM
    if rope:
        ada_map = lambda i: (layer, 1 + i // LAT_TILES_PER_SEQ, 0, 0)
    else:
        ada_map = lambda i: (layer, 0, 0, 0)
    in_specs = [
        pl.BlockSpec((TM, D), lambda i: (i + off, 0)),
        pl.BlockSpec((None, None, N_ADA, D), ada_map),
        _const_spec((1, D)),
        _const_spec((D, IN_PAD)),
        _const_spec((B_Q_LORA, B_HEADS * SLOT)),
        _const_spec((B_KV_LORA, B_HEADS * SLOT + B_HEADS * B_V)),
        _const_spec((LANE, B_HEADS * SLOT)),
        _const_spec((1, 512)), _const_spec((1, LANE)), _const_spec((1, B_Q_LORA)),
        _const_spec((1, B_KV_LORA)), _const_spec((1, B_HEADS * SLOT)), _const_spec((1, B_HEADS * SLOT)),
    ]
    args = [x, ada, g, lw["w_in"], lw["w_qb"], lw["w_kv"], lw["place"],
            lw["gqa"], lw["gka"], lw["gql"], lw["gkl"], lw["gqb"], lw["gkb"]]
    if rope:
        tmap = lambda i: (i % LAT_TILES_PER_SEQ, 0)
        in_specs += [pl.BlockSpec((TM, 512), tmap)] * 3 + [pl.BlockSpec((TM, B_HEADS * SLOT), tmap)] * 3
        args += list(tabs)
    row = lambda w: pl.BlockSpec((TM, w), lambda i: (i, 0))
    out_shape = [jax.ShapeDtypeStruct((n_rows, w), BF16) for w in (512, 512, 512, 1024, 1024, 512, 512)]
    out_specs = [row(w) for w in (512, 512, 512, 1024, 1024, 512, 512)]
    if not rope:
        out_shape += [jax.ShapeDtypeStruct((n_rows, LANE), F32)] * 4
        out_specs += [row(LANE)] * 4
    return pl.pallas_call(
        functools.partial(_pre_attn_kernel, rope),
        out_shape=out_shape, grid=(n_tiles,), in_specs=in_specs, out_specs=out_specs,
        compiler_params=pltpu.CompilerParams(
            dimension_semantics=("arbitrary",), vmem_limit_bytes=VMEM_BIG),
        name="pre_attn_lat" if rope else "pre_attn_ctx",
    )(*args)


def _cache_kv_kernel(ckv_ref, kr_ref, wkv_ref, place_ref, gkb_ref, kb_o, vlo_o, vhi_o):
    lo = _lo_mask(PAST)
    _mla_kv(ckv_ref[...], kr_ref[...], wkv_ref, place_ref, gkb_ref, lo, kb_o, vlo_o, vhi_o, None)


def _cache_kv_call(ckv, kr, lw):
    rows = DEC_BATCH * PAST
    row = lambda w: pl.BlockSpec((PAST, w), lambda i: (i, 0))
    return pl.pallas_call(
        _cache_kv_kernel,
        out_shape=[jax.ShapeDtypeStruct((rows, 1024), BF16), jax.ShapeDtypeStruct((rows, 512), BF16),
                   jax.ShapeDtypeStruct((rows, 512), BF16)],
        grid=(DEC_BATCH,),
        in_specs=[row(LANE), row(LANE),
                  _const_spec((B_KV_LORA, B_HEADS * SLOT + B_HEADS * B_V)),
                  _const_spec((LANE, B_HEADS * SLOT)), _const_spec((1, B_HEADS * SLOT))],
        out_specs=[row(1024), row(512), row(512)],
        compiler_params=pltpu.CompilerParams(dimension_semantics=("arbitrary",)),
        name="cache_kv",
    )(ckv, kr, lw["w_kv"], lw["place"], lw["gkb"])


def _softmax_pv(scores, values, sink):
    m = jnp.max(scores[0], axis=-1, keepdims=True)
    for s in scores[1:]:
        m = jnp.maximum(m, jnp.max(s, axis=-1, keepdims=True))
    if sink is not None:
        m = jnp.maximum(m, sink)
    ps = [jnp.exp(s - m) for s in scores]
    den = jnp.sum(ps[0], axis=-1, keepdims=True)
    for p in ps[1:]:
        den = den + jnp.sum(p, axis=-1, keepdims=True)
    if sink is not None:
        den = den + jnp.exp(sink - m)
    inv = 1.0 / den
    out = _dot((ps[0] * inv).astype(BF16), values[0])
    for p, v in zip(ps[1:], values[1:]):
        out = out + _dot((p * inv).astype(BF16), v)
    return out


def _attn_ctx_kernel(sink_ref, qa_ref, ka4_ref, va4_ref, qb_ref, kb_ref, vlo_ref, vhi_ref, o_ref):
    for j in range(4):
        q = qa_ref[:, j * LANE:(j + 1) * LANE]
        acc = None
        for par in range(2):
            c = (2 * (j // 2) + par) * LANE
            s = _dot_nt(q, ka4_ref[:, c:c + LANE])
            o = _softmax_pv([s], [va4_ref[:, c:c + LANE]], sink_ref[2 * j + par])
            acc = o if acc is None else acc + o
        o_ref[:, j * LANE:(j + 1) * LANE] = acc.astype(BF16)
    for j in range(4):
        acc = None
        for par, v_ref in enumerate((vlo_ref, vhi_ref)):
            hh = 2 * j + par
            s = _dot_nt(qb_ref[:, hh * SLOT:(hh + 1) * SLOT], kb_ref[:, hh * SLOT:(hh + 1) * SLOT])
            o = _softmax_pv([s], [v_ref[:, j * LANE:(j + 1) * LANE]], None)
            acc = o if acc is None else acc + o
        o_ref[:, 512 + j * LANE:512 + (j + 1) * LANE] = acc.astype(BF16)


def _attn_ctx_call(sink, qa, ka4, va4, qb, kb, vlo, vhi):
    row = lambda w: pl.BlockSpec((SEQ, w), lambda b: (b, 0))
    return pl.pallas_call(
        _attn_ctx_kernel,
        out_shape=jax.ShapeDtypeStruct((T_CTX, D), BF16),
        grid=(BATCH,),
        in_specs=[pl.BlockSpec(memory_space=pltpu.SMEM),
                  row(512), row(512), row(512), row(1024), row(1024), row(512), row(512)],
        out_specs=row(D),
        compiler_params=pltpu.CompilerParams(dimension_semantics=("arbitrary",)),
        name="attn_ctx",
    )(sink, qa, ka4, va4, qb, kb, vlo, vhi)


QB = 128
N_QB = DEC_SEQ // QB
KPAD = DEC_SEQ + 2 * QB
KALL = PAST + DEC_SEQ


def _attn_lat_kernel(sink_ref, qa_ref, ka4p_ref, va4p_ref, ka4c_ref, va4c_ref,
                     qb_ref, kb_ref, vlo_ref, vhi_ref, o_ref):
    n = pl.program_id(1)
    r = lax.broadcasted_iota(jnp.int32, (QB, 3 * QB), 0)
    c = lax.broadcasted_iota(jnp.int32, (QB, 3 * QB), 1)
    rel = c - QB - r
    kpos = (n - 1) * QB + c
    valid = (jnp.abs(rel) <= WINDOW) & (kpos >= 0) & (kpos < DEC_SEQ)
    k0 = pl.multiple_of(n * QB, QB)
    for j in range(4):
        q = qa_ref[:, j * LANE:(j + 1) * LANE]
        acc = None
        for par in range(2):
            cc = (2 * (j // 2) + par) * LANE
            s_loc = _dot_nt(q, ka4p_ref[pl.ds(k0, 3 * QB), cc:cc + LANE])
            s_loc = jnp.where(valid, s_loc, -jnp.inf)
            s_ctx = _dot_nt(q, ka4c_ref[:, cc:cc + LANE])
            o = _softmax_pv([s_ctx, s_loc],
                            [va4c_ref[:, cc:cc + LANE], va4p_ref[pl.ds(k0, 3 * QB), cc:cc + LANE]],
                            sink_ref[2 * j + par])
            acc = o if acc is None else acc + o
        o_ref[:, j * LANE:(j + 1) * LANE] = acc.astype(BF16)
    for j in range(4):
        acc = None
        for par, v_ref in enumerate((vlo_ref, vhi_ref)):
            hh = 2 * j + par
            s = _dot_nt(qb_ref[:, hh * SLOT:(hh + 1) * SLOT], kb_ref[:, hh * SLOT:(hh + 1) * SLOT])
            o = _softmax_pv([s], [v_ref[:, j * LANE:(j + 1) * LANE]], None)
            acc = o if acc is None else acc + o
        o_ref[:, 512 + j * LANE:512 + (j + 1) * LANE] = acc.astype(BF16)


def _attn_lat_call(sink, qa, ka4p, va4p, ka4c, va4c, qb, kball, vlo_all, vhi_all):
    qrow = lambda w: pl.BlockSpec((QB, w), lambda b, n: (b * N_QB + n, 0))
    per_b = lambda rows, w: pl.BlockSpec((None, rows, w), lambda b, n: (b, 0, 0))
    return pl.pallas_call(
        _attn_lat_kernel,
        out_shape=jax.ShapeDtypeStruct((T_LAT, D), BF16),
        grid=(DEC_BATCH, N_QB),
        in_specs=[pl.BlockSpec(memory_space=pltpu.SMEM),
                  qrow(512), per_b(KPAD, 512), per_b(KPAD, 512), per_b(PAST, 512), per_b(PAST, 512),
                  qrow(1024), per_b(KALL, 1024), per_b(KALL, 512), per_b(KALL, 512)],
        out_specs=qrow(D),
        compiler_params=pltpu.CompilerParams(dimension_semantics=("arbitrary", "arbitrary")),
        name="attn_lat",
    )(sink, qa, ka4p, va4p, ka4c, va4c, qb, kball, vlo_all, vhi_all)


def _post_attn_kernel(x_ref, o_ref, ada_ref, g_ref, wo_ref, wg_ref, wu_ref, wd_ref, y_ref):
    mix = _dot(o_ref[...], wo_ref[...])
    x1 = x_ref[...] + ada_ref[2:3, :] * mix
    h = _modulate(x1, g_ref[...], ada_ref[3:4, :], ada_ref[4:5, :]).astype(BF16)
    acc = None
    for c0 in range(0, D_FF, FFN_CH):
        gg = _dot(h, wg_ref[:, c0:c0 + FFN_CH])
        uu = _dot(h, wu_ref[:, c0:c0 + FFN_CH])
        a = (gg * _sigmoid(gg) * uu).astype(BF16)
        part = _dot(a, wd_ref[c0:c0 + FFN_CH, :])
        acc = part if acc is None else acc + part
    y_ref[...] = x1 + ada_ref[5:6, :] * acc


def _tile_cond(i, tiles_per_ctx, tiles_per_lat_seq):
    return jnp.where(i < tiles_per_ctx, 0, 1 + (i - tiles_per_ctx) // tiles_per_lat_seq)


def _post_attn_call(x, o, ada, layer, g, wo, wg, wu, wd):
    n_tiles = T_ALL // FFN_TM
    ada_map = lambda i: (layer, _tile_cond(i, T_CTX // FFN_TM, DEC_SEQ // FFN_TM), 0, 0)
    one = pl.Buffered(1)
    return pl.pallas_call(
        _post_attn_kernel,
        out_shape=jax.ShapeDtypeStruct((T_ALL, D), F32),
        grid=(n_tiles,),
        in_specs=[pl.BlockSpec((FFN_TM, D), lambda i: (i, 0)),
                  pl.BlockSpec((FFN_TM, D), lambda i: (i, 0)),
                  pl.BlockSpec((None, None, N_ADA, D), ada_map),
                  _const_spec((1, D)),
                  pl.BlockSpec((D, D), lambda i: (0, 0), pipeline_mode=one),
                  pl.BlockSpec((D, D_FF), lambda i: (0, 0), pipeline_mode=one),
                  pl.BlockSpec((D, D_FF), lambda i: (0, 0), pipeline_mode=one),
                  pl.BlockSpec((D_FF, D), lambda i: (0, 0), pipeline_mode=one)],
        out_specs=pl.BlockSpec((FFN_TM, D), lambda i: (i, 0)),
        compiler_params=pltpu.CompilerParams(
            dimension_semantics=("arbitrary",), vmem_limit_bytes=VMEM_BIG),
        name="post_attn_ffn",
    )(x, o, ada, g, wo, wg, wu, wd)


HALO = 16
EXT = TM + 2 * HALO
CONV_RC = 64
CONV_LC = 128
CONV_WIN = CONV_RC + 24
CONV_LN_ROWS = 32


def _conv_router_kernel(x_ref, xp_ref, xn_ref, ada_ref, gm_ref, gf_ref, w1_ref, b1_ref, wdw_ref, bdw_ref,
                        gln_ref, bln_ref, w2_ref, b2_ref, wr_ref,
                        x1_o, h2_o, ri_o, rw_o, cnt_o, p_ref, c_ref, v_ref, run_ref):
    i = pl.program_id(0)

    @pl.when(i == 0)
    def _():
        run_ref[...] = jnp.zeros_like(run_ref)

    is_lat = i >= N_CTX_TILES
    j = (i - N_CTX_TILES) % LAT_TILES_PER_SEQ
    lflag = jnp.where(is_lat & (j != 0), 1.0, 0.0)
    rflag = jnp.where(is_lat & (j != LAT_TILES_PER_SEQ - 1), 1.0, 0.0)

    x = x_ref[...]
    xx = jnp.concatenate([xp_ref[...], x, xn_ref[...]], axis=0)
    h = _modulate(xx, gm_ref[...], ada_ref[0:1, :], ada_ref[1:2, :])
    a = _dot(h.astype(BF16), w1_ref[...]) + b1_ref[...]
    u = a[:, :D] * _sigmoid(a[:, D:])
    row = lax.broadcasted_iota(jnp.int32, (EXT, 1), 0)
    u = u * jnp.where(row < HALO, lflag, jnp.where(row >= HALO + TM, rflag, 1.0))

    p_ref[...] = u
    for rc in range(TM // CONV_RC):
        for lc in range(D // CONV_LC):
            ls = slice(lc * CONV_LC, (lc + 1) * CONV_LC)
            acc = jnp.zeros((CONV_RC, CONV_LC), F32) + bdw_ref[:, ls]
            base = p_ref[rc * CONV_RC:rc * CONV_RC + CONV_WIN + 8, ls]
            for r in range(8):
                win = base if r == 0 else pltpu.roll(base, CONV_WIN + 8 - r, 0)
                for q in range(4):
                    o = 8 * q + r
                    if 1 <= o <= CONV_W:
                        acc = acc + wdw_ref[o - 1:o, ls] * win[8 * q:8 * q + CONV_RC]
            c_ref[rc * CONV_RC:(rc + 1) * CONV_RC, ls] = acc

    for c0 in range(0, TM, CONV_LN_ROWS):
        acc = c_ref[c0:c0 + CONV_LN_ROWS, :]
        mu = jnp.mean(acc, axis=-1, keepdims=True)
        xc = acc - mu
        y = xc * lax.rsqrt(jnp.mean(xc * xc, axis=-1, keepdims=True) + EPS)
        y = y * gln_ref[...] + bln_ref[...]
        v_ref[c0:c0 + CONV_LN_ROWS, :] = (y * _sigmoid(y)).astype(BF16)

    mix = _dot(v_ref[...], w2_ref[...]) + b2_ref[...]
    x1 = x + ada_ref[2:3, :] * mix
    x1_o[...] = x1
    h2 = _modulate(x1, gf_ref[...], ada_ref[3:4, :], ada_ref[4:5, :])
    h2_o[...] = h2.astype(BF16)

    lane = lax.broadcasted_iota(jnp.int32, (TM, LANE), 1)
    lane_f = lane.astype(F32)
    logits = jnp.where(lane < N_EXP, _dot_x3(h2, wr_ref[...]), -jnp.inf)
    m1 = jnp.max(logits, axis=-1, keepdims=True)
    i1 = jnp.min(jnp.where(logits == m1, lane_f, float(LANE)), axis=-1, keepdims=True)
    rest = jnp.where(lane_f == i1, -jnp.inf, logits)
    m2 = jnp.max(rest, axis=-1, keepdims=True)
    i2 = jnp.min(jnp.where(rest == m2, lane_f, float(LANE)), axis=-1, keepdims=True)
    e = jnp.exp(m2 - m1)
    w1 = 1.0 / (1.0 + e)
    w2 = e / (1.0 + e)

    oh1 = jnp.where(lane_f == i1, 1.0, 0.0)
    oh2 = jnp.where(lane_f == i2, 1.0, 0.0)
    tr = lax.broadcasted_iota(jnp.int32, (TM, TM), 0)
    tc = lax.broadcasted_iota(jnp.int32, (TM, TM), 1)
    tri = jnp.where(tr > tc, 1.0, 0.0).astype(BF16)
    run = run_ref[0:1, :]
    cnt1 = jnp.sum(oh1, axis=0, keepdims=True)
    cnt2 = jnp.sum(oh2, axis=0, keepdims=True)
    pre1 = _dot(tri, oh1.astype(BF16)) + run
    pre2 = _dot(tri, oh2.astype(BF16)) + (run + cnt1)
    rank1 = jnp.sum(oh1 * pre1, axis=-1, keepdims=True)
    rank2 = jnp.sum(oh2 * pre2, axis=-1, keepdims=True)
    new_run = run + cnt1 + cnt2
    run_ref[...] = jnp.broadcast_to(new_run, run_ref.shape)
    srow = lax.broadcasted_iota(jnp.int32, cnt_o.shape, 0)
    cnt_o[...] = jnp.where(srow == 0, run, jnp.where(srow == 1, cnt1 + cnt2, 0.0))

    ri = jnp.where(lane == 0, i1, jnp.where(lane == 1, i2, jnp.where(lane == 2, rank1, jnp.where(lane == 3, rank2, 0.0))))
    ri_o[...] = ri.astype(jnp.int32)

    rw_o[...] = jnp.where(lane == 0, w1, jnp.where(lane == 1, w2, 0.0))


def _conv_router_call(x, ada, layer, gm, gf, cw):
    ada_map = lambda i: (layer, _tile_cond(i, N_CTX_TILES, LAT_TILES_PER_SEQ), 0, 0)
    hb = TM // HALO
    row = lambda w: pl.BlockSpec((TM, w), lambda i: (i, 0))
    return pl.pallas_call(
        _conv_router_kernel,
        out_shape=[jax.ShapeDtypeStruct((T_ALL, D), F32), jax.ShapeDtypeStruct((T_ALL, D), BF16),
                   jax.ShapeDtypeStruct((T_ALL, LANE), jnp.int32), jax.ShapeDtypeStruct((T_ALL, LANE), F32),
                   jax.ShapeDtypeStruct((N_TILES, 8, LANE), F32)],
        grid=(N_TILES,),
        in_specs=[row(D),
                  pl.BlockSpec((HALO, D), lambda i: (jnp.maximum(i * hb - 1, 0), 0)),
                  pl.BlockSpec((HALO, D), lambda i: (jnp.minimum((i + 1) * hb, T_ALL // HALO - 1), 0)),
                  pl.BlockSpec((None, None, N_ADA, D), ada_map),
                  _const_spec((1, D)), _const_spec((1, D)),
                  _const_spec((D, 2 * D)), _const_spec((1, 2 * D)),
                  _const_spec((CONV_W, D)), _const_spec((1, D)),
                  _const_spec((1, D)), _const_spec((1, D)),
                  _const_spec((D, D)), _const_spec((1, D)),
                  _const_spec((D, LANE))],
        out_specs=[row(D), row(D), row(LANE), row(LANE), pl.BlockSpec((None, 8, LANE), lambda i: (i, 0, 0))],
        scratch_shapes=[pltpu.VMEM((EXT, D), F32), pltpu.VMEM((TM, D), F32), pltpu.VMEM((TM, D), BF16),
                        pltpu.VMEM((8, LANE), F32)],
        compiler_params=pltpu.CompilerParams(
            dimension_semantics=("arbitrary",), vmem_limit_bytes=VMEM_BIG),
        name="conv_router",
    )(x, x, x, ada, gm, gf, cw["w1"], cw["b1"], cw["wdw"], cw["bdw"], cw["gln"], cw["bln"],
      cw["w2"], cw["b2"], cw["wr"])


def _moe_kernel(se_ref, row0_ref, nsub_ref, ta_ref, tn_ref, h_ref, post_ref, rwt_ref, wg_ref, wu_ref, wd_ref,
                ys_in_ref, ys_ref, hb_ref, acc_ref, gs_ref, hstage, wgb, wub, wdb, sem_in, sem_out):
    del ys_in_ref
    j = pl.program_id(0)
    f = pl.program_id(1)
    nsub = nsub_ref[j]
    row0 = row0_ref[j]
    e = se_ref[j]

    def tile_load(i, slot):
        r = pl.multiple_of(i * TM, TM)
        return pltpu.make_async_copy(h_ref.at[pl.ds(r, TM), :], hstage.at[slot], sem_in.at[slot])

    def store(s):
        r = pl.multiple_of(s * MOE_SUB, MOE_SUB)
        dst = ys_ref.at[pl.ds(pl.multiple_of(row0 + s * MOE_SUB, MOE_SUB), MOE_SUB), :]
        return pltpu.make_async_copy(hb_ref.at[pl.ds(r, MOE_SUB), :], dst, sem_out)

    def gather_tile(i, carry):
        slot = i % MOE_GSLOTS
        tile_load(i, slot).wait()

        @pl.when(i + MOE_GDEPTH < N_TILES)
        def _():
            tile_load(i + MOE_GDEPTH, (i + MOE_GDEPTH) % MOE_GSLOTS).start()

        a = ta_ref[i * N_EXP + e] - row0
        n = tn_ref[i * N_EXP + e]
        last = a + n - 1

        @pl.when((n > 0) & (last >= 0) & (a < nsub * MOE_SUB))
        def _():
            lo = jnp.maximum(a, 0) // MOE_GB
            hi = jnp.minimum(last, nsub * MOE_SUB - 1) // MOE_GB
            pp = post_ref[pl.ds(pl.multiple_of(i * 8, 8), 8), :]
            riota = lax.broadcasted_iota(jnp.int32, (MOE_GB, TM), 0)

            ww = rwt_ref[pl.ds(pl.multiple_of(i * 8, 8), 8), :]

            def block(b):
                base = row0 + b * MOE_GB
                hit0 = (pp[0:1, :] - base) == riota
                hit1 = (pp[1:2, :] - base) == riota
                rr = pl.multiple_of(b * MOE_GB, MOE_GB)
                acc_ref[pl.ds(rr, MOE_GB), :] += _dot(jnp.where(hit0 | hit1, 1.0, 0.0).astype(BF16),
                                                      hstage[slot])
                gate = jnp.sum(jnp.where(hit0, ww[0:1, :], 0.0) + jnp.where(hit1, ww[1:2, :], 0.0),
                               axis=-1, keepdims=True)
                gs_ref[pl.ds(rr, MOE_GB), :] += jnp.broadcast_to(gate, (MOE_GB, LANE))

            block(lo)
            for extra in range(1, TM // MOE_GB + 1):
                @pl.when(hi >= lo + extra)
                def _():
                    block(lo + extra)

        return carry

    def part(r):
        hh = hb_ref[pl.ds(r, MOE_SUB), :]
        out = None
        for c0 in range(0, MOE_TF, MOE_TC):
            gg = _dot(hh, wgb[:, c0:c0 + MOE_TC])
            uu = _dot(hh, wub[:, c0:c0 + MOE_TC])
            a = (gg * _sigmoid(gg) * uu).astype(BF16)
            piece = _dot(a, wdb[c0:c0 + MOE_TC, :])
            out = piece if out is None else out + piece
        return out

    def for_each_subtile(fn):
        def pair(p, carry):
            fn(2 * p)
            fn(2 * p + 1)
            return carry

        lax.fori_loop(0, nsub // 2, pair, 0)

        @pl.when(nsub % 2 == 1)
        def _():
            fn(nsub - 1)

    @pl.when(nsub > 0)
    def _():
        wgb[...] = wg_ref[...].astype(BF16)
        wub[...] = wu_ref[...].astype(BF16)
        wdb[...] = wd_ref[...].astype(BF16)

        @pl.when(f == 0)
        def _():
            for d in range(MOE_GDEPTH):
                tile_load(d, d).start()

            def clear(s, carry):
                r = pl.multiple_of(s * MOE_SUB, MOE_SUB)
                acc_ref[pl.ds(r, MOE_SUB), :] = jnp.zeros((MOE_SUB, D), F32)
                gs_ref[pl.ds(r, MOE_SUB), :] = jnp.zeros((MOE_SUB, LANE), F32)
                return carry

            lax.fori_loop(0, nsub, clear, 0)
            lax.fori_loop(0, N_TILES, gather_tile, 0)

            def first(s):
                r = pl.multiple_of(s * MOE_SUB, MOE_SUB)
                hb_ref[pl.ds(r, MOE_SUB), :] = acc_ref[pl.ds(r, MOE_SUB), :].astype(BF16)
                acc_ref[pl.ds(r, MOE_SUB), :] = part(r)

            for_each_subtile(first)

        @pl.when((f > 0) & (f < MOE_NF - 1))
        def _():
            def middle(s):
                r = pl.multiple_of(s * MOE_SUB, MOE_SUB)
                acc_ref[pl.ds(r, MOE_SUB), :] += part(r)

            for_each_subtile(middle)

        @pl.when(f == MOE_NF - 1)
        def _():
            def final(s):
                r = pl.multiple_of(s * MOE_SUB, MOE_SUB)
                y = acc_ref[pl.ds(r, MOE_SUB), :] + part(r)
                hb_ref[pl.ds(r, MOE_SUB), :] = (y * gs_ref[pl.ds(r, MOE_SUB), 0:1]).astype(BF16)
                store(s).start()

            for_each_subtile(final)

            def drain(s, carry):
                store(s).wait()
                return carry

            lax.fori_loop(0, nsub, drain, 0)


def _moe_call(st_e, st_row0, st_nsub, t_a, t_n, h2b, pos_t, rw_t, wg, wu, wd, ys0, lidx):
    f_eff = lambda j, f, ns: jnp.where(ns[j] > 0, f, MOE_NF - 1)
    wmap_in = lambda j, f, se, r0, ns, ta, tn: (lidx, se[j], 0, f_eff(j, f, ns))
    wmap_out = lambda j, f, se, r0, ns, ta, tn: (lidx, se[j], f_eff(j, f, ns), 0)
    return pl.pallas_call(
        _moe_kernel,
        out_shape=jax.ShapeDtypeStruct((MOE_ROWS, D), BF16),
        grid_spec=pltpu.PrefetchScalarGridSpec(
            num_scalar_prefetch=5, grid=(MOE_NS, MOE_NF),
            in_specs=[
                pl.BlockSpec(memory_space=pl.ANY),
                pl.BlockSpec((N_TILES * 8, TM), lambda j, f, se, r0, ns, ta, tn: (0, 0)),
                pl.BlockSpec((N_TILES * 8, TM), lambda j, f, se, r0, ns, ta, tn: (0, 0)),
                pl.BlockSpec((None, None, D, MOE_TF), wmap_in),
                pl.BlockSpec((None, None, D, MOE_TF), wmap_in),
                pl.BlockSpec((None, None, MOE_TF, D), wmap_out),
                pl.BlockSpec(memory_space=pl.ANY),
            ],
            out_specs=pl.BlockSpec(memory_space=pl.ANY),
            scratch_shapes=[pltpu.VMEM((MOE_R, D), BF16), pltpu.VMEM((MOE_R, D), F32),
                            pltpu.VMEM((MOE_R, LANE), F32),
                            pltpu.VMEM((MOE_GSLOTS, TM, D), BF16),
                            pltpu.VMEM((D, MOE_TF), BF16), pltpu.VMEM((D, MOE_TF), BF16),
                            pltpu.VMEM((MOE_TF, D), BF16),
                            pltpu.SemaphoreType.DMA((MOE_GSLOTS,)), pltpu.SemaphoreType.DMA(())]),
        input_output_aliases={11: 0},
        compiler_params=pltpu.CompilerParams(
            dimension_semantics=("arbitrary", "arbitrary"), vmem_limit_bytes=VMEM_BIG,
            has_side_effects=True),
        name="moe_experts",
    )(st_e, st_row0, st_nsub, t_a, t_n, h2b, pos_t, rw_t, wg, wu, wd, ys0)


def _combine_kernel(split, ta_ref, tn_ref, x_ref, ada_ref, posc_ref, ys_ref, *rest):
    if split:
        oc_ref, ol_ref, ybuf, oacc, bc, sem = rest
    else:
        o_ref, ybuf, oacc, bc, sem = rest
    g = pl.program_id(0)

    def blocks(i, e):
        a = ta_ref[i * N_EXP + e]
        n = tn_ref[i * N_EXP + e]
        lo = jnp.minimum(a // MOE_SUB, MOE_ROWS // MOE_SUB - 1)
        return n > 0, lo, (n > 0) & ((a + n - 1) // MOE_SUB > lo)

    def block_copy(slot, e, k, blk):
        src = ys_ref.at[pl.ds(pl.multiple_of(blk * MOE_SUB, MOE_SUB), MOE_SUB), :]
        return pltpu.make_async_copy(src, ybuf.at[slot, 2 * e + k], sem.at[slot, 2 * e + k])

    @pl.when(g < N_TILES)
    def _():
        for e in range(N_EXP):
            _, lo, two = blocks(g, e)
            block_copy(g % 2, e, 0, lo).start()

            @pl.when(two)
            def _():
                block_copy(g % 2, e, 1, lo + 1).start()

    @pl.when(g > 0)
    def _():
        i = g - 1
        slot = i % 2
        for n in range(2):
            bc[n] = jnp.broadcast_to(posc_ref[:, n:n + 1], (TM, MOE_SUB))
        liota = lax.broadcasted_iota(jnp.int32, (TM, MOE_SUB), 1)

        def weighted(has, blk, y):
            base = jnp.where(has, blk * MOE_SUB, -MOE_ROWS)
            hit = (bc[0] - base == liota) | (bc[1] - base == liota)
            return _dot(jnp.where(hit, 1.0, 0.0).astype(BF16), y)

        total = None
        for e in range(N_EXP):
            has, lo, _ = blocks(i, e)
            block_copy(slot, e, 0, lo).wait()
            t = weighted(has, lo, ybuf[slot, 2 * e])
            total = t if total is None else total + t
        oacc[...] = total
        for e in range(N_EXP):
            has, lo, two = blocks(i, e)

            @pl.when(two)
            def _():
                block_copy(slot, e, 1, lo + 1).wait()
                oacc[...] += weighted(has, lo + 1, ybuf[slot, 2 * e + 1])

        out = x_ref[...] + ada_ref[5:6, :] * oacc[...]
        if split:
            @pl.when(i < N_CTX_TILES)
            def _():
                oc_ref[...] = out

            @pl.when(i >= N_CTX_TILES)
            def _():
                ol_ref[...] = out
        else:
            o_ref[...] = out


def _combine_call(split, t_a, t_n, x1, ada, layer, posc, ys):
    prev = lambda g: jnp.maximum(g - 1, 0)
    ada_map = lambda g, a, n: (layer, _tile_cond(prev(g), N_CTX_TILES, LAT_TILES_PER_SEQ), 0, 0)
    if split:
        out_shape = [jax.ShapeDtypeStruct((T_CTX, D), F32), jax.ShapeDtypeStruct((T_LAT, D), F32)]
        out_specs = [pl.BlockSpec((TM, D), lambda g, a, n: (jnp.minimum(prev(g), N_CTX_TILES - 1), 0)),
                     pl.BlockSpec((TM, D), lambda g, a, n: (jnp.maximum(prev(g) - N_CTX_TILES, 0), 0))]
    else:
        out_shape = jax.ShapeDtypeStruct((T_ALL, D), F32)
        out_specs = pl.BlockSpec((TM, D), lambda g, a, n: (prev(g), 0))
    return pl.pallas_call(
        functools.partial(_combine_kernel, split),
        out_shape=out_shape,
        grid_spec=pltpu.PrefetchScalarGridSpec(
            num_scalar_prefetch=2, grid=(N_TILES + 1,),
            in_specs=[pl.BlockSpec((TM, D), lambda g, a, n: (prev(g), 0)),
                      pl.BlockSpec((None, None, N_ADA, D), ada_map),
                      pl.BlockSpec((TM, LANE), lambda g, a, n: (prev(g), 0)),
                      pl.BlockSpec(memory_space=pl.ANY)],
            out_specs=out_specs,
            scratch_shapes=[pltpu.VMEM((2, 2 * N_EXP, MOE_SUB, D), BF16), pltpu.VMEM((TM, D), F32),
                            pltpu.VMEM((2, TM, MOE_SUB), jnp.int32),
                            pltpu.SemaphoreType.DMA((2, 2 * N_EXP))]),
        compiler_params=pltpu.CompilerParams(
            dimension_semantics=("arbitrary",), vmem_limit_bytes=VMEM_BIG),
        name="moe_combine",
    )(t_a, t_n, x1, ada, posc, ys)


def _moe_layer(x1, h2b, ri, rw, runs, ada, layer, wg, wu, wd, lidx, split):
    idx = ri[:, 0:2]
    rank = ri[:, 2:4]
    before = runs[:, 0, :N_EXP].astype(jnp.int32)
    t_n = runs[:, 1, :N_EXP].astype(jnp.int32)
    counts = before[-1] + t_n[-1]
    padded = ((counts + MOE_SUB - 1) // MOE_SUB) * MOE_SUB
    ends = jnp.cumsum(padded)
    starts = ends - padded
    sel = idx[:, :, None] == jnp.arange(N_EXP, dtype=jnp.int32)[None, None, :]
    pos = (jnp.sum(jnp.where(sel, starts[None, None, :], 0), axis=-1) + rank).astype(jnp.int32)
    t_a = (starts[None, :] + before).reshape(-1).astype(jnp.int32)
    t_n = t_n.reshape(-1)
    pos_t = jnp.pad(pos.reshape(N_TILES, TM, 2).transpose(0, 2, 1), ((0, 0), (0, 6), (0, 0)))
    pos_t = pos_t.reshape(N_TILES * 8, TM)
    posc = jnp.pad(pos, ((0, 0), (0, LANE - 2)))
    n_st = (padded + MOE_R - 1) // MOE_R
    st_end = jnp.cumsum(n_st)
    st_begin = st_end - n_st
    j = jnp.arange(MOE_NS, dtype=jnp.int32)
    se = jnp.sum((j[:, None] >= st_end[None, :]).astype(jnp.int32), axis=1)
    used = se < N_EXP
    se_c = jnp.minimum(se, N_EXP - 1)
    k = j - st_begin[se_c]
    row0 = jnp.where(used, starts[se_c] + k * MOE_R, 0).astype(jnp.int32)
    nsub = jnp.where(used, jnp.minimum(MOE_R, padded[se_c] - k * MOE_R) // MOE_SUB, 0).astype(jnp.int32)
    last = jnp.maximum(jnp.sum(used.astype(jnp.int32)) - 1, 0)
    se_eff = jnp.where(used, se_c, se_c[last]).astype(jnp.int32)
    rw_t = jnp.pad(rw[:, 0:2].reshape(N_TILES, TM, 2).transpose(0, 2, 1), ((0, 0), (0, 6), (0, 0)))
    rw_t = rw_t.reshape(N_TILES * 8, TM)
    ys = _moe_call(se_eff, row0, nsub, t_a, t_n, h2b, pos_t, rw_t, wg, wu, wd,
                   jnp.zeros((MOE_ROWS, D), BF16), lidx)
    return _combine_call(split, t_a, t_n, x1, ada, layer, posc, ys)


def _rope_tables():
    t = jnp.arange(DEC_SEQ)
    rows = (t // GRID_W).astype(F32)
    cols = (t % GRID_W).astype(F32)

    def axis_tabs(pos, d):
        inv = ROPE_BASE ** (-jnp.arange(0, d, 2, dtype=F32) / d)
        ang = pos[:, None] * inv[None, :]
        cos, sin = jnp.cos(ang), jnp.sin(ang)
        z = jnp.zeros_like(sin)
        return (jnp.concatenate([cos, cos], -1), jnp.concatenate([-sin, z], -1), jnp.concatenate([z, sin], -1))

    def axial(d):
        r = axis_tabs(rows, d // 2)
        c = axis_tabs(cols, d // 2)
        return [jnp.concatenate([a, b], -1) for a, b in zip(r, c)]

    tabs_a = [jnp.tile(tb, (1, A_HEADS)) for tb in axial(A_DH)]
    cb, sub, sdb = axial(B_ROPE)
    ones = jnp.ones((DEC_SEQ, B_NOPE), F32)
    zer = jnp.zeros((DEC_SEQ, B_NOPE), F32)
    pad = jnp.zeros((DEC_SEQ, SLOT - B_QK), F32)
    slot = lambda first, tb: jnp.tile(jnp.concatenate([first, tb, pad], -1), (1, B_HEADS))
    tabs_b = [slot(ones, cb), slot(zer, sub), slot(zer, sdb)]
    return tabs_a + tabs_b


def _attn_weights(e, w_in_attn, g_qk_a_q, g_qk_a_k, g_mla_q_a, w_mla_q_b, g_mla_kv_a, w_mla_kv_b,
                  g_qk_b_q, g_qk_b_k):
    w_in = jnp.pad(w_in_attn[e], ((0, 0), (0, IN_PAD - IN_DIM))).astype(BF16)
    w_qb = jnp.pad(w_mla_q_b[e].reshape(B_Q_LORA, B_HEADS, B_QK), ((0, 0), (0, 0), (0, SLOT - B_QK)))
    w_qb = w_qb.reshape(B_Q_LORA, B_HEADS * SLOT).astype(BF16)
    wkv = w_mla_kv_b[e].reshape(B_KV_LORA, B_HEADS, B_NOPE + B_V)
    wk = jnp.pad(wkv[:, :, :B_NOPE], ((0, 0), (0, 0), (0, SLOT - B_NOPE))).reshape(B_KV_LORA, B_HEADS * SLOT)
    wv = wkv[:, :, B_NOPE:].reshape(B_KV_LORA, B_HEADS * B_V)
    w_kv = jnp.concatenate([wk, wv], axis=1).astype(BF16)
    place = jnp.zeros((LANE, B_HEADS, SLOT), F32)
    ii = jnp.arange(B_ROPE)
    place = place.at[ii, :, B_NOPE + ii].set(1.0).reshape(LANE, B_HEADS * SLOT).astype(BF16)
    slot_gain = lambda g: jnp.tile(jnp.pad(g, (0, SLOT - B_QK)), B_HEADS).reshape(1, B_HEADS * SLOT)
    return dict(
        w_in=w_in, w_qb=w_qb, w_kv=w_kv, place=place,
        gqa=jnp.tile(g_qk_a_q[e], A_HEADS).reshape(1, 512),
        gka=jnp.tile(g_qk_a_k[e], A_KV).reshape(1, LANE),
        gql=g_mla_q_a[e].reshape(1, B_Q_LORA), gkl=g_mla_kv_a[e].reshape(1, B_KV_LORA),
        gqb=slot_gain(g_qk_b_q[e]), gkb=slot_gain(g_qk_b_k[e]))


def _pair_variants_host(x):
    h0, h1 = x[..., :64], x[..., 64:]
    z = jnp.zeros_like(h0)
    return jnp.concatenate([h0, z, z, h0, h1, z, z, h1], axis=-1)


def kernel(x_prompt, x_sample, c, cache_k_win, cache_v_win, cache_ckv, cache_krope, c_ctx, w_ada, b_ada, g_norm_mix, g_norm_ffn, w_in_attn, g_qk_a_q, g_qk_a_k, sink_a, g_mla_q_a, w_mla_q_b, g_mla_kv_a, w_mla_kv_b, g_qk_b_q, g_qk_b_k, w_out_attn, w_ffn_gate, w_ffn_up, w_ffn_down, w_conv_pw1, b_conv_pw1, w_conv_dw, b_conv_dw, g_conv_ln, b_conv_ln, w_conv_pw2, b_conv_pw2, w_router, w_moe_gate, w_moe_up, w_moe_down):
    x = jnp.concatenate([x_prompt.reshape(T_CTX, D), x_sample.reshape(T_LAT, D)], axis=0)
    cond8 = jnp.concatenate([c_ctx[None, :], c, jnp.zeros((8 - 1 - DEC_BATCH, D), F32)], axis=0)
    ada = _ada_call(cond8, w_ada, b_ada).reshape(DEPTH, 8, N_ADA, D)
    tabs = _rope_tables()

    wo_b = w_out_attn.astype(BF16)
    wg_b = w_ffn_gate.astype(BF16)
    wu_b = w_ffn_up.astype(BF16)
    wd_b = w_ffn_down.astype(BF16)
    w1_b = w_conv_pw1.astype(BF16)
    w2_b = w_conv_pw2.astype(BF16)

    new_k, new_v, new_ckv, new_kr = [], [], [], []
    for layer in range(DEPTH):
        gm = g_norm_mix[layer].reshape(1, D)
        gf = g_norm_ffn[layer].reshape(1, D)
        if layer % 2 == 0:
            e = layer // 2
            lw = _attn_weights(e, w_in_attn, g_qk_a_q, g_qk_a_k, g_mla_q_a, w_mla_q_b, g_mla_kv_a,
                               w_mla_kv_b, g_qk_b_q, g_qk_b_k)
            sink = sink_a[e].reshape(A_HEADS)
            (qa, ka4, va4, qb, kb, vlo, vhi, kaf, vaf, ckvf, krf) = _pre_attn_call(False, x, ada, layer, gm, lw, None)
            o_ctx = _attn_ctx_call(sink, qa, ka4, va4, qb, kb, vlo, vhi)
            new_k.append(kaf.reshape(BATCH, SEQ, A_KV, A_DH))
            new_v.append(vaf.reshape(BATCH, SEQ, A_KV, A_DH))
            new_ckv.append(ckvf.reshape(BATCH, SEQ, B_KV_LORA))
            new_kr.append(krf[:, :B_ROPE].reshape(BATCH, SEQ, B_ROPE))
            (qa, ka4, va4, qb, kb, vlo, vhi) = _pre_attn_call(True, x, ada, layer, gm, lw, tabs)
            kr_c = jnp.pad(cache_krope[:, e].reshape(DEC_BATCH * PAST, B_ROPE), ((0, 0), (0, LANE - B_ROPE)))
            kb_c, vlo_c, vhi_c = _cache_kv_call(cache_ckv[:, e].reshape(DEC_BATCH * PAST, B_KV_LORA), kr_c, lw)
            pad_rows = lambda a: jnp.pad(a.reshape(DEC_BATCH, DEC_SEQ, 512), ((0, 0), (QB, QB), (0, 0)))
            ka4c = _pair_variants_host(cache_k_win[:, e].reshape(DEC_BATCH, PAST, LANE)).astype(BF16)
            va4c = _pair_variants_host(cache_v_win[:, e].reshape(DEC_BATCH, PAST, LANE)).astype(BF16)
            cat = lambda a, b, w: jnp.concatenate(
                [a.reshape(DEC_BATCH, PAST, w), b.reshape(DEC_BATCH, DEC_SEQ, w)], axis=1)
            o_lat = _attn_lat_call(sink, qa, pad_rows(ka4), pad_rows(va4), ka4c, va4c, qb,
                                   cat(kb_c, kb, 1024), cat(vlo_c, vlo, 512), cat(vhi_c, vhi, 512))
            o = jnp.concatenate([o_ctx, o_lat], axis=0)
            x = _post_attn_call(x, o, ada, layer, gf, wo_b[e], wg_b[e], wu_b[e], wd_b[e])
        else:
            o_ = layer // 2
            cw = dict(w1=w1_b[o_], b1=b_conv_pw1[o_].reshape(1, 2 * D), wdw=w_conv_dw[o_],
                      bdw=b_conv_dw[o_].reshape(1, D), gln=g_conv_ln[o_].reshape(1, D),
                      bln=b_conv_ln[o_].reshape(1, D), w2=w2_b[o_], b2=b_conv_pw2[o_].reshape(1, D),
                      wr=jnp.pad(w_router[o_], ((0, 0), (0, LANE - N_EXP))))
            x1, h2b, ri, rw, runs = _conv_router_call(x, ada, layer, gm, gf, cw)
            x = _moe_layer(x1, h2b, ri, rw, runs, ada, layer, w_moe_gate, w_moe_up, w_moe_down, o_,
                           split=(layer == DEPTH - 1))

    y_prompt = x[0].reshape(BATCH, SEQ, D)
    y_sample = x[1].reshape(DEC_BATCH, DEC_SEQ, D)
    return (y_prompt, y_sample, jnp.stack(new_k, axis=1), jnp.stack(new_v, axis=1),
            jnp.stack(new_ckv, axis=1), jnp.stack(new_kr, axis=1))
```

```python
import functools

import jax
import jax.numpy as jnp
from jax import lax
from jax.experimental import pallas as pl
from jax.experimental.pallas import tpu as pltpu

F32 = jnp.float32
BF16 = jnp.bfloat16

D = 1024
BATCH = 32
SEQ = 256
DEPTH = 4
DEC_BATCH = 2
DEC_SEQ = 1024
PAST = 256
GRID_W = 64
N_EVEN = 2
A_HEADS = 8
A_KV = 2
A_DH = 64
WINDOW = 128
B_HEADS = 8
B_Q_LORA = 256
B_KV_LORA = 128
B_NOPE = 64
B_ROPE = 32
B_QK = B_NOPE + B_ROPE
B_V = 64
IN_DIM = 1184
IN_PAD = 1280
CONV_W = 31
D_FF = 2816
N_EXP = 8
D_FFE = 3584
ROPE_BASE = 10000.0
EPS = 1e-6
N_ADA = 6

T_CTX = BATCH * SEQ
T_LAT = DEC_BATCH * DEC_SEQ
T_ALL = T_CTX + T_LAT
TM = 256
N_TILES = T_ALL // TM
N_CTX_TILES = T_CTX // TM
LAT_TILES_PER_SEQ = DEC_SEQ // TM
LANE = 128
SLOT = 128

MOE_SUB = 256
MOE_R = 3072
MOE_TF = 512
MOE_NF = D_FFE // MOE_TF
MOE_TC = 512
MOE_GB = 128
MOE_GDEPTH = 7
MOE_GSLOTS = MOE_GDEPTH + 1
MOE_ROWS = 2 * T_ALL + N_EXP * MOE_SUB
MOE_NS = MOE_ROWS // MOE_R + N_EXP

FFN_TM = 512
FFN_CH = 1408

VMEM_BIG = 56 * 1024 * 1024


def _dot(a, b):
    return jnp.dot(a, b, preferred_element_type=F32)


def _dot_nt(a, b):
    return lax.dot_general(a, b, (((1,), (1,)), ((), ())), preferred_element_type=F32)


def _split(a):
    hi = a.astype(BF16)
    lo = (a - hi.astype(F32)).astype(BF16)
    return hi, lo


def _dot_x3(a, b):
    ah, al = _split(a)
    bh, bl = _split(b)
    return _dot(ah, bh) + (_dot(ah, bl) + _dot(al, bh))


def _sigmoid(x):
    return 1.0 / (1.0 + jnp.exp(-x))


def _rms(x, n):
    return x * lax.rsqrt(jnp.sum(x * x, axis=-1, keepdims=True) * (1.0 / n) + EPS)


def _modulate(x, g, shift, scale):
    return _rms(x, D) * g * (1.0 + scale) + shift


def _lo_mask(rows):
    return lax.broadcasted_iota(jnp.int32, (rows, LANE), 1) < 64


def _ada_kernel(cond_ref, w_ref, b_ref, o_ref):
    c = cond_ref[...]
    s = c * _sigmoid(c)
    o_ref[...] = _dot_x3(s, w_ref[...]) + b_ref[...]


def _ada_call(cond8, w_ada, b_ada):
    tn = 1024
    return pl.pallas_call(
        _ada_kernel,
        out_shape=jax.ShapeDtypeStruct((DEPTH, 8, N_ADA * D), F32),
        grid=(DEPTH, N_ADA * D // tn),
        in_specs=[
            pl.BlockSpec((8, D), lambda l, j: (0, 0)),
            pl.BlockSpec((None, D, tn), lambda l, j: (l, 0, j)),
            pl.BlockSpec((None, 1, tn), lambda l, j: (l, 0, j)),
        ],
        out_specs=pl.BlockSpec((None, 8, tn), lambda l, j: (l, 0, j)),
        compiler_params=pltpu.CompilerParams(
            dimension_semantics=("arbitrary", "arbitrary"), vmem_limit_bytes=VMEM_BIG),
        name="ada",
    )(cond8, w_ada, b_ada.reshape(DEPTH, 1, N_ADA * D))


def _rope(x, cos, sup, sdn, shift):
    up = pltpu.roll(x, LANE - shift, 1)
    dn = pltpu.roll(x, shift, 1)
    return x * cos + up * sup + dn * sdn


def _norm64_block(blk, lo):
    sq = blk * blk
    s_lo = jnp.sum(jnp.where(lo, sq, 0.0), axis=-1, keepdims=True)
    s_hi = jnp.sum(jnp.where(lo, 0.0, sq), axis=-1, keepdims=True)
    r = jnp.where(lo, lax.rsqrt(s_lo * (1.0 / A_DH) + EPS), lax.rsqrt(s_hi * (1.0 / A_DH) + EPS))
    return blk * r


def _pair_variants(x, lo):
    r = pltpu.roll(x, 64, 1)
    return (jnp.where(lo, x, 0.0), jnp.where(lo, 0.0, r), jnp.where(lo, r, 0.0), jnp.where(lo, 0.0, x))


def _mla_kv(ckv_n, kr, wkv_ref, place_ref, gk_ref, lo, kb_ref, vlo_ref, vhi_ref, rope_tabs):
    kv = _dot(ckv_n.astype(BF16), wkv_ref[...])
    kr_hi, kr_lo = _split(kr)
    place = place_ref[...]
    krs = _dot(kr_hi, place) + _dot(kr_lo, place)
    for h in range(B_HEADS):
        sl = slice(h * SLOT, (h + 1) * SLOT)
        blk = kv[:, sl] + krs[:, sl]
        blk = _rms(blk, B_QK) * gk_ref[:, sl]
        if rope_tabs is not None:
            cos, sup, sdn = rope_tabs
            blk = _rope(blk, cos[:, sl], sup[:, sl], sdn[:, sl], 8)
        kb_ref[:, sl] = blk.astype(BF16)
    for j in range(B_HEADS // 2):
        sl = slice(j * LANE, (j + 1) * LANE)
        v = kv[:, B_HEADS * SLOT + j * LANE: B_HEADS * SLOT + (j + 1) * LANE]
        vlo_ref[:, sl] = jnp.where(lo, v, 0.0).astype(BF16)
        vhi_ref[:, sl] = jnp.where(lo, 0.0, v).astype(BF16)


def _pre_attn_kernel(rope, *refs):
    if rope:
        (x_ref, ada_ref, g_ref, win_ref, wqb_ref, wkv_ref, place_ref,
         gqa_ref, gka_ref, gql_ref, gkl_ref, gqb_ref, gkb_ref,
         ca_ref, sua_ref, sda_ref, cb_ref, sub_ref, sdb_ref,
         qa_o, ka4_o, va4_o, qb_o, kb_o, vlo_o, vhi_o) = refs
    else:
        (x_ref, ada_ref, g_ref, win_ref, wqb_ref, wkv_ref, place_ref,
         gqa_ref, gka_ref, gql_ref, gkl_ref, gqb_ref, gkb_ref,
         qa_o, ka4_o, va4_o, qb_o, kb_o, vlo_o, vhi_o,
         kaf_o, vaf_o, ckvf_o, krf_o) = refs
    lo = _lo_mask(TM)
    h = _modulate(x_ref[...], g_ref[...], ada_ref[0:1, :], ada_ref[1:2, :])
    p = _dot(h.astype(BF16), win_ref[...])

    for j in range(4):
        sl = slice(j * LANE, (j + 1) * LANE)
        blk = _norm64_block(p[:, sl], lo) * gqa_ref[:, sl]
        if rope:
            blk = _rope(blk, ca_ref[:, sl], sua_ref[:, sl], sda_ref[:, sl], 16)
        qa_o[:, sl] = (blk * (A_DH ** -0.5)).astype(BF16)
    ka = _norm64_block(p[:, 512:640], lo) * gka_ref[...]
    va = p[:, 640:768]
    if not rope:
        kaf_o[...] = ka
        vaf_o[...] = va
    else:
        ka = _rope(ka, ca_ref[:, 0:LANE], sua_ref[:, 0:LANE], sda_ref[:, 0:LANE], 16)
    for n, (kk, vv) in enumerate(zip(_pair_variants(ka, lo), _pair_variants(va, lo))):
        sl = slice(n * LANE, (n + 1) * LANE)
        ka4_o[:, sl] = kk.astype(BF16)
        va4_o[:, sl] = vv.astype(BF16)

    cq = _rms(p[:, 768:1024], B_Q_LORA) * gql_ref[...]
    qb = _dot(cq.astype(BF16), wqb_ref[...])
    for hh in range(B_HEADS):
        sl = slice(hh * SLOT, (hh + 1) * SLOT)
        blk = _rms(qb[:, sl], B_QK) * gqb_ref[:, sl]
        if rope:
            blk = _rope(blk, cb_ref[:, sl], sub_ref[:, sl], sdb_ref[:, sl], 8)
        qb_o[:, sl] = (blk * (B_QK ** -0.5)).astype(BF16)

    ckv = _rms(p[:, 1024:1152], B_KV_LORA) * gkl_ref[...]
    kr = p[:, 1152:1280]
    if not rope:
        ckvf_o[...] = ckv
        krf_o[...] = kr
    tabs = (cb_ref, sub_ref, sdb_ref) if rope else None
    _mla_kv(ckv, kr, wkv_ref, place_ref, gkb_ref, lo, kb_o, vlo_o, vhi_o, tabs)


def _const_spec(shape):
    nd = len(shape)
    return pl.BlockSpec(shape, lambda *a: (0,) * nd)


def _pre_attn_call(rope, x, off, ada, layer, g, lw, tabs):
    n_tiles = (T_LAT if rope else T_CTX) // TM
    n_rows = n_tiles * TM
    if rope:
        ada_map = lambda i: (layer, 1 + i // LAT_TILES_PER_SEQ, 0, 0)
    else:
        ada_map = lambda i: (layer, 0, 0, 0)
    in_specs = [
        pl.BlockSpec((TM, D), lambda i: (i + off, 0)),
        pl.BlockSpec((None, None, N_ADA, D), ada_map),
        _const_spec((1, D)),
        _const_spec((D, IN_PAD)),
        _const_spec((B_Q_LORA, B_HEADS * SLOT)),
        _const_spec((B_KV_LORA, B_HEADS * SLOT + B_HEADS * B_V)),
        _const_spec((LANE, B_HEADS * SLOT)),
        _const_spec((1, 512)), _const_spec((1, LANE)), _const_spec((1, B_Q_LORA)),
        _const_spec((1, B_KV_LORA)), _const_spec((1, B_HEADS * SLOT)), _const_spec((1, B_HEADS * SLOT)),
    ]
    args = [x, ada, g, lw["w_in"], lw["w_qb"], lw["w_kv"], lw["place"],
            lw["gqa"], lw["gka"], lw["gql"], lw["gkl"], lw["gqb"], lw["gkb"]]
    if rope:
        tmap = lambda i: (i % LAT_TILES_PER_SEQ, 0)
        in_specs += [pl.BlockSpec((TM, 512), tmap)] * 3 + [pl.BlockSpec((TM, B_HEADS * SLOT), tmap)] * 3
        args += list(tabs)
    row = lambda w: pl.BlockSpec((TM, w), lambda i: (i, 0))
    out_shape = [jax.ShapeDtypeStruct((n_rows, w), BF16) for w in (512, 512, 512, 1024, 1024, 512, 512)]
    out_specs = [row(w) for w in (512, 512, 512, 1024, 1024, 512, 512)]
    if not rope:
        out_shape += [jax.ShapeDtypeStruct((n_rows, LANE), F32)] * 4
        out_specs += [row(LANE)] * 4
    return pl.pallas_call(
        functools.partial(_pre_attn_kernel, rope),
        out_shape=out_shape, grid=(n_tiles,), in_specs=in_specs, out_specs=out_specs,
        compiler_params=pltpu.CompilerParams(
            dimension_semantics=("arbitrary",), vmem_limit_bytes=VMEM_BIG),
        name="pre_attn_lat" if rope else "pre_attn_ctx",
    )(*args)


def _cache_kv_kernel(ckv_ref, kr_ref, wkv_ref, place_ref, gkb_ref, kb_o, vlo_o, vhi_o):
    lo = _lo_mask(PAST)
    _mla_kv(ckv_ref[...], kr_ref[...], wkv_ref, place_ref, gkb_ref, lo, kb_o, vlo_o, vhi_o, None)


def _cache_kv_call(ckv, kr, lw):
    rows = DEC_BATCH * PAST
    row = lambda w: pl.BlockSpec((PAST, w), lambda i: (i, 0))
    return pl.pallas_call(
        _cache_kv_kernel,
        out_shape=[jax.ShapeDtypeStruct((rows, 1024), BF16), jax.ShapeDtypeStruct((rows, 512), BF16),
                   jax.ShapeDtypeStruct((rows, 512), BF16)],
        grid=(DEC_BATCH,),
        in_specs=[row(LANE), row(LANE),
                  _const_spec((B_KV_LORA, B_HEADS * SLOT + B_HEADS * B_V)),
                  _const_spec((LANE, B_HEADS * SLOT)), _const_spec((1, B_HEADS * SLOT))],
        out_specs=[row(1024), row(512), row(512)],
        compiler_params=pltpu.CompilerParams(dimension_semantics=("arbitrary",)),
        name="cache_kv",
    )(ckv, kr, lw["w_kv"], lw["place"], lw["gkb"])


def _softmax_pv(scores, values, sink):
    m = jnp.max(scores[0], axis=-1, keepdims=True)
    for s in scores[1:]:
        m = jnp.maximum(m, jnp.max(s, axis=-1, keepdims=True))
    if sink is not None:
        m = jnp.maximum(m, sink)
    ps = [jnp.exp(s - m) for s in scores]
    den = jnp.sum(ps[0], axis=-1, keepdims=True)
    for p in ps[1:]:
        den = den + jnp.sum(p, axis=-1, keepdims=True)
    if sink is not None:
        den = den + jnp.exp(sink - m)
    inv = 1.0 / den
    out = _dot((ps[0] * inv).astype(BF16), values[0])
    for p, v in zip(ps[1:], values[1:]):
        out = out + _dot((p * inv).astype(BF16), v)
    return out


def _attn_ctx_kernel(sink_ref, qa_ref, ka4_ref, va4_ref, qb_ref, kb_ref, vlo_ref, vhi_ref, o_ref):
    for j in range(4):
        q = qa_ref[:, j * LANE:(j + 1) * LANE]
        acc = None
        for par in range(2):
            c = (2 * (j // 2) + par) * LANE
            s = _dot_nt(q, ka4_ref[:, c:c + LANE])
            o = _softmax_pv([s], [va4_ref[:, c:c + LANE]], sink_ref[2 * j + par])
            acc = o if acc is None else acc + o
        o_ref[:, j * LANE:(j + 1) * LANE] = acc.astype(BF16)
    for j in range(4):
        acc = None
        for par, v_ref in enumerate((vlo_ref, vhi_ref)):
            hh = 2 * j + par
            s = _dot_nt(qb_ref[:, hh * SLOT:(hh + 1) * SLOT], kb_ref[:, hh * SLOT:(hh + 1) * SLOT])
            o = _softmax_pv([s], [v_ref[:, j * LANE:(j + 1) * LANE]], None)
            acc = o if acc is None else acc + o
        o_ref[:, 512 + j * LANE:512 + (j + 1) * LANE] = acc.astype(BF16)


def _attn_ctx_call(sink, qa, ka4, va4, qb, kb, vlo, vhi):
    row = lambda w: pl.BlockSpec((SEQ, w), lambda b: (b, 0))
    return pl.pallas_call(
        _attn_ctx_kernel,
        out_shape=jax.ShapeDtypeStruct((T_CTX, D), BF16),
        grid=(BATCH,),
        in_specs=[pl.BlockSpec(memory_space=pltpu.SMEM),
                  row(512), row(512), row(512), row(1024), row(1024), row(512), row(512)],
        out_specs=row(D),
        compiler_params=pltpu.CompilerParams(dimension_semantics=("arbitrary",)),
        name="attn_ctx",
    )(sink, qa, ka4, va4, qb, kb, vlo, vhi)


QB = 128
N_QB = DEC_SEQ // QB
KPAD = DEC_SEQ + 2 * QB
KALL = PAST + DEC_SEQ


def _attn_lat_kernel(sink_ref, qa_ref, ka4p_ref, va4p_ref, ka4c_ref, va4c_ref,
                     qb_ref, kb_ref, vlo_ref, vhi_ref, o_ref):
    n = pl.program_id(1)
    r = lax.broadcasted_iota(jnp.int32, (QB, 3 * QB), 0)
    c = lax.broadcasted_iota(jnp.int32, (QB, 3 * QB), 1)
    rel = c - QB - r
    kpos = (n - 1) * QB + c
    valid = (jnp.abs(rel) <= WINDOW) & (kpos >= 0) & (kpos < DEC_SEQ)
    k0 = pl.multiple_of(n * QB, QB)
    for j in range(4):
        q = qa_ref[:, j * LANE:(j + 1) * LANE]
        acc = None
        for par in range(2):
            cc = (2 * (j // 2) + par) * LANE
            s_loc = _dot_nt(q, ka4p_ref[pl.ds(k0, 3 * QB), cc:cc + LANE])
            s_loc = jnp.where(valid, s_loc, -jnp.inf)
            s_ctx = _dot_nt(q, ka4c_ref[:, cc:cc + LANE])
            o = _softmax_pv([s_ctx, s_loc],
                            [va4c_ref[:, cc:cc + LANE], va4p_ref[pl.ds(k0, 3 * QB), cc:cc + LANE]],
                            sink_ref[2 * j + par])
            acc = o if acc is None else acc + o
        o_ref[:, j * LANE:(j + 1) * LANE] = acc.astype(BF16)
    for j in range(4):
        acc = None
        for par, v_ref in enumerate((vlo_ref, vhi_ref)):
            hh = 2 * j + par
            s = _dot_nt(qb_ref[:, hh * SLOT:(hh + 1) * SLOT], kb_ref[:, hh * SLOT:(hh + 1) * SLOT])
            o = _softmax_pv([s], [v_ref[:, j * LANE:(j + 1) * LANE]], None)
            acc = o if acc is None else acc + o
        o_ref[:, 512 + j * LANE:512 + (j + 1) * LANE] = acc.astype(BF16)


def _attn_lat_call(sink, qa, ka4p, va4p, ka4c, va4c, qb, kball, vlo_all, vhi_all):
    qrow = lambda w: pl.BlockSpec((QB, w), lambda b, n: (b * N_QB + n, 0))
    per_b = lambda rows, w: pl.BlockSpec((None, rows, w), lambda b, n: (b, 0, 0))
    return pl.pallas_call(
        _attn_lat_kernel,
        out_shape=jax.ShapeDtypeStruct((T_LAT, D), BF16),
        grid=(DEC_BATCH, N_QB),
        in_specs=[pl.BlockSpec(memory_space=pltpu.SMEM),
                  qrow(512), per_b(KPAD, 512), per_b(KPAD, 512), per_b(PAST, 512), per_b(PAST, 512),
                  qrow(1024), per_b(KALL, 1024), per_b(KALL, 512), per_b(KALL, 512)],
        out_specs=qrow(D),
        compiler_params=pltpu.CompilerParams(dimension_semantics=("arbitrary", "arbitrary")),
        name="attn_lat",
    )(sink, qa, ka4p, va4p, ka4c, va4c, qb, kball, vlo_all, vhi_all)


def _post_attn_kernel(xc_ref, xl_ref, oc_ref, ol_ref, ada_ref, g_ref, wo_ref, wg_ref, wu_ref, wd_ref, y_ref):
    is_ctx = pl.program_id(0) < T_CTX // FFN_TM
    x = jnp.where(is_ctx, xc_ref[...], xl_ref[...])
    o = jnp.where(is_ctx, oc_ref[...], ol_ref[...])
    mix = _dot(o, wo_ref[...])
    x1 = x + ada_ref[2:3, :] * mix
    h = _modulate(x1, g_ref[...], ada_ref[3:4, :], ada_ref[4:5, :]).astype(BF16)
    acc = None
    for c0 in range(0, D_FF, FFN_CH):
        gg = _dot(h, wg_ref[:, c0:c0 + FFN_CH])
        uu = _dot(h, wu_ref[:, c0:c0 + FFN_CH])
        a = (gg * _sigmoid(gg) * uu).astype(BF16)
        part = _dot(a, wd_ref[c0:c0 + FFN_CH, :])
        acc = part if acc is None else acc + part
    y_ref[...] = x1 + ada_ref[5:6, :] * acc


def _tile_cond(i, tiles_per_ctx, tiles_per_lat_seq):
    return jnp.where(i < tiles_per_ctx, 0, 1 + (i - tiles_per_ctx) // tiles_per_lat_seq)


def _post_attn_call(x_ctx, x_lat, x_lat_off, o_ctx, o_lat, ada, layer, g, wo, wg, wu, wd):
    n_tiles = T_ALL // FFN_TM
    n_ctx = T_CTX // FFN_TM
    ada_map = lambda i: (layer, _tile_cond(i, n_ctx, DEC_SEQ // FFN_TM), 0, 0)
    ctx_map = lambda i: (jnp.minimum(i, n_ctx - 1), 0)
    one = pl.Buffered(1)
    return pl.pallas_call(
        _post_attn_kernel,
        out_shape=jax.ShapeDtypeStruct((T_ALL, D), F32),
        grid=(n_tiles,),
        in_specs=[pl.BlockSpec((FFN_TM, D), ctx_map),
                  pl.BlockSpec((FFN_TM, D), lambda i: (jnp.maximum(i - n_ctx, 0) + x_lat_off, 0)),
                  pl.BlockSpec((FFN_TM, D), ctx_map),
                  pl.BlockSpec((FFN_TM, D), lambda i: (jnp.maximum(i - n_ctx, 0), 0)),
                  pl.BlockSpec((None, None, N_ADA, D), ada_map),
                  _const_spec((1, D)),
                  pl.BlockSpec((D, D), lambda i: (0, 0), pipeline_mode=one),
                  pl.BlockSpec((D, D_FF), lambda i: (0, 0), pipeline_mode=one),
                  pl.BlockSpec((D, D_FF), lambda i: (0, 0), pipeline_mode=one),
                  pl.BlockSpec((D_FF, D), lambda i: (0, 0), pipeline_mode=one)],
        out_specs=pl.BlockSpec((FFN_TM, D), lambda i: (i, 0)),
        compiler_params=pltpu.CompilerParams(
            dimension_semantics=("arbitrary",), vmem_limit_bytes=VMEM_BIG),
        name="post_attn_ffn",
    )(x_ctx, x_lat, o_ctx, o_lat, ada, g, wo, wg, wu, wd)


HALO = 16
EXT = TM + 2 * HALO
CONV_RC = 64
CONV_LC = 128
CONV_WIN = CONV_RC + 24
CONV_LN_ROWS = 32


def _conv_router_kernel(x_ref, xp_ref, xn_ref, ada_ref, gm_ref, gf_ref, w1_ref, b1_ref, wdw_ref, bdw_ref,
                        gln_ref, bln_ref, w2_ref, b2_ref, wr_ref,
                        x1_o, h2_o, ri_o, rw_o, cnt_o, p_ref, c_ref, v_ref, run_ref):
    i = pl.program_id(0)

    @pl.when(i == 0)
    def _():
        run_ref[...] = jnp.zeros_like(run_ref)

    is_lat = i >= N_CTX_TILES
    j = (i - N_CTX_TILES) % LAT_TILES_PER_SEQ
    lflag = jnp.where(is_lat & (j != 0), 1.0, 0.0)
    rflag = jnp.where(is_lat & (j != LAT_TILES_PER_SEQ - 1), 1.0, 0.0)

    x = x_ref[...]
    xx = jnp.concatenate([xp_ref[...], x, xn_ref[...]], axis=0)
    h = _modulate(xx, gm_ref[...], ada_ref[0:1, :], ada_ref[1:2, :])
    a = _dot(h.astype(BF16), w1_ref[...]) + b1_ref[...]
    u = a[:, :D] * _sigmoid(a[:, D:])
    row = lax.broadcasted_iota(jnp.int32, (EXT, 1), 0)
    u = u * jnp.where(row < HALO, lflag, jnp.where(row >= HALO + TM, rflag, 1.0))

    p_ref[...] = u
    for rc in range(TM // CONV_RC):
        for lc in range(D // CONV_LC):
            ls = slice(lc * CONV_LC, (lc + 1) * CONV_LC)
            acc = jnp.zeros((CONV_RC, CONV_LC), F32) + bdw_ref[:, ls]
            base = p_ref[rc * CONV_RC:rc * CONV_RC + CONV_WIN + 8, ls]
            for r in range(8):
                win = base if r == 0 else pltpu.roll(base, CONV_WIN + 8 - r, 0)
                for q in range(4):
                    o = 8 * q + r
                    if 1 <= o <= CONV_W:
                        acc = acc + wdw_ref[o - 1:o, ls] * win[8 * q:8 * q + CONV_RC]
            c_ref[rc * CONV_RC:(rc + 1) * CONV_RC, ls] = acc

    for c0 in range(0, TM, CONV_LN_ROWS):
        acc = c_ref[c0:c0 + CONV_LN_ROWS, :]
        mu = jnp.mean(acc, axis=-1, keepdims=True)
        xc = acc - mu
        y = xc * lax.rsqrt(jnp.mean(xc * xc, axis=-1, keepdims=True) + EPS)
        y = y * gln_ref[...] + bln_ref[...]
        v_ref[c0:c0 + CONV_LN_ROWS, :] = (y * _sigmoid(y)).astype(BF16)

    mix = _dot(v_ref[...], w2_ref[...]) + b2_ref[...]
    x1 = x + ada_ref[2:3, :] * mix
    x1_o[...] = x1
    h2 = _modulate(x1, gf_ref[...], ada_ref[3:4, :], ada_ref[4:5, :])
    h2_o[...] = h2.astype(BF16)

    lane = lax.broadcasted_iota(jnp.int32, (TM, LANE), 1)
    lane_f = lane.astype(F32)
    logits = jnp.where(lane < N_EXP, _dot_x3(h2, wr_ref[...]), -jnp.inf)
    m1 = jnp.max(logits, axis=-1, keepdims=True)
    i1 = jnp.min(jnp.where(logits == m1, lane_f, float(LANE)), axis=-1, keepdims=True)
    rest = jnp.where(lane_f == i1, -jnp.inf, logits)
    m2 = jnp.max(rest, axis=-1, keepdims=True)
    i2 = jnp.min(jnp.where(rest == m2, lane_f, float(LANE)), axis=-1, keepdims=True)
    e = jnp.exp(m2 - m1)
    w1 = 1.0 / (1.0 + e)
    w2 = e / (1.0 + e)

    oh1 = jnp.where(lane_f == i1, 1.0, 0.0)
    oh2 = jnp.where(lane_f == i2, 1.0, 0.0)
    tr = lax.broadcasted_iota(jnp.int32, (TM, TM), 0)
    tc = lax.broadcasted_iota(jnp.int32, (TM, TM), 1)
    tri = jnp.where(tr > tc, 1.0, 0.0).astype(BF16)
    run = run_ref[0:1, :]
    cnt1 = jnp.sum(oh1, axis=0, keepdims=True)
    cnt2 = jnp.sum(oh2, axis=0, keepdims=True)
    pre1 = _dot(tri, oh1.astype(BF16)) + run
    pre2 = _dot(tri, oh2.astype(BF16)) + (run + cnt1)
    rank1 = jnp.sum(oh1 * pre1, axis=-1, keepdims=True)
    rank2 = jnp.sum(oh2 * pre2, axis=-1, keepdims=True)
    new_run = run + cnt1 + cnt2
    run_ref[...] = jnp.broadcast_to(new_run, run_ref.shape)
    srow = lax.broadcasted_iota(jnp.int32, cnt_o.shape, 0)
    cnt_o[...] = jnp.where(srow == 0, run, jnp.where(srow == 1, cnt1 + cnt2, 0.0))

    ri = jnp.where(lane == 0, i1, jnp.where(lane == 1, i2, jnp.where(lane == 2, rank1, jnp.where(lane == 3, rank2, 0.0))))
    ri_o[...] = ri.astype(jnp.int32)

    rw_o[...] = jnp.where(lane == 0, w1, jnp.where(lane == 1, w2, 0.0))


def _conv_router_call(x, ada, layer, gm, gf, cw):
    ada_map = lambda i: (layer, _tile_cond(i, N_CTX_TILES, LAT_TILES_PER_SEQ), 0, 0)
    hb = TM // HALO
    row = lambda w: pl.BlockSpec((TM, w), lambda i: (i, 0))
    return pl.pallas_call(
        _conv_router_kernel,
        out_shape=[jax.ShapeDtypeStruct((T_ALL, D), F32), jax.ShapeDtypeStruct((T_ALL, D), BF16),
                   jax.ShapeDtypeStruct((T_ALL, LANE), jnp.int32), jax.ShapeDtypeStruct((T_ALL, LANE), F32),
                   jax.ShapeDtypeStruct((N_TILES, 8, LANE), F32)],
        grid=(N_TILES,),
        in_specs=[row(D),
                  pl.BlockSpec((HALO, D), lambda i: (jnp.maximum(i * hb - 1, 0), 0)),
                  pl.BlockSpec((HALO, D), lambda i: (jnp.minimum((i + 1) * hb, T_ALL // HALO - 1), 0)),
                  pl.BlockSpec((None, None, N_ADA, D), ada_map),
                  _const_spec((1, D)), _const_spec((1, D)),
                  _const_spec((D, 2 * D)), _const_spec((1, 2 * D)),
                  _const_spec((CONV_W, D)), _const_spec((1, D)),
                  _const_spec((1, D)), _const_spec((1, D)),
                  _const_spec((D, D)), _const_spec((1, D)),
                  _const_spec((D, LANE))],
        out_specs=[row(D), row(D), row(LANE), row(LANE), pl.BlockSpec((None, 8, LANE), lambda i: (i, 0, 0))],
        scratch_shapes=[pltpu.VMEM((EXT, D), F32), pltpu.VMEM((TM, D), F32), pltpu.VMEM((TM, D), BF16),
                        pltpu.VMEM((8, LANE), F32)],
        compiler_params=pltpu.CompilerParams(
            dimension_semantics=("arbitrary",), vmem_limit_bytes=VMEM_BIG),
        name="conv_router",
    )(x, x, x, ada, gm, gf, cw["w1"], cw["b1"], cw["wdw"], cw["bdw"], cw["gln"], cw["bln"],
      cw["w2"], cw["b2"], cw["wr"])


def _moe_kernel(se_ref, row0_ref, nsub_ref, ta_ref, tn_ref, h_ref, post_ref, rwt_ref, wg_ref, wu_ref, wd_ref,
                ys_in_ref, ys_ref, hb_ref, acc_ref, gs_ref, hstage, wgb, wub, wdb, sem_in, sem_out):
    del ys_in_ref
    j = pl.program_id(0)
    f = pl.program_id(1)
    nsub = nsub_ref[j]
    row0 = row0_ref[j]
    e = se_ref[j]

    def tile_load(i, slot):
        r = pl.multiple_of(i * TM, TM)
        return pltpu.make_async_copy(h_ref.at[pl.ds(r, TM), :], hstage.at[slot], sem_in.at[slot])

    def store(s):
        r = pl.multiple_of(s * MOE_SUB, MOE_SUB)
        dst = ys_ref.at[pl.ds(pl.multiple_of(row0 + s * MOE_SUB, MOE_SUB), MOE_SUB), :]
        return pltpu.make_async_copy(hb_ref.at[pl.ds(r, MOE_SUB), :], dst, sem_out)

    def gather_tile(i, carry):
        slot = i % MOE_GSLOTS
        tile_load(i, slot).wait()

        @pl.when(i + MOE_GDEPTH < N_TILES)
        def _():
            tile_load(i + MOE_GDEPTH, (i + MOE_GDEPTH) % MOE_GSLOTS).start()

        a = ta_ref[i * N_EXP + e] - row0
        n = tn_ref[i * N_EXP + e]
        last = a + n - 1

        @pl.when((n > 0) & (last >= 0) & (a < nsub * MOE_SUB))
        def _():
            lo = jnp.maximum(a, 0) // MOE_GB
            hi = jnp.minimum(last, nsub * MOE_SUB - 1) // MOE_GB
            pp = post_ref[pl.ds(pl.multiple_of(i * 8, 8), 8), :]
            riota = lax.broadcasted_iota(jnp.int32, (MOE_GB, TM), 0)

            ww = rwt_ref[pl.ds(pl.multiple_of(i * 8, 8), 8), :]

            def block(b):
                base = row0 + b * MOE_GB
                hit0 = (pp[0:1, :] - base) == riota
                hit1 = (pp[1:2, :] - base) == riota
                rr = pl.multiple_of(b * MOE_GB, MOE_GB)
                acc_ref[pl.ds(rr, MOE_GB), :] += _dot(jnp.where(hit0 | hit1, 1.0, 0.0).astype(BF16),
                                                      hstage[slot])
                gate = jnp.sum(jnp.where(hit0, ww[0:1, :], 0.0) + jnp.where(hit1, ww[1:2, :], 0.0),
                               axis=-1, keepdims=True)
                gs_ref[pl.ds(rr, MOE_GB), :] += jnp.broadcast_to(gate, (MOE_GB, LANE))

            block(lo)
            for extra in range(1, TM // MOE_GB + 1):
                @pl.when(hi >= lo + extra)
                def _():
                    block(lo + extra)

        return carry

    def part(r):
        hh = hb_ref[pl.ds(r, MOE_SUB), :]
        out = None
        for c0 in range(0, MOE_TF, MOE_TC):
            gg = _dot(hh, wgb[:, c0:c0 + MOE_TC])
            uu = _dot(hh, wub[:, c0:c0 + MOE_TC])
            a = (gg * _sigmoid(gg) * uu).astype(BF16)
            piece = _dot(a, wdb[c0:c0 + MOE_TC, :])
            out = piece if out is None else out + piece
        return out

    def for_each_subtile(fn):
        def pair(p, carry):
            fn(2 * p)
            fn(2 * p + 1)
            return carry

        lax.fori_loop(0, nsub // 2, pair, 0)

        @pl.when(nsub % 2 == 1)
        def _():
            fn(nsub - 1)

    @pl.when(nsub > 0)
    def _():
        wgb[...] = wg_ref[...].astype(BF16)
        wub[...] = wu_ref[...].astype(BF16)
        wdb[...] = wd_ref[...].astype(BF16)

        @pl.when(f == 0)
        def _():
            for d in range(MOE_GDEPTH):
                tile_load(d, d).start()

            def clear(s, carry):
                r = pl.multiple_of(s * MOE_SUB, MOE_SUB)
                acc_ref[pl.ds(r, MOE_SUB), :] = jnp.zeros((MOE_SUB, D), F32)
                gs_ref[pl.ds(r, MOE_SUB), :] = jnp.zeros((MOE_SUB, LANE), F32)
                return carry

            lax.fori_loop(0, nsub, clear, 0)
            lax.fori_loop(0, N_TILES, gather_tile, 0)

            def first(s):
                r = pl.multiple_of(s * MOE_SUB, MOE_SUB)
                hb_ref[pl.ds(r, MOE_SUB), :] = acc_ref[pl.ds(r, MOE_SUB), :].astype(BF16)
                acc_ref[pl.ds(r, MOE_SUB), :] = part(r)

            for_each_subtile(first)

        @pl.when((f > 0) & (f < MOE_NF - 1))
        def _():
            def middle(s):
                r = pl.multiple_of(s * MOE_SUB, MOE_SUB)
                acc_ref[pl.ds(r, MOE_SUB), :] += part(r)

            for_each_subtile(middle)

        @pl.when(f == MOE_NF - 1)
        def _():
            def final(s):
                r = pl.multiple_of(s * MOE_SUB, MOE_SUB)
                y = acc_ref[pl.ds(r, MOE_SUB), :] + part(r)
                hb_ref[pl.ds(r, MOE_SUB), :] = (y * gs_ref[pl.ds(r, MOE_SUB), 0:1]).astype(BF16)
                store(s).start()

            for_each_subtile(final)

            def drain(s, carry):
                store(s).wait()
                return carry

            lax.fori_loop(0, nsub, drain, 0)


def _moe_call(st_e, st_row0, st_nsub, t_a, t_n, h2b, pos_t, rw_t, wg, wu, wd, ys0, lidx):
    f_eff = lambda j, f, ns: jnp.where(ns[j] > 0, f, MOE_NF - 1)
    wmap_in = lambda j, f, se, r0, ns, ta, tn: (lidx, se[j], 0, f_eff(j, f, ns))
    wmap_out = lambda j, f, se, r0, ns, ta, tn: (lidx, se[j], f_eff(j, f, ns), 0)
    return pl.pallas_call(
        _moe_kernel,
        out_shape=jax.ShapeDtypeStruct((MOE_ROWS, D), BF16),
        grid_spec=pltpu.PrefetchScalarGridSpec(
            num_scalar_prefetch=5, grid=(MOE_NS, MOE_NF),
            in_specs=[
                pl.BlockSpec(memory_space=pl.ANY),
                pl.BlockSpec((N_TILES * 8, TM), lambda j, f, se, r0, ns, ta, tn: (0, 0)),
                pl.BlockSpec((N_TILES * 8, TM), lambda j, f, se, r0, ns, ta, tn: (0, 0)),
                pl.BlockSpec((None, None, D, MOE_TF), wmap_in),
                pl.BlockSpec((None, None, D, MOE_TF), wmap_in),
                pl.BlockSpec((None, None, MOE_TF, D), wmap_out),
                pl.BlockSpec(memory_space=pl.ANY),
            ],
            out_specs=pl.BlockSpec(memory_space=pl.ANY),
            scratch_shapes=[pltpu.VMEM((MOE_R, D), BF16), pltpu.VMEM((MOE_R, D), F32),
                            pltpu.VMEM((MOE_R, LANE), F32),
                            pltpu.VMEM((MOE_GSLOTS, TM, D), BF16),
                            pltpu.VMEM((D, MOE_TF), BF16), pltpu.VMEM((D, MOE_TF), BF16),
                            pltpu.VMEM((MOE_TF, D), BF16),
                            pltpu.SemaphoreType.DMA((MOE_GSLOTS,)), pltpu.SemaphoreType.DMA(())]),
        input_output_aliases={11: 0},
        compiler_params=pltpu.CompilerParams(
            dimension_semantics=("arbitrary", "arbitrary"), vmem_limit_bytes=VMEM_BIG,
            has_side_effects=True),
        name="moe_experts",
    )(st_e, st_row0, st_nsub, t_a, t_n, h2b, pos_t, rw_t, wg, wu, wd, ys0)


def _combine_kernel(split, ta_ref, tn_ref, x_ref, ada_ref, posc_ref, ys_ref, *rest):
    if split:
        oc_ref, ol_ref, ybuf, oacc, bc, sem = rest
    else:
        o_ref, ybuf, oacc, bc, sem = rest
    g = pl.program_id(0)

    def blocks(i, e):
        a = ta_ref[i * N_EXP + e]
        n = tn_ref[i * N_EXP + e]
        lo = jnp.minimum(a // MOE_SUB, MOE_ROWS // MOE_SUB - 1)
        return n > 0, lo, (n > 0) & ((a + n - 1) // MOE_SUB > lo)

    def block_copy(slot, e, k, blk):
        src = ys_ref.at[pl.ds(pl.multiple_of(blk * MOE_SUB, MOE_SUB), MOE_SUB), :]
        return pltpu.make_async_copy(src, ybuf.at[slot, 2 * e + k], sem.at[slot, 2 * e + k])

    @pl.when(g < N_TILES)
    def _():
        for e in range(N_EXP):
            _, lo, two = blocks(g, e)
            block_copy(g % 2, e, 0, lo).start()

            @pl.when(two)
            def _():
                block_copy(g % 2, e, 1, lo + 1).start()

    @pl.when(g > 0)
    def _():
        i = g - 1
        slot = i % 2
        for n in range(2):
            bc[n] = jnp.broadcast_to(posc_ref[:, n:n + 1], (TM, MOE_SUB))
        liota = lax.broadcasted_iota(jnp.int32, (TM, MOE_SUB), 1)

        def weighted(has, blk, y):
            base = jnp.where(has, blk * MOE_SUB, -MOE_ROWS)
            hit = (bc[0] - base == liota) | (bc[1] - base == liota)
            return _dot(jnp.where(hit, 1.0, 0.0).astype(BF16), y)

        total = None
        for e in range(N_EXP):
            has, lo, _ = blocks(i, e)
            block_copy(slot, e, 0, lo).wait()
            t = weighted(has, lo, ybuf[slot, 2 * e])
            total = t if total is None else total + t
        oacc[...] = total
        for e in range(N_EXP):
            has, lo, two = blocks(i, e)

            @pl.when(two)
            def _():
                block_copy(slot, e, 1, lo + 1).wait()
                oacc[...] += weighted(has, lo + 1, ybuf[slot, 2 * e + 1])

        out = x_ref[...] + ada_ref[5:6, :] * oacc[...]
        if split:
            @pl.when(i < N_CTX_TILES)
            def _():
                oc_ref[...] = out

            @pl.when(i >= N_CTX_TILES)
            def _():
                ol_ref[...] = out
        else:
            o_ref[...] = out


def _combine_call(split, t_a, t_n, x1, ada, layer, posc, ys):
    prev = lambda g: jnp.maximum(g - 1, 0)
    ada_map = lambda g, a, n: (layer, _tile_cond(prev(g), N_CTX_TILES, LAT_TILES_PER_SEQ), 0, 0)
    if split:
        out_shape = [jax.ShapeDtypeStruct((T_CTX, D), F32), jax.ShapeDtypeStruct((T_LAT, D), F32)]
        out_specs = [pl.BlockSpec((TM, D), lambda g, a, n: (jnp.minimum(prev(g), N_CTX_TILES - 1), 0)),
                     pl.BlockSpec((TM, D), lambda g, a, n: (jnp.maximum(prev(g) - N_CTX_TILES, 0), 0))]
    else:
        out_shape = jax.ShapeDtypeStruct((T_ALL, D), F32)
        out_specs = pl.BlockSpec((TM, D), lambda g, a, n: (prev(g), 0))
    return pl.pallas_call(
        functools.partial(_combine_kernel, split),
        out_shape=out_shape,
        grid_spec=pltpu.PrefetchScalarGridSpec(
            num_scalar_prefetch=2, grid=(N_TILES + 1,),
            in_specs=[pl.BlockSpec((TM, D), lambda g, a, n: (prev(g), 0)),
                      pl.BlockSpec((None, None, N_ADA, D), ada_map),
                      pl.BlockSpec((TM, LANE), lambda g, a, n: (prev(g), 0)),
                      pl.BlockSpec(memory_space=pl.ANY)],
            out_specs=out_specs,
            scratch_shapes=[pltpu.VMEM((2, 2 * N_EXP, MOE_SUB, D), BF16), pltpu.VMEM((TM, D), F32),
                            pltpu.VMEM((2, TM, MOE_SUB), jnp.int32),
                            pltpu.SemaphoreType.DMA((2, 2 * N_EXP))]),
        compiler_params=pltpu.CompilerParams(
            dimension_semantics=("arbitrary",), vmem_limit_bytes=VMEM_BIG),
        name="moe_combine",
    )(t_a, t_n, x1, ada, posc, ys)


def _moe_layer(x1, h2b, ri, rw, runs, ada, layer, wg, wu, wd, lidx, split):
    idx = ri[:, 0:2]
    rank = ri[:, 2:4]
    before = runs[:, 0, :N_EXP].astype(jnp.int32)
    t_n = runs[:, 1, :N_EXP].astype(jnp.int32)
    counts = before[-1] + t_n[-1]
    padded = ((counts + MOE_SUB - 1) // MOE_SUB) * MOE_SUB
    ends = jnp.cumsum(padded)
    starts = ends - padded
    sel = idx[:, :, None] == jnp.arange(N_EXP, dtype=jnp.int32)[None, None, :]
    pos = (jnp.sum(jnp.where(sel, starts[None, None, :], 0), axis=-1) + rank).astype(jnp.int32)
    t_a = (starts[None, :] + before).reshape(-1).astype(jnp.int32)
    t_n = t_n.reshape(-1)
    pos_t = jnp.pad(pos.reshape(N_TILES, TM, 2).transpose(0, 2, 1), ((0, 0), (0, 6), (0, 0)))
    pos_t = pos_t.reshape(N_TILES * 8, TM)
    posc = jnp.pad(pos, ((0, 0), (0, LANE - 2)))
    n_st = (padded + MOE_R - 1) // MOE_R
    st_end = jnp.cumsum(n_st)
    st_begin = st_end - n_st
    j = jnp.arange(MOE_NS, dtype=jnp.int32)
    se = jnp.sum((j[:, None] >= st_end[None, :]).astype(jnp.int32), axis=1)
    used = se < N_EXP
    se_c = jnp.minimum(se, N_EXP - 1)
    k = j - st_begin[se_c]
    row0 = jnp.where(used, starts[se_c] + k * MOE_R, 0).astype(jnp.int32)
    nsub = jnp.where(used, jnp.minimum(MOE_R, padded[se_c] - k * MOE_R) // MOE_SUB, 0).astype(jnp.int32)
    last = jnp.maximum(jnp.sum(used.astype(jnp.int32)) - 1, 0)
    se_eff = jnp.where(used, se_c, se_c[last]).astype(jnp.int32)
    rw_t = jnp.pad(rw[:, 0:2].reshape(N_TILES, TM, 2).transpose(0, 2, 1), ((0, 0), (0, 6), (0, 0)))
    rw_t = rw_t.reshape(N_TILES * 8, TM)
    ys = _moe_call(se_eff, row0, nsub, t_a, t_n, h2b, pos_t, rw_t, wg, wu, wd,
                   jnp.zeros((MOE_ROWS, D), BF16), lidx)
    return _combine_call(split, t_a, t_n, x1, ada, layer, posc, ys)


def _rope_tables():
    t = jnp.arange(DEC_SEQ)
    rows = (t // GRID_W).astype(F32)
    cols = (t % GRID_W).astype(F32)

    def axis_tabs(pos, d):
        inv = ROPE_BASE ** (-jnp.arange(0, d, 2, dtype=F32) / d)
        ang = pos[:, None] * inv[None, :]
        cos, sin = jnp.cos(ang), jnp.sin(ang)
        z = jnp.zeros_like(sin)
        return (jnp.concatenate([cos, cos], -1), jnp.concatenate([-sin, z], -1), jnp.concatenate([z, sin], -1))

    def axial(d):
        r = axis_tabs(rows, d // 2)
        c = axis_tabs(cols, d // 2)
        return [jnp.concatenate([a, b], -1) for a, b in zip(r, c)]

    tabs_a = [jnp.tile(tb, (1, A_HEADS)) for tb in axial(A_DH)]
    cb, sub, sdb = axial(B_ROPE)
    ones = jnp.ones((DEC_SEQ, B_NOPE), F32)
    zer = jnp.zeros((DEC_SEQ, B_NOPE), F32)
    pad = jnp.zeros((DEC_SEQ, SLOT - B_QK), F32)
    slot = lambda first, tb: jnp.tile(jnp.concatenate([first, tb, pad], -1), (1, B_HEADS))
    tabs_b = [slot(ones, cb), slot(zer, sub), slot(zer, sdb)]
    return tabs_a + tabs_b


def _attn_weights(e, w_in_attn, g_qk_a_q, g_qk_a_k, g_mla_q_a, w_mla_q_b, g_mla_kv_a, w_mla_kv_b,
                  g_qk_b_q, g_qk_b_k):
    w_in = jnp.pad(w_in_attn[e], ((0, 0), (0, IN_PAD - IN_DIM))).astype(BF16)
    w_qb = jnp.pad(w_mla_q_b[e].reshape(B_Q_LORA, B_HEADS, B_QK), ((0, 0), (0, 0), (0, SLOT - B_QK)))
    w_qb = w_qb.reshape(B_Q_LORA, B_HEADS * SLOT).astype(BF16)
    wkv = w_mla_kv_b[e].reshape(B_KV_LORA, B_HEADS, B_NOPE + B_V)
    wk = jnp.pad(wkv[:, :, :B_NOPE], ((0, 0), (0, 0), (0, SLOT - B_NOPE))).reshape(B_KV_LORA, B_HEADS * SLOT)
    wv = wkv[:, :, B_NOPE:].reshape(B_KV_LORA, B_HEADS * B_V)
    w_kv = jnp.concatenate([wk, wv], axis=1).astype(BF16)
    place = jnp.zeros((LANE, B_HEADS, SLOT), F32)
    ii = jnp.arange(B_ROPE)
    place = place.at[ii, :, B_NOPE + ii].set(1.0).reshape(LANE, B_HEADS * SLOT).astype(BF16)
    slot_gain = lambda g: jnp.tile(jnp.pad(g, (0, SLOT - B_QK)), B_HEADS).reshape(1, B_HEADS * SLOT)
    return dict(
        w_in=w_in, w_qb=w_qb, w_kv=w_kv, place=place,
        gqa=jnp.tile(g_qk_a_q[e], A_HEADS).reshape(1, 512),
        gka=jnp.tile(g_qk_a_k[e], A_KV).reshape(1, LANE),
        gql=g_mla_q_a[e].reshape(1, B_Q_LORA), gkl=g_mla_kv_a[e].reshape(1, B_KV_LORA),
        gqb=slot_gain(g_qk_b_q[e]), gkb=slot_gain(g_qk_b_k[e]))


def _pair_variants_host(x):
    h0, h1 = x[..., :64], x[..., 64:]
    z = jnp.zeros_like(h0)
    return jnp.concatenate([h0, z, z, h0, h1, z, z, h1], axis=-1)


def kernel(x_prompt, x_sample, c, cache_k_win, cache_v_win, cache_ckv, cache_krope, c_ctx, w_ada, b_ada, g_norm_mix, g_norm_ffn, w_in_attn, g_qk_a_q, g_qk_a_k, sink_a, g_mla_q_a, w_mla_q_b, g_mla_kv_a, w_mla_kv_b, g_qk_b_q, g_qk_b_k, w_out_attn, w_ffn_gate, w_ffn_up, w_ffn_down, w_conv_pw1, b_conv_pw1, w_conv_dw, b_conv_dw, g_conv_ln, b_conv_ln, w_conv_pw2, b_conv_pw2, w_router, w_moe_gate, w_moe_up, w_moe_down):
    x_ctx, x_lat, lat_tiles_off = x_prompt.reshape(T_CTX, D), x_sample.reshape(T_LAT, D), 0
    cond8 = jnp.concatenate([c_ctx[None, :], c, jnp.zeros((8 - 1 - DEC_BATCH, D), F32)], axis=0)
    ada = _ada_call(cond8, w_ada, b_ada).reshape(DEPTH, 8, N_ADA, D)
    tabs = _rope_tables()

    wo_b = w_out_attn.astype(BF16)
    wg_b = w_ffn_gate.astype(BF16)
    wu_b = w_ffn_up.astype(BF16)
    wd_b = w_ffn_down.astype(BF16)
    w1_b = w_conv_pw1.astype(BF16)
    w2_b = w_conv_pw2.astype(BF16)

    new_k, new_v, new_ckv, new_kr = [], [], [], []
    for layer in range(DEPTH):
        gm = g_norm_mix[layer].reshape(1, D)
        gf = g_norm_ffn[layer].reshape(1, D)
        if layer % 2 == 0:
            e = layer // 2
            lw = _attn_weights(e, w_in_attn, g_qk_a_q, g_qk_a_k, g_mla_q_a, w_mla_q_b, g_mla_kv_a,
                               w_mla_kv_b, g_qk_b_q, g_qk_b_k)
            sink = sink_a[e].reshape(A_HEADS)
            (qa, ka4, va4, qb, kb, vlo, vhi, kaf, vaf, ckvf, krf) = _pre_attn_call(
                False, x_ctx, 0, ada, layer, gm, lw, None)
            o_ctx = _attn_ctx_call(sink, qa, ka4, va4, qb, kb, vlo, vhi)
            new_k.append(kaf.reshape(BATCH, SEQ, A_KV, A_DH))
            new_v.append(vaf.reshape(BATCH, SEQ, A_KV, A_DH))
            new_ckv.append(ckvf.reshape(BATCH, SEQ, B_KV_LORA))
            new_kr.append(krf[:, :B_ROPE].reshape(BATCH, SEQ, B_ROPE))
            (qa, ka4, va4, qb, kb, vlo, vhi) = _pre_attn_call(
                True, x_lat, lat_tiles_off * (T_CTX // TM), ada, layer, gm, lw, tabs)
            kr_c = jnp.pad(cache_krope[:, e].reshape(DEC_BATCH * PAST, B_ROPE), ((0, 0), (0, LANE - B_ROPE)))
            kb_c, vlo_c, vhi_c = _cache_kv_call(cache_ckv[:, e].reshape(DEC_BATCH * PAST, B_KV_LORA), kr_c, lw)
            pad_rows = lambda a: jnp.pad(a.reshape(DEC_BATCH, DEC_SEQ, 512), ((0, 0), (QB, QB), (0, 0)))
            ka4c = _pair_variants_host(cache_k_win[:, e].reshape(DEC_BATCH, PAST, LANE)).astype(BF16)
            va4c = _pair_variants_host(cache_v_win[:, e].reshape(DEC_BATCH, PAST, LANE)).astype(BF16)
            cat = lambda a, b, w: jnp.concatenate(
                [a.reshape(DEC_BATCH, PAST, w), b.reshape(DEC_BATCH, DEC_SEQ, w)], axis=1)
            o_lat = _attn_lat_call(sink, qa, pad_rows(ka4), pad_rows(va4), ka4c, va4c, qb,
                                   cat(kb_c, kb, 1024), cat(vlo_c, vlo, 512), cat(vhi_c, vhi, 512))
            x = _post_attn_call(x_ctx, x_lat, lat_tiles_off * (T_CTX // FFN_TM), o_ctx, o_lat, ada, layer, gf,
                                wo_b[e], wg_b[e], wu_b[e], wd_b[e])
        else:
            o_ = layer // 2
            cw = dict(w1=w1_b[o_], b1=b_conv_pw1[o_].reshape(1, 2 * D), wdw=w_conv_dw[o_],
                      bdw=b_conv_dw[o_].reshape(1, D), gln=g_conv_ln[o_].reshape(1, D),
                      bln=b_conv_ln[o_].reshape(1, D), w2=w2_b[o_], b2=b_conv_pw2[o_].reshape(1, D),
                      wr=jnp.pad(w_router[o_], ((0, 0), (0, LANE - N_EXP))))
            x1, h2b, ri, rw, runs = _conv_router_call(x, ada, layer, gm, gf, cw)
            x = _moe_layer(x1, h2b, ri, rw, runs, ada, layer, w_moe_gate, w_moe_up, w_moe_down, o_,
                           split=(layer == DEPTH - 1))
            x_ctx, x_lat, lat_tiles_off = x, x, 1

    y_prompt = x[0].reshape(BATCH, SEQ, D)
    y_sample = x[1].reshape(DEC_BATCH, DEC_SEQ, D)
    return (y_prompt, y_sample, jnp.stack(new_k, axis=1), jnp.stack(new_v, axis=1),
            jnp.stack(new_ckv, axis=1), jnp.stack(new_kr, axis=1))
```

```python
import functools

import jax
import jax.numpy as jnp
from jax import lax
from jax.experimental import pallas as pl
from jax.experimental.pallas import tpu as pltpu

F32 = jnp.float32
BF16 = jnp.bfloat16

D = 1024
BATCH = 32
SEQ = 256
DEPTH = 4
DEC_BATCH = 2
DEC_SEQ = 1024
PAST = 256
GRID_W = 64
N_EVEN = 2
A_HEADS = 8
A_KV = 2
A_DH = 64
WINDOW = 128
B_HEADS = 8
B_Q_LORA = 256
B_KV_LORA = 128
B_NOPE = 64
B_ROPE = 32
B_QK = B_NOPE + B_ROPE
B_V = 64
IN_DIM = 1184
IN_PAD = 1280
CONV_W = 31
D_FF = 2816
N_EXP = 8
D_FFE = 3584
ROPE_BASE = 10000.0
EPS = 1e-6
N_ADA = 6

T_CTX = BATCH * SEQ
T_LAT = DEC_BATCH * DEC_SEQ
T_ALL = T_CTX + T_LAT
TM = 256
N_TILES = T_ALL // TM
N_CTX_TILES = T_CTX // TM
LAT_TILES_PER_SEQ = DEC_SEQ // TM
LANE = 128
SLOT = 128

MOE_SUB = 256
MOE_R = 3584
MOE_TF = 512
MOE_NF = D_FFE // MOE_TF
MOE_TC = 512
MOE_GB = 128
MOE_GDEPTH = 7
MOE_GSLOTS = MOE_GDEPTH + 1
MOE_ROWS = 2 * T_ALL + N_EXP * MOE_SUB
MOE_NS = MOE_ROWS // MOE_R + N_EXP

FFN_TM = 512
FFN_CH = 1408

VMEM_BIG = 56 * 1024 * 1024


def _dot(a, b):
    return jnp.dot(a, b, preferred_element_type=F32)


def _dot_nt(a, b):
    return lax.dot_general(a, b, (((1,), (1,)), ((), ())), preferred_element_type=F32)


def _split(a):
    hi = a.astype(BF16)
    lo = (a - hi.astype(F32)).astype(BF16)
    return hi, lo


def _dot_x3(a, b):
    ah, al = _split(a)
    bh, bl = _split(b)
    return _dot(ah, bh) + (_dot(ah, bl) + _dot(al, bh))


def _sigmoid(x):
    return 1.0 / (1.0 + jnp.exp(-x))


def _rms(x, n):
    return x * lax.rsqrt(jnp.sum(x * x, axis=-1, keepdims=True) * (1.0 / n) + EPS)


def _modulate(x, g, shift, scale):
    return _rms(x, D) * g * (1.0 + scale) + shift


def _lo_mask(rows):
    return lax.broadcasted_iota(jnp.int32, (rows, LANE), 1) < 64


def _ada_kernel(cond_ref, w_ref, b_ref, o_ref):
    c = cond_ref[...]
    s = c * _sigmoid(c)
    o_ref[...] = _dot_x3(s, w_ref[...]) + b_ref[...]


def _ada_call(cond8, w_ada, b_ada):
    tn = 1024
    return pl.pallas_call(
        _ada_kernel,
        out_shape=jax.ShapeDtypeStruct((DEPTH, 8, N_ADA * D), F32),
        grid=(DEPTH, N_ADA * D // tn),
        in_specs=[
            pl.BlockSpec((8, D), lambda l, j: (0, 0)),
            pl.BlockSpec((None, D, tn), lambda l, j: (l, 0, j)),
            pl.BlockSpec((None, 1, tn), lambda l, j: (l, 0, j)),
        ],
        out_specs=pl.BlockSpec((None, 8, tn), lambda l, j: (l, 0, j)),
        compiler_params=pltpu.CompilerParams(
            dimension_semantics=("arbitrary", "arbitrary"), vmem_limit_bytes=VMEM_BIG),
        name="ada",
    )(cond8, w_ada, b_ada.reshape(DEPTH, 1, N_ADA * D))


def _rope(x, cos, sup, sdn, shift):
    up = pltpu.roll(x, LANE - shift, 1)
    dn = pltpu.roll(x, shift, 1)
    return x * cos + up * sup + dn * sdn


def _norm64_block(blk, lo):
    sq = blk * blk
    s_lo = jnp.sum(jnp.where(lo, sq, 0.0), axis=-1, keepdims=True)
    s_hi = jnp.sum(jnp.where(lo, 0.0, sq), axis=-1, keepdims=True)
    r = jnp.where(lo, lax.rsqrt(s_lo * (1.0 / A_DH) + EPS), lax.rsqrt(s_hi * (1.0 / A_DH) + EPS))
    return blk * r


def _pair_variants(x, lo):
    r = pltpu.roll(x, 64, 1)
    return (jnp.where(lo, x, 0.0), jnp.where(lo, 0.0, r), jnp.where(lo, r, 0.0), jnp.where(lo, 0.0, x))


def _mla_kv(ckv_n, kr, wkv_ref, place_ref, gk_ref, lo, kb_ref, vlo_ref, vhi_ref, rope_tabs):
    kv = _dot(ckv_n.astype(BF16), wkv_ref[...])
    kr_hi, kr_lo = _split(kr)
    place = place_ref[...]
    krs = _dot(kr_hi, place) + _dot(kr_lo, place)
    for h in range(B_HEADS):
        sl = slice(h * SLOT, (h + 1) * SLOT)
        blk = kv[:, sl] + krs[:, sl]
        blk = _rms(blk, B_QK) * gk_ref[:, sl]
        if rope_tabs is not None:
            cos, sup, sdn = rope_tabs
            blk = _rope(blk, cos[:, sl], sup[:, sl], sdn[:, sl], 8)
        kb_ref[:, sl] = blk.astype(BF16)
    for j in range(B_HEADS // 2):
        sl = slice(j * LANE, (j + 1) * LANE)
        v = kv[:, B_HEADS * SLOT + j * LANE: B_HEADS * SLOT + (j + 1) * LANE]
        vlo_ref[:, sl] = jnp.where(lo, v, 0.0).astype(BF16)
        vhi_ref[:, sl] = jnp.where(lo, 0.0, v).astype(BF16)


def _pre_attn_kernel(rope, *refs):
    if rope:
        (x_ref, ada_ref, g_ref, win_ref, wqb_ref, wkv_ref, place_ref,
         gqa_ref, gka_ref, gql_ref, gkl_ref, gqb_ref, gkb_ref,
         ca_ref, sua_ref, sda_ref, cb_ref, sub_ref, sdb_ref,
         qa_o, ka4_o, va4_o, qb_o, kb_o, vlo_o, vhi_o) = refs
    else:
        (x_ref, ada_ref, g_ref, win_ref, wqb_ref, wkv_ref, place_ref,
         gqa_ref, gka_ref, gql_ref, gkl_ref, gqb_ref, gkb_ref,
         qa_o, ka4_o, va4_o, qb_o, kb_o, vlo_o, vhi_o,
         kaf_o, vaf_o, ckvf_o, krf_o) = refs
    lo = _lo_mask(TM)
    h = _modulate(x_ref[...], g_ref[...], ada_ref[0:1, :], ada_ref[1:2, :])
    p = _dot(h.astype(BF16), win_ref[...])

    for j in range(4):
        sl = slice(j * LANE, (j + 1) * LANE)
        blk = _norm64_block(p[:, sl], lo) * gqa_ref[:, sl]
        if rope:
            blk = _rope(blk, ca_ref[:, sl], sua_ref[:, sl], sda_ref[:, sl], 16)
        qa_o[:, sl] = (blk * (A_DH ** -0.5)).astype(BF16)
    ka = _norm64_block(p[:, 512:640], lo) * gka_ref[...]
    va = p[:, 640:768]
    if not rope:
        kaf_o[...] = ka
        vaf_o[...] = va
    else:
        ka = _rope(ka, ca_ref[:, 0:LANE], sua_ref[:, 0:LANE], sda_ref[:, 0:LANE], 16)
    for n, (kk, vv) in enumerate(zip(_pair_variants(ka, lo), _pair_variants(va, lo))):
        sl = slice(n * LANE, (n + 1) * LANE)
        ka4_o[:, sl] = kk.astype(BF16)
        va4_o[:, sl] = vv.astype(BF16)

    cq = _rms(p[:, 768:1024], B_Q_LORA) * gql_ref[...]
    qb = _dot(cq.astype(BF16), wqb_ref[...])
    for hh in range(B_HEADS):
        sl = slice(hh * SLOT, (hh + 1) * SLOT)
        blk = _rms(qb[:, sl], B_QK) * gqb_ref[:, sl]
        if rope:
            blk = _rope(blk, cb_ref[:, sl], sub_ref[:, sl], sdb_ref[:, sl], 8)
        qb_o[:, sl] = (blk * (B_QK ** -0.5)).astype(BF16)

    ckv = _rms(p[:, 1024:1152], B_KV_LORA) * gkl_ref[...]
    kr = p[:, 1152:1280]
    if not rope:
        ckvf_o[...] = ckv
        krf_o[...] = kr
    tabs = (cb_ref, sub_ref, sdb_ref) if rope else None
    _mla_kv(ckv, kr, wkv_ref, place_ref, gkb_ref, lo, kb_o, vlo_o, vhi_o, tabs)


def _const_spec(shape):
    nd = len(shape)
    return pl.BlockSpec(shape, lambda *a: (0,) * nd)


def _pre_attn_call(rope, x, off, ada, layer, g, lw, tabs):
    n_tiles = (T_LAT if rope else T_CTX) // TM
    n_rows = n_tiles * TM
    if rope:
        ada_map = lambda i: (layer, 1 + i // LAT_TILES_PER_SEQ, 0, 0)
    else:
        ada_map = lambda i: (layer, 0, 0, 0)
    in_specs = [
        pl.BlockSpec((TM, D), lambda i: (i + off, 0)),
        pl.BlockSpec((None, None, N_ADA, D), ada_map),
        _const_spec((1, D)),
        _const_spec((D, IN_PAD)),
        _const_spec((B_Q_LORA, B_HEADS * SLOT)),
        _const_spec((B_KV_LORA, B_HEADS * SLOT + B_HEADS * B_V)),
        _const_spec((LANE, B_HEADS * SLOT)),
        _const_spec((1, 512)), _const_spec((1, LANE)), _const_spec((1, B_Q_LORA)),
        _const_spec((1, B_KV_LORA)), _const_spec((1, B_HEADS * SLOT)), _const_spec((1, B_HEADS * SLOT)),
    ]
    args = [x, ada, g, lw["w_in"], lw["w_qb"], lw["w_kv"], lw["place"],
            lw["gqa"], lw["gka"], lw["gql"], lw["gkl"], lw["gqb"], lw["gkb"]]
    if rope:
        tmap = lambda i: (i % LAT_TILES_PER_SEQ, 0)
        in_specs += [pl.BlockSpec((TM, 512), tmap)] * 3 + [pl.BlockSpec((TM, B_HEADS * SLOT), tmap)] * 3
        args += list(tabs)
    row = lambda w: pl.BlockSpec((TM, w), lambda i: (i, 0))
    out_shape = [jax.ShapeDtypeStruct((n_rows, w), BF16) for w in (512, 512, 512, 1024, 1024, 512, 512)]
    out_specs = [row(w) for w in (512, 512, 512, 1024, 1024, 512, 512)]
    if not rope:
        out_shape += [jax.ShapeDtypeStruct((n_rows, LANE), F32)] * 4
        out_specs += [row(LANE)] * 4
    return pl.pallas_call(
        functools.partial(_pre_attn_kernel, rope),
        out_shape=out_shape, grid=(n_tiles,), in_specs=in_specs, out_specs=out_specs,
        compiler_params=pltpu.CompilerParams(
            dimension_semantics=("arbitrary",), vmem_limit_bytes=VMEM_BIG),
        name="pre_attn_lat" if rope else "pre_attn_ctx",
    )(*args)


def _cache_kv_kernel(ckv_ref, kr_ref, wkv_ref, place_ref, gkb_ref, kb_o, vlo_o, vhi_o):
    lo = _lo_mask(PAST)
    _mla_kv(ckv_ref[...], kr_ref[...], wkv_ref, place_ref, gkb_ref, lo, kb_o, vlo_o, vhi_o, None)


def _cache_kv_call(ckv, kr, lw):
    rows = DEC_BATCH * PAST
    row = lambda w: pl.BlockSpec((PAST, w), lambda i: (i, 0))
    return pl.pallas_call(
        _cache_kv_kernel,
        out_shape=[jax.ShapeDtypeStruct((rows, 1024), BF16), jax.ShapeDtypeStruct((rows, 512), BF16),
                   jax.ShapeDtypeStruct((rows, 512), BF16)],
        grid=(DEC_BATCH,),
        in_specs=[row(LANE), row(LANE),
                  _const_spec((B_KV_LORA, B_HEADS * SLOT + B_HEADS * B_V)),
                  _const_spec((LANE, B_HEADS * SLOT)), _const_spec((1, B_HEADS * SLOT))],
        out_specs=[row(1024), row(512), row(512)],
        compiler_params=pltpu.CompilerParams(dimension_semantics=("arbitrary",)),
        name="cache_kv",
    )(ckv, kr, lw["w_kv"], lw["place"], lw["gkb"])


def _softmax_pv(scores, values, sink):
    m = jnp.max(scores[0], axis=-1, keepdims=True)
    for s in scores[1:]:
        m = jnp.maximum(m, jnp.max(s, axis=-1, keepdims=True))
    if sink is not None:
        m = jnp.maximum(m, sink)
    ps = [jnp.exp(s - m) for s in scores]
    den = jnp.sum(ps[0], axis=-1, keepdims=True)
    for p in ps[1:]:
        den = den + jnp.sum(p, axis=-1, keepdims=True)
    if sink is not None:
        den = den + jnp.exp(sink - m)
    inv = 1.0 / den
    out = _dot((ps[0] * inv).astype(BF16), values[0])
    for p, v in zip(ps[1:], values[1:]):
        out = out + _dot((p * inv).astype(BF16), v)
    return out


def _attn_ctx_kernel(sink_ref, qa_ref, ka4_ref, va4_ref, qb_ref, kb_ref, vlo_ref, vhi_ref, o_ref):
    for j in range(4):
        q = qa_ref[:, j * LANE:(j + 1) * LANE]
        acc = None
        for par in range(2):
            c = (2 * (j // 2) + par) * LANE
            s = _dot_nt(q, ka4_ref[:, c:c + LANE])
            o = _softmax_pv([s], [va4_ref[:, c:c + LANE]], sink_ref[2 * j + par])
            acc = o if acc is None else acc + o
        o_ref[:, j * LANE:(j + 1) * LANE] = acc.astype(BF16)
    for j in range(4):
        acc = None
        for par, v_ref in enumerate((vlo_ref, vhi_ref)):
            hh = 2 * j + par
            s = _dot_nt(qb_ref[:, hh * SLOT:(hh + 1) * SLOT], kb_ref[:, hh * SLOT:(hh + 1) * SLOT])
            o = _softmax_pv([s], [v_ref[:, j * LANE:(j + 1) * LANE]], None)
            acc = o if acc is None else acc + o
        o_ref[:, 512 + j * LANE:512 + (j + 1) * LANE] = acc.astype(BF16)


def _attn_ctx_call(sink, qa, ka4, va4, qb, kb, vlo, vhi):
    row = lambda w: pl.BlockSpec((SEQ, w), lambda b: (b, 0))
    return pl.pallas_call(
        _attn_ctx_kernel,
        out_shape=jax.ShapeDtypeStruct((T_CTX, D), BF16),
        grid=(BATCH,),
        in_specs=[pl.BlockSpec(memory_space=pltpu.SMEM),
                  row(512), row(512), row(512), row(1024), row(1024), row(512), row(512)],
        out_specs=row(D),
        compiler_params=pltpu.CompilerParams(dimension_semantics=("arbitrary",)),
        name="attn_ctx",
    )(sink, qa, ka4, va4, qb, kb, vlo, vhi)


QB = 128
N_QB = DEC_SEQ // QB
KPAD = DEC_SEQ + 2 * QB
KALL = PAST + DEC_SEQ


def _attn_lat_kernel(sink_ref, qa_ref, ka4p_ref, va4p_ref, ka4c_ref, va4c_ref,
                     qb_ref, kb_ref, vlo_ref, vhi_ref, o_ref):
    n = pl.program_id(1)
    r = lax.broadcasted_iota(jnp.int32, (QB, 3 * QB), 0)
    c = lax.broadcasted_iota(jnp.int32, (QB, 3 * QB), 1)
    rel = c - QB - r
    kpos = (n - 1) * QB + c
    valid = (jnp.abs(rel) <= WINDOW) & (kpos >= 0) & (kpos < DEC_SEQ)
    k0 = pl.multiple_of(n * QB, QB)
    for j in range(4):
        q = qa_ref[:, j * LANE:(j + 1) * LANE]
        acc = None
        for par in range(2):
            cc = (2 * (j // 2) + par) * LANE
            s_loc = _dot_nt(q, ka4p_ref[pl.ds(k0, 3 * QB), cc:cc + LANE])
            s_loc = jnp.where(valid, s_loc, -jnp.inf)
            s_ctx = _dot_nt(q, ka4c_ref[:, cc:cc + LANE])
            o = _softmax_pv([s_ctx, s_loc],
                            [va4c_ref[:, cc:cc + LANE], va4p_ref[pl.ds(k0, 3 * QB), cc:cc + LANE]],
                            sink_ref[2 * j + par])
            acc = o if acc is None else acc + o
        o_ref[:, j * LANE:(j + 1) * LANE] = acc.astype(BF16)
    for j in range(4):
        acc = None
        for par, v_ref in enumerate((vlo_ref, vhi_ref)):
            hh = 2 * j + par
            s = _dot_nt(qb_ref[:, hh * SLOT:(hh + 1) * SLOT], kb_ref[:, hh * SLOT:(hh + 1) * SLOT])
            o = _softmax_pv([s], [v_ref[:, j * LANE:(j + 1) * LANE]], None)
            acc = o if acc is None else acc + o
        o_ref[:, 512 + j * LANE:512 + (j + 1) * LANE] = acc.astype(BF16)


def _attn_lat_call(sink, qa, ka4p, va4p, ka4c, va4c, qb, kball, vlo_all, vhi_all):
    qrow = lambda w: pl.BlockSpec((QB, w), lambda b, n: (b * N_QB + n, 0))
    per_b = lambda rows, w: pl.BlockSpec((None, rows, w), lambda b, n: (b, 0, 0))
    return pl.pallas_call(
        _attn_lat_kernel,
        out_shape=jax.ShapeDtypeStruct((T_LAT, D), BF16),
        grid=(DEC_BATCH, N_QB),
        in_specs=[pl.BlockSpec(memory_space=pltpu.SMEM),
                  qrow(512), per_b(KPAD, 512), per_b(KPAD, 512), per_b(PAST, 512), per_b(PAST, 512),
                  qrow(1024), per_b(KALL, 1024), per_b(KALL, 512), per_b(KALL, 512)],
        out_specs=qrow(D),
        compiler_params=pltpu.CompilerParams(dimension_semantics=("arbitrary", "arbitrary")),
        name="attn_lat",
    )(sink, qa, ka4p, va4p, ka4c, va4c, qb, kball, vlo_all, vhi_all)


def _post_attn_kernel(xc_ref, xl_ref, oc_ref, ol_ref, ada_ref, g_ref, wo_ref, wg_ref, wu_ref, wd_ref, y_ref):
    is_ctx = pl.program_id(0) < T_CTX // FFN_TM
    x = jnp.where(is_ctx, xc_ref[...], xl_ref[...])
    o = jnp.where(is_ctx, oc_ref[...], ol_ref[...])
    mix = _dot(o, wo_ref[...])
    x1 = x + ada_ref[2:3, :] * mix
    h = _modulate(x1, g_ref[...], ada_ref[3:4, :], ada_ref[4:5, :]).astype(BF16)
    acc = None
    for c0 in range(0, D_FF, FFN_CH):
        gg = _dot(h, wg_ref[:, c0:c0 + FFN_CH])
        uu = _dot(h, wu_ref[:, c0:c0 + FFN_CH])
        a = (gg * _sigmoid(gg) * uu).astype(BF16)
        part = _dot(a, wd_ref[c0:c0 + FFN_CH, :])
        acc = part if acc is None else acc + part
    y_ref[...] = x1 + ada_ref[5:6, :] * acc


def _tile_cond(i, tiles_per_ctx, tiles_per_lat_seq):
    return jnp.where(i < tiles_per_ctx, 0, 1 + (i - tiles_per_ctx) // tiles_per_lat_seq)


def _post_attn_call(x_ctx, x_lat, x_lat_off, o_ctx, o_lat, ada, layer, g, wo, wg, wu, wd):
    n_tiles = T_ALL // FFN_TM
    n_ctx = T_CTX // FFN_TM
    ada_map = lambda i: (layer, _tile_cond(i, n_ctx, DEC_SEQ // FFN_TM), 0, 0)
    ctx_map = lambda i: (jnp.minimum(i, n_ctx - 1), 0)
    one = pl.Buffered(1)
    return pl.pallas_call(
        _post_attn_kernel,
        out_shape=jax.ShapeDtypeStruct((T_ALL, D), F32),
        grid=(n_tiles,),
        in_specs=[pl.BlockSpec((FFN_TM, D), ctx_map),
                  pl.BlockSpec((FFN_TM, D), lambda i: (jnp.maximum(i - n_ctx, 0) + x_lat_off, 0)),
                  pl.BlockSpec((FFN_TM, D), ctx_map),
                  pl.BlockSpec((FFN_TM, D), lambda i: (jnp.maximum(i - n_ctx, 0), 0)),
                  pl.BlockSpec((None, None, N_ADA, D), ada_map),
                  _const_spec((1, D)),
                  pl.BlockSpec((D, D), lambda i: (0, 0), pipeline_mode=one),
                  pl.BlockSpec((D, D_FF), lambda i: (0, 0), pipeline_mode=one),
                  pl.BlockSpec((D, D_FF), lambda i: (0, 0), pipeline_mode=one),
                  pl.BlockSpec((D_FF, D), lambda i: (0, 0), pipeline_mode=one)],
        out_specs=pl.BlockSpec((FFN_TM, D), lambda i: (i, 0)),
        compiler_params=pltpu.CompilerParams(
            dimension_semantics=("arbitrary",), vmem_limit_bytes=VMEM_BIG),
        name="post_attn_ffn",
    )(x_ctx, x_lat, o_ctx, o_lat, ada, g, wo, wg, wu, wd)


HALO = 16
EXT = TM + 2 * HALO
CONV_RC = 64
CONV_LC = 128
CONV_WIN = CONV_RC + 24
CONV_LN_ROWS = 32


def _conv_router_kernel(x_ref, xp_ref, xn_ref, ada_ref, gm_ref, gf_ref, w1_ref, b1_ref, wdw_ref, bdw_ref,
                        gln_ref, bln_ref, w2_ref, b2_ref, wr_ref,
                        x1_o, h2_o, ri_o, rw_o, cnt_o, p_ref, c_ref, v_ref, run_ref):
    i = pl.program_id(0)

    @pl.when(i == 0)
    def _():
        run_ref[...] = jnp.zeros_like(run_ref)

    is_lat = i >= N_CTX_TILES
    j = (i - N_CTX_TILES) % LAT_TILES_PER_SEQ
    lflag = jnp.where(is_lat & (j != 0), 1.0, 0.0)
    rflag = jnp.where(is_lat & (j != LAT_TILES_PER_SEQ - 1), 1.0, 0.0)

    x = x_ref[...]
    xx = jnp.concatenate([xp_ref[...], x, xn_ref[...]], axis=0)
    h = _modulate(xx, gm_ref[...], ada_ref[0:1, :], ada_ref[1:2, :])
    a = _dot(h.astype(BF16), w1_ref[...]) + b1_ref[...]
    u = a[:, :D] * _sigmoid(a[:, D:])
    row = lax.broadcasted_iota(jnp.int32, (EXT, 1), 0)
    u = u * jnp.where(row < HALO, lflag, jnp.where(row >= HALO + TM, rflag, 1.0))

    p_ref[...] = u
    for rc in range(TM // CONV_RC):
        for lc in range(D // CONV_LC):
            ls = slice(lc * CONV_LC, (lc + 1) * CONV_LC)
            acc = jnp.zeros((CONV_RC, CONV_LC), F32) + bdw_ref[:, ls]
            base = p_ref[rc * CONV_RC:rc * CONV_RC + CONV_WIN + 8, ls]
            for r in range(8):
                win = base if r == 0 else pltpu.roll(base, CONV_WIN + 8 - r, 0)
                for q in range(4):
                    o = 8 * q + r
                    if 1 <= o <= CONV_W:
                        acc = acc + wdw_ref[o - 1:o, ls] * win[8 * q:8 * q + CONV_RC]
            c_ref[rc * CONV_RC:(rc + 1) * CONV_RC, ls] = acc

    for c0 in range(0, TM, CONV_LN_ROWS):
        acc = c_ref[c0:c0 + CONV_LN_ROWS, :]
        mu = jnp.mean(acc, axis=-1, keepdims=True)
        xc = acc - mu
        y = xc * lax.rsqrt(jnp.mean(xc * xc, axis=-1, keepdims=True) + EPS)
        y = y * gln_ref[...] + bln_ref[...]
        v_ref[c0:c0 + CONV_LN_ROWS, :] = (y * _sigmoid(y)).astype(BF16)

    mix = _dot(v_ref[...], w2_ref[...]) + b2_ref[...]
    x1 = x + ada_ref[2:3, :] * mix
    x1_o[...] = x1
    h2 = _modulate(x1, gf_ref[...], ada_ref[3:4, :], ada_ref[4:5, :])
    h2_o[...] = h2.astype(BF16)

    lane = lax.broadcasted_iota(jnp.int32, (TM, LANE), 1)
    lane_f = lane.astype(F32)
    logits = jnp.where(lane < N_EXP, _dot_x3(h2, wr_ref[...]), -jnp.inf)
    m1 = jnp.max(logits, axis=-1, keepdims=True)
    i1 = jnp.min(jnp.where(logits == m1, lane_f, float(LANE)), axis=-1, keepdims=True)
    rest = jnp.where(lane_f == i1, -jnp.inf, logits)
    m2 = jnp.max(rest, axis=-1, keepdims=True)
    i2 = jnp.min(jnp.where(rest == m2, lane_f, float(LANE)), axis=-1, keepdims=True)
    e = jnp.exp(m2 - m1)
    w1 = 1.0 / (1.0 + e)
    w2 = e / (1.0 + e)

    oh1 = jnp.where(lane_f == i1, 1.0, 0.0)
    oh2 = jnp.where(lane_f == i2, 1.0, 0.0)
    tr = lax.broadcasted_iota(jnp.int32, (TM, TM), 0)
    tc = lax.broadcasted_iota(jnp.int32, (TM, TM), 1)
    tri = jnp.where(tr > tc, 1.0, 0.0).astype(BF16)
    run = run_ref[0:1, :]
    cnt1 = jnp.sum(oh1, axis=0, keepdims=True)
    cnt2 = jnp.sum(oh2, axis=0, keepdims=True)
    pre1 = _dot(tri, oh1.astype(BF16)) + run
    pre2 = _dot(tri, oh2.astype(BF16)) + (run + cnt1)
    rank1 = jnp.sum(oh1 * pre1, axis=-1, keepdims=True)
    rank2 = jnp.sum(oh2 * pre2, axis=-1, keepdims=True)
    new_run = run + cnt1 + cnt2
    run_ref[...] = jnp.broadcast_to(new_run, run_ref.shape)
    srow = lax.broadcasted_iota(jnp.int32, cnt_o.shape, 0)
    cnt_o[...] = jnp.where(srow == 0, run, jnp.where(srow == 1, cnt1 + cnt2, 0.0))

    ri = jnp.where(lane == 0, i1, jnp.where(lane == 1, i2, jnp.where(lane == 2, rank1, jnp.where(lane == 3, rank2, 0.0))))
    ri_o[...] = ri.astype(jnp.int32)

    rw_o[...] = jnp.where(lane == 0, w1, jnp.where(lane == 1, w2, 0.0))


def _conv_router_call(x, ada, layer, gm, gf, cw):
    ada_map = lambda i: (layer, _tile_cond(i, N_CTX_TILES, LAT_TILES_PER_SEQ), 0, 0)
    hb = TM // HALO
    row = lambda w: pl.BlockSpec((TM, w), lambda i: (i, 0))
    return pl.pallas_call(
        _conv_router_kernel,
        out_shape=[jax.ShapeDtypeStruct((T_ALL, D), F32), jax.ShapeDtypeStruct((T_ALL, D), BF16),
                   jax.ShapeDtypeStruct((T_ALL, LANE), jnp.int32), jax.ShapeDtypeStruct((T_ALL, LANE), F32),
                   jax.ShapeDtypeStruct((N_TILES, 8, LANE), F32)],
        grid=(N_TILES,),
        in_specs=[row(D),
                  pl.BlockSpec((HALO, D), lambda i: (jnp.maximum(i * hb - 1, 0), 0)),
                  pl.BlockSpec((HALO, D), lambda i: (jnp.minimum((i + 1) * hb, T_ALL // HALO - 1), 0)),
                  pl.BlockSpec((None, None, N_ADA, D), ada_map),
                  _const_spec((1, D)), _const_spec((1, D)),
                  _const_spec((D, 2 * D)), _const_spec((1, 2 * D)),
                  _const_spec((CONV_W, D)), _const_spec((1, D)),
                  _const_spec((1, D)), _const_spec((1, D)),
                  _const_spec((D, D)), _const_spec((1, D)),
                  _const_spec((D, LANE))],
        out_specs=[row(D), row(D), row(LANE), row(LANE), pl.BlockSpec((None, 8, LANE), lambda i: (i, 0, 0))],
        scratch_shapes=[pltpu.VMEM((EXT, D), F32), pltpu.VMEM((TM, D), F32), pltpu.VMEM((TM, D), BF16),
                        pltpu.VMEM((8, LANE), F32)],
        compiler_params=pltpu.CompilerParams(
            dimension_semantics=("arbitrary",), vmem_limit_bytes=VMEM_BIG),
        name="conv_router",
    )(x, x, x, ada, gm, gf, cw["w1"], cw["b1"], cw["wdw"], cw["bdw"], cw["gln"], cw["bln"],
      cw["w2"], cw["b2"], cw["wr"])


def _moe_kernel(se_ref, row0_ref, nsub_ref, ta_ref, tn_ref, h_ref, post_ref, rwt_ref, wg_ref, wu_ref, wd_ref,
                ys_in_ref, ys_ref, hb_ref, acc_ref, gs_ref, hstage, wgb, wub, wdb, sem_in, sem_out):
    del ys_in_ref
    j = pl.program_id(0)
    f = pl.program_id(1)
    nsub = nsub_ref[j]
    row0 = row0_ref[j]
    e = se_ref[j]

    def tile_load(i, slot):
        r = pl.multiple_of(i * TM, TM)
        return pltpu.make_async_copy(h_ref.at[pl.ds(r, TM), :], hstage.at[slot], sem_in.at[slot])

    def store(s):
        r = pl.multiple_of(s * MOE_SUB, MOE_SUB)
        dst = ys_ref.at[pl.ds(pl.multiple_of(row0 + s * MOE_SUB, MOE_SUB), MOE_SUB), :]
        return pltpu.make_async_copy(hb_ref.at[pl.ds(r, MOE_SUB), :], dst, sem_out)

    def gather_tile(i, carry):
        slot = i % MOE_GSLOTS
        tile_load(i, slot).wait()

        @pl.when(i + MOE_GDEPTH < N_TILES)
        def _():
            tile_load(i + MOE_GDEPTH, (i + MOE_GDEPTH) % MOE_GSLOTS).start()

        a = ta_ref[i * N_EXP + e] - row0
        n = tn_ref[i * N_EXP + e]
        last = a + n - 1

        @pl.when((n > 0) & (last >= 0) & (a < nsub * MOE_SUB))
        def _():
            start = (jnp.maximum(a, 0) // 8) * 8
            end = jnp.minimum(last, nsub * MOE_SUB - 1)
            pp = post_ref[pl.ds(pl.multiple_of(i * 8, 8), 8), :]
            riota = lax.broadcasted_iota(jnp.int32, (MOE_GB, TM), 0)
            ww = rwt_ref[pl.ds(pl.multiple_of(i * 8, 8), 8), :]

            def window(w):
                rr = pl.multiple_of(start + w * MOE_GB, 8)
                base = row0 + rr
                hit0 = (pp[0:1, :] - base) == riota
                hit1 = (pp[1:2, :] - base) == riota
                acc_ref[pl.ds(rr, MOE_GB), :] += _dot(jnp.where(hit0 | hit1, 1.0, 0.0).astype(BF16),
                                                      hstage[slot])
                gate = jnp.sum(jnp.where(hit0, ww[0:1, :], 0.0) + jnp.where(hit1, ww[1:2, :], 0.0),
                               axis=-1, keepdims=True)
                gs_ref[pl.ds(rr, MOE_GB), :] += jnp.broadcast_to(gate, (MOE_GB, LANE))

            window(0)
            for extra in range(1, (TM + 8) // MOE_GB + 1):
                @pl.when(start + extra * MOE_GB <= end)
                def _():
                    window(extra)

        return carry

    def part(r):
        hh = hb_ref[pl.ds(r, MOE_SUB), :]
        out = None
        for c0 in range(0, MOE_TF, MOE_TC):
            gg = _dot(hh, wgb[:, c0:c0 + MOE_TC])
            uu = _dot(hh, wub[:, c0:c0 + MOE_TC])
            a = (gg * _sigmoid(gg) * uu).astype(BF16)
            piece = _dot(a, wdb[c0:c0 + MOE_TC, :])
            out = piece if out is None else out + piece
        return out

    def for_each_subtile(fn):
        def pair(p, carry):
            fn(2 * p)
            fn(2 * p + 1)
            return carry

        lax.fori_loop(0, nsub // 2, pair, 0)

        @pl.when(nsub % 2 == 1)
        def _():
            fn(nsub - 1)

    @pl.when(nsub > 0)
    def _():
        wgb[...] = wg_ref[...].astype(BF16)
        wub[...] = wu_ref[...].astype(BF16)
        wdb[...] = wd_ref[...].astype(BF16)

        @pl.when(f == 0)
        def _():
            for d in range(MOE_GDEPTH):
                tile_load(d, d).start()

            def clear(s, carry):
                r = pl.multiple_of(s * MOE_SUB, MOE_SUB)
                acc_ref[pl.ds(r, MOE_SUB), :] = jnp.zeros((MOE_SUB, D), F32)
                gs_ref[pl.ds(r, MOE_SUB), :] = jnp.zeros((MOE_SUB, LANE), F32)
                return carry

            lax.fori_loop(0, nsub + 1, clear, 0)
            lax.fori_loop(0, N_TILES, gather_tile, 0)

            def first(s):
                r = pl.multiple_of(s * MOE_SUB, MOE_SUB)
                hb_ref[pl.ds(r, MOE_SUB), :] = acc_ref[pl.ds(r, MOE_SUB), :].astype(BF16)
                acc_ref[pl.ds(r, MOE_SUB), :] = part(r)

            for_each_subtile(first)

        @pl.when((f > 0) & (f < MOE_NF - 1))
        def _():
            def middle(s):
                r = pl.multiple_of(s * MOE_SUB, MOE_SUB)
                acc_ref[pl.ds(r, MOE_SUB), :] += part(r)

            for_each_subtile(middle)

        @pl.when(f == MOE_NF - 1)
        def _():
            def final(s):
                r = pl.multiple_of(s * MOE_SUB, MOE_SUB)
                y = acc_ref[pl.ds(r, MOE_SUB), :] + part(r)
                hb_ref[pl.ds(r, MOE_SUB), :] = (y * gs_ref[pl.ds(r, MOE_SUB), 0:1]).astype(BF16)
                store(s).start()

            for_each_subtile(final)

            def drain(s, carry):
                store(s).wait()
                return carry

            lax.fori_loop(0, nsub, drain, 0)


def _moe_call(st_e, st_row0, st_nsub, t_a, t_n, h2b, pos_t, rw_t, wg, wu, wd, ys0, lidx):
    f_eff = lambda j, f, ns: jnp.where(ns[j] > 0, f, MOE_NF - 1)
    wmap_in = lambda j, f, se, r0, ns, ta, tn: (lidx, se[j], 0, f_eff(j, f, ns))
    wmap_out = lambda j, f, se, r0, ns, ta, tn: (lidx, se[j], f_eff(j, f, ns), 0)
    return pl.pallas_call(
        _moe_kernel,
        out_shape=jax.ShapeDtypeStruct((MOE_ROWS, D), BF16),
        grid_spec=pltpu.PrefetchScalarGridSpec(
            num_scalar_prefetch=5, grid=(MOE_NS, MOE_NF),
            in_specs=[
                pl.BlockSpec(memory_space=pl.ANY),
                pl.BlockSpec((N_TILES * 8, TM), lambda j, f, se, r0, ns, ta, tn: (0, 0)),
                pl.BlockSpec((N_TILES * 8, TM), lambda j, f, se, r0, ns, ta, tn: (0, 0)),
                pl.BlockSpec((None, None, D, MOE_TF), wmap_in),
                pl.BlockSpec((None, None, D, MOE_TF), wmap_in),
                pl.BlockSpec((None, None, MOE_TF, D), wmap_out),
                pl.BlockSpec(memory_space=pl.ANY),
            ],
            out_specs=pl.BlockSpec(memory_space=pl.ANY),
            scratch_shapes=[pltpu.VMEM((MOE_R, D), BF16), pltpu.VMEM((MOE_R + MOE_SUB, D), F32),
                            pltpu.VMEM((MOE_R + MOE_SUB, LANE), F32),
                            pltpu.VMEM((MOE_GSLOTS, TM, D), BF16),
                            pltpu.VMEM((D, MOE_TF), BF16), pltpu.VMEM((D, MOE_TF), BF16),
                            pltpu.VMEM((MOE_TF, D), BF16),
                            pltpu.SemaphoreType.DMA((MOE_GSLOTS,)), pltpu.SemaphoreType.DMA(())]),
        input_output_aliases={11: 0},
        compiler_params=pltpu.CompilerParams(
            dimension_semantics=("arbitrary", "arbitrary"), vmem_limit_bytes=VMEM_BIG,
            has_side_effects=True),
        name="moe_experts",
    )(st_e, st_row0, st_nsub, t_a, t_n, h2b, pos_t, rw_t, wg, wu, wd, ys0)


def _combine_kernel(split, ta_ref, tn_ref, x_ref, ada_ref, posc_ref, ys_ref, *rest):
    if split:
        oc_ref, ol_ref, ybuf, oacc, bc, sem = rest
    else:
        o_ref, ybuf, oacc, bc, sem = rest
    g = pl.program_id(0)

    def blocks(i, e):
        a = ta_ref[i * N_EXP + e]
        n = tn_ref[i * N_EXP + e]
        lo = jnp.minimum(a // MOE_SUB, MOE_ROWS // MOE_SUB - 1)
        return n > 0, lo, (n > 0) & ((a + n - 1) // MOE_SUB > lo)

    def block_copy(slot, e, k, blk):
        src = ys_ref.at[pl.ds(pl.multiple_of(blk * MOE_SUB, MOE_SUB), MOE_SUB), :]
        return pltpu.make_async_copy(src, ybuf.at[slot, 2 * e + k], sem.at[slot, 2 * e + k])

    @pl.when(g < N_TILES)
    def _():
        for e in range(N_EXP):
            _, lo, two = blocks(g, e)
            block_copy(g % 2, e, 0, lo).start()

            @pl.when(two)
            def _():
                block_copy(g % 2, e, 1, lo + 1).start()

    @pl.when(g > 0)
    def _():
        i = g - 1
        slot = i % 2
        for n in range(2):
            bc[n] = jnp.broadcast_to(posc_ref[:, n:n + 1], (TM, MOE_SUB))
        liota = lax.broadcasted_iota(jnp.int32, (TM, MOE_SUB), 1)

        def weighted(has, blk, y):
            base = jnp.where(has, blk * MOE_SUB, -MOE_ROWS)
            hit = (bc[0] - base == liota) | (bc[1] - base == liota)
            return _dot(jnp.where(hit, 1.0, 0.0).astype(BF16), y)

        total = None
        for e in range(N_EXP):
            has, lo, _ = blocks(i, e)
            block_copy(slot, e, 0, lo).wait()
            t = weighted(has, lo, ybuf[slot, 2 * e])
            total = t if total is None else total + t
        oacc[...] = total
        for e in range(N_EXP):
            has, lo, two = blocks(i, e)

            @pl.when(two)
            def _():
                block_copy(slot, e, 1, lo + 1).wait()
                oacc[...] += weighted(has, lo + 1, ybuf[slot, 2 * e + 1])

        out = x_ref[...] + ada_ref[5:6, :] * oacc[...]
        if split:
            @pl.when(i < N_CTX_TILES)
            def _():
                oc_ref[...] = out

            @pl.when(i >= N_CTX_TILES)
            def _():
                ol_ref[...] = out
        else:
            o_ref[...] = out


def _combine_call(split, t_a, t_n, x1, ada, layer, posc, ys):
    prev = lambda g: jnp.maximum(g - 1, 0)
    ada_map = lambda g, a, n: (layer, _tile_cond(prev(g), N_CTX_TILES, LAT_TILES_PER_SEQ), 0, 0)
    if split:
        out_shape = [jax.ShapeDtypeStruct((T_CTX, D), F32), jax.ShapeDtypeStruct((T_LAT, D), F32)]
        out_specs = [pl.BlockSpec((TM, D), lambda g, a, n: (jnp.minimum(prev(g), N_CTX_TILES - 1), 0)),
                     pl.BlockSpec((TM, D), lambda g, a, n: (jnp.maximum(prev(g) - N_CTX_TILES, 0), 0))]
    else:
        out_shape = jax.ShapeDtypeStruct((T_ALL, D), F32)
        out_specs = pl.BlockSpec((TM, D), lambda g, a, n: (prev(g), 0))
    return pl.pallas_call(
        functools.partial(_combine_kernel, split),
        out_shape=out_shape,
        grid_spec=pltpu.PrefetchScalarGridSpec(
            num_scalar_prefetch=2, grid=(N_TILES + 1,),
            in_specs=[pl.BlockSpec((TM, D), lambda g, a, n: (prev(g), 0)),
                      pl.BlockSpec((None, None, N_ADA, D), ada_map),
                      pl.BlockSpec((TM, LANE), lambda g, a, n: (prev(g), 0)),
                      pl.BlockSpec(memory_space=pl.ANY)],
            out_specs=out_specs,
            scratch_shapes=[pltpu.VMEM((2, 2 * N_EXP, MOE_SUB, D), BF16), pltpu.VMEM((TM, D), F32),
                            pltpu.VMEM((2, TM, MOE_SUB), jnp.int32),
                            pltpu.SemaphoreType.DMA((2, 2 * N_EXP))]),
        compiler_params=pltpu.CompilerParams(
            dimension_semantics=("arbitrary",), vmem_limit_bytes=VMEM_BIG),
        name="moe_combine",
    )(t_a, t_n, x1, ada, posc, ys)


def _moe_layer(x1, h2b, ri, rw, runs, ada, layer, wg, wu, wd, lidx, split):
    idx = ri[:, 0:2]
    rank = ri[:, 2:4]
    before = runs[:, 0, :N_EXP].astype(jnp.int32)
    t_n = runs[:, 1, :N_EXP].astype(jnp.int32)
    counts = before[-1] + t_n[-1]
    padded = ((counts + MOE_SUB - 1) // MOE_SUB) * MOE_SUB
    ends = jnp.cumsum(padded)
    starts = ends - padded
    sel = idx[:, :, None] == jnp.arange(N_EXP, dtype=jnp.int32)[None, None, :]
    pos = (jnp.sum(jnp.where(sel, starts[None, None, :], 0), axis=-1) + rank).astype(jnp.int32)
    t_a = (starts[None, :] + before).reshape(-1).astype(jnp.int32)
    t_n = t_n.reshape(-1)
    pos_t = jnp.pad(pos.reshape(N_TILES, TM, 2).transpose(0, 2, 1), ((0, 0), (0, 6), (0, 0)))
    pos_t = pos_t.reshape(N_TILES * 8, TM)
    posc = jnp.pad(pos, ((0, 0), (0, LANE - 2)))
    n_st = (padded + MOE_R - 1) // MOE_R
    st_end = jnp.cumsum(n_st)
    st_begin = st_end - n_st
    j = jnp.arange(MOE_NS, dtype=jnp.int32)
    se = jnp.sum((j[:, None] >= st_end[None, :]).astype(jnp.int32), axis=1)
    used = se < N_EXP
    se_c = jnp.minimum(se, N_EXP - 1)
    k = j - st_begin[se_c]
    row0 = jnp.where(used, starts[se_c] + k * MOE_R, 0).astype(jnp.int32)
    nsub = jnp.where(used, jnp.minimum(MOE_R, padded[se_c] - k * MOE_R) // MOE_SUB, 0).astype(jnp.int32)
    last = jnp.maximum(jnp.sum(used.astype(jnp.int32)) - 1, 0)
    se_eff = jnp.where(used, se_c, se_c[last]).astype(jnp.int32)
    rw_t = jnp.pad(rw[:, 0:2].reshape(N_TILES, TM, 2).transpose(0, 2, 1), ((0, 0), (0, 6), (0, 0)))
    rw_t = rw_t.reshape(N_TILES * 8, TM)
    ys = _moe_call(se_eff, row0, nsub, t_a, t_n, h2b, pos_t, rw_t, wg, wu, wd,
                   jnp.zeros((MOE_ROWS, D), BF16), lidx)
    return _combine_call(split, t_a, t_n, x1, ada, layer, posc, ys)


def _rope_tables():
    t = jnp.arange(DEC_SEQ)
    rows = (t // GRID_W).astype(F32)
    cols = (t % GRID_W).astype(F32)

    def axis_tabs(pos, d):
        inv = ROPE_BASE ** (-jnp.arange(0, d, 2, dtype=F32) / d)
        ang = pos[:, None] * inv[None, :]
        cos, sin = jnp.cos(ang), jnp.sin(ang)
        z = jnp.zeros_like(sin)
        return (jnp.concatenate([cos, cos], -1), jnp.concatenate([-sin, z], -1), jnp.concatenate([z, sin], -1))

    def axial(d):
        r = axis_tabs(rows, d // 2)
        c = axis_tabs(cols, d // 2)
        return [jnp.concatenate([a, b], -1) for a, b in zip(r, c)]

    tabs_a = [jnp.tile(tb, (1, A_HEADS)) for tb in axial(A_DH)]
    cb, sub, sdb = axial(B_ROPE)
    ones = jnp.ones((DEC_SEQ, B_NOPE), F32)
    zer = jnp.zeros((DEC_SEQ, B_NOPE), F32)
    pad = jnp.zeros((DEC_SEQ, SLOT - B_QK), F32)
    slot = lambda first, tb: jnp.tile(jnp.concatenate([first, tb, pad], -1), (1, B_HEADS))
    tabs_b = [slot(ones, cb), slot(zer, sub), slot(zer, sdb)]
    return tabs_a + tabs_b


def _attn_weights(e, w_in_attn, g_qk_a_q, g_qk_a_k, g_mla_q_a, w_mla_q_b, g_mla_kv_a, w_mla_kv_b,
                  g_qk_b_q, g_qk_b_k):
    w_in = jnp.pad(w_in_attn[e], ((0, 0), (0, IN_PAD - IN_DIM))).astype(BF16)
    w_qb = jnp.pad(w_mla_q_b[e].reshape(B_Q_LORA, B_HEADS, B_QK), ((0, 0), (0, 0), (0, SLOT - B_QK)))
    w_qb = w_qb.reshape(B_Q_LORA, B_HEADS * SLOT).astype(BF16)
    wkv = w_mla_kv_b[e].reshape(B_KV_LORA, B_HEADS, B_NOPE + B_V)
    wk = jnp.pad(wkv[:, :, :B_NOPE], ((0, 0), (0, 0), (0, SLOT - B_NOPE))).reshape(B_KV_LORA, B_HEADS * SLOT)
    wv = wkv[:, :, B_NOPE:].reshape(B_KV_LORA, B_HEADS * B_V)
    w_kv = jnp.concatenate([wk, wv], axis=1).astype(BF16)
    place = jnp.zeros((LANE, B_HEADS, SLOT), F32)
    ii = jnp.arange(B_ROPE)
    place = place.at[ii, :, B_NOPE + ii].set(1.0).reshape(LANE, B_HEADS * SLOT).astype(BF16)
    slot_gain = lambda g: jnp.tile(jnp.pad(g, (0, SLOT - B_QK)), B_HEADS).reshape(1, B_HEADS * SLOT)
    return dict(
        w_in=w_in, w_qb=w_qb, w_kv=w_kv, place=place,
        gqa=jnp.tile(g_qk_a_q[e], A_HEADS).reshape(1, 512),
        gka=jnp.tile(g_qk_a_k[e], A_KV).reshape(1, LANE),
        gql=g_mla_q_a[e].reshape(1, B_Q_LORA), gkl=g_mla_kv_a[e].reshape(1, B_KV_LORA),
        gqb=slot_gain(g_qk_b_q[e]), gkb=slot_gain(g_qk_b_k[e]))


def _pair_variants_host(x):
    h0, h1 = x[..., :64], x[..., 64:]
    z = jnp.zeros_like(h0)
    return jnp.concatenate([h0, z, z, h0, h1, z, z, h1], axis=-1)


def kernel(x_prompt, x_sample, c, cache_k_win, cache_v_win, cache_ckv, cache_krope, c_ctx, w_ada, b_ada, g_norm_mix, g_norm_ffn, w_in_attn, g_qk_a_q, g_qk_a_k, sink_a, g_mla_q_a, w_mla_q_b, g_mla_kv_a, w_mla_kv_b, g_qk_b_q, g_qk_b_k, w_out_attn, w_ffn_gate, w_ffn_up, w_ffn_down, w_conv_pw1, b_conv_pw1, w_conv_dw, b_conv_dw, g_conv_ln, b_conv_ln, w_conv_pw2, b_conv_pw2, w_router, w_moe_gate, w_moe_up, w_moe_down):
    x_ctx, x_lat, lat_tiles_off = x_prompt.reshape(T_CTX, D), x_sample.reshape(T_LAT, D), 0
    cond8 = jnp.concatenate([c_ctx[None, :], c, jnp.zeros((8 - 1 - DEC_BATCH, D), F32)], axis=0)
    ada = _ada_call(cond8, w_ada, b_ada).reshape(DEPTH, 8, N_ADA, D)
    tabs = _rope_tables()

    wo_b = w_out_attn.astype(BF16)
    wg_b = w_ffn_gate.astype(BF16)
    wu_b = w_ffn_up.astype(BF16)
    wd_b = w_ffn_down.astype(BF16)
    w1_b = w_conv_pw1.astype(BF16)
    w2_b = w_conv_pw2.astype(BF16)

    new_k, new_v, new_ckv, new_kr = [], [], [], []
    for layer in range(DEPTH):
        gm = g_norm_mix[layer].reshape(1, D)
        gf = g_norm_ffn[layer].reshape(1, D)
        if layer % 2 == 0:
            e = layer // 2
            lw = _attn_weights(e, w_in_attn, g_qk_a_q, g_qk_a_k, g_mla_q_a, w_mla_q_b, g_mla_kv_a,
                               w_mla_kv_b, g_qk_b_q, g_qk_b_k)
            sink = sink_a[e].reshape(A_HEADS)
            (qa, ka4, va4, qb, kb, vlo, vhi, kaf, vaf, ckvf, krf) = _pre_attn_call(
                False, x_ctx, 0, ada, layer, gm, lw, None)
            o_ctx = _attn_ctx_call(sink, qa, ka4, va4, qb, kb, vlo, vhi)
            new_k.append(kaf.reshape(BATCH, SEQ, A_KV, A_DH))
            new_v.append(vaf.reshape(BATCH, SEQ, A_KV, A_DH))
            new_ckv.append(ckvf.reshape(BATCH, SEQ, B_KV_LORA))
            new_kr.append(krf[:, :B_ROPE].reshape(BATCH, SEQ, B_ROPE))
            (qa, ka4, va4, qb, kb, vlo, vhi) = _pre_attn_call(
                True, x_lat, lat_tiles_off * (T_CTX // TM), ada, layer, gm, lw, tabs)
            kr_c = jnp.pad(cache_krope[:, e].reshape(DEC_BATCH * PAST, B_ROPE), ((0, 0), (0, LANE - B_ROPE)))
            kb_c, vlo_c, vhi_c = _cache_kv_call(cache_ckv[:, e].reshape(DEC_BATCH * PAST, B_KV_LORA), kr_c, lw)
            pad_rows = lambda a: jnp.pad(a.reshape(DEC_BATCH, DEC_SEQ, 512), ((0, 0), (QB, QB), (0, 0)))
            ka4c = _pair_variants_host(cache_k_win[:, e].reshape(DEC_BATCH, PAST, LANE)).astype(BF16)
            va4c = _pair_variants_host(cache_v_win[:, e].reshape(DEC_BATCH, PAST, LANE)).astype(BF16)
            cat = lambda a, b, w: jnp.concatenate(
                [a.reshape(DEC_BATCH, PAST, w), b.reshape(DEC_BATCH, DEC_SEQ, w)], axis=1)
            o_lat = _attn_lat_call(sink, qa, pad_rows(ka4), pad_rows(va4), ka4c, va4c, qb,
                                   cat(kb_c, kb, 1024), cat(vlo_c, vlo, 512), cat(vhi_c, vhi, 512))
            x = _post_attn_call(x_ctx, x_lat, lat_tiles_off * (T_CTX // FFN_TM), o_ctx, o_lat, ada, layer, gf,
                                wo_b[e], wg_b[e], wu_b[e], wd_b[e])
        else:
            o_ = layer // 2
            cw = dict(w1=w1_b[o_], b1=b_conv_pw1[o_].reshape(1, 2 * D), wdw=w_conv_dw[o_],
                      bdw=b_conv_dw[o_].reshape(1, D), gln=g_conv_ln[o_].reshape(1, D),
                      bln=b_conv_ln[o_].reshape(1, D), w2=w2_b[o_], b2=b_conv_pw2[o_].reshape(1, D),
                      wr=jnp.pad(w_router[o_], ((0, 0), (0, LANE - N_EXP))))
            x1, h2b, ri, rw, runs = _conv_router_call(x, ada, layer, gm, gf, cw)
            x = _moe_layer(x1, h2b, ri, rw, runs, ada, layer, w_moe_gate, w_moe_up, w_moe_down, o_,
                           split=(layer == DEPTH - 1))
            x_ctx, x_lat, lat_tiles_off = x, x, 1

    y_prompt = x[0].reshape(BATCH, SEQ, D)
    y_sample = x[1].reshape(DEC_BATCH, DEC_SEQ, D)
    return (y_prompt, y_sample, jnp.stack(new_k, axis=1), jnp.stack(new_v, axis=1),
            jnp.stack(new_ckv, axis=1), jnp.stack(new_kr, axis=1))
```

```python
import functools

import jax
import jax.numpy as jnp
from jax import lax
from jax.experimental import pallas as pl
from jax.experimental.pallas import tpu as pltpu

F32 = jnp.float32
BF16 = jnp.bfloat16

D = 1024
BATCH = 32
SEQ = 256
DEPTH = 4
DEC_BATCH = 2
DEC_SEQ = 1024
PAST = 256
GRID_W = 64
N_EVEN = 2
A_HEADS = 8
A_KV = 2
A_DH = 64
WINDOW = 128
B_HEADS = 8
B_Q_LORA = 256
B_KV_LORA = 128
B_NOPE = 64
B_ROPE = 32
B_QK = B_NOPE + B_ROPE
B_V = 64
IN_DIM = 1184
IN_PAD = 1280
CONV_W = 31
D_FF = 2816
N_EXP = 8
D_FFE = 3584
ROPE_BASE = 10000.0
EPS = 1e-6
N_ADA = 6

T_CTX = BATCH * SEQ
T_LAT = DEC_BATCH * DEC_SEQ
T_ALL = T_CTX + T_LAT
TM = 256
N_TILES = T_ALL // TM
N_CTX_TILES = T_CTX // TM
LAT_TILES_PER_SEQ = DEC_SEQ // TM
LANE = 128
SLOT = 128

MOE_SUB = 256
MOE_R = 3584
MOE_TF = 512
MOE_NF = D_FFE // MOE_TF
MOE_TC = 512
MOE_GB = 128
MOE_GDEPTH = 7
MOE_GSLOTS = MOE_GDEPTH + 1
MOE_ROWS = 2 * T_ALL + N_EXP * MOE_SUB
MOE_NS = MOE_ROWS // MOE_R + N_EXP

FFN_TM = 512
FFN_CH = 1408

VMEM_BIG = 56 * 1024 * 1024


def _dot(a, b):
    return jnp.dot(a, b, preferred_element_type=F32)


def _dot_nt(a, b):
    return lax.dot_general(a, b, (((1,), (1,)), ((), ())), preferred_element_type=F32)


def _split(a):
    hi = a.astype(BF16)
    lo = (a - hi.astype(F32)).astype(BF16)
    return hi, lo


def _dot_x3(a, b):
    ah, al = _split(a)
    bh, bl = _split(b)
    return _dot(ah, bh) + (_dot(ah, bl) + _dot(al, bh))


def _sigmoid(x):
    return 1.0 / (1.0 + jnp.exp(-x))


def _rms(x, n):
    return x * lax.rsqrt(jnp.sum(x * x, axis=-1, keepdims=True) * (1.0 / n) + EPS)


def _modulate(x, g, shift, scale):
    return _rms(x, D) * g * (1.0 + scale) + shift


def _lo_mask(rows):
    return lax.broadcasted_iota(jnp.int32, (rows, LANE), 1) < 64


def _ada_kernel(cond_ref, w_ref, b_ref, o_ref):
    c = cond_ref[...]
    s = c * _sigmoid(c)
    o_ref[...] = _dot_x3(s, w_ref[...]) + b_ref[...]


def _ada_call(cond8, w_ada, b_ada):
    tn = 1024
    return pl.pallas_call(
        _ada_kernel,
        out_shape=jax.ShapeDtypeStruct((DEPTH, 8, N_ADA * D), F32),
        grid=(DEPTH, N_ADA * D // tn),
        in_specs=[
            pl.BlockSpec((8, D), lambda l, j: (0, 0)),
            pl.BlockSpec((None, D, tn), lambda l, j: (l, 0, j)),
            pl.BlockSpec((None, 1, tn), lambda l, j: (l, 0, j)),
        ],
        out_specs=pl.BlockSpec((None, 8, tn), lambda l, j: (l, 0, j)),
        compiler_params=pltpu.CompilerParams(
            dimension_semantics=("arbitrary", "arbitrary"), vmem_limit_bytes=VMEM_BIG),
        name="ada",
    )(cond8, w_ada, b_ada.reshape(DEPTH, 1, N_ADA * D))


def _rope(x, cos, sup, sdn, shift):
    up = pltpu.roll(x, LANE - shift, 1)
    dn = pltpu.roll(x, shift, 1)
    return x * cos + up * sup + dn * sdn


def _norm64_block(blk, lo):
    sq = blk * blk
    s_lo = jnp.sum(jnp.where(lo, sq, 0.0), axis=-1, keepdims=True)
    s_hi = jnp.sum(jnp.where(lo, 0.0, sq), axis=-1, keepdims=True)
    r = jnp.where(lo, lax.rsqrt(s_lo * (1.0 / A_DH) + EPS), lax.rsqrt(s_hi * (1.0 / A_DH) + EPS))
    return blk * r


def _pair_variants(x, lo):
    r = pltpu.roll(x, 64, 1)
    return (jnp.where(lo, x, 0.0), jnp.where(lo, 0.0, r), jnp.where(lo, r, 0.0), jnp.where(lo, 0.0, x))


def _mla_kv(ckv_n, kr, wkv_ref, place_ref, gk_ref, lo, kb_ref, vlo_ref, vhi_ref, rope_tabs):
    kv = _dot(ckv_n.astype(BF16), wkv_ref[...])
    kr_hi, kr_lo = _split(kr)
    place = place_ref[...]
    krs = _dot(kr_hi, place) + _dot(kr_lo, place)
    for h in range(B_HEADS):
        sl = slice(h * SLOT, (h + 1) * SLOT)
        blk = kv[:, sl] + krs[:, sl]
        blk = _rms(blk, B_QK) * gk_ref[:, sl]
        if rope_tabs is not None:
            cos, sup, sdn = rope_tabs
            blk = _rope(blk, cos[:, sl], sup[:, sl], sdn[:, sl], 8)
        kb_ref[:, sl] = blk.astype(BF16)
    for j in range(B_HEADS // 2):
        sl = slice(j * LANE, (j + 1) * LANE)
        v = kv[:, B_HEADS * SLOT + j * LANE: B_HEADS * SLOT + (j + 1) * LANE]
        vlo_ref[:, sl] = jnp.where(lo, v, 0.0).astype(BF16)
        vhi_ref[:, sl] = jnp.where(lo, 0.0, v).astype(BF16)


def _pre_attn_kernel(rope, *refs):
    if rope:
        (x_ref, ada_ref, g_ref, win_ref, wqb_ref, wkv_ref, place_ref,
         gqa_ref, gka_ref, gql_ref, gkl_ref, gqb_ref, gkb_ref,
         ca_ref, sua_ref, sda_ref, cb_ref, sub_ref, sdb_ref,
         qa_o, ka4_o, va4_o, qb_o, kb_o, vlo_o, vhi_o) = refs
    else:
        (x_ref, ada_ref, g_ref, win_ref, wqb_ref, wkv_ref, place_ref,
         gqa_ref, gka_ref, gql_ref, gkl_ref, gqb_ref, gkb_ref,
         qa_o, ka4_o, va4_o, qb_o, kb_o, vlo_o, vhi_o,
         kaf_o, vaf_o, ckvf_o, krf_o) = refs
    lo = _lo_mask(TM)
    h = _modulate(x_ref[...], g_ref[...], ada_ref[0:1, :], ada_ref[1:2, :])
    p = _dot(h.astype(BF16), win_ref[...])

    for j in range(4):
        sl = slice(j * LANE, (j + 1) * LANE)
        blk = _norm64_block(p[:, sl], lo) * gqa_ref[:, sl]
        if rope:
            blk = _rope(blk, ca_ref[:, sl], sua_ref[:, sl], sda_ref[:, sl], 16)
        qa_o[:, sl] = (blk * (A_DH ** -0.5)).astype(BF16)
    ka = _norm64_block(p[:, 512:640], lo) * gka_ref[...]
    va = p[:, 640:768]
    if not rope:
        kaf_o[...] = ka
        vaf_o[...] = va
    else:
        ka = _rope(ka, ca_ref[:, 0:LANE], sua_ref[:, 0:LANE], sda_ref[:, 0:LANE], 16)
    for n, (kk, vv) in enumerate(zip(_pair_variants(ka, lo), _pair_variants(va, lo))):
        sl = slice(n * LANE, (n + 1) * LANE)
        ka4_o[:, sl] = kk.astype(BF16)
        va4_o[:, sl] = vv.astype(BF16)

    cq = _rms(p[:, 768:1024], B_Q_LORA) * gql_ref[...]
    qb = _dot(cq.astype(BF16), wqb_ref[...])
    for hh in range(B_HEADS):
        sl = slice(hh * SLOT, (hh + 1) * SLOT)
        blk = _rms(qb[:, sl], B_QK) * gqb_ref[:, sl]
        if rope:
            blk = _rope(blk, cb_ref[:, sl], sub_ref[:, sl], sdb_ref[:, sl], 8)
        qb_o[:, sl] = (blk * (B_QK ** -0.5)).astype(BF16)

    ckv = _rms(p[:, 1024:1152], B_KV_LORA) * gkl_ref[...]
    kr = p[:, 1152:1280]
    if not rope:
        ckvf_o[...] = ckv
        krf_o[...] = kr
    tabs = (cb_ref, sub_ref, sdb_ref) if rope else None
    _mla_kv(ckv, kr, wkv_ref, place_ref, gkb_ref, lo, kb_o, vlo_o, vhi_o, tabs)


def _const_spec(shape):
    nd = len(shape)
    return pl.BlockSpec(shape, lambda *a: (0,) * nd)


def _pre_attn_call(rope, x, off, ada, layer, g, lw, tabs):
    n_tiles = (T_LAT if rope else T_CTX) // TM
    n_rows = n_tiles * TM
    if rope:
        ada_map = lambda i: (layer, 1 + i // LAT_TILES_PER_SEQ, 0, 0)
    else:
        ada_map = lambda i: (layer, 0, 0, 0)
    in_specs = [
        pl.BlockSpec((TM, D), lambda i: (i + off, 0)),
        pl.BlockSpec((None, None, N_ADA, D), ada_map),
        _const_spec((1, D)),
        _const_spec((D, IN_PAD)),
        _const_spec((B_Q_LORA, B_HEADS * SLOT)),
        _const_spec((B_KV_LORA, B_HEADS * SLOT + B_HEADS * B_V)),
        _const_spec((LANE, B_HEADS * SLOT)),
        _const_spec((1, 512)), _const_spec((1, LANE)), _const_spec((1, B_Q_LORA)),
        _const_spec((1, B_KV_LORA)), _const_spec((1, B_HEADS * SLOT)), _const_spec((1, B_HEADS * SLOT)),
    ]
    args = [x, ada, g, lw["w_in"], lw["w_qb"], lw["w_kv"], lw["place"],
            lw["gqa"], lw["gka"], lw["gql"], lw["gkl"], lw["gqb"], lw["gkb"]]
    if rope:
        tmap = lambda i: (i % LAT_TILES_PER_SEQ, 0)
        in_specs += [pl.BlockSpec((TM, 512), tmap)] * 3 + [pl.BlockSpec((TM, B_HEADS * SLOT), tmap)] * 3
        args += list(tabs)
    row = lambda w: pl.BlockSpec((TM, w), lambda i: (i, 0))
    out_shape = [jax.ShapeDtypeStruct((n_rows, w), BF16) for w in (512, 512, 512, 1024, 1024, 512, 512)]
    out_specs = [row(w) for w in (512, 512, 512, 1024, 1024, 512, 512)]
    if not rope:
        out_shape += [jax.ShapeDtypeStruct((n_rows, LANE), F32)] * 4
        out_specs += [row(LANE)] * 4
    return pl.pallas_call(
        functools.partial(_pre_attn_kernel, rope),
        out_shape=out_shape, grid=(n_tiles,), in_specs=in_specs, out_specs=out_specs,
        compiler_params=pltpu.CompilerParams(
            dimension_semantics=("arbitrary",), vmem_limit_bytes=VMEM_BIG),
        name="pre_attn_lat" if rope else "pre_attn_ctx",
    )(*args)


def _cache_kv_kernel(ckv_ref, kr_ref, wkv_ref, place_ref, gkb_ref, kb_o, vlo_o, vhi_o):
    lo = _lo_mask(PAST)
    _mla_kv(ckv_ref[...], kr_ref[...], wkv_ref, place_ref, gkb_ref, lo, kb_o, vlo_o, vhi_o, None)


def _cache_kv_call(ckv, kr, lw):
    rows = DEC_BATCH * PAST
    row = lambda w: pl.BlockSpec((PAST, w), lambda i: (i, 0))
    return pl.pallas_call(
        _cache_kv_kernel,
        out_shape=[jax.ShapeDtypeStruct((rows, 1024), BF16), jax.ShapeDtypeStruct((rows, 512), BF16),
                   jax.ShapeDtypeStruct((rows, 512), BF16)],
        grid=(DEC_BATCH,),
        in_specs=[row(LANE), row(LANE),
                  _const_spec((B_KV_LORA, B_HEADS * SLOT + B_HEADS * B_V)),
                  _const_spec((LANE, B_HEADS * SLOT)), _const_spec((1, B_HEADS * SLOT))],
        out_specs=[row(1024), row(512), row(512)],
        compiler_params=pltpu.CompilerParams(dimension_semantics=("arbitrary",)),
        name="cache_kv",
    )(ckv, kr, lw["w_kv"], lw["place"], lw["gkb"])


def _softmax_pv(scores, values, sink):
    m = jnp.max(scores[0], axis=-1, keepdims=True)
    for s in scores[1:]:
        m = jnp.maximum(m, jnp.max(s, axis=-1, keepdims=True))
    if sink is not None:
        m = jnp.maximum(m, sink)
    ps = [jnp.exp(s - m) for s in scores]
    den = jnp.sum(ps[0], axis=-1, keepdims=True)
    for p in ps[1:]:
        den = den + jnp.sum(p, axis=-1, keepdims=True)
    if sink is not None:
        den = den + jnp.exp(sink - m)
    out = _dot(ps[0].astype(BF16), values[0])
    for p, v in zip(ps[1:], values[1:]):
        out = out + _dot(p.astype(BF16), v)
    return out * (1.0 / den)


def _attn_ctx_kernel(sink_ref, qa_ref, ka4_ref, va4_ref, qb_ref, kb_ref, vlo_ref, vhi_ref, o_ref):
    for j in range(4):
        q = qa_ref[:, j * LANE:(j + 1) * LANE]
        acc = None
        for par in range(2):
            c = (2 * (j // 2) + par) * LANE
            s = _dot_nt(q, ka4_ref[:, c:c + LANE])
            o = _softmax_pv([s], [va4_ref[:, c:c + LANE]], sink_ref[2 * j + par])
            acc = o if acc is None else acc + o
        o_ref[:, j * LANE:(j + 1) * LANE] = acc.astype(BF16)
    for j in range(4):
        acc = None
        for par, v_ref in enumerate((vlo_ref, vhi_ref)):
            hh = 2 * j + par
            s = _dot_nt(qb_ref[:, hh * SLOT:(hh + 1) * SLOT], kb_ref[:, hh * SLOT:(hh + 1) * SLOT])
            o = _softmax_pv([s], [v_ref[:, j * LANE:(j + 1) * LANE]], None)
            acc = o if acc is None else acc + o
        o_ref[:, 512 + j * LANE:512 + (j + 1) * LANE] = acc.astype(BF16)


def _attn_ctx_call(sink, qa, ka4, va4, qb, kb, vlo, vhi):
    row = lambda w: pl.BlockSpec((SEQ, w), lambda b: (b, 0))
    return pl.pallas_call(
        _attn_ctx_kernel,
        out_shape=jax.ShapeDtypeStruct((T_CTX, D), BF16),
        grid=(BATCH,),
        in_specs=[pl.BlockSpec(memory_space=pltpu.SMEM),
                  row(512), row(512), row(512), row(1024), row(1024), row(512), row(512)],
        out_specs=row(D),
        compiler_params=pltpu.CompilerParams(dimension_semantics=("arbitrary",)),
        name="attn_ctx",
    )(sink, qa, ka4, va4, qb, kb, vlo, vhi)


QB = 128
N_QB = DEC_SEQ // QB
KPAD = DEC_SEQ + 2 * QB
KALL = PAST + DEC_SEQ


def _attn_lat_kernel(sink_ref, qa_ref, ka4p_ref, va4p_ref, ka4c_ref, va4c_ref,
                     qb_ref, kb_ref, vlo_ref, vhi_ref, o_ref):
    n = pl.program_id(1)
    r = lax.broadcasted_iota(jnp.int32, (QB, 3 * QB), 0)
    c = lax.broadcasted_iota(jnp.int32, (QB, 3 * QB), 1)
    rel = c - QB - r
    kpos = (n - 1) * QB + c
    valid = (jnp.abs(rel) <= WINDOW) & (kpos >= 0) & (kpos < DEC_SEQ)
    k0 = pl.multiple_of(n * QB, QB)
    for j in range(4):
        q = qa_ref[:, j * LANE:(j + 1) * LANE]
        acc = None
        for par in range(2):
            cc = (2 * (j // 2) + par) * LANE
            s_loc = _dot_nt(q, ka4p_ref[pl.ds(k0, 3 * QB), cc:cc + LANE])
            s_loc = jnp.where(valid, s_loc, -jnp.inf)
            s_ctx = _dot_nt(q, ka4c_ref[:, cc:cc + LANE])
            o = _softmax_pv([s_ctx, s_loc],
                            [va4c_ref[:, cc:cc + LANE], va4p_ref[pl.ds(k0, 3 * QB), cc:cc + LANE]],
                            sink_ref[2 * j + par])
            acc = o if acc is None else acc + o
        o_ref[:, j * LANE:(j + 1) * LANE] = acc.astype(BF16)
    for j in range(4):
        acc = None
        for par, v_ref in enumerate((vlo_ref, vhi_ref)):
            hh = 2 * j + par
            s = _dot_nt(qb_ref[:, hh * SLOT:(hh + 1) * SLOT], kb_ref[:, hh * SLOT:(hh + 1) * SLOT])
            o = _softmax_pv([s], [v_ref[:, j * LANE:(j + 1) * LANE]], None)
            acc = o if acc is None else acc + o
        o_ref[:, 512 + j * LANE:512 + (j + 1) * LANE] = acc.astype(BF16)


def _attn_lat_call(sink, qa, ka4p, va4p, ka4c, va4c, qb, kball, vlo_all, vhi_all):
    qrow = lambda w: pl.BlockSpec((QB, w), lambda b, n: (b * N_QB + n, 0))
    per_b = lambda rows, w: pl.BlockSpec((None, rows, w), lambda b, n: (b, 0, 0))
    return pl.pallas_call(
        _attn_lat_kernel,
        out_shape=jax.ShapeDtypeStruct((T_LAT, D), BF16),
        grid=(DEC_BATCH, N_QB),
        in_specs=[pl.BlockSpec(memory_space=pltpu.SMEM),
                  qrow(512), per_b(KPAD, 512), per_b(KPAD, 512), per_b(PAST, 512), per_b(PAST, 512),
                  qrow(1024), per_b(KALL, 1024), per_b(KALL, 512), per_b(KALL, 512)],
        out_specs=qrow(D),
        compiler_params=pltpu.CompilerParams(dimension_semantics=("arbitrary", "arbitrary")),
        name="attn_lat",
    )(sink, qa, ka4p, va4p, ka4c, va4c, qb, kball, vlo_all, vhi_all)


def _post_attn_kernel(xc_ref, xl_ref, oc_ref, ol_ref, ada_ref, g_ref, wo_ref, wg_ref, wu_ref, wd_ref, y_ref):
    is_ctx = pl.program_id(0) < T_CTX // FFN_TM
    x = jnp.where(is_ctx, xc_ref[...], xl_ref[...])
    o = jnp.where(is_ctx, oc_ref[...], ol_ref[...])
    mix = _dot(o, wo_ref[...])
    x1 = x + ada_ref[2:3, :] * mix
    h = _modulate(x1, g_ref[...], ada_ref[3:4, :], ada_ref[4:5, :]).astype(BF16)
    acc = None
    for c0 in range(0, D_FF, FFN_CH):
        gg = _dot(h, wg_ref[:, c0:c0 + FFN_CH])
        uu = _dot(h, wu_ref[:, c0:c0 + FFN_CH])
        a = (gg * _sigmoid(gg) * uu).astype(BF16)
        part = _dot(a, wd_ref[c0:c0 + FFN_CH, :])
        acc = part if acc is None else acc + part
    y_ref[...] = x1 + ada_ref[5:6, :] * acc


def _tile_cond(i, tiles_per_ctx, tiles_per_lat_seq):
    return jnp.where(i < tiles_per_ctx, 0, 1 + (i - tiles_per_ctx) // tiles_per_lat_seq)


def _post_attn_call(x_ctx, x_lat, x_lat_off, o_ctx, o_lat, ada, layer, g, wo, wg, wu, wd):
    n_tiles = T_ALL // FFN_TM
    n_ctx = T_CTX // FFN_TM
    ada_map = lambda i: (layer, _tile_cond(i, n_ctx, DEC_SEQ // FFN_TM), 0, 0)
    ctx_map = lambda i: (jnp.minimum(i, n_ctx - 1), 0)
    one = pl.Buffered(1)
    return pl.pallas_call(
        _post_attn_kernel,
        out_shape=jax.ShapeDtypeStruct((T_ALL, D), F32),
        grid=(n_tiles,),
        in_specs=[pl.BlockSpec((FFN_TM, D), ctx_map),
                  pl.BlockSpec((FFN_TM, D), lambda i: (jnp.maximum(i - n_ctx, 0) + x_lat_off, 0)),
                  pl.BlockSpec((FFN_TM, D), ctx_map),
                  pl.BlockSpec((FFN_TM, D), lambda i: (jnp.maximum(i - n_ctx, 0), 0)),
                  pl.BlockSpec((None, None, N_ADA, D), ada_map),
                  _const_spec((1, D)),
                  pl.BlockSpec((D, D), lambda i: (0, 0), pipeline_mode=one),
                  pl.BlockSpec((D, D_FF), lambda i: (0, 0), pipeline_mode=one),
                  pl.BlockSpec((D, D_FF), lambda i: (0, 0), pipeline_mode=one),
                  pl.BlockSpec((D_FF, D), lambda i: (0, 0), pipeline_mode=one)],
        out_specs=pl.BlockSpec((FFN_TM, D), lambda i: (i, 0)),
        compiler_params=pltpu.CompilerParams(
            dimension_semantics=("arbitrary",), vmem_limit_bytes=VMEM_BIG),
        name="post_attn_ffn",
    )(x_ctx, x_lat, o_ctx, o_lat, ada, g, wo, wg, wu, wd)


HALO = 16
EXT = TM + 2 * HALO
CONV_RC = 64
CONV_LC = 128
CONV_WIN = CONV_RC + 24
CONV_LN_ROWS = 32


def _conv_router_kernel(x_ref, xp_ref, xn_ref, ada_ref, gm_ref, gf_ref, w1_ref, b1_ref, wdw_ref, bdw_ref,
                        gln_ref, bln_ref, w2_ref, b2_ref, wr_ref,
                        x1_o, h2_o, ri_o, rw_o, cnt_o, p_ref, c_ref, v_ref, run_ref):
    i = pl.program_id(0)

    @pl.when(i == 0)
    def _():
        run_ref[...] = jnp.zeros_like(run_ref)

    is_lat = i >= N_CTX_TILES
    j = (i - N_CTX_TILES) % LAT_TILES_PER_SEQ
    lflag = jnp.where(is_lat & (j != 0), 1.0, 0.0)
    rflag = jnp.where(is_lat & (j != LAT_TILES_PER_SEQ - 1), 1.0, 0.0)

    x = x_ref[...]
    xx = jnp.concatenate([xp_ref[...], x, xn_ref[...]], axis=0)
    h = _modulate(xx, gm_ref[...], ada_ref[0:1, :], ada_ref[1:2, :])
    a = _dot(h.astype(BF16), w1_ref[...]) + b1_ref[...]
    u = a[:, :D] * _sigmoid(a[:, D:])
    row = lax.broadcasted_iota(jnp.int32, (EXT, 1), 0)
    u = u * jnp.where(row < HALO, lflag, jnp.where(row >= HALO + TM, rflag, 1.0))

    p_ref[...] = u
    for rc in range(TM // CONV_RC):
        for lc in range(D // CONV_LC):
            ls = slice(lc * CONV_LC, (lc + 1) * CONV_LC)
            acc = jnp.zeros((CONV_RC, CONV_LC), F32) + bdw_ref[:, ls]
            base = p_ref[rc * CONV_RC:rc * CONV_RC + CONV_WIN + 8, ls]
            for r in range(8):
                win = base if r == 0 else pltpu.roll(base, CONV_WIN + 8 - r, 0)
                for q in range(4):
                    o = 8 * q + r
                    if 1 <= o <= CONV_W:
                        acc = acc + wdw_ref[o - 1:o, ls] * win[8 * q:8 * q + CONV_RC]
            c_ref[rc * CONV_RC:(rc + 1) * CONV_RC, ls] = acc

    for c0 in range(0, TM, CONV_LN_ROWS):
        acc = c_ref[c0:c0 + CONV_LN_ROWS, :]
        mu = jnp.mean(acc, axis=-1, keepdims=True)
        xc = acc - mu
        y = xc * lax.rsqrt(jnp.mean(xc * xc, axis=-1, keepdims=True) + EPS)
        y = y * gln_ref[...] + bln_ref[...]
        v_ref[c0:c0 + CONV_LN_ROWS, :] = (y * _sigmoid(y)).astype(BF16)

    mix = _dot(v_ref[...], w2_ref[...]) + b2_ref[...]
    x1 = x + ada_ref[2:3, :] * mix
    x1_o[...] = x1
    h2 = _modulate(x1, gf_ref[...], ada_ref[3:4, :], ada_ref[4:5, :])
    h2_o[...] = h2.astype(BF16)

    lane = lax.broadcasted_iota(jnp.int32, (TM, LANE), 1)
    lane_f = lane.astype(F32)
    logits = jnp.where(lane < N_EXP, _dot_x3(h2, wr_ref[...]), -jnp.inf)
    m1 = jnp.max(logits, axis=-1, keepdims=True)
    i1 = jnp.min(jnp.where(logits == m1, lane_f, float(LANE)), axis=-1, keepdims=True)
    rest = jnp.where(lane_f == i1, -jnp.inf, logits)
    m2 = jnp.max(rest, axis=-1, keepdims=True)
    i2 = jnp.min(jnp.where(rest == m2, lane_f, float(LANE)), axis=-1, keepdims=True)
    e = jnp.exp(m2 - m1)
    w1 = 1.0 / (1.0 + e)
    w2 = e / (1.0 + e)

    oh1 = jnp.where(lane_f == i1, 1.0, 0.0)
    oh2 = jnp.where(lane_f == i2, 1.0, 0.0)
    tr = lax.broadcasted_iota(jnp.int32, (TM, TM), 0)
    tc = lax.broadcasted_iota(jnp.int32, (TM, TM), 1)
    tri = jnp.where(tr > tc, 1.0, 0.0).astype(BF16)
    run = run_ref[0:1, :]
    cnt1 = jnp.sum(oh1, axis=0, keepdims=True)
    cnt2 = jnp.sum(oh2, axis=0, keepdims=True)
    pre1 = _dot(tri, oh1.astype(BF16)) + run
    pre2 = _dot(tri, oh2.astype(BF16)) + (run + cnt1)
    rank1 = jnp.sum(oh1 * pre1, axis=-1, keepdims=True)
    rank2 = jnp.sum(oh2 * pre2, axis=-1, keepdims=True)
    new_run = run + cnt1 + cnt2
    run_ref[...] = jnp.broadcast_to(new_run, run_ref.shape)
    srow = lax.broadcasted_iota(jnp.int32, cnt_o.shape, 0)
    cnt_o[...] = jnp.where(srow == 0, run, jnp.where(srow == 1, cnt1 + cnt2, 0.0))

    ri = jnp.where(lane == 0, i1, jnp.where(lane == 1, i2, jnp.where(lane == 2, rank1, jnp.where(lane == 3, rank2, 0.0))))
    ri_o[...] = ri.astype(jnp.int32)

    rw_o[...] = jnp.where(lane == 0, w1, jnp.where(lane == 1, w2, 0.0))


def _conv_router_call(x, ada, layer, gm, gf, cw):
    ada_map = lambda i: (layer, _tile_cond(i, N_CTX_TILES, LAT_TILES_PER_SEQ), 0, 0)
    hb = TM // HALO
    row = lambda w: pl.BlockSpec((TM, w), lambda i: (i, 0))
    return pl.pallas_call(
        _conv_router_kernel,
        out_shape=[jax.ShapeDtypeStruct((T_ALL, D), F32), jax.ShapeDtypeStruct((T_ALL, D), BF16),
                   jax.ShapeDtypeStruct((T_ALL, LANE), jnp.int32), jax.ShapeDtypeStruct((T_ALL, LANE), F32),
                   jax.ShapeDtypeStruct((N_TILES, 8, LANE), F32)],
        grid=(N_TILES,),
        in_specs=[row(D),
                  pl.BlockSpec((HALO, D), lambda i: (jnp.maximum(i * hb - 1, 0), 0)),
                  pl.BlockSpec((HALO, D), lambda i: (jnp.minimum((i + 1) * hb, T_ALL // HALO - 1), 0)),
                  pl.BlockSpec((None, None, N_ADA, D), ada_map),
                  _const_spec((1, D)), _const_spec((1, D)),
                  _const_spec((D, 2 * D)), _const_spec((1, 2 * D)),
                  _const_spec((CONV_W, D)), _const_spec((1, D)),
                  _const_spec((1, D)), _const_spec((1, D)),
                  _const_spec((D, D)), _const_spec((1, D)),
                  _const_spec((D, LANE))],
        out_specs=[row(D), row(D), row(LANE), row(LANE), pl.BlockSpec((None, 8, LANE), lambda i: (i, 0, 0))],
        scratch_shapes=[pltpu.VMEM((EXT, D), F32), pltpu.VMEM((TM, D), F32), pltpu.VMEM((TM, D), BF16),
                        pltpu.VMEM((8, LANE), F32)],
        compiler_params=pltpu.CompilerParams(
            dimension_semantics=("arbitrary",), vmem_limit_bytes=VMEM_BIG),
        name="conv_router",
    )(x, x, x, ada, gm, gf, cw["w1"], cw["b1"], cw["wdw"], cw["bdw"], cw["gln"], cw["bln"],
      cw["w2"], cw["b2"], cw["wr"])


def _moe_kernel(se_ref, row0_ref, nsub_ref, ta_ref, tn_ref, h_ref, post_ref, rwt_ref, wg_ref, wu_ref, wd_ref,
                ys_in_ref, ys_ref, hb_ref, acc_ref, gs_ref, hstage, wgb, wub, wdb, sem_in, sem_out):
    del ys_in_ref
    j = pl.program_id(0)
    f = pl.program_id(1)
    nsub = nsub_ref[j]
    row0 = row0_ref[j]
    e = se_ref[j]

    def tile_load(i, slot):
        r = pl.multiple_of(i * TM, TM)
        return pltpu.make_async_copy(h_ref.at[pl.ds(r, TM), :], hstage.at[slot], sem_in.at[slot])

    def store(s):
        r = pl.multiple_of(s * MOE_SUB, MOE_SUB)
        dst = ys_ref.at[pl.ds(pl.multiple_of(row0 + s * MOE_SUB, MOE_SUB), MOE_SUB), :]
        return pltpu.make_async_copy(hb_ref.at[pl.ds(r, MOE_SUB), :], dst, sem_out)

    def gather_tile(i, carry):
        slot = i % MOE_GSLOTS
        tile_load(i, slot).wait()

        @pl.when(i + MOE_GDEPTH < N_TILES)
        def _():
            tile_load(i + MOE_GDEPTH, (i + MOE_GDEPTH) % MOE_GSLOTS).start()

        a = ta_ref[i * N_EXP + e] - row0
        n = tn_ref[i * N_EXP + e]
        last = a + n - 1

        @pl.when((n > 0) & (last >= 0) & (a < nsub * MOE_SUB))
        def _():
            start = (jnp.maximum(a, 0) // 8) * 8
            end = jnp.minimum(last, nsub * MOE_SUB - 1)
            pp = post_ref[pl.ds(pl.multiple_of(i * 8, 8), 8), :]
            riota = lax.broadcasted_iota(jnp.int32, (MOE_GB, TM), 0)
            ww = rwt_ref[pl.ds(pl.multiple_of(i * 8, 8), 8), :]

            def window(w):
                rr = pl.multiple_of(start + w * MOE_GB, 8)
                base = row0 + rr
                hit0 = (pp[0:1, :] - base) == riota
                hit1 = (pp[1:2, :] - base) == riota
                acc_ref[pl.ds(rr, MOE_GB), :] += _dot(jnp.where(hit0 | hit1, 1.0, 0.0).astype(BF16),
                                                      hstage[slot])
                gate = jnp.sum(jnp.where(hit0, ww[0:1, :], 0.0) + jnp.where(hit1, ww[1:2, :], 0.0),
                               axis=-1, keepdims=True)
                gs_ref[pl.ds(rr, MOE_GB), :] += jnp.broadcast_to(gate, (MOE_GB, LANE))

            window(0)
            for extra in range(1, (TM + 8) // MOE_GB + 1):
                @pl.when(start + extra * MOE_GB <= end)
                def _():
                    window(extra)

        return carry

    def part(r, rows):
        hh = hb_ref[pl.ds(r, rows), :]
        out = None
        for c0 in range(0, MOE_TF, MOE_TC):
            gg = _dot(hh, wgb[:, c0:c0 + MOE_TC])
            uu = _dot(hh, wub[:, c0:c0 + MOE_TC])
            a = (gg * _sigmoid(gg) * uu).astype(BF16)
            piece = _dot(a, wdb[c0:c0 + MOE_TC, :])
            out = piece if out is None else out + piece
        return out

    def for_each_subtile(fn):
        def pair(p, carry):
            fn(2 * p, 2 * MOE_SUB)
            return carry

        lax.fori_loop(0, nsub // 2, pair, 0)

        @pl.when(nsub % 2 == 1)
        def _():
            fn(nsub - 1, MOE_SUB)

    @pl.when(nsub > 0)
    def _():
        wgb[...] = wg_ref[...].astype(BF16)
        wub[...] = wu_ref[...].astype(BF16)
        wdb[...] = wd_ref[...].astype(BF16)

        @pl.when(f == 0)
        def _():
            for d in range(MOE_GDEPTH):
                tile_load(d, d).start()

            def clear(s, carry):
                r = pl.multiple_of(s * MOE_SUB, MOE_SUB)
                acc_ref[pl.ds(r, MOE_SUB), :] = jnp.zeros((MOE_SUB, D), F32)
                gs_ref[pl.ds(r, MOE_SUB), :] = jnp.zeros((MOE_SUB, LANE), F32)
                return carry

            lax.fori_loop(0, nsub + 1, clear, 0)
            lax.fori_loop(0, N_TILES, gather_tile, 0)

            def first(s, rows):
                r = pl.multiple_of(s * MOE_SUB, MOE_SUB)
                hb_ref[pl.ds(r, rows), :] = acc_ref[pl.ds(r, rows), :].astype(BF16)
                acc_ref[pl.ds(r, rows), :] = part(r, rows)

            for_each_subtile(first)

        @pl.when((f > 0) & (f < MOE_NF - 1))
        def _():
            def middle(s, rows):
                r = pl.multiple_of(s * MOE_SUB, MOE_SUB)
                acc_ref[pl.ds(r, rows), :] += part(r, rows)

            for_each_subtile(middle)

        @pl.when(f == MOE_NF - 1)
        def _():
            def final(s, rows):
                r = pl.multiple_of(s * MOE_SUB, MOE_SUB)
                y = acc_ref[pl.ds(r, rows), :] + part(r, rows)
                hb_ref[pl.ds(r, rows), :] = (y * gs_ref[pl.ds(r, rows), 0:1]).astype(BF16)
                for k in range(rows // MOE_SUB):
                    store(s + k).start()

            for_each_subtile(final)

            def drain(s, carry):
                store(s).wait()
                return carry

            lax.fori_loop(0, nsub, drain, 0)


def _moe_call(st_e, st_row0, st_nsub, t_a, t_n, h2b, pos_t, rw_t, wg, wu, wd, ys0, lidx):
    f_eff = lambda j, f, ns: jnp.where(ns[j] > 0, f, MOE_NF - 1)
    wmap_in = lambda j, f, se, r0, ns, ta, tn: (lidx, se[j], 0, f_eff(j, f, ns))
    wmap_out = lambda j, f, se, r0, ns, ta, tn: (lidx, se[j], f_eff(j, f, ns), 0)
    return pl.pallas_call(
        _moe_kernel,
        out_shape=jax.ShapeDtypeStruct((MOE_ROWS, D), BF16),
        grid_spec=pltpu.PrefetchScalarGridSpec(
            num_scalar_prefetch=5, grid=(MOE_NS, MOE_NF),
            in_specs=[
                pl.BlockSpec(memory_space=pl.ANY),
                pl.BlockSpec((N_TILES * 8, TM), lambda j, f, se, r0, ns, ta, tn: (0, 0)),
                pl.BlockSpec((N_TILES * 8, TM), lambda j, f, se, r0, ns, ta, tn: (0, 0)),
                pl.BlockSpec((None, None, D, MOE_TF), wmap_in),
                pl.BlockSpec((None, None, D, MOE_TF), wmap_in),
                pl.BlockSpec((None, None, MOE_TF, D), wmap_out),
                pl.BlockSpec(memory_space=pl.ANY),
            ],
            out_specs=pl.BlockSpec(memory_space=pl.ANY),
            scratch_shapes=[pltpu.VMEM((MOE_R, D), BF16), pltpu.VMEM((MOE_R + MOE_SUB, D), F32),
                            pltpu.VMEM((MOE_R + MOE_SUB, LANE), F32),
                            pltpu.VMEM((MOE_GSLOTS, TM, D), BF16),
                            pltpu.VMEM((D, MOE_TF), BF16), pltpu.VMEM((D, MOE_TF), BF16),
                            pltpu.VMEM((MOE_TF, D), BF16),
                            pltpu.SemaphoreType.DMA((MOE_GSLOTS,)), pltpu.SemaphoreType.DMA(())]),
        input_output_aliases={11: 0},
        compiler_params=pltpu.CompilerParams(
            dimension_semantics=("arbitrary", "arbitrary"), vmem_limit_bytes=VMEM_BIG,
            has_side_effects=True),
        name="moe_experts",
    )(st_e, st_row0, st_nsub, t_a, t_n, h2b, pos_t, rw_t, wg, wu, wd, ys0)


def _combine_kernel(split, ta_ref, tn_ref, x_ref, ada_ref, posc_ref, ys_ref, *rest):
    if split:
        oc_ref, ol_ref, ybuf, oacc, bc, sem = rest
    else:
        o_ref, ybuf, oacc, bc, sem = rest
    g = pl.program_id(0)

    def blocks(i, e):
        a = ta_ref[i * N_EXP + e]
        n = tn_ref[i * N_EXP + e]
        lo = jnp.minimum(a // MOE_SUB, MOE_ROWS // MOE_SUB - 1)
        return n > 0, lo, (n > 0) & ((a + n - 1) // MOE_SUB > lo)

    def block_copy(slot, e, k, blk):
        src = ys_ref.at[pl.ds(pl.multiple_of(blk * MOE_SUB, MOE_SUB), MOE_SUB), :]
        return pltpu.make_async_copy(src, ybuf.at[slot, 2 * e + k], sem.at[slot, 2 * e + k])

    @pl.when(g < N_TILES)
    def _():
        for e in range(N_EXP):
            _, lo, two = blocks(g, e)
            block_copy(g % 2, e, 0, lo).start()

            @pl.when(two)
            def _():
                block_copy(g % 2, e, 1, lo + 1).start()

    @pl.when(g > 0)
    def _():
        i = g - 1
        slot = i % 2
        for n in range(2):
            bc[n] = jnp.broadcast_to(posc_ref[:, n:n + 1], (TM, MOE_SUB))
        liota = lax.broadcasted_iota(jnp.int32, (TM, MOE_SUB), 1)

        def weighted(has, blk, y):
            base = jnp.where(has, blk * MOE_SUB, -MOE_ROWS)
            hit = (bc[0] - base == liota) | (bc[1] - base == liota)
            return _dot(jnp.where(hit, 1.0, 0.0).astype(BF16), y)

        total = None
        for e in range(N_EXP):
            has, lo, _ = blocks(i, e)
            block_copy(slot, e, 0, lo).wait()
            t = weighted(has, lo, ybuf[slot, 2 * e])
            total = t if total is None else total + t
        oacc[...] = total
        for e in range(N_EXP):
            has, lo, two = blocks(i, e)

            @pl.when(two)
            def _():
                block_copy(slot, e, 1, lo + 1).wait()
                oacc[...] += weighted(has, lo + 1, ybuf[slot, 2 * e + 1])

        out = x_ref[...] + ada_ref[5:6, :] * oacc[...]
        if split:
            @pl.when(i < N_CTX_TILES)
            def _():
                oc_ref[...] = out

            @pl.when(i >= N_CTX_TILES)
            def _():
                ol_ref[...] = out
        else:
            o_ref[...] = out


def _combine_call(split, t_a, t_n, x1, ada, layer, posc, ys):
    prev = lambda g: jnp.maximum(g - 1, 0)
    ada_map = lambda g, a, n: (layer, _tile_cond(prev(g), N_CTX_TILES, LAT_TILES_PER_SEQ), 0, 0)
    if split:
        out_shape = [jax.ShapeDtypeStruct((T_CTX, D), F32), jax.ShapeDtypeStruct((T_LAT, D), F32)]
        out_specs = [pl.BlockSpec((TM, D), lambda g, a, n: (jnp.minimum(prev(g), N_CTX_TILES - 1), 0)),
                     pl.BlockSpec((TM, D), lambda g, a, n: (jnp.maximum(prev(g) - N_CTX_TILES, 0), 0))]
    else:
        out_shape = jax.ShapeDtypeStruct((T_ALL, D), F32)
        out_specs = pl.BlockSpec((TM, D), lambda g, a, n: (prev(g), 0))
    return pl.pallas_call(
        functools.partial(_combine_kernel, split),
        out_shape=out_shape,
        grid_spec=pltpu.PrefetchScalarGridSpec(
            num_scalar_prefetch=2, grid=(N_TILES + 1,),
            in_specs=[pl.BlockSpec((TM, D), lambda g, a, n: (prev(g), 0)),
                      pl.BlockSpec((None, None, N_ADA, D), ada_map),
                      pl.BlockSpec((TM, LANE), lambda g, a, n: (prev(g), 0)),
                      pl.BlockSpec(memory_space=pl.ANY)],
            out_specs=out_specs,
            scratch_shapes=[pltpu.VMEM((2, 2 * N_EXP, MOE_SUB, D), BF16), pltpu.VMEM((TM, D), F32),
                            pltpu.VMEM((2, TM, MOE_SUB), jnp.int32),
                            pltpu.SemaphoreType.DMA((2, 2 * N_EXP))]),
        compiler_params=pltpu.CompilerParams(
            dimension_semantics=("arbitrary",), vmem_limit_bytes=VMEM_BIG),
        name="moe_combine",
    )(t_a, t_n, x1, ada, posc, ys)


def _moe_layer(x1, h2b, ri, rw, runs, ada, layer, wg, wu, wd, lidx, split):
    idx = ri[:, 0:2]
    rank = ri[:, 2:4]
    before = runs[:, 0, :N_EXP].astype(jnp.int32)
    t_n = runs[:, 1, :N_EXP].astype(jnp.int32)
    counts = before[-1] + t_n[-1]
    padded = ((counts + MOE_SUB - 1) // MOE_SUB) * MOE_SUB
    ends = jnp.cumsum(padded)
    starts = ends - padded
    sel = idx[:, :, None] == jnp.arange(N_EXP, dtype=jnp.int32)[None, None, :]
    pos = (jnp.sum(jnp.where(sel, starts[None, None, :], 0), axis=-1) + rank).astype(jnp.int32)
    t_a = (starts[None, :] + before).reshape(-1).astype(jnp.int32)
    t_n = t_n.reshape(-1)
    pos_t = jnp.pad(pos.reshape(N_TILES, TM, 2).transpose(0, 2, 1), ((0, 0), (0, 6), (0, 0)))
    pos_t = pos_t.reshape(N_TILES * 8, TM)
    posc = jnp.pad(pos, ((0, 0), (0, LANE - 2)))
    n_st = (padded + MOE_R - 1) // MOE_R
    st_end = jnp.cumsum(n_st)
    st_begin = st_end - n_st
    j = jnp.arange(MOE_NS, dtype=jnp.int32)
    se = jnp.sum((j[:, None] >= st_end[None, :]).astype(jnp.int32), axis=1)
    used = se < N_EXP
    se_c = jnp.minimum(se, N_EXP - 1)
    k = j - st_begin[se_c]
    row0 = jnp.where(used, starts[se_c] + k * MOE_R, 0).astype(jnp.int32)
    nsub = jnp.where(used, jnp.minimum(MOE_R, padded[se_c] - k * MOE_R) // MOE_SUB, 0).astype(jnp.int32)
    last = jnp.maximum(jnp.sum(used.astype(jnp.int32)) - 1, 0)
    se_eff = jnp.where(used, se_c, se_c[last]).astype(jnp.int32)
    rw_t = jnp.pad(rw[:, 0:2].reshape(N_TILES, TM, 2).transpose(0, 2, 1), ((0, 0), (0, 6), (0, 0)))
    rw_t = rw_t.reshape(N_TILES * 8, TM)
    ys = _moe_call(se_eff, row0, nsub, t_a, t_n, h2b, pos_t, rw_t, wg, wu, wd,
                   jnp.zeros((MOE_ROWS, D), BF16), lidx)
    return _combine_call(split, t_a, t_n, x1, ada, layer, posc, ys)


def _rope_tables():
    t = jnp.arange(DEC_SEQ)
    rows = (t // GRID_W).astype(F32)
    cols = (t % GRID_W).astype(F32)

    def axis_tabs(pos, d):
        inv = ROPE_BASE ** (-jnp.arange(0, d, 2, dtype=F32) / d)
        ang = pos[:, None] * inv[None, :]
        cos, sin = jnp.cos(ang), jnp.sin(ang)
        z = jnp.zeros_like(sin)
        return (jnp.concatenate([cos, cos], -1), jnp.concatenate([-sin, z], -1), jnp.concatenate([z, sin], -1))

    def axial(d):
        r = axis_tabs(rows, d // 2)
        c = axis_tabs(cols, d // 2)
        return [jnp.concatenate([a, b], -1) for a, b in zip(r, c)]

    tabs_a = [jnp.tile(tb, (1, A_HEADS)) for tb in axial(A_DH)]
    cb, sub, sdb = axial(B_ROPE)
    ones = jnp.ones((DEC_SEQ, B_NOPE), F32)
    zer = jnp.zeros((DEC_SEQ, B_NOPE), F32)
    pad = jnp.zeros((DEC_SEQ, SLOT - B_QK), F32)
    slot = lambda first, tb: jnp.tile(jnp.concatenate([first, tb, pad], -1), (1, B_HEADS))
    tabs_b = [slot(ones, cb), slot(zer, sub), slot(zer, sdb)]
    return tabs_a + tabs_b


def _attn_weights(e, w_in_attn, g_qk_a_q, g_qk_a_k, g_mla_q_a, w_mla_q_b, g_mla_kv_a, w_mla_kv_b,
                  g_qk_b_q, g_qk_b_k):
    w_in = jnp.pad(w_in_attn[e], ((0, 0), (0, IN_PAD - IN_DIM))).astype(BF16)
    w_qb = jnp.pad(w_mla_q_b[e].reshape(B_Q_LORA, B_HEADS, B_QK), ((0, 0), (0, 0), (0, SLOT - B_QK)))
    w_qb = w_qb.reshape(B_Q_LORA, B_HEADS * SLOT).astype(BF16)
    wkv = w_mla_kv_b[e].reshape(B_KV_LORA, B_HEADS, B_NOPE + B_V)
    wk = jnp.pad(wkv[:, :, :B_NOPE], ((0, 0), (0, 0), (0, SLOT - B_NOPE))).reshape(B_KV_LORA, B_HEADS * SLOT)
    wv = wkv[:, :, B_NOPE:].reshape(B_KV_LORA, B_HEADS * B_V)
    w_kv = jnp.concatenate([wk, wv], axis=1).astype(BF16)
    place = jnp.zeros((LANE, B_HEADS, SLOT), F32)
    ii = jnp.arange(B_ROPE)
    place = place.at[ii, :, B_NOPE + ii].set(1.0).reshape(LANE, B_HEADS * SLOT).astype(BF16)
    slot_gain = lambda g: jnp.tile(jnp.pad(g, (0, SLOT - B_QK)), B_HEADS).reshape(1, B_HEADS * SLOT)
    return dict(
        w_in=w_in, w_qb=w_qb, w_kv=w_kv, place=place,
        gqa=jnp.tile(g_qk_a_q[e], A_HEADS).reshape(1, 512),
        gka=jnp.tile(g_qk_a_k[e], A_KV).reshape(1, LANE),
        gql=g_mla_q_a[e].reshape(1, B_Q_LORA), gkl=g_mla_kv_a[e].reshape(1, B_KV_LORA),
        gqb=slot_gain(g_qk_b_q[e]), gkb=slot_gain(g_qk_b_k[e]))


def _pair_variants_host(x):
    h0, h1 = x[..., :64], x[..., 64:]
    z = jnp.zeros_like(h0)
    return jnp.concatenate([h0, z, z, h0, h1, z, z, h1], axis=-1)


def kernel(x_prompt, x_sample, c, cache_k_win, cache_v_win, cache_ckv, cache_krope, c_ctx, w_ada, b_ada, g_norm_mix, g_norm_ffn, w_in_attn, g_qk_a_q, g_qk_a_k, sink_a, g_mla_q_a, w_mla_q_b, g_mla_kv_a, w_mla_kv_b, g_qk_b_q, g_qk_b_k, w_out_attn, w_ffn_gate, w_ffn_up, w_ffn_down, w_conv_pw1, b_conv_pw1, w_conv_dw, b_conv_dw, g_conv_ln, b_conv_ln, w_conv_pw2, b_conv_pw2, w_router, w_moe_gate, w_moe_up, w_moe_down):
    x_ctx, x_lat, lat_tiles_off = x_prompt.reshape(T_CTX, D), x_sample.reshape(T_LAT, D), 0
    cond8 = jnp.concatenate([c_ctx[None, :], c, jnp.zeros((8 - 1 - DEC_BATCH, D), F32)], axis=0)
    ada = _ada_call(cond8, w_ada, b_ada).reshape(DEPTH, 8, N_ADA, D)
    tabs = _rope_tables()

    wo_b = w_out_attn.astype(BF16)
    wg_b = w_ffn_gate.astype(BF16)
    wu_b = w_ffn_up.astype(BF16)
    wd_b = w_ffn_down.astype(BF16)
    w1_b = w_conv_pw1.astype(BF16)
    w2_b = w_conv_pw2.astype(BF16)

    new_k, new_v, new_ckv, new_kr = [], [], [], []
    for layer in range(DEPTH):
        gm = g_norm_mix[layer].reshape(1, D)
        gf = g_norm_ffn[layer].reshape(1, D)
        if layer % 2 == 0:
            e = layer // 2
            lw = _attn_weights(e, w_in_attn, g_qk_a_q, g_qk_a_k, g_mla_q_a, w_mla_q_b, g_mla_kv_a,
                               w_mla_kv_b, g_qk_b_q, g_qk_b_k)
            sink = sink_a[e].reshape(A_HEADS)
            (qa, ka4, va4, qb, kb, vlo, vhi, kaf, vaf, ckvf, krf) = _pre_attn_call(
                False, x_ctx, 0, ada, layer, gm, lw, None)
            o_ctx = _attn_ctx_call(sink, qa, ka4, va4, qb, kb, vlo, vhi)
            new_k.append(kaf.reshape(BATCH, SEQ, A_KV, A_DH))
            new_v.append(vaf.reshape(BATCH, SEQ, A_KV, A_DH))
            new_ckv.append(ckvf.reshape(BATCH, SEQ, B_KV_LORA))
            new_kr.append(krf[:, :B_ROPE].reshape(BATCH, SEQ, B_ROPE))
            (qa, ka4, va4, qb, kb, vlo, vhi) = _pre_attn_call(
                True, x_lat, lat_tiles_off * (T_CTX // TM), ada, layer, gm, lw, tabs)
            kr_c = jnp.pad(cache_krope[:, e].reshape(DEC_BATCH * PAST, B_ROPE), ((0, 0), (0, LANE - B_ROPE)))
            kb_c, vlo_c, vhi_c = _cache_kv_call(cache_ckv[:, e].reshape(DEC_BATCH * PAST, B_KV_LORA), kr_c, lw)
            pad_rows = lambda a: jnp.pad(a.reshape(DEC_BATCH, DEC_SEQ, 512), ((0, 0), (QB, QB), (0, 0)))
            ka4c = _pair_variants_host(cache_k_win[:, e].reshape(DEC_BATCH, PAST, LANE)).astype(BF16)
            va4c = _pair_variants_host(cache_v_win[:, e].reshape(DEC_BATCH, PAST, LANE)).astype(BF16)
            cat = lambda a, b, w: jnp.concatenate(
                [a.reshape(DEC_BATCH, PAST, w), b.reshape(DEC_BATCH, DEC_SEQ, w)], axis=1)
            o_lat = _attn_lat_call(sink, qa, pad_rows(ka4), pad_rows(va4), ka4c, va4c, qb,
                                   cat(kb_c, kb, 1024), cat(vlo_c, vlo, 512), cat(vhi_c, vhi, 512))
            x = _post_attn_call(x_ctx, x_lat, lat_tiles_off * (T_CTX // FFN_TM), o_ctx, o_lat, ada, layer, gf,
                                wo_b[e], wg_b[e], wu_b[e], wd_b[e])
        else:
            o_ = layer // 2
            cw = dict(w1=w1_b[o_], b1=b_conv_pw1[o_].reshape(1, 2 * D), wdw=w_conv_dw[o_],
                      bdw=b_conv_dw[o_].reshape(1, D), gln=g_conv_ln[o_].reshape(1, D),
                      bln=b_conv_ln[o_].reshape(1, D), w2=w2_b[o_], b2=b_conv_pw2[o_].reshape(1, D),
                      wr=jnp.pad(w_router[o_], ((0, 0), (0, LANE - N_EXP))))
            x1, h2b, ri, rw, runs = _conv_router_call(x, ada, layer, gm, gf, cw)
            x = _moe_layer(x1, h2b, ri, rw, runs, ada, layer, w_moe_gate, w_moe_up, w_moe_down, o_,
                           split=(layer == DEPTH - 1))
            x_ctx, x_lat, lat_tiles_off = x, x, 1

    y_prompt = x[0].reshape(BATCH, SEQ, D)
    y_sample = x[1].reshape(DEC_BATCH, DEC_SEQ, D)
    return (y_prompt, y_sample, jnp.stack(new_k, axis=1), jnp.stack(new_v, axis=1),
            jnp.stack(new_ckv, axis=1), jnp.stack(new_kr, axis=1))
```

```python
import functools

import jax
import jax.numpy as jnp
from jax import lax
from jax.experimental import pallas as pl
from jax.experimental.pallas import tpu as pltpu

F32 = jnp.float32
BF16 = jnp.bfloat16

D = 1024
BATCH = 32
SEQ = 256
DEPTH = 4
DEC_BATCH = 2
DEC_SEQ = 1024
PAST = 256
GRID_W = 64
N_EVEN = 2
A_HEADS = 8
A_KV = 2
A_DH = 64
WINDOW = 128
B_HEADS = 8
B_Q_LORA = 256
B_KV_LORA = 128
B_NOPE = 64
B_ROPE = 32
B_QK = B_NOPE + B_ROPE
B_V = 64
IN_DIM = 1184
IN_PAD = 1280
CONV_W = 31
D_FF = 2816
N_EXP = 8
D_FFE = 3584
ROPE_BASE = 10000.0
EPS = 1e-6
N_ADA = 6

T_CTX = BATCH * SEQ
T_LAT = DEC_BATCH * DEC_SEQ
T_ALL = T_CTX + T_LAT
TM = 256
N_TILES = T_ALL // TM
N_CTX_TILES = T_CTX // TM
LAT_TILES_PER_SEQ = DEC_SEQ // TM
LANE = 128
SLOT = 128

MOE_SUB = 256
MOE_R = 3584
MOE_TF = 512
MOE_NF = D_FFE // MOE_TF
MOE_TC = 512
MOE_GB = 128
MOE_GDEPTH = 7
MOE_GSLOTS = MOE_GDEPTH + 1
MOE_ROWS = 2 * T_ALL + N_EXP * MOE_SUB
MOE_NS = MOE_ROWS // MOE_R + N_EXP

FFN_TM = 512
FFN_CH = 1408

VMEM_BIG = 56 * 1024 * 1024


def _dot(a, b):
    return jnp.dot(a, b, preferred_element_type=F32)


def _dot_nt(a, b):
    return lax.dot_general(a, b, (((1,), (1,)), ((), ())), preferred_element_type=F32)


def _split(a):
    hi = a.astype(BF16)
    lo = (a - hi.astype(F32)).astype(BF16)
    return hi, lo


def _dot_x3(a, b):
    ah, al = _split(a)
    bh, bl = _split(b)
    return _dot(ah, bh) + (_dot(ah, bl) + _dot(al, bh))


def _sigmoid(x):
    return 1.0 / (1.0 + jnp.exp(-x))


def _rms(x, n):
    return x * lax.rsqrt(jnp.sum(x * x, axis=-1, keepdims=True) * (1.0 / n) + EPS)


def _modulate(x, g, shift, scale):
    return _rms(x, D) * g * (1.0 + scale) + shift


def _lo_mask(rows):
    return lax.broadcasted_iota(jnp.int32, (rows, LANE), 1) < 64


def _ada_kernel(cond_ref, w_ref, b_ref, o_ref):
    c = cond_ref[...]
    s = c * _sigmoid(c)
    o_ref[...] = _dot_x3(s, w_ref[...]) + b_ref[...]


def _ada_call(cond8, w_ada, b_ada):
    tn = 1024
    return pl.pallas_call(
        _ada_kernel,
        out_shape=jax.ShapeDtypeStruct((DEPTH, 8, N_ADA * D), F32),
        grid=(DEPTH, N_ADA * D // tn),
        in_specs=[
            pl.BlockSpec((8, D), lambda l, j: (0, 0)),
            pl.BlockSpec((None, D, tn), lambda l, j: (l, 0, j)),
            pl.BlockSpec((None, 1, tn), lambda l, j: (l, 0, j)),
        ],
        out_specs=pl.BlockSpec((None, 8, tn), lambda l, j: (l, 0, j)),
        compiler_params=pltpu.CompilerParams(
            dimension_semantics=("arbitrary", "arbitrary"), vmem_limit_bytes=VMEM_BIG),
        name="ada",
    )(cond8, w_ada, b_ada.reshape(DEPTH, 1, N_ADA * D))


def _rope(x, cos, sup, sdn, shift):
    up = pltpu.roll(x, LANE - shift, 1)
    dn = pltpu.roll(x, shift, 1)
    return x * cos + up * sup + dn * sdn


def _norm64_block(blk, lo):
    sq = blk * blk
    s_lo = jnp.sum(jnp.where(lo, sq, 0.0), axis=-1, keepdims=True)
    s_hi = jnp.sum(jnp.where(lo, 0.0, sq), axis=-1, keepdims=True)
    r = jnp.where(lo, lax.rsqrt(s_lo * (1.0 / A_DH) + EPS), lax.rsqrt(s_hi * (1.0 / A_DH) + EPS))
    return blk * r


def _pair_variants(x, lo):
    r = pltpu.roll(x, 64, 1)
    return (jnp.where(lo, x, 0.0), jnp.where(lo, 0.0, r), jnp.where(lo, r, 0.0), jnp.where(lo, 0.0, x))


def _mla_kv(ckv_n, kr, wkv_ref, place_ref, gk_ref, lo, kb_ref, vlo_ref, vhi_ref, rope_tabs):
    kv = _dot(ckv_n.astype(BF16), wkv_ref[...])
    kr_hi, kr_lo = _split(kr)
    place = place_ref[...]
    krs = _dot(kr_hi, place) + _dot(kr_lo, place)
    for h in range(B_HEADS):
        sl = slice(h * SLOT, (h + 1) * SLOT)
        blk = kv[:, sl] + krs[:, sl]
        blk = _rms(blk, B_QK) * gk_ref[:, sl]
        if rope_tabs is not None:
            cos, sup, sdn = rope_tabs
            blk = _rope(blk, cos[:, sl], sup[:, sl], sdn[:, sl], 8)
        kb_ref[:, sl] = blk.astype(BF16)
    for j in range(B_HEADS // 2):
        sl = slice(j * LANE, (j + 1) * LANE)
        v = kv[:, B_HEADS * SLOT + j * LANE: B_HEADS * SLOT + (j + 1) * LANE]
        vlo_ref[:, sl] = jnp.where(lo, v, 0.0).astype(BF16)
        vhi_ref[:, sl] = jnp.where(lo, 0.0, v).astype(BF16)


def _pre_attn_kernel(rope, *refs):
    if rope:
        (x_ref, ada_ref, g_ref, win_ref, wqb_ref, wkv_ref, place_ref,
         gqa_ref, gka_ref, gql_ref, gkl_ref, gqb_ref, gkb_ref,
         ca_ref, sua_ref, sda_ref, cb_ref, sub_ref, sdb_ref,
         qa_o, ka4_o, va4_o, qb_o, kb_o, vlo_o, vhi_o) = refs
    else:
        (x_ref, ada_ref, g_ref, win_ref, wqb_ref, wkv_ref, place_ref,
         gqa_ref, gka_ref, gql_ref, gkl_ref, gqb_ref, gkb_ref,
         qa_o, ka4_o, va4_o, qb_o, kb_o, vlo_o, vhi_o,
         kaf_o, vaf_o, ckvf_o, krf_o) = refs
    lo = _lo_mask(TM)
    h = _modulate(x_ref[...], g_ref[...], ada_ref[0:1, :], ada_ref[1:2, :])
    p = _dot(h.astype(BF16), win_ref[...])

    for j in range(4):
        sl = slice(j * LANE, (j + 1) * LANE)
        blk = _norm64_block(p[:, sl], lo) * gqa_ref[:, sl]
        if rope:
            blk = _rope(blk, ca_ref[:, sl], sua_ref[:, sl], sda_ref[:, sl], 16)
        qa_o[:, sl] = (blk * (A_DH ** -0.5)).astype(BF16)
    ka = _norm64_block(p[:, 512:640], lo) * gka_ref[...]
    va = p[:, 640:768]
    if not rope:
        kaf_o[...] = ka
        vaf_o[...] = va
    else:
        ka = _rope(ka, ca_ref[:, 0:LANE], sua_ref[:, 0:LANE], sda_ref[:, 0:LANE], 16)
    for n, (kk, vv) in enumerate(zip(_pair_variants(ka, lo), _pair_variants(va, lo))):
        sl = slice(n * LANE, (n + 1) * LANE)
        ka4_o[:, sl] = kk.astype(BF16)
        va4_o[:, sl] = vv.astype(BF16)

    cq = _rms(p[:, 768:1024], B_Q_LORA) * gql_ref[...]
    qb = _dot(cq.astype(BF16), wqb_ref[...])
    for hh in range(B_HEADS):
        sl = slice(hh * SLOT, (hh + 1) * SLOT)
        blk = _rms(qb[:, sl], B_QK) * gqb_ref[:, sl]
        if rope:
            blk = _rope(blk, cb_ref[:, sl], sub_ref[:, sl], sdb_ref[:, sl], 8)
        qb_o[:, sl] = (blk * (B_QK ** -0.5)).astype(BF16)

    ckv = _rms(p[:, 1024:1152], B_KV_LORA) * gkl_ref[...]
    kr = p[:, 1152:1280]
    if not rope:
        ckvf_o[...] = ckv
        krf_o[...] = kr
    tabs = (cb_ref, sub_ref, sdb_ref) if rope else None
    _mla_kv(ckv, kr, wkv_ref, place_ref, gkb_ref, lo, kb_o, vlo_o, vhi_o, tabs)


def _const_spec(shape):
    nd = len(shape)
    return pl.BlockSpec(shape, lambda *a: (0,) * nd)


def _pre_attn_call(rope, x, off, ada, layer, g, lw, tabs):
    n_tiles = (T_LAT if rope else T_CTX) // TM
    n_rows = n_tiles * TM
    if rope:
        ada_map = lambda i: (layer, 1 + i // LAT_TILES_PER_SEQ, 0, 0)
    else:
        ada_map = lambda i: (layer, 0, 0, 0)
    in_specs = [
        pl.BlockSpec((TM, D), lambda i: (i + off, 0)),
        pl.BlockSpec((None, None, N_ADA, D), ada_map),
        _const_spec((1, D)),
        _const_spec((D, IN_PAD)),
        _const_spec((B_Q_LORA, B_HEADS * SLOT)),
        _const_spec((B_KV_LORA, B_HEADS * SLOT + B_HEADS * B_V)),
        _const_spec((LANE, B_HEADS * SLOT)),
        _const_spec((1, 512)), _const_spec((1, LANE)), _const_spec((1, B_Q_LORA)),
        _const_spec((1, B_KV_LORA)), _const_spec((1, B_HEADS * SLOT)), _const_spec((1, B_HEADS * SLOT)),
    ]
    args = [x, ada, g, lw["w_in"], lw["w_qb"], lw["w_kv"], lw["place"],
            lw["gqa"], lw["gka"], lw["gql"], lw["gkl"], lw["gqb"], lw["gkb"]]
    if rope:
        tmap = lambda i: (i % LAT_TILES_PER_SEQ, 0)
        in_specs += [pl.BlockSpec((TM, 512), tmap)] * 3 + [pl.BlockSpec((TM, B_HEADS * SLOT), tmap)] * 3
        args += list(tabs)
    row = lambda w: pl.BlockSpec((TM, w), lambda i: (i, 0))
    out_shape = [jax.ShapeDtypeStruct((n_rows, w), BF16) for w in (512, 512, 512, 1024, 1024, 512, 512)]
    out_specs = [row(w) for w in (512, 512, 512, 1024, 1024, 512, 512)]
    if not rope:
        out_shape += [jax.ShapeDtypeStruct((n_rows, LANE), F32)] * 4
        out_specs += [row(LANE)] * 4
    return pl.pallas_call(
        functools.partial(_pre_attn_kernel, rope),
        out_shape=out_shape, grid=(n_tiles,), in_specs=in_specs, out_specs=out_specs,
        compiler_params=pltpu.CompilerParams(
            dimension_semantics=("arbitrary",), vmem_limit_bytes=VMEM_BIG),
        name="pre_attn_lat" if rope else "pre_attn_ctx",
    )(*args)


def _cache_kv_kernel(ckv_ref, kr_ref, wkv_ref, place_ref, gkb_ref, kb_o, vlo_o, vhi_o):
    lo = _lo_mask(PAST)
    _mla_kv(ckv_ref[...], kr_ref[...], wkv_ref, place_ref, gkb_ref, lo, kb_o, vlo_o, vhi_o, None)


def _cache_kv_call(ckv, kr, lw):
    rows = DEC_BATCH * PAST
    row = lambda w: pl.BlockSpec((PAST, w), lambda i: (i, 0))
    return pl.pallas_call(
        _cache_kv_kernel,
        out_shape=[jax.ShapeDtypeStruct((rows, 1024), BF16), jax.ShapeDtypeStruct((rows, 512), BF16),
                   jax.ShapeDtypeStruct((rows, 512), BF16)],
        grid=(DEC_BATCH,),
        in_specs=[row(LANE), row(LANE),
                  _const_spec((B_KV_LORA, B_HEADS * SLOT + B_HEADS * B_V)),
                  _const_spec((LANE, B_HEADS * SLOT)), _const_spec((1, B_HEADS * SLOT))],
        out_specs=[row(1024), row(512), row(512)],
        compiler_params=pltpu.CompilerParams(dimension_semantics=("arbitrary",)),
        name="cache_kv",
    )(ckv, kr, lw["w_kv"], lw["place"], lw["gkb"])


def _softmax_pv(scores, values, sink):
    m = jnp.max(scores[0], axis=-1, keepdims=True)
    for s in scores[1:]:
        m = jnp.maximum(m, jnp.max(s, axis=-1, keepdims=True))
    if sink is not None:
        m = jnp.maximum(m, sink)
    ps = [jnp.exp(s - m) for s in scores]
    den = jnp.sum(ps[0], axis=-1, keepdims=True)
    for p in ps[1:]:
        den = den + jnp.sum(p, axis=-1, keepdims=True)
    if sink is not None:
        den = den + jnp.exp(sink - m)
    out = _dot(ps[0].astype(BF16), values[0])
    for p, v in zip(ps[1:], values[1:]):
        out = out + _dot(p.astype(BF16), v)
    return out * (1.0 / den)


def _attn_ctx_kernel(sink_ref, qa_ref, ka4_ref, va4_ref, qb_ref, kb_ref, vlo_ref, vhi_ref, o_ref):
    for j in range(4):
        q = qa_ref[:, j * LANE:(j + 1) * LANE]
        acc = None
        for par in range(2):
            c = (2 * (j // 2) + par) * LANE
            s = _dot_nt(q, ka4_ref[:, c:c + LANE])
            o = _softmax_pv([s], [va4_ref[:, c:c + LANE]], sink_ref[2 * j + par])
            acc = o if acc is None else acc + o
        o_ref[:, j * LANE:(j + 1) * LANE] = acc.astype(BF16)
    for j in range(4):
        acc = None
        for par, v_ref in enumerate((vlo_ref, vhi_ref)):
            hh = 2 * j + par
            s = _dot_nt(qb_ref[:, hh * SLOT:(hh + 1) * SLOT], kb_ref[:, hh * SLOT:(hh + 1) * SLOT])
            o = _softmax_pv([s], [v_ref[:, j * LANE:(j + 1) * LANE]], None)
            acc = o if acc is None else acc + o
        o_ref[:, 512 + j * LANE:512 + (j + 1) * LANE] = acc.astype(BF16)


def _attn_ctx_call(sink, qa, ka4, va4, qb, kb, vlo, vhi):
    row = lambda w: pl.BlockSpec((SEQ, w), lambda b: (b, 0))
    return pl.pallas_call(
        _attn_ctx_kernel,
        out_shape=jax.ShapeDtypeStruct((T_CTX, D), BF16),
        grid=(BATCH,),
        in_specs=[pl.BlockSpec(memory_space=pltpu.SMEM),
                  row(512), row(512), row(512), row(1024), row(1024), row(512), row(512)],
        out_specs=row(D),
        compiler_params=pltpu.CompilerParams(dimension_semantics=("arbitrary",)),
        name="attn_ctx",
    )(sink, qa, ka4, va4, qb, kb, vlo, vhi)


QB = 128
N_QB = DEC_SEQ // QB
KPAD = DEC_SEQ + 2 * QB
KALL = PAST + DEC_SEQ


def _attn_lat_kernel(sink_ref, qa_ref, ka4p_ref, va4p_ref, ka4c_ref, va4c_ref,
                     qb_ref, kb_ref, vlo_ref, vhi_ref, o_ref):
    n = pl.program_id(1)
    r = lax.broadcasted_iota(jnp.int32, (QB, 3 * QB), 0)
    c = lax.broadcasted_iota(jnp.int32, (QB, 3 * QB), 1)
    rel = c - QB - r
    kpos = (n - 1) * QB + c
    valid = (jnp.abs(rel) <= WINDOW) & (kpos >= 0) & (kpos < DEC_SEQ)
    k0 = pl.multiple_of(n * QB, QB)
    for j in range(4):
        q = qa_ref[:, j * LANE:(j + 1) * LANE]
        acc = None
        for par in range(2):
            cc = (2 * (j // 2) + par) * LANE
            s_loc = _dot_nt(q, ka4p_ref[pl.ds(k0, 3 * QB), cc:cc + LANE])
            s_loc = jnp.where(valid, s_loc, -jnp.inf)
            s_ctx = _dot_nt(q, ka4c_ref[:, cc:cc + LANE])
            o = _softmax_pv([s_ctx, s_loc],
                            [va4c_ref[:, cc:cc + LANE], va4p_ref[pl.ds(k0, 3 * QB), cc:cc + LANE]],
                            sink_ref[2 * j + par])
            acc = o if acc is None else acc + o
        o_ref[:, j * LANE:(j + 1) * LANE] = acc.astype(BF16)
    for j in range(4):
        acc = None
        for par, v_ref in enumerate((vlo_ref, vhi_ref)):
            hh = 2 * j + par
            s = _dot_nt(qb_ref[:, hh * SLOT:(hh + 1) * SLOT], kb_ref[:, hh * SLOT:(hh + 1) * SLOT])
            o = _softmax_pv([s], [v_ref[:, j * LANE:(j + 1) * LANE]], None)
            acc = o if acc is None else acc + o
        o_ref[:, 512 + j * LANE:512 + (j + 1) * LANE] = acc.astype(BF16)


def _attn_lat_call(sink, qa, ka4p, va4p, ka4c, va4c, qb, kball, vlo_all, vhi_all):
    qrow = lambda w: pl.BlockSpec((QB, w), lambda b, n: (b * N_QB + n, 0))
    per_b = lambda rows, w: pl.BlockSpec((None, rows, w), lambda b, n: (b, 0, 0))
    return pl.pallas_call(
        _attn_lat_kernel,
        out_shape=jax.ShapeDtypeStruct((T_LAT, D), BF16),
        grid=(DEC_BATCH, N_QB),
        in_specs=[pl.BlockSpec(memory_space=pltpu.SMEM),
                  qrow(512), per_b(KPAD, 512), per_b(KPAD, 512), per_b(PAST, 512), per_b(PAST, 512),
                  qrow(1024), per_b(KALL, 1024), per_b(KALL, 512), per_b(KALL, 512)],
        out_specs=qrow(D),
        compiler_params=pltpu.CompilerParams(dimension_semantics=("arbitrary", "arbitrary")),
        name="attn_lat",
    )(sink, qa, ka4p, va4p, ka4c, va4c, qb, kball, vlo_all, vhi_all)


def _post_attn_kernel(xc_ref, xl_ref, oc_ref, ol_ref, ada_ref, g_ref, wo_ref, wg_ref, wu_ref, wd_ref, y_ref):
    is_ctx = pl.program_id(0) < T_CTX // FFN_TM
    x = jnp.where(is_ctx, xc_ref[...], xl_ref[...])
    o = jnp.where(is_ctx, oc_ref[...], ol_ref[...])
    mix = _dot(o, wo_ref[...])
    x1 = x + ada_ref[2:3, :] * mix
    h = _modulate(x1, g_ref[...], ada_ref[3:4, :], ada_ref[4:5, :]).astype(BF16)
    acc = None
    for c0 in range(0, D_FF, FFN_CH):
        gg = _dot(h, wg_ref[:, c0:c0 + FFN_CH])
        uu = _dot(h, wu_ref[:, c0:c0 + FFN_CH])
        a = (gg * _sigmoid(gg) * uu).astype(BF16)
        part = _dot(a, wd_ref[c0:c0 + FFN_CH, :])
        acc = part if acc is None else acc + part
    y_ref[...] = x1 + ada_ref[5:6, :] * acc


def _tile_cond(i, tiles_per_ctx, tiles_per_lat_seq):
    return jnp.where(i < tiles_per_ctx, 0, 1 + (i - tiles_per_ctx) // tiles_per_lat_seq)


def _post_attn_call(x_ctx, x_lat, x_lat_off, o_ctx, o_lat, ada, layer, g, wo, wg, wu, wd):
    n_tiles = T_ALL // FFN_TM
    n_ctx = T_CTX // FFN_TM
    ada_map = lambda i: (layer, _tile_cond(i, n_ctx, DEC_SEQ // FFN_TM), 0, 0)
    ctx_map = lambda i: (jnp.minimum(i, n_ctx - 1), 0)
    one = pl.Buffered(1)
    return pl.pallas_call(
        _post_attn_kernel,
        out_shape=jax.ShapeDtypeStruct((T_ALL, D), F32),
        grid=(n_tiles,),
        in_specs=[pl.BlockSpec((FFN_TM, D), ctx_map),
                  pl.BlockSpec((FFN_TM, D), lambda i: (jnp.maximum(i - n_ctx, 0) + x_lat_off, 0)),
                  pl.BlockSpec((FFN_TM, D), ctx_map),
                  pl.BlockSpec((FFN_TM, D), lambda i: (jnp.maximum(i - n_ctx, 0), 0)),
                  pl.BlockSpec((None, None, N_ADA, D), ada_map),
                  _const_spec((1, D)),
                  pl.BlockSpec((D, D), lambda i: (0, 0), pipeline_mode=one),
                  pl.BlockSpec((D, D_FF), lambda i: (0, 0), pipeline_mode=one),
                  pl.BlockSpec((D, D_FF), lambda i: (0, 0), pipeline_mode=one),
                  pl.BlockSpec((D_FF, D), lambda i: (0, 0), pipeline_mode=one)],
        out_specs=pl.BlockSpec((FFN_TM, D), lambda i: (i, 0)),
        compiler_params=pltpu.CompilerParams(
            dimension_semantics=("arbitrary",), vmem_limit_bytes=VMEM_BIG),
        name="post_attn_ffn",
    )(x_ctx, x_lat, o_ctx, o_lat, ada, g, wo, wg, wu, wd)


HALO = 16
EXT = TM + 2 * HALO
CONV_RC = 64
CONV_LC = 128
CONV_WIN = CONV_RC + 24
CONV_LN_ROWS = 32


def _conv_router_kernel(x_ref, xp_ref, xn_ref, ada_ref, gm_ref, gf_ref, w1_ref, b1_ref, wdw_ref, bdw_ref,
                        gln_ref, bln_ref, w2_ref, b2_ref, wr_ref,
                        x1_o, h2_o, ri_o, rw_o, cnt_o, p_ref, c_ref, v_ref, run_ref):
    i = pl.program_id(0)

    @pl.when(i == 0)
    def _():
        run_ref[...] = jnp.zeros_like(run_ref)

    is_lat = i >= N_CTX_TILES
    j = (i - N_CTX_TILES) % LAT_TILES_PER_SEQ
    lflag = jnp.where(is_lat & (j != 0), 1.0, 0.0)
    rflag = jnp.where(is_lat & (j != LAT_TILES_PER_SEQ - 1), 1.0, 0.0)

    x = x_ref[...]
    xx = jnp.concatenate([xp_ref[...], x, xn_ref[...]], axis=0)
    h = _modulate(xx, gm_ref[...], ada_ref[0:1, :], ada_ref[1:2, :])
    a = _dot(h.astype(BF16), w1_ref[...]) + b1_ref[...]
    u = a[:, :D] * _sigmoid(a[:, D:])
    row = lax.broadcasted_iota(jnp.int32, (EXT, 1), 0)
    u = u * jnp.where(row < HALO, lflag, jnp.where(row >= HALO + TM, rflag, 1.0))

    p_ref[...] = u
    for rc in range(TM // CONV_RC):
        for lc in range(D // CONV_LC):
            ls = slice(lc * CONV_LC, (lc + 1) * CONV_LC)
            acc = jnp.zeros((CONV_RC, CONV_LC), F32) + bdw_ref[:, ls]
            base = p_ref[rc * CONV_RC:rc * CONV_RC + CONV_WIN + 8, ls]
            for r in range(8):
                win = base if r == 0 else pltpu.roll(base, CONV_WIN + 8 - r, 0)
                for q in range(4):
                    o = 8 * q + r
                    if 1 <= o <= CONV_W:
                        acc = acc + wdw_ref[o - 1:o, ls] * win[8 * q:8 * q + CONV_RC]
            c_ref[rc * CONV_RC:(rc + 1) * CONV_RC, ls] = acc

    for c0 in range(0, TM, CONV_LN_ROWS):
        acc = c_ref[c0:c0 + CONV_LN_ROWS, :]
        mu = jnp.mean(acc, axis=-1, keepdims=True)
        xc = acc - mu
        y = xc * lax.rsqrt(jnp.mean(xc * xc, axis=-1, keepdims=True) + EPS)
        y = y * gln_ref[...] + bln_ref[...]
        v_ref[c0:c0 + CONV_LN_ROWS, :] = (y * _sigmoid(y)).astype(BF16)

    mix = _dot(v_ref[...], w2_ref[...]) + b2_ref[...]
    x1 = x + ada_ref[2:3, :] * mix
    x1_o[...] = x1
    h2 = _modulate(x1, gf_ref[...], ada_ref[3:4, :], ada_ref[4:5, :])
    h2_o[...] = h2.astype(BF16)

    lane = lax.broadcasted_iota(jnp.int32, (TM, LANE), 1)
    lane_f = lane.astype(F32)
    logits = jnp.where(lane < N_EXP, _dot_x3(h2, wr_ref[...]), -jnp.inf)
    m1 = jnp.max(logits, axis=-1, keepdims=True)
    i1 = jnp.min(jnp.where(logits == m1, lane_f, float(LANE)), axis=-1, keepdims=True)
    rest = jnp.where(lane_f == i1, -jnp.inf, logits)
    m2 = jnp.max(rest, axis=-1, keepdims=True)
    i2 = jnp.min(jnp.where(rest == m2, lane_f, float(LANE)), axis=-1, keepdims=True)
    e = jnp.exp(m2 - m1)
    w1 = 1.0 / (1.0 + e)
    w2 = e / (1.0 + e)

    oh1 = jnp.where(lane_f == i1, 1.0, 0.0)
    oh2 = jnp.where(lane_f == i2, 1.0, 0.0)
    tr = lax.broadcasted_iota(jnp.int32, (TM, TM), 0)
    tc = lax.broadcasted_iota(jnp.int32, (TM, TM), 1)
    tri = jnp.where(tr > tc, 1.0, 0.0).astype(BF16)
    run = run_ref[0:1, :]
    cnt1 = jnp.sum(oh1, axis=0, keepdims=True)
    cnt2 = jnp.sum(oh2, axis=0, keepdims=True)
    pre1 = _dot(tri, oh1.astype(BF16)) + run
    pre2 = _dot(tri, oh2.astype(BF16)) + (run + cnt1)
    rank1 = jnp.sum(oh1 * pre1, axis=-1, keepdims=True)
    rank2 = jnp.sum(oh2 * pre2, axis=-1, keepdims=True)
    new_run = run + cnt1 + cnt2
    run_ref[...] = jnp.broadcast_to(new_run, run_ref.shape)
    srow = lax.broadcasted_iota(jnp.int32, cnt_o.shape, 0)
    cnt_o[...] = jnp.where(srow == 0, run, jnp.where(srow == 1, cnt1 + cnt2, 0.0))

    ri = jnp.where(lane == 0, i1, jnp.where(lane == 1, i2, jnp.where(lane == 2, rank1, jnp.where(lane == 3, rank2, 0.0))))
    ri_o[...] = ri.astype(jnp.int32)

    rw_o[...] = jnp.where(lane == 0, w1, jnp.where(lane == 1, w2, 0.0))


def _conv_router_call(x, ada, layer, gm, gf, cw):
    ada_map = lambda i: (layer, _tile_cond(i, N_CTX_TILES, LAT_TILES_PER_SEQ), 0, 0)
    hb = TM // HALO
    row = lambda w: pl.BlockSpec((TM, w), lambda i: (i, 0))
    return pl.pallas_call(
        _conv_router_kernel,
        out_shape=[jax.ShapeDtypeStruct((T_ALL, D), F32), jax.ShapeDtypeStruct((T_ALL, D), BF16),
                   jax.ShapeDtypeStruct((T_ALL, LANE), jnp.int32), jax.ShapeDtypeStruct((T_ALL, LANE), F32),
                   jax.ShapeDtypeStruct((N_TILES, 8, LANE), F32)],
        grid=(N_TILES,),
        in_specs=[row(D),
                  pl.BlockSpec((HALO, D), lambda i: (jnp.maximum(i * hb - 1, 0), 0)),
                  pl.BlockSpec((HALO, D), lambda i: (jnp.minimum((i + 1) * hb, T_ALL // HALO - 1), 0)),
                  pl.BlockSpec((None, None, N_ADA, D), ada_map),
                  _const_spec((1, D)), _const_spec((1, D)),
                  _const_spec((D, 2 * D)), _const_spec((1, 2 * D)),
                  _const_spec((CONV_W, D)), _const_spec((1, D)),
                  _const_spec((1, D)), _const_spec((1, D)),
                  _const_spec((D, D)), _const_spec((1, D)),
                  _const_spec((D, LANE))],
        out_specs=[row(D), row(D), row(LANE), row(LANE), pl.BlockSpec((None, 8, LANE), lambda i: (i, 0, 0))],
        scratch_shapes=[pltpu.VMEM((EXT, D), F32), pltpu.VMEM((TM, D), F32), pltpu.VMEM((TM, D), BF16),
                        pltpu.VMEM((8, LANE), F32)],
        compiler_params=pltpu.CompilerParams(
            dimension_semantics=("arbitrary",), vmem_limit_bytes=VMEM_BIG),
        name="conv_router",
    )(x, x, x, ada, gm, gf, cw["w1"], cw["b1"], cw["wdw"], cw["bdw"], cw["gln"], cw["bln"],
      cw["w2"], cw["b2"], cw["wr"])


def _moe_kernel(se_ref, row0_ref, nsub_ref, ta_ref, tn_ref, h_ref, post_ref, rwt_ref, wg_ref, wu_ref, wd_ref,
                ys_in_ref, ys_ref, hb_ref, acc_ref, gs_ref, hstage, wgb, wub, wdb, sem_in, sem_out):
    del ys_in_ref
    j = pl.program_id(0)
    f = pl.program_id(1)
    nsub = nsub_ref[j]
    row0 = row0_ref[j]
    e = se_ref[j]

    def tile_load(i, slot):
        r = pl.multiple_of(i * TM, TM)
        return pltpu.make_async_copy(h_ref.at[pl.ds(r, TM), :], hstage.at[slot], sem_in.at[slot])

    def store(s):
        r = pl.multiple_of(s * MOE_SUB, MOE_SUB)
        dst = ys_ref.at[pl.ds(pl.multiple_of(row0 + s * MOE_SUB, MOE_SUB), MOE_SUB), :]
        return pltpu.make_async_copy(hb_ref.at[pl.ds(r, MOE_SUB), :], dst, sem_out)

    def gather_tile(i, carry):
        slot = i % MOE_GSLOTS
        tile_load(i, slot).wait()

        @pl.when(i + MOE_GDEPTH < N_TILES)
        def _():
            tile_load(i + MOE_GDEPTH, (i + MOE_GDEPTH) % MOE_GSLOTS).start()

        a = ta_ref[i * N_EXP + e] - row0
        n = tn_ref[i * N_EXP + e]
        last = a + n - 1

        @pl.when((n > 0) & (last >= 0) & (a < nsub * MOE_SUB))
        def _():
            start = (jnp.maximum(a, 0) // 8) * 8
            end = jnp.minimum(last, nsub * MOE_SUB - 1)
            pp = post_ref[pl.ds(pl.multiple_of(i * 8, 8), 8), :]
            riota = lax.broadcasted_iota(jnp.int32, (MOE_GB, TM), 0)
            ww = rwt_ref[pl.ds(pl.multiple_of(i * 8, 8), 8), :]

            def window(w):
                rr = pl.multiple_of(start + w * MOE_GB, 8)
                base = row0 + rr
                hit0 = (pp[0:1, :] - base) == riota
                hit1 = (pp[1:2, :] - base) == riota
                acc_ref[pl.ds(rr, MOE_GB), :] += _dot(jnp.where(hit0 | hit1, 1.0, 0.0).astype(BF16),
                                                      hstage[slot])
                gate = jnp.sum(jnp.where(hit0, ww[0:1, :], 0.0) + jnp.where(hit1, ww[1:2, :], 0.0),
                               axis=-1, keepdims=True)
                gs_ref[pl.ds(rr, MOE_GB), :] += jnp.broadcast_to(gate, (MOE_GB, LANE))

            window(0)
            for extra in range(1, (TM + 8) // MOE_GB + 1):
                @pl.when(start + extra * MOE_GB <= end)
                def _():
                    window(extra)

        return carry

    def part(r, rows):
        hh = hb_ref[pl.ds(r, rows), :]
        out = None
        for c0 in range(0, MOE_TF, MOE_TC):
            gg = _dot(hh, wgb[:, c0:c0 + MOE_TC])
            uu = _dot(hh, wub[:, c0:c0 + MOE_TC])
            a = (gg * _sigmoid(gg) * uu).astype(BF16)
            piece = _dot(a, wdb[c0:c0 + MOE_TC, :])
            out = piece if out is None else out + piece
        return out

    def for_each_subtile(fn):
        def quad(p, carry):
            fn(4 * p, 4 * MOE_SUB)
            return carry

        lax.fori_loop(0, nsub // 4, quad, 0)
        rem = nsub % 4

        @pl.when(rem >= 2)
        def _():
            fn(nsub - rem, 2 * MOE_SUB)

        @pl.when(rem % 2 == 1)
        def _():
            fn(nsub - 1, MOE_SUB)

    @pl.when(nsub > 0)
    def _():
        wgb[...] = wg_ref[...].astype(BF16)
        wub[...] = wu_ref[...].astype(BF16)
        wdb[...] = wd_ref[...].astype(BF16)

        @pl.when(f == 0)
        def _():
            for d in range(MOE_GDEPTH):
                tile_load(d, d).start()

            def clear(s, carry):
                r = pl.multiple_of(s * MOE_SUB, MOE_SUB)
                acc_ref[pl.ds(r, MOE_SUB), :] = jnp.zeros((MOE_SUB, D), F32)
                gs_ref[pl.ds(r, MOE_SUB), :] = jnp.zeros((MOE_SUB, LANE), F32)
                return carry

            lax.fori_loop(0, nsub + 1, clear, 0)
            lax.fori_loop(0, N_TILES, gather_tile, 0)

            def first(s, rows):
                r = pl.multiple_of(s * MOE_SUB, MOE_SUB)
                hb_ref[pl.ds(r, rows), :] = acc_ref[pl.ds(r, rows), :].astype(BF16)
                acc_ref[pl.ds(r, rows), :] = part(r, rows)

            for_each_subtile(first)

        @pl.when((f > 0) & (f < MOE_NF - 1))
        def _():
            def middle(s, rows):
                r = pl.multiple_of(s * MOE_SUB, MOE_SUB)
                acc_ref[pl.ds(r, rows), :] += part(r, rows)

            for_each_subtile(middle)

        @pl.when(f == MOE_NF - 1)
        def _():
            def final(s, rows):
                r = pl.multiple_of(s * MOE_SUB, MOE_SUB)
                y = acc_ref[pl.ds(r, rows), :] + part(r, rows)
                hb_ref[pl.ds(r, rows), :] = (y * gs_ref[pl.ds(r, rows), 0:1]).astype(BF16)
                for k in range(rows // MOE_SUB):
                    store(s + k).start()

            for_each_subtile(final)

            def drain(s, carry):
                store(s).wait()
                return carry

            lax.fori_loop(0, nsub, drain, 0)


def _moe_call(st_e, st_row0, st_nsub, t_a, t_n, h2b, pos_t, rw_t, wg, wu, wd, ys0, lidx):
    f_eff = lambda j, f, ns: jnp.where(ns[j] > 0, f, MOE_NF - 1)
    wmap_in = lambda j, f, se, r0, ns, ta, tn: (lidx, se[j], 0, f_eff(j, f, ns))
    wmap_out = lambda j, f, se, r0, ns, ta, tn: (lidx, se[j], f_eff(j, f, ns), 0)
    return pl.pallas_call(
        _moe_kernel,
        out_shape=jax.ShapeDtypeStruct((MOE_ROWS, D), BF16),
        grid_spec=pltpu.PrefetchScalarGridSpec(
            num_scalar_prefetch=5, grid=(MOE_NS, MOE_NF),
            in_specs=[
                pl.BlockSpec(memory_space=pl.ANY),
                pl.BlockSpec((N_TILES * 8, TM), lambda j, f, se, r0, ns, ta, tn: (0, 0)),
                pl.BlockSpec((N_TILES * 8, TM), lambda j, f, se, r0, ns, ta, tn: (0, 0)),
                pl.BlockSpec((None, None, D, MOE_TF), wmap_in),
                pl.BlockSpec((None, None, D, MOE_TF), wmap_in),
                pl.BlockSpec((None, None, MOE_TF, D), wmap_out),
                pl.BlockSpec(memory_space=pl.ANY),
            ],
            out_specs=pl.BlockSpec(memory_space=pl.ANY),
            scratch_shapes=[pltpu.VMEM((MOE_R, D), BF16), pltpu.VMEM((MOE_R + MOE_SUB, D), F32),
                            pltpu.VMEM((MOE_R + MOE_SUB, LANE), F32),
                            pltpu.VMEM((MOE_GSLOTS, TM, D), BF16),
                            pltpu.VMEM((D, MOE_TF), BF16), pltpu.VMEM((D, MOE_TF), BF16),
                            pltpu.VMEM((MOE_TF, D), BF16),
                            pltpu.SemaphoreType.DMA((MOE_GSLOTS,)), pltpu.SemaphoreType.DMA(())]),
        input_output_aliases={11: 0},
        compiler_params=pltpu.CompilerParams(
            dimension_semantics=("arbitrary", "arbitrary"), vmem_limit_bytes=VMEM_BIG,
            has_side_effects=True),
        name="moe_experts",
    )(st_e, st_row0, st_nsub, t_a, t_n, h2b, pos_t, rw_t, wg, wu, wd, ys0)


def _combine_kernel(split, ta_ref, tn_ref, x_ref, ada_ref, posc_ref, ys_ref, *rest):
    if split:
        oc_ref, ol_ref, ybuf, oacc, bc, sem = rest
    else:
        o_ref, ybuf, oacc, bc, sem = rest
    g = pl.program_id(0)

    def blocks(i, e):
        a = ta_ref[i * N_EXP + e]
        n = tn_ref[i * N_EXP + e]
        lo = jnp.minimum(a // MOE_SUB, MOE_ROWS // MOE_SUB - 1)
        return n > 0, lo, (n > 0) & ((a + n - 1) // MOE_SUB > lo)

    def block_copy(slot, e, k, blk):
        src = ys_ref.at[pl.ds(pl.multiple_of(blk * MOE_SUB, MOE_SUB), MOE_SUB), :]
        return pltpu.make_async_copy(src, ybuf.at[slot, 2 * e + k], sem.at[slot, 2 * e + k])

    @pl.when(g < N_TILES)
    def _():
        for e in range(N_EXP):
            _, lo, two = blocks(g, e)
            block_copy(g % 2, e, 0, lo).start()

            @pl.when(two)
            def _():
                block_copy(g % 2, e, 1, lo + 1).start()

    @pl.when(g > 0)
    def _():
        i = g - 1
        slot = i % 2
        for n in range(2):
            bc[n] = jnp.broadcast_to(posc_ref[:, n:n + 1], (TM, MOE_SUB))
        liota = lax.broadcasted_iota(jnp.int32, (TM, MOE_SUB), 1)

        def weighted(has, blk, y):
            base = jnp.where(has, blk * MOE_SUB, -MOE_ROWS)
            hit = (bc[0] - base == liota) | (bc[1] - base == liota)
            return _dot(jnp.where(hit, 1.0, 0.0).astype(BF16), y)

        total = None
        for e in range(N_EXP):
            has, lo, _ = blocks(i, e)
            block_copy(slot, e, 0, lo).wait()
            t = weighted(has, lo, ybuf[slot, 2 * e])
            total = t if total is None else total + t
        oacc[...] = total
        for e in range(N_EXP):
            has, lo, two = blocks(i, e)

            @pl.when(two)
            def _():
                block_copy(slot, e, 1, lo + 1).wait()
                oacc[...] += weighted(has, lo + 1, ybuf[slot, 2 * e + 1])

        out = x_ref[...] + ada_ref[5:6, :] * oacc[...]
        if split:
            @pl.when(i < N_CTX_TILES)
            def _():
                oc_ref[...] = out

            @pl.when(i >= N_CTX_TILES)
            def _():
                ol_ref[...] = out
        else:
            o_ref[...] = out


def _combine_call(split, t_a, t_n, x1, ada, layer, posc, ys):
    prev = lambda g: jnp.maximum(g - 1, 0)
    ada_map = lambda g, a, n: (layer, _tile_cond(prev(g), N_CTX_TILES, LAT_TILES_PER_SEQ), 0, 0)
    if split:
        out_shape = [jax.ShapeDtypeStruct((T_CTX, D), F32), jax.ShapeDtypeStruct((T_LAT, D), F32)]
        out_specs = [pl.BlockSpec((TM, D), lambda g, a, n: (jnp.minimum(prev(g), N_CTX_TILES - 1), 0)),
                     pl.BlockSpec((TM, D), lambda g, a, n: (jnp.maximum(prev(g) - N_CTX_TILES, 0), 0))]
    else:
        out_shape = jax.ShapeDtypeStruct((T_ALL, D), F32)
        out_specs = pl.BlockSpec((TM, D), lambda g, a, n: (prev(g), 0))
    return pl.pallas_call(
        functools.partial(_combine_kernel, split),
        out_shape=out_shape,
        grid_spec=pltpu.PrefetchScalarGridSpec(
            num_scalar_prefetch=2, grid=(N_TILES + 1,),
            in_specs=[pl.BlockSpec((TM, D), lambda g, a, n: (prev(g), 0)),
                      pl.BlockSpec((None, None, N_ADA, D), ada_map),
                      pl.BlockSpec((TM, LANE), lambda g, a, n: (prev(g), 0)),
                      pl.BlockSpec(memory_space=pl.ANY)],
            out_specs=out_specs,
            scratch_shapes=[pltpu.VMEM((2, 2 * N_EXP, MOE_SUB, D), BF16), pltpu.VMEM((TM, D), F32),
                            pltpu.VMEM((2, TM, MOE_SUB), jnp.int32),
                            pltpu.SemaphoreType.DMA((2, 2 * N_EXP))]),
        compiler_params=pltpu.CompilerParams(
            dimension_semantics=("arbitrary",), vmem_limit_bytes=VMEM_BIG),
        name="moe_combine",
    )(t_a, t_n, x1, ada, posc, ys)


def _moe_layer(x1, h2b, ri, rw, runs, ada, layer, wg, wu, wd, lidx, split):
    idx = ri[:, 0:2]
    rank = ri[:, 2:4]
    before = runs[:, 0, :N_EXP].astype(jnp.int32)
    t_n = runs[:, 1, :N_EXP].astype(jnp.int32)
    counts = before[-1] + t_n[-1]
    padded = ((counts + MOE_SUB - 1) // MOE_SUB) * MOE_SUB
    ends = jnp.cumsum(padded)
    starts = ends - padded
    sel = idx[:, :, None] == jnp.arange(N_EXP, dtype=jnp.int32)[None, None, :]
    pos = (jnp.sum(jnp.where(sel, starts[None, None, :], 0), axis=-1) + rank).astype(jnp.int32)
    t_a = (starts[None, :] + before).reshape(-1).astype(jnp.int32)
    t_n = t_n.reshape(-1)
    pos_t = jnp.pad(pos.reshape(N_TILES, TM, 2).transpose(0, 2, 1), ((0, 0), (0, 6), (0, 0)))
    pos_t = pos_t.reshape(N_TILES * 8, TM)
    posc = jnp.pad(pos, ((0, 0), (0, LANE - 2)))
    n_st = (padded + MOE_R - 1) // MOE_R
    st_end = jnp.cumsum(n_st)
    st_begin = st_end - n_st
    j = jnp.arange(MOE_NS, dtype=jnp.int32)
    se = jnp.sum((j[:, None] >= st_end[None, :]).astype(jnp.int32), axis=1)
    used = se < N_EXP
    se_c = jnp.minimum(se, N_EXP - 1)
    k = j - st_begin[se_c]
    row0 = jnp.where(used, starts[se_c] + k * MOE_R, 0).astype(jnp.int32)
    nsub = jnp.where(used, jnp.minimum(MOE_R, padded[se_c] - k * MOE_R) // MOE_SUB, 0).astype(jnp.int32)
    last = jnp.maximum(jnp.sum(used.astype(jnp.int32)) - 1, 0)
    se_eff = jnp.where(used, se_c, se_c[last]).astype(jnp.int32)
    rw_t = jnp.pad(rw[:, 0:2].reshape(N_TILES, TM, 2).transpose(0, 2, 1), ((0, 0), (0, 6), (0, 0)))
    rw_t = rw_t.reshape(N_TILES * 8, TM)
    ys = _moe_call(se_eff, row0, nsub, t_a, t_n, h2b, pos_t, rw_t, wg, wu, wd,
                   jnp.zeros((MOE_ROWS, D), BF16), lidx)
    return _combine_call(split, t_a, t_n, x1, ada, layer, posc, ys)


def _rope_tables():
    t = jnp.arange(DEC_SEQ)
    rows = (t // GRID_W).astype(F32)
    cols = (t % GRID_W).astype(F32)

    def axis_tabs(pos, d):
        inv = ROPE_BASE ** (-jnp.arange(0, d, 2, dtype=F32) / d)
        ang = pos[:, None] * inv[None, :]
        cos, sin = jnp.cos(ang), jnp.sin(ang)
        z = jnp.zeros_like(sin)
        return (jnp.concatenate([cos, cos], -1), jnp.concatenate([-sin, z], -1), jnp.concatenate([z, sin], -1))

    def axial(d):
        r = axis_tabs(rows, d // 2)
        c = axis_tabs(cols, d // 2)
        return [jnp.concatenate([a, b], -1) for a, b in zip(r, c)]

    tabs_a = [jnp.tile(tb, (1, A_HEADS)) for tb in axial(A_DH)]
    cb, sub, sdb = axial(B_ROPE)
    ones = jnp.ones((DEC_SEQ, B_NOPE), F32)
    zer = jnp.zeros((DEC_SEQ, B_NOPE), F32)
    pad = jnp.zeros((DEC_SEQ, SLOT - B_QK), F32)
    slot = lambda first, tb: jnp.tile(jnp.concatenate([first, tb, pad], -1), (1, B_HEADS))
    tabs_b = [slot(ones, cb), slot(zer, sub), slot(zer, sdb)]
    return tabs_a + tabs_b


def _attn_weights(e, w_in_attn, g_qk_a_q, g_qk_a_k, g_mla_q_a, w_mla_q_b, g_mla_kv_a, w_mla_kv_b,
                  g_qk_b_q, g_qk_b_k):
    w_in = jnp.pad(w_in_attn[e], ((0, 0), (0, IN_PAD - IN_DIM))).astype(BF16)
    w_qb = jnp.pad(w_mla_q_b[e].reshape(B_Q_LORA, B_HEADS, B_QK), ((0, 0), (0, 0), (0, SLOT - B_QK)))
    w_qb = w_qb.reshape(B_Q_LORA, B_HEADS * SLOT).astype(BF16)
    wkv = w_mla_kv_b[e].reshape(B_KV_LORA, B_HEADS, B_NOPE + B_V)
    wk = jnp.pad(wkv[:, :, :B_NOPE], ((0, 0), (0, 0), (0, SLOT - B_NOPE))).reshape(B_KV_LORA, B_HEADS * SLOT)
    wv = wkv[:, :, B_NOPE:].reshape(B_KV_LORA, B_HEADS * B_V)
    w_kv = jnp.concatenate([wk, wv], axis=1).astype(BF16)
    place = jnp.zeros((LANE, B_HEADS, SLOT), F32)
    ii = jnp.arange(B_ROPE)
    place = place.at[ii, :, B_NOPE + ii].set(1.0).reshape(LANE, B_HEADS * SLOT).astype(BF16)
    slot_gain = lambda g: jnp.tile(jnp.pad(g, (0, SLOT - B_QK)), B_HEADS).reshape(1, B_HEADS * SLOT)
    return dict(
        w_in=w_in, w_qb=w_qb, w_kv=w_kv, place=place,
        gqa=jnp.tile(g_qk_a_q[e], A_HEADS).reshape(1, 512),
        gka=jnp.tile(g_qk_a_k[e], A_KV).reshape(1, LANE),
        gql=g_mla_q_a[e].reshape(1, B_Q_LORA), gkl=g_mla_kv_a[e].reshape(1, B_KV_LORA),
        gqb=slot_gain(g_qk_b_q[e]), gkb=slot_gain(g_qk_b_k[e]))


def _pair_variants_host(x):
    h0, h1 = x[..., :64], x[..., 64:]
    z = jnp.zeros_like(h0)
    return jnp.concatenate([h0, z, z, h0, h1, z, z, h1], axis=-1)


def kernel(x_prompt, x_sample, c, cache_k_win, cache_v_win, cache_ckv, cache_krope, c_ctx, w_ada, b_ada, g_norm_mix, g_norm_ffn, w_in_attn, g_qk_a_q, g_qk_a_k, sink_a, g_mla_q_a, w_mla_q_b, g_mla_kv_a, w_mla_kv_b, g_qk_b_q, g_qk_b_k, w_out_attn, w_ffn_gate, w_ffn_up, w_ffn_down, w_conv_pw1, b_conv_pw1, w_conv_dw, b_conv_dw, g_conv_ln, b_conv_ln, w_conv_pw2, b_conv_pw2, w_router, w_moe_gate, w_moe_up, w_moe_down):
    x_ctx, x_lat, lat_tiles_off = x_prompt.reshape(T_CTX, D), x_sample.reshape(T_LAT, D), 0
    cond8 = jnp.concatenate([c_ctx[None, :], c, jnp.zeros((8 - 1 - DEC_BATCH, D), F32)], axis=0)
    ada = _ada_call(cond8, w_ada, b_ada).reshape(DEPTH, 8, N_ADA, D)
    tabs = _rope_tables()

    wo_b = w_out_attn.astype(BF16)
    wg_b = w_ffn_gate.astype(BF16)
    wu_b = w_ffn_up.astype(BF16)
    wd_b = w_ffn_down.astype(BF16)
    w1_b = w_conv_pw1.astype(BF16)
    w2_b = w_conv_pw2.astype(BF16)

    new_k, new_v, new_ckv, new_kr = [], [], [], []
    for layer in range(DEPTH):
        gm = g_norm_mix[layer].reshape(1, D)
        gf = g_norm_ffn[layer].reshape(1, D)
        if layer % 2 == 0:
            e = layer // 2
            lw = _attn_weights(e, w_in_attn, g_qk_a_q, g_qk_a_k, g_mla_q_a, w_mla_q_b, g_mla_kv_a,
                               w_mla_kv_b, g_qk_b_q, g_qk_b_k)
            sink = sink_a[e].reshape(A_HEADS)
            (qa, ka4, va4, qb, kb, vlo, vhi, kaf, vaf, ckvf, krf) = _pre_attn_call(
                False, x_ctx, 0, ada, layer, gm, lw, None)
            o_ctx = _attn_ctx_call(sink, qa, ka4, va4, qb, kb, vlo, vhi)
            new_k.append(kaf.reshape(BATCH, SEQ, A_KV, A_DH))
            new_v.append(vaf.reshape(BATCH, SEQ, A_KV, A_DH))
            new_ckv.append(ckvf.reshape(BATCH, SEQ, B_KV_LORA))
            new_kr.append(krf[:, :B_ROPE].reshape(BATCH, SEQ, B_ROPE))
            (qa, ka4, va4, qb, kb, vlo, vhi) = _pre_attn_call(
                True, x_lat, lat_tiles_off * (T_CTX // TM), ada, layer, gm, lw, tabs)
            kr_c = jnp.pad(cache_krope[:, e].reshape(DEC_BATCH * PAST, B_ROPE), ((0, 0), (0, LANE - B_ROPE)))
            kb_c, vlo_c, vhi_c = _cache_kv_call(cache_ckv[:, e].reshape(DEC_BATCH * PAST, B_KV_LORA), kr_c, lw)
            pad_rows = lambda a: jnp.pad(a.reshape(DEC_BATCH, DEC_SEQ, 512), ((0, 0), (QB, QB), (0, 0)))
            ka4c = _pair_variants_host(cache_k_win[:, e].reshape(DEC_BATCH, PAST, LANE)).astype(BF16)
            va4c = _pair_variants_host(cache_v_win[:, e].reshape(DEC_BATCH, PAST, LANE)).astype(BF16)
            cat = lambda a, b, w: jnp.concatenate(
                [a.reshape(DEC_BATCH, PAST, w), b.reshape(DEC_BATCH, DEC_SEQ, w)], axis=1)
            o_lat = _attn_lat_call(sink, qa, pad_rows(ka4), pad_rows(va4), ka4c, va4c, qb,
                                   cat(kb_c, kb, 1024), cat(vlo_c, vlo, 512), cat(vhi_c, vhi, 512))
            x = _post_attn_call(x_ctx, x_lat, lat_tiles_off * (T_CTX // FFN_TM), o_ctx, o_lat, ada, layer, gf,
                                wo_b[e], wg_b[e], wu_b[e], wd_b[e])
        else:
            o_ = layer // 2
            cw = dict(w1=w1_b[o_], b1=b_conv_pw1[o_].reshape(1, 2 * D), wdw=w_conv_dw[o_],
                      bdw=b_conv_dw[o_].reshape(1, D), gln=g_conv_ln[o_].reshape(1, D),
                      bln=b_conv_ln[o_].reshape(1, D), w2=w2_b[o_], b2=b_conv_pw2[o_].reshape(1, D),
                      wr=jnp.pad(w_router[o_], ((0, 0), (0, LANE - N_EXP))))
            x1, h2b, ri, rw, runs = _conv_router_call(x, ada, layer, gm, gf, cw)
            x = _moe_layer(x1, h2b, ri, rw, runs, ada, layer, w_moe_gate, w_moe_up, w_moe_down, o_,
                           split=(layer == DEPTH - 1))
            x_ctx, x_lat, lat_tiles_off = x, x, 1

    y_prompt = x[0].reshape(BATCH, SEQ, D)
    y_sample = x[1].reshape(DEC_BATCH, DEC_SEQ, D)
    return (y_prompt, y_sample, jnp.stack(new_k, axis=1), jnp.stack(new_v, axis=1),
            jnp.stack(new_ckv, axis=1), jnp.stack(new_kr, axis=1))
```

```python
import functools

import jax
import jax.numpy as jnp
from jax import lax
from jax.experimental import pallas as pl
from jax.experimental.pallas import tpu as pltpu

F32 = jnp.float32
BF16 = jnp.bfloat16

D = 1024
BATCH = 32
SEQ = 256
DEPTH = 4
DEC_BATCH = 2
DEC_SEQ = 1024
PAST = 256
GRID_W = 64
N_EVEN = 2
A_HEADS = 8
A_KV = 2
A_DH = 64
WINDOW = 128
B_HEADS = 8
B_Q_LORA = 256
B_KV_LORA = 128
B_NOPE = 64
B_ROPE = 32
B_QK = B_NOPE + B_ROPE
B_V = 64
IN_DIM = 1184
IN_PAD = 1280
CONV_W = 31
D_FF = 2816
N_EXP = 8
D_FFE = 3584
ROPE_BASE = 10000.0
EPS = 1e-6
N_ADA = 6

T_CTX = BATCH * SEQ
T_LAT = DEC_BATCH * DEC_SEQ
T_ALL = T_CTX + T_LAT
TM = 256
N_TILES = T_ALL // TM
N_CTX_TILES = T_CTX // TM
LAT_TILES_PER_SEQ = DEC_SEQ // TM
LANE = 128
SLOT = 128

MOE_SUB = 256
MOE_R = 3584
MOE_TF = 512
MOE_NF = D_FFE // MOE_TF
MOE_TC = 512
MOE_GB = 128
MOE_GDEPTH = 7
MOE_GSLOTS = MOE_GDEPTH + 1
MOE_ROWS = 2 * T_ALL + N_EXP * MOE_SUB
MOE_NS = MOE_ROWS // MOE_R + N_EXP

FFN_TM = 512
FFN_CH = 1408

VMEM_BIG = 56 * 1024 * 1024


def _dot(a, b):
    return jnp.dot(a, b, preferred_element_type=F32)


def _dot_nt(a, b):
    return lax.dot_general(a, b, (((1,), (1,)), ((), ())), preferred_element_type=F32)


def _split(a):
    hi = a.astype(BF16)
    lo = (a - hi.astype(F32)).astype(BF16)
    return hi, lo


def _dot_x3(a, b):
    ah, al = _split(a)
    bh, bl = _split(b)
    return _dot(ah, bh) + (_dot(ah, bl) + _dot(al, bh))


def _sigmoid(x):
    return 1.0 / (1.0 + jnp.exp(-x))


def _rms(x, n):
    return x * lax.rsqrt(jnp.sum(x * x, axis=-1, keepdims=True) * (1.0 / n) + EPS)


def _modulate(x, g, shift, scale):
    return _rms(x, D) * g * (1.0 + scale) + shift


def _lo_mask(rows):
    return lax.broadcasted_iota(jnp.int32, (rows, LANE), 1) < 64


def _ada_kernel(cond_ref, w_ref, b_ref, o_ref):
    c = cond_ref[...]
    s = c * _sigmoid(c)
    o_ref[...] = _dot_x3(s, w_ref[...]) + b_ref[...]


def _ada_call(cond8, w_ada, b_ada):
    tn = 1024
    return pl.pallas_call(
        _ada_kernel,
        out_shape=jax.ShapeDtypeStruct((DEPTH, 8, N_ADA * D), F32),
        grid=(DEPTH, N_ADA * D // tn),
        in_specs=[
            pl.BlockSpec((8, D), lambda l, j: (0, 0)),
            pl.BlockSpec((None, D, tn), lambda l, j: (l, 0, j)),
            pl.BlockSpec((None, 1, tn), lambda l, j: (l, 0, j)),
        ],
        out_specs=pl.BlockSpec((None, 8, tn), lambda l, j: (l, 0, j)),
        compiler_params=pltpu.CompilerParams(
            dimension_semantics=("arbitrary", "arbitrary"), vmem_limit_bytes=VMEM_BIG),
        name="ada",
    )(cond8, w_ada, b_ada.reshape(DEPTH, 1, N_ADA * D))


def _rope(x, cos, sup, sdn, shift):
    up = pltpu.roll(x, LANE - shift, 1)
    dn = pltpu.roll(x, shift, 1)
    return x * cos + up * sup + dn * sdn


def _norm64_block(blk, lo):
    sq = blk * blk
    s_lo = jnp.sum(jnp.where(lo, sq, 0.0), axis=-1, keepdims=True)
    s_hi = jnp.sum(jnp.where(lo, 0.0, sq), axis=-1, keepdims=True)
    r = jnp.where(lo, lax.rsqrt(s_lo * (1.0 / A_DH) + EPS), lax.rsqrt(s_hi * (1.0 / A_DH) + EPS))
    return blk * r


def _pair_variants(x, lo):
    r = pltpu.roll(x, 64, 1)
    return (jnp.where(lo, x, 0.0), jnp.where(lo, 0.0, r), jnp.where(lo, r, 0.0), jnp.where(lo, 0.0, x))


def _mla_kv(ckv_n, kr, wkv_ref, place_ref, gk_ref, lo, kb_ref, vlo_ref, vhi_ref, rope_tabs):
    kv = _dot(ckv_n.astype(BF16), wkv_ref[...])
    kr_hi, kr_lo = _split(kr)
    place = place_ref[...]
    krs = _dot(kr_hi, place) + _dot(kr_lo, place)
    for h in range(B_HEADS):
        sl = slice(h * SLOT, (h + 1) * SLOT)
        blk = kv[:, sl] + krs[:, sl]
        blk = _rms(blk, B_QK) * gk_ref[:, sl]
        if rope_tabs is not None:
            cos, sup, sdn = rope_tabs
            blk = _rope(blk, cos[:, sl], sup[:, sl], sdn[:, sl], 8)
        kb_ref[:, sl] = blk.astype(BF16)
    for j in range(B_HEADS // 2):
        sl = slice(j * LANE, (j + 1) * LANE)
        v = kv[:, B_HEADS * SLOT + j * LANE: B_HEADS * SLOT + (j + 1) * LANE]
        vlo_ref[:, sl] = jnp.where(lo, v, 0.0).astype(BF16)
        vhi_ref[:, sl] = jnp.where(lo, 0.0, v).astype(BF16)


def _pre_attn_kernel(rope, *refs):
    if rope:
        (x_ref, ada_ref, g_ref, win_ref, wqb_ref, wkv_ref, place_ref,
         gqa_ref, gka_ref, gql_ref, gkl_ref, gqb_ref, gkb_ref,
         ca_ref, sua_ref, sda_ref, cb_ref, sub_ref, sdb_ref,
         qa_o, ka4_o, va4_o, qb_o, kb_o, vlo_o, vhi_o) = refs
    else:
        (x_ref, ada_ref, g_ref, win_ref, wqb_ref, wkv_ref, place_ref,
         gqa_ref, gka_ref, gql_ref, gkl_ref, gqb_ref, gkb_ref,
         qa_o, ka4_o, va4_o, qb_o, kb_o, vlo_o, vhi_o,
         kaf_o, vaf_o, ckvf_o, krf_o) = refs
    lo = _lo_mask(TM)
    h = _modulate(x_ref[...], g_ref[...], ada_ref[0:1, :], ada_ref[1:2, :])
    p = _dot(h.astype(BF16), win_ref[...])

    for j in range(4):
        sl = slice(j * LANE, (j + 1) * LANE)
        blk = _norm64_block(p[:, sl], lo) * gqa_ref[:, sl]
        if rope:
            blk = _rope(blk, ca_ref[:, sl], sua_ref[:, sl], sda_ref[:, sl], 16)
        qa_o[:, sl] = (blk * (A_DH ** -0.5)).astype(BF16)
    ka = _norm64_block(p[:, 512:640], lo) * gka_ref[...]
    va = p[:, 640:768]
    if not rope:
        kaf_o[...] = ka
        vaf_o[...] = va
    else:
        ka = _rope(ka, ca_ref[:, 0:LANE], sua_ref[:, 0:LANE], sda_ref[:, 0:LANE], 16)
    for n, (kk, vv) in enumerate(zip(_pair_variants(ka, lo), _pair_variants(va, lo))):
        sl = slice(n * LANE, (n + 1) * LANE)
        ka4_o[:, sl] = kk.astype(BF16)
        va4_o[:, sl] = vv.astype(BF16)

    cq = _rms(p[:, 768:1024], B_Q_LORA) * gql_ref[...]
    qb = _dot(cq.astype(BF16), wqb_ref[...])
    for hh in range(B_HEADS):
        sl = slice(hh * SLOT, (hh + 1) * SLOT)
        blk = _rms(qb[:, sl], B_QK) * gqb_ref[:, sl]
        if rope:
            blk = _rope(blk, cb_ref[:, sl], sub_ref[:, sl], sdb_ref[:, sl], 8)
        qb_o[:, sl] = (blk * (B_QK ** -0.5)).astype(BF16)

    ckv = _rms(p[:, 1024:1152], B_KV_LORA) * gkl_ref[...]
    kr = p[:, 1152:1280]
    if not rope:
        ckvf_o[...] = ckv
        krf_o[...] = kr
    tabs = (cb_ref, sub_ref, sdb_ref) if rope else None
    _mla_kv(ckv, kr, wkv_ref, place_ref, gkb_ref, lo, kb_o, vlo_o, vhi_o, tabs)


def _const_spec(shape):
    nd = len(shape)
    return pl.BlockSpec(shape, lambda *a: (0,) * nd)


def _pre_attn_call(rope, x, off, ada, layer, g, lw, tabs):
    n_tiles = (T_LAT if rope else T_CTX) // TM
    n_rows = n_tiles * TM
    if rope:
        ada_map = lambda i: (layer, 1 + i // LAT_TILES_PER_SEQ, 0, 0)
    else:
        ada_map = lambda i: (layer, 0, 0, 0)
    in_specs = [
        pl.BlockSpec((TM, D), lambda i: (i + off, 0)),
        pl.BlockSpec((None, None, N_ADA, D), ada_map),
        _const_spec((1, D)),
        _const_spec((D, IN_PAD)),
        _const_spec((B_Q_LORA, B_HEADS * SLOT)),
        _const_spec((B_KV_LORA, B_HEADS * SLOT + B_HEADS * B_V)),
        _const_spec((LANE, B_HEADS * SLOT)),
        _const_spec((1, 512)), _const_spec((1, LANE)), _const_spec((1, B_Q_LORA)),
        _const_spec((1, B_KV_LORA)), _const_spec((1, B_HEADS * SLOT)), _const_spec((1, B_HEADS * SLOT)),
    ]
    args = [x, ada, g, lw["w_in"], lw["w_qb"], lw["w_kv"], lw["place"],
            lw["gqa"], lw["gka"], lw["gql"], lw["gkl"], lw["gqb"], lw["gkb"]]
    if rope:
        tmap = lambda i: (i % LAT_TILES_PER_SEQ, 0)
        in_specs += [pl.BlockSpec((TM, 512), tmap)] * 3 + [pl.BlockSpec((TM, B_HEADS * SLOT), tmap)] * 3
        args += list(tabs)
    row = lambda w: pl.BlockSpec((TM, w), lambda i: (i, 0))
    out_shape = [jax.ShapeDtypeStruct((n_rows, w), BF16) for w in (512, 512, 512, 1024, 1024, 512, 512)]
    out_specs = [row(w) for w in (512, 512, 512, 1024, 1024, 512, 512)]
    if not rope:
        out_shape += [jax.ShapeDtypeStruct((n_rows, LANE), F32)] * 4
        out_specs += [row(LANE)] * 4
    return pl.pallas_call(
        functools.partial(_pre_attn_kernel, rope),
        out_shape=out_shape, grid=(n_tiles,), in_specs=in_specs, out_specs=out_specs,
        compiler_params=pltpu.CompilerParams(
            dimension_semantics=("arbitrary",), vmem_limit_bytes=VMEM_BIG),
        name="pre_attn_lat" if rope else "pre_attn_ctx",
    )(*args)


def _cache_kv_kernel(ckv_ref, kr_ref, wkv_ref, place_ref, gkb_ref, kb_o, vlo_o, vhi_o):
    lo = _lo_mask(PAST)
    _mla_kv(ckv_ref[...], kr_ref[...], wkv_ref, place_ref, gkb_ref, lo, kb_o, vlo_o, vhi_o, None)


def _cache_kv_call(ckv, kr, lw):
    rows = DEC_BATCH * PAST
    row = lambda w: pl.BlockSpec((PAST, w), lambda i: (i, 0))
    return pl.pallas_call(
        _cache_kv_kernel,
        out_shape=[jax.ShapeDtypeStruct((rows, 1024), BF16), jax.ShapeDtypeStruct((rows, 512), BF16),
                   jax.ShapeDtypeStruct((rows, 512), BF16)],
        grid=(DEC_BATCH,),
        in_specs=[row(LANE), row(LANE),
                  _const_spec((B_KV_LORA, B_HEADS * SLOT + B_HEADS * B_V)),
                  _const_spec((LANE, B_HEADS * SLOT)), _const_spec((1, B_HEADS * SLOT))],
        out_specs=[row(1024), row(512), row(512)],
        compiler_params=pltpu.CompilerParams(dimension_semantics=("arbitrary",)),
        name="cache_kv",
    )(ckv, kr, lw["w_kv"], lw["place"], lw["gkb"])


def _softmax_pv(scores, values, sink):
    m = jnp.max(scores[0], axis=-1, keepdims=True)
    for s in scores[1:]:
        m = jnp.maximum(m, jnp.max(s, axis=-1, keepdims=True))
    if sink is not None:
        m = jnp.maximum(m, sink)
    ps = [jnp.exp(s - m) for s in scores]
    den = jnp.sum(ps[0], axis=-1, keepdims=True)
    for p in ps[1:]:
        den = den + jnp.sum(p, axis=-1, keepdims=True)
    if sink is not None:
        den = den + jnp.exp(sink - m)
    out = _dot(ps[0].astype(BF16), values[0])
    for p, v in zip(ps[1:], values[1:]):
        out = out + _dot(p.astype(BF16), v)
    return out * (1.0 / den)


def _attn_ctx_kernel(sink_ref, qa_ref, ka4_ref, va4_ref, qb_ref, kb_ref, vlo_ref, vhi_ref, o_ref):
    for j in range(4):
        q = qa_ref[:, j * LANE:(j + 1) * LANE]
        acc = None
        for par in range(2):
            c = (2 * (j // 2) + par) * LANE
            s = _dot_nt(q, ka4_ref[:, c:c + LANE])
            o = _softmax_pv([s], [va4_ref[:, c:c + LANE]], sink_ref[2 * j + par])
            acc = o if acc is None else acc + o
        o_ref[:, j * LANE:(j + 1) * LANE] = acc.astype(BF16)
    for j in range(4):
        acc = None
        for par, v_ref in enumerate((vlo_ref, vhi_ref)):
            hh = 2 * j + par
            s = _dot_nt(qb_ref[:, hh * SLOT:(hh + 1) * SLOT], kb_ref[:, hh * SLOT:(hh + 1) * SLOT])
            o = _softmax_pv([s], [v_ref[:, j * LANE:(j + 1) * LANE]], None)
            acc = o if acc is None else acc + o
        o_ref[:, 512 + j * LANE:512 + (j + 1) * LANE] = acc.astype(BF16)


def _attn_ctx_call(sink, qa, ka4, va4, qb, kb, vlo, vhi):
    row = lambda w: pl.BlockSpec((SEQ, w), lambda b: (b, 0))
    return pl.pallas_call(
        _attn_ctx_kernel,
        out_shape=jax.ShapeDtypeStruct((T_CTX, D), BF16),
        grid=(BATCH,),
        in_specs=[pl.BlockSpec(memory_space=pltpu.SMEM),
                  row(512), row(512), row(512), row(1024), row(1024), row(512), row(512)],
        out_specs=row(D),
        compiler_params=pltpu.CompilerParams(dimension_semantics=("arbitrary",)),
        name="attn_ctx",
    )(sink, qa, ka4, va4, qb, kb, vlo, vhi)


QB = 256
N_QB = DEC_SEQ // QB
LWIN = QB + 2 * WINDOW
KPAD = DEC_SEQ + 2 * WINDOW
KALL = PAST + DEC_SEQ


def _attn_lat_kernel(sink_ref, qa_ref, ka4p_ref, va4p_ref, ka4c_ref, va4c_ref,
                     qb_ref, kb_ref, vlo_ref, vhi_ref, o_ref):
    n = pl.program_id(1)
    r = lax.broadcasted_iota(jnp.int32, (QB, LWIN), 0)
    c = lax.broadcasted_iota(jnp.int32, (QB, LWIN), 1)
    rel = c - WINDOW - r
    kpos = n * QB - WINDOW + c
    valid = (jnp.abs(rel) <= WINDOW) & (kpos >= 0) & (kpos < DEC_SEQ)
    k0 = pl.multiple_of(n * QB, QB)
    for j in range(4):
        q = qa_ref[:, j * LANE:(j + 1) * LANE]
        acc = None
        for par in range(2):
            cc = (2 * (j // 2) + par) * LANE
            s_loc = _dot_nt(q, ka4p_ref[pl.ds(k0, LWIN), cc:cc + LANE])
            s_loc = jnp.where(valid, s_loc, -jnp.inf)
            s_ctx = _dot_nt(q, ka4c_ref[:, cc:cc + LANE])
            o = _softmax_pv([s_ctx, s_loc],
                            [va4c_ref[:, cc:cc + LANE], va4p_ref[pl.ds(k0, LWIN), cc:cc + LANE]],
                            sink_ref[2 * j + par])
            acc = o if acc is None else acc + o
        o_ref[:, j * LANE:(j + 1) * LANE] = acc.astype(BF16)
    for j in range(4):
        acc = None
        for par, v_ref in enumerate((vlo_ref, vhi_ref)):
            hh = 2 * j + par
            s = _dot_nt(qb_ref[:, hh * SLOT:(hh + 1) * SLOT], kb_ref[:, hh * SLOT:(hh + 1) * SLOT])
            o = _softmax_pv([s], [v_ref[:, j * LANE:(j + 1) * LANE]], None)
            acc = o if acc is None else acc + o
        o_ref[:, 512 + j * LANE:512 + (j + 1) * LANE] = acc.astype(BF16)


def _attn_lat_call(sink, qa, ka4p, va4p, ka4c, va4c, qb, kball, vlo_all, vhi_all):
    qrow = lambda w: pl.BlockSpec((QB, w), lambda b, n: (b * N_QB + n, 0))
    per_b = lambda rows, w: pl.BlockSpec((None, rows, w), lambda b, n: (b, 0, 0))
    return pl.pallas_call(
        _attn_lat_kernel,
        out_shape=jax.ShapeDtypeStruct((T_LAT, D), BF16),
        grid=(DEC_BATCH, N_QB),
        in_specs=[pl.BlockSpec(memory_space=pltpu.SMEM),
                  qrow(512), per_b(KPAD, 512), per_b(KPAD, 512), per_b(PAST, 512), per_b(PAST, 512),
                  qrow(1024), per_b(KALL, 1024), per_b(KALL, 512), per_b(KALL, 512)],
        out_specs=qrow(D),
        compiler_params=pltpu.CompilerParams(dimension_semantics=("arbitrary", "arbitrary")),
        name="attn_lat",
    )(sink, qa, ka4p, va4p, ka4c, va4c, qb, kball, vlo_all, vhi_all)


def _post_attn_kernel(xc_ref, xl_ref, oc_ref, ol_ref, ada_ref, g_ref, wo_ref, wg_ref, wu_ref, wd_ref, y_ref):
    is_ctx = pl.program_id(0) < T_CTX // FFN_TM
    x = jnp.where(is_ctx, xc_ref[...], xl_ref[...])
    o = jnp.where(is_ctx, oc_ref[...], ol_ref[...])
    mix = _dot(o, wo_ref[...])
    x1 = x + ada_ref[2:3, :] * mix
    h = _modulate(x1, g_ref[...], ada_ref[3:4, :], ada_ref[4:5, :]).astype(BF16)
    acc = None
    for c0 in range(0, D_FF, FFN_CH):
        gg = _dot(h, wg_ref[:, c0:c0 + FFN_CH])
        uu = _dot(h, wu_ref[:, c0:c0 + FFN_CH])
        a = (gg * _sigmoid(gg) * uu).astype(BF16)
        part = _dot(a, wd_ref[c0:c0 + FFN_CH, :])
        acc = part if acc is None else acc + part
    y_ref[...] = x1 + ada_ref[5:6, :] * acc


def _tile_cond(i, tiles_per_ctx, tiles_per_lat_seq):
    return jnp.where(i < tiles_per_ctx, 0, 1 + (i - tiles_per_ctx) // tiles_per_lat_seq)


def _post_attn_call(x_ctx, x_lat, x_lat_off, o_ctx, o_lat, ada, layer, g, wo, wg, wu, wd):
    n_tiles = T_ALL // FFN_TM
    n_ctx = T_CTX // FFN_TM
    ada_map = lambda i: (layer, _tile_cond(i, n_ctx, DEC_SEQ // FFN_TM), 0, 0)
    ctx_map = lambda i: (jnp.minimum(i, n_ctx - 1), 0)
    one = pl.Buffered(1)
    return pl.pallas_call(
        _post_attn_kernel,
        out_shape=jax.ShapeDtypeStruct((T_ALL, D), F32),
        grid=(n_tiles,),
        in_specs=[pl.BlockSpec((FFN_TM, D), ctx_map),
                  pl.BlockSpec((FFN_TM, D), lambda i: (jnp.maximum(i - n_ctx, 0) + x_lat_off, 0)),
                  pl.BlockSpec((FFN_TM, D), ctx_map),
                  pl.BlockSpec((FFN_TM, D), lambda i: (jnp.maximum(i - n_ctx, 0), 0)),
                  pl.BlockSpec((None, None, N_ADA, D), ada_map),
                  _const_spec((1, D)),
                  pl.BlockSpec((D, D), lambda i: (0, 0), pipeline_mode=one),
                  pl.BlockSpec((D, D_FF), lambda i: (0, 0), pipeline_mode=one),
                  pl.BlockSpec((D, D_FF), lambda i: (0, 0), pipeline_mode=one),
                  pl.BlockSpec((D_FF, D), lambda i: (0, 0), pipeline_mode=one)],
        out_specs=pl.BlockSpec((FFN_TM, D), lambda i: (i, 0)),
        compiler_params=pltpu.CompilerParams(
            dimension_semantics=("arbitrary",), vmem_limit_bytes=VMEM_BIG),
        name="post_attn_ffn",
    )(x_ctx, x_lat, o_ctx, o_lat, ada, g, wo, wg, wu, wd)


HALO = 16
EXT = TM + 2 * HALO
CONV_RC = 64
CONV_LC = 128
CONV_WIN = CONV_RC + 24
CONV_LN_ROWS = 32


def _conv_router_kernel(x_ref, xp_ref, xn_ref, ada_ref, gm_ref, gf_ref, w1_ref, b1_ref, wdw_ref, bdw_ref,
                        gln_ref, bln_ref, w2_ref, b2_ref, wr_ref,
                        x1_o, h2_o, ri_o, rw_o, cnt_o, p_ref, c_ref, v_ref, run_ref):
    i = pl.program_id(0)

    @pl.when(i == 0)
    def _():
        run_ref[...] = jnp.zeros_like(run_ref)

    is_lat = i >= N_CTX_TILES
    j = (i - N_CTX_TILES) % LAT_TILES_PER_SEQ
    lflag = jnp.where(is_lat & (j != 0), 1.0, 0.0)
    rflag = jnp.where(is_lat & (j != LAT_TILES_PER_SEQ - 1), 1.0, 0.0)

    x = x_ref[...]
    xx = jnp.concatenate([xp_ref[...], x, xn_ref[...]], axis=0)
    h = _modulate(xx, gm_ref[...], ada_ref[0:1, :], ada_ref[1:2, :])
    a = _dot(h.astype(BF16), w1_ref[...]) + b1_ref[...]
    u = a[:, :D] * _sigmoid(a[:, D:])
    row = lax.broadcasted_iota(jnp.int32, (EXT, 1), 0)
    u = u * jnp.where(row < HALO, lflag, jnp.where(row >= HALO + TM, rflag, 1.0))

    p_ref[...] = u
    for rc in range(TM // CONV_RC):
        for lc in range(D // CONV_LC):
            ls = slice(lc * CONV_LC, (lc + 1) * CONV_LC)
            acc = jnp.zeros((CONV_RC, CONV_LC), F32) + bdw_ref[:, ls]
            base = p_ref[rc * CONV_RC:rc * CONV_RC + CONV_WIN + 8, ls]
            for r in range(8):
                win = base if r == 0 else pltpu.roll(base, CONV_WIN + 8 - r, 0)
                for q in range(4):
                    o = 8 * q + r
                    if 1 <= o <= CONV_W:
                        acc = acc + wdw_ref[o - 1:o, ls] * win[8 * q:8 * q + CONV_RC]
            c_ref[rc * CONV_RC:(rc + 1) * CONV_RC, ls] = acc

    for c0 in range(0, TM, CONV_LN_ROWS):
        acc = c_ref[c0:c0 + CONV_LN_ROWS, :]
        mu = jnp.mean(acc, axis=-1, keepdims=True)
        xc = acc - mu
        y = xc * lax.rsqrt(jnp.mean(xc * xc, axis=-1, keepdims=True) + EPS)
        y = y * gln_ref[...] + bln_ref[...]
        v_ref[c0:c0 + CONV_LN_ROWS, :] = (y * _sigmoid(y)).astype(BF16)

    mix = _dot(v_ref[...], w2_ref[...]) + b2_ref[...]
    x1 = x + ada_ref[2:3, :] * mix
    x1_o[...] = x1
    h2 = _modulate(x1, gf_ref[...], ada_ref[3:4, :], ada_ref[4:5, :])
    h2_o[...] = h2.astype(BF16)

    lane = lax.broadcasted_iota(jnp.int32, (TM, LANE), 1)
    lane_f = lane.astype(F32)
    logits = jnp.where(lane < N_EXP, _dot_x3(h2, wr_ref[...]), -jnp.inf)
    m1 = jnp.max(logits, axis=-1, keepdims=True)
    i1 = jnp.min(jnp.where(logits == m1, lane_f, float(LANE)), axis=-1, keepdims=True)
    rest = jnp.where(lane_f == i1, -jnp.inf, logits)
    m2 = jnp.max(rest, axis=-1, keepdims=True)
    i2 = jnp.min(jnp.where(rest == m2, lane_f, float(LANE)), axis=-1, keepdims=True)
    e = jnp.exp(m2 - m1)
    w1 = 1.0 / (1.0 + e)
    w2 = e / (1.0 + e)

    oh1 = jnp.where(lane_f == i1, 1.0, 0.0)
    oh2 = jnp.where(lane_f == i2, 1.0, 0.0)
    tr = lax.broadcasted_iota(jnp.int32, (TM, TM), 0)
    tc = lax.broadcasted_iota(jnp.int32, (TM, TM), 1)
    tri = jnp.where(tr > tc, 1.0, 0.0).astype(BF16)
    run = run_ref[0:1, :]
    cnt1 = jnp.sum(oh1, axis=0, keepdims=True)
    cnt2 = jnp.sum(oh2, axis=0, keepdims=True)
    pre1 = _dot(tri, oh1.astype(BF16)) + run
    pre2 = _dot(tri, oh2.astype(BF16)) + (run + cnt1)
    rank1 = jnp.sum(oh1 * pre1, axis=-1, keepdims=True)
    rank2 = jnp.sum(oh2 * pre2, axis=-1, keepdims=True)
    new_run = run + cnt1 + cnt2
    run_ref[...] = jnp.broadcast_to(new_run, run_ref.shape)
    srow = lax.broadcasted_iota(jnp.int32, cnt_o.shape, 0)
    cnt_o[...] = jnp.where(srow == 0, run, jnp.where(srow == 1, cnt1 + cnt2, 0.0))

    ri = jnp.where(lane == 0, i1, jnp.where(lane == 1, i2, jnp.where(lane == 2, rank1, jnp.where(lane == 3, rank2, 0.0))))
    ri_o[...] = ri.astype(jnp.int32)

    rw_o[...] = jnp.where(lane == 0, w1, jnp.where(lane == 1, w2, 0.0))


def _conv_router_call(x, ada, layer, gm, gf, cw):
    ada_map = lambda i: (layer, _tile_cond(i, N_CTX_TILES, LAT_TILES_PER_SEQ), 0, 0)
    hb = TM // HALO
    row = lambda w: pl.BlockSpec((TM, w), lambda i: (i, 0))
    return pl.pallas_call(
        _conv_router_kernel,
        out_shape=[jax.ShapeDtypeStruct((T_ALL, D), F32), jax.ShapeDtypeStruct((T_ALL, D), BF16),
                   jax.ShapeDtypeStruct((T_ALL, LANE), jnp.int32), jax.ShapeDtypeStruct((T_ALL, LANE), F32),
                   jax.ShapeDtypeStruct((N_TILES, 8, LANE), F32)],
        grid=(N_TILES,),
        in_specs=[row(D),
                  pl.BlockSpec((HALO, D), lambda i: (jnp.maximum(i * hb - 1, 0), 0)),
                  pl.BlockSpec((HALO, D), lambda i: (jnp.minimum((i + 1) * hb, T_ALL // HALO - 1), 0)),
                  pl.BlockSpec((None, None, N_ADA, D), ada_map),
                  _const_spec((1, D)), _const_spec((1, D)),
                  _const_spec((D, 2 * D)), _const_spec((1, 2 * D)),
                  _const_spec((CONV_W, D)), _const_spec((1, D)),
                  _const_spec((1, D)), _const_spec((1, D)),
                  _const_spec((D, D)), _const_spec((1, D)),
                  _const_spec((D, LANE))],
        out_specs=[row(D), row(D), row(LANE), row(LANE), pl.BlockSpec((None, 8, LANE), lambda i: (i, 0, 0))],
        scratch_shapes=[pltpu.VMEM((EXT, D), F32), pltpu.VMEM((TM, D), F32), pltpu.VMEM((TM, D), BF16),
                        pltpu.VMEM((8, LANE), F32)],
        compiler_params=pltpu.CompilerParams(
            dimension_semantics=("arbitrary",), vmem_limit_bytes=VMEM_BIG),
        name="conv_router",
    )(x, x, x, ada, gm, gf, cw["w1"], cw["b1"], cw["wdw"], cw["bdw"], cw["gln"], cw["bln"],
      cw["w2"], cw["b2"], cw["wr"])


def _moe_kernel(se_ref, row0_ref, nsub_ref, ta_ref, tn_ref, h_ref, post_ref, rwt_ref, wg_ref, wu_ref, wd_ref,
                ys_in_ref, ys_ref, hb_ref, acc_ref, gs_ref, hstage, wgb, wub, wdb, sem_in, sem_out):
    del ys_in_ref
    j = pl.program_id(0)
    f = pl.program_id(1)
    nsub = nsub_ref[j]
    row0 = row0_ref[j]
    e = se_ref[j]

    def tile_load(i, slot):
        r = pl.multiple_of(i * TM, TM)
        return pltpu.make_async_copy(h_ref.at[pl.ds(r, TM), :], hstage.at[slot], sem_in.at[slot])

    def store(s):
        r = pl.multiple_of(s * MOE_SUB, MOE_SUB)
        dst = ys_ref.at[pl.ds(pl.multiple_of(row0 + s * MOE_SUB, MOE_SUB), MOE_SUB), :]
        return pltpu.make_async_copy(hb_ref.at[pl.ds(r, MOE_SUB), :], dst, sem_out)

    def gather_tile(i, carry):
        slot = i % MOE_GSLOTS
        tile_load(i, slot).wait()

        @pl.when(i + MOE_GDEPTH < N_TILES)
        def _():
            tile_load(i + MOE_GDEPTH, (i + MOE_GDEPTH) % MOE_GSLOTS).start()

        a = ta_ref[i * N_EXP + e] - row0
        n = tn_ref[i * N_EXP + e]
        last = a + n - 1

        @pl.when((n > 0) & (last >= 0) & (a < nsub * MOE_SUB))
        def _():
            start = (jnp.maximum(a, 0) // 8) * 8
            end = jnp.minimum(last, nsub * MOE_SUB - 1)
            pp = post_ref[pl.ds(pl.multiple_of(i * 8, 8), 8), :]
            riota = lax.broadcasted_iota(jnp.int32, (MOE_GB, TM), 0)
            ww = rwt_ref[pl.ds(pl.multiple_of(i * 8, 8), 8), :]

            def window(w):
                rr = pl.multiple_of(start + w * MOE_GB, 8)
                base = row0 + rr
                hit0 = (pp[0:1, :] - base) == riota
                hit1 = (pp[1:2, :] - base) == riota
                acc_ref[pl.ds(rr, MOE_GB), :] += _dot(jnp.where(hit0 | hit1, 1.0, 0.0).astype(BF16),
                                                      hstage[slot])
                gate = jnp.sum(jnp.where(hit0, ww[0:1, :], 0.0) + jnp.where(hit1, ww[1:2, :], 0.0),
                               axis=-1, keepdims=True)
                gs_ref[pl.ds(rr, MOE_GB), :] += jnp.broadcast_to(gate, (MOE_GB, LANE))

            window(0)
            for extra in range(1, (TM + 8) // MOE_GB + 1):
                @pl.when(start + extra * MOE_GB <= end)
                def _():
                    window(extra)

        return carry

    def part(r, rows):
        hh = hb_ref[pl.ds(r, rows), :]
        out = None
        for c0 in range(0, MOE_TF, MOE_TC):
            gg = _dot(hh, wgb[:, c0:c0 + MOE_TC])
            uu = _dot(hh, wub[:, c0:c0 + MOE_TC])
            a = (gg * _sigmoid(gg) * uu).astype(BF16)
            piece = _dot(a, wdb[c0:c0 + MOE_TC, :])
            out = piece if out is None else out + piece
        return out

    def for_each_subtile(fn):
        def quad(p, carry):
            fn(4 * p, 4 * MOE_SUB)
            return carry

        lax.fori_loop(0, nsub // 4, quad, 0)
        rem = nsub % 4

        @pl.when(rem >= 2)
        def _():
            fn(nsub - rem, 2 * MOE_SUB)

        @pl.when(rem % 2 == 1)
        def _():
            fn(nsub - 1, MOE_SUB)

    @pl.when(nsub > 0)
    def _():
        wgb[...] = wg_ref[...].astype(BF16)
        wub[...] = wu_ref[...].astype(BF16)
        wdb[...] = wd_ref[...].astype(BF16)

        @pl.when(f == 0)
        def _():
            for d in range(MOE_GDEPTH):
                tile_load(d, d).start()

            def clear(s, carry):
                r = pl.multiple_of(s * MOE_SUB, MOE_SUB)
                acc_ref[pl.ds(r, MOE_SUB), :] = jnp.zeros((MOE_SUB, D), F32)
                gs_ref[pl.ds(r, MOE_SUB), :] = jnp.zeros((MOE_SUB, LANE), F32)
                return carry

            lax.fori_loop(0, nsub + 1, clear, 0)
            lax.fori_loop(0, N_TILES, gather_tile, 0)

            def first(s, rows):
                r = pl.multiple_of(s * MOE_SUB, MOE_SUB)
                hb_ref[pl.ds(r, rows), :] = acc_ref[pl.ds(r, rows), :].astype(BF16)
                acc_ref[pl.ds(r, rows), :] = part(r, rows)

            for_each_subtile(first)

        @pl.when((f > 0) & (f < MOE_NF - 1))
        def _():
            def middle(s, rows):
                r = pl.multiple_of(s * MOE_SUB, MOE_SUB)
                acc_ref[pl.ds(r, rows), :] += part(r, rows)

            for_each_subtile(middle)

        @pl.when(f == MOE_NF - 1)
        def _():
            def final(s, rows):
                r = pl.multiple_of(s * MOE_SUB, MOE_SUB)
                y = acc_ref[pl.ds(r, rows), :] + part(r, rows)
                hb_ref[pl.ds(r, rows), :] = (y * gs_ref[pl.ds(r, rows), 0:1]).astype(BF16)
                for k in range(rows // MOE_SUB):
                    store(s + k).start()

            for_each_subtile(final)

            def drain(s, carry):
                store(s).wait()
                return carry

            lax.fori_loop(0, nsub, drain, 0)


def _moe_call(st_e, st_row0, st_nsub, t_a, t_n, h2b, pos_t, rw_t, wg, wu, wd, ys0, lidx):
    f_eff = lambda j, f, ns: jnp.where(ns[j] > 0, f, MOE_NF - 1)
    wmap_in = lambda j, f, se, r0, ns, ta, tn: (lidx, se[j], 0, f_eff(j, f, ns))
    wmap_out = lambda j, f, se, r0, ns, ta, tn: (lidx, se[j], f_eff(j, f, ns), 0)
    return pl.pallas_call(
        _moe_kernel,
        out_shape=jax.ShapeDtypeStruct((MOE_ROWS, D), BF16),
        grid_spec=pltpu.PrefetchScalarGridSpec(
            num_scalar_prefetch=5, grid=(MOE_NS, MOE_NF),
            in_specs=[
                pl.BlockSpec(memory_space=pl.ANY),
                pl.BlockSpec((N_TILES * 8, TM), lambda j, f, se, r0, ns, ta, tn: (0, 0)),
                pl.BlockSpec((N_TILES * 8, TM), lambda j, f, se, r0, ns, ta, tn: (0, 0)),
                pl.BlockSpec((None, None, D, MOE_TF), wmap_in),
                pl.BlockSpec((None, None, D, MOE_TF), wmap_in),
                pl.BlockSpec((None, None, MOE_TF, D), wmap_out),
                pl.BlockSpec(memory_space=pl.ANY),
            ],
            out_specs=pl.BlockSpec(memory_space=pl.ANY),
            scratch_shapes=[pltpu.VMEM((MOE_R, D), BF16), pltpu.VMEM((MOE_R + MOE_SUB, D), F32),
                            pltpu.VMEM((MOE_R + MOE_SUB, LANE), F32),
                            pltpu.VMEM((MOE_GSLOTS, TM, D), BF16),
                            pltpu.VMEM((D, MOE_TF), BF16), pltpu.VMEM((D, MOE_TF), BF16),
                            pltpu.VMEM((MOE_TF, D), BF16),
                            pltpu.SemaphoreType.DMA((MOE_GSLOTS,)), pltpu.SemaphoreType.DMA(())]),
        input_output_aliases={11: 0},
        compiler_params=pltpu.CompilerParams(
            dimension_semantics=("arbitrary", "arbitrary"), vmem_limit_bytes=VMEM_BIG,
            has_side_effects=True),
        name="moe_experts",
    )(st_e, st_row0, st_nsub, t_a, t_n, h2b, pos_t, rw_t, wg, wu, wd, ys0)


def _combine_kernel(split, ta_ref, tn_ref, x_ref, ada_ref, posc_ref, ys_ref, *rest):
    if split:
        oc_ref, ol_ref, ybuf, oacc, bc, sem = rest
    else:
        o_ref, ybuf, oacc, bc, sem = rest
    g = pl.program_id(0)

    def blocks(i, e):
        a = ta_ref[i * N_EXP + e]
        n = tn_ref[i * N_EXP + e]
        lo = jnp.minimum(a // MOE_SUB, MOE_ROWS // MOE_SUB - 1)
        return n > 0, lo, (n > 0) & ((a + n - 1) // MOE_SUB > lo)

    def block_copy(slot, e, k, blk):
        src = ys_ref.at[pl.ds(pl.multiple_of(blk * MOE_SUB, MOE_SUB), MOE_SUB), :]
        return pltpu.make_async_copy(src, ybuf.at[slot, 2 * e + k], sem.at[slot, 2 * e + k])

    @pl.when(g < N_TILES)
    def _():
        for e in range(N_EXP):
            _, lo, two = blocks(g, e)
            block_copy(g % 2, e, 0, lo).start()

            @pl.when(two)
            def _():
                block_copy(g % 2, e, 1, lo + 1).start()

    @pl.when(g > 0)
    def _():
        i = g - 1
        slot = i % 2
        for n in range(2):
            bc[n] = jnp.broadcast_to(posc_ref[:, n:n + 1], (TM, MOE_SUB))
        liota = lax.broadcasted_iota(jnp.int32, (TM, MOE_SUB), 1)

        def weighted(has, blk, y):
            base = jnp.where(has, blk * MOE_SUB, -MOE_ROWS)
            hit = (bc[0] - base == liota) | (bc[1] - base == liota)
            return _dot(jnp.where(hit, 1.0, 0.0).astype(BF16), y)

        total = None
        for e in range(N_EXP):
            has, lo, _ = blocks(i, e)
            block_copy(slot, e, 0, lo).wait()
            t = weighted(has, lo, ybuf[slot, 2 * e])
            total = t if total is None else total + t
        oacc[...] = total
        for e in range(N_EXP):
            has, lo, two = blocks(i, e)

            @pl.when(two)
            def _():
                block_copy(slot, e, 1, lo + 1).wait()
                oacc[...] += weighted(has, lo + 1, ybuf[slot, 2 * e + 1])

        out = x_ref[...] + ada_ref[5:6, :] * oacc[...]
        if split:
            @pl.when(i < N_CTX_TILES)
            def _():
                oc_ref[...] = out

            @pl.when(i >= N_CTX_TILES)
            def _():
                ol_ref[...] = out
        else:
            o_ref[...] = out


def _combine_call(split, t_a, t_n, x1, ada, layer, posc, ys):
    prev = lambda g: jnp.maximum(g - 1, 0)
    ada_map = lambda g, a, n: (layer, _tile_cond(prev(g), N_CTX_TILES, LAT_TILES_PER_SEQ), 0, 0)
    if split:
        out_shape = [jax.ShapeDtypeStruct((T_CTX, D), F32), jax.ShapeDtypeStruct((T_LAT, D), F32)]
        out_specs = [pl.BlockSpec((TM, D), lambda g, a, n: (jnp.minimum(prev(g), N_CTX_TILES - 1), 0)),
                     pl.BlockSpec((TM, D), lambda g, a, n: (jnp.maximum(prev(g) - N_CTX_TILES, 0), 0))]
    else:
        out_shape = jax.ShapeDtypeStruct((T_ALL, D), F32)
        out_specs = pl.BlockSpec((TM, D), lambda g, a, n: (prev(g), 0))
    return pl.pallas_call(
        functools.partial(_combine_kernel, split),
        out_shape=out_shape,
        grid_spec=pltpu.PrefetchScalarGridSpec(
            num_scalar_prefetch=2, grid=(N_TILES + 1,),
            in_specs=[pl.BlockSpec((TM, D), lambda g, a, n: (prev(g), 0)),
                      pl.BlockSpec((None, None, N_ADA, D), ada_map),
                      pl.BlockSpec((TM, LANE), lambda g, a, n: (prev(g), 0)),
                      pl.BlockSpec(memory_space=pl.ANY)],
            out_specs=out_specs,
            scratch_shapes=[pltpu.VMEM((2, 2 * N_EXP, MOE_SUB, D), BF16), pltpu.VMEM((TM, D), F32),
                            pltpu.VMEM((2, TM, MOE_SUB), jnp.int32),
                            pltpu.SemaphoreType.DMA((2, 2 * N_EXP))]),
        compiler_params=pltpu.CompilerParams(
            dimension_semantics=("arbitrary",), vmem_limit_bytes=VMEM_BIG),
        name="moe_combine",
    )(t_a, t_n, x1, ada, posc, ys)


def _moe_layer(x1, h2b, ri, rw, runs, ada, layer, wg, wu, wd, lidx, split):
    idx = ri[:, 0:2]
    rank = ri[:, 2:4]
    before = runs[:, 0, :N_EXP].astype(jnp.int32)
    t_n = runs[:, 1, :N_EXP].astype(jnp.int32)
    counts = before[-1] + t_n[-1]
    padded = ((counts + MOE_SUB - 1) // MOE_SUB) * MOE_SUB
    ends = jnp.cumsum(padded)
    starts = ends - padded
    sel = idx[:, :, None] == jnp.arange(N_EXP, dtype=jnp.int32)[None, None, :]
    pos = (jnp.sum(jnp.where(sel, starts[None, None, :], 0), axis=-1) + rank).astype(jnp.int32)
    t_a = (starts[None, :] + before).reshape(-1).astype(jnp.int32)
    t_n = t_n.reshape(-1)
    pos_t = jnp.pad(pos.reshape(N_TILES, TM, 2).transpose(0, 2, 1), ((0, 0), (0, 6), (0, 0)))
    pos_t = pos_t.reshape(N_TILES * 8, TM)
    posc = jnp.pad(pos, ((0, 0), (0, LANE - 2)))
    n_st = (padded + MOE_R - 1) // MOE_R
    st_end = jnp.cumsum(n_st)
    st_begin = st_end - n_st
    j = jnp.arange(MOE_NS, dtype=jnp.int32)
    se = jnp.sum((j[:, None] >= st_end[None, :]).astype(jnp.int32), axis=1)
    used = se < N_EXP
    se_c = jnp.minimum(se, N_EXP - 1)
    k = j - st_begin[se_c]
    row0 = jnp.where(used, starts[se_c] + k * MOE_R, 0).astype(jnp.int32)
    nsub = jnp.where(used, jnp.minimum(MOE_R, padded[se_c] - k * MOE_R) // MOE_SUB, 0).astype(jnp.int32)
    last = jnp.maximum(jnp.sum(used.astype(jnp.int32)) - 1, 0)
    se_eff = jnp.where(used, se_c, se_c[last]).astype(jnp.int32)
    rw_t = jnp.pad(rw[:, 0:2].reshape(N_TILES, TM, 2).transpose(0, 2, 1), ((0, 0), (0, 6), (0, 0)))
    rw_t = rw_t.reshape(N_TILES * 8, TM)
    ys = _moe_call(se_eff, row0, nsub, t_a, t_n, h2b, pos_t, rw_t, wg, wu, wd,
                   jnp.zeros((MOE_ROWS, D), BF16), lidx)
    return _combine_call(split, t_a, t_n, x1, ada, layer, posc, ys)


def _rope_tables():
    t = jnp.arange(DEC_SEQ)
    rows = (t // GRID_W).astype(F32)
    cols = (t % GRID_W).astype(F32)

    def axis_tabs(pos, d):
        inv = ROPE_BASE ** (-jnp.arange(0, d, 2, dtype=F32) / d)
        ang = pos[:, None] * inv[None, :]
        cos, sin = jnp.cos(ang), jnp.sin(ang)
        z = jnp.zeros_like(sin)
        return (jnp.concatenate([cos, cos], -1), jnp.concatenate([-sin, z], -1), jnp.concatenate([z, sin], -1))

    def axial(d):
        r = axis_tabs(rows, d // 2)
        c = axis_tabs(cols, d // 2)
        return [jnp.concatenate([a, b], -1) for a, b in zip(r, c)]

    tabs_a = [jnp.tile(tb, (1, A_HEADS)) for tb in axial(A_DH)]
    cb, sub, sdb = axial(B_ROPE)
    ones = jnp.ones((DEC_SEQ, B_NOPE), F32)
    zer = jnp.zeros((DEC_SEQ, B_NOPE), F32)
    pad = jnp.zeros((DEC_SEQ, SLOT - B_QK), F32)
    slot = lambda first, tb: jnp.tile(jnp.concatenate([first, tb, pad], -1), (1, B_HEADS))
    tabs_b = [slot(ones, cb), slot(zer, sub), slot(zer, sdb)]
    return tabs_a + tabs_b


def _attn_weights(e, w_in_attn, g_qk_a_q, g_qk_a_k, g_mla_q_a, w_mla_q_b, g_mla_kv_a, w_mla_kv_b,
                  g_qk_b_q, g_qk_b_k):
    w_in = jnp.pad(w_in_attn[e], ((0, 0), (0, IN_PAD - IN_DIM))).astype(BF16)
    w_qb = jnp.pad(w_mla_q_b[e].reshape(B_Q_LORA, B_HEADS, B_QK), ((0, 0), (0, 0), (0, SLOT - B_QK)))
    w_qb = w_qb.reshape(B_Q_LORA, B_HEADS * SLOT).astype(BF16)
    wkv = w_mla_kv_b[e].reshape(B_KV_LORA, B_HEADS, B_NOPE + B_V)
    wk = jnp.pad(wkv[:, :, :B_NOPE], ((0, 0), (0, 0), (0, SLOT - B_NOPE))).reshape(B_KV_LORA, B_HEADS * SLOT)
    wv = wkv[:, :, B_NOPE:].reshape(B_KV_LORA, B_HEADS * B_V)
    w_kv = jnp.concatenate([wk, wv], axis=1).astype(BF16)
    place = jnp.zeros((LANE, B_HEADS, SLOT), F32)
    ii = jnp.arange(B_ROPE)
    place = place.at[ii, :, B_NOPE + ii].set(1.0).reshape(LANE, B_HEADS * SLOT).astype(BF16)
    slot_gain = lambda g: jnp.tile(jnp.pad(g, (0, SLOT - B_QK)), B_HEADS).reshape(1, B_HEADS * SLOT)
    return dict(
        w_in=w_in, w_qb=w_qb, w_kv=w_kv, place=place,
        gqa=jnp.tile(g_qk_a_q[e], A_HEADS).reshape(1, 512),
        gka=jnp.tile(g_qk_a_k[e], A_KV).reshape(1, LANE),
        gql=g_mla_q_a[e].reshape(1, B_Q_LORA), gkl=g_mla_kv_a[e].reshape(1, B_KV_LORA),
        gqb=slot_gain(g_qk_b_q[e]), gkb=slot_gain(g_qk_b_k[e]))


def _pair_variants_host(x):
    h0, h1 = x[..., :64], x[..., 64:]
    z = jnp.zeros_like(h0)
    return jnp.concatenate([h0, z, z, h0, h1, z, z, h1], axis=-1)


def kernel(x_prompt, x_sample, c, cache_k_win, cache_v_win, cache_ckv, cache_krope, c_ctx, w_ada, b_ada, g_norm_mix, g_norm_ffn, w_in_attn, g_qk_a_q, g_qk_a_k, sink_a, g_mla_q_a, w_mla_q_b, g_mla_kv_a, w_mla_kv_b, g_qk_b_q, g_qk_b_k, w_out_attn, w_ffn_gate, w_ffn_up, w_ffn_down, w_conv_pw1, b_conv_pw1, w_conv_dw, b_conv_dw, g_conv_ln, b_conv_ln, w_conv_pw2, b_conv_pw2, w_router, w_moe_gate, w_moe_up, w_moe_down):
    x_ctx, x_lat, lat_tiles_off = x_prompt.reshape(T_CTX, D), x_sample.reshape(T_LAT, D), 0
    cond8 = jnp.concatenate([c_ctx[None, :], c, jnp.zeros((8 - 1 - DEC_BATCH, D), F32)], axis=0)
    ada = _ada_call(cond8, w_ada, b_ada).reshape(DEPTH, 8, N_ADA, D)
    tabs = _rope_tables()

    wo_b = w_out_attn.astype(BF16)
    wg_b = w_ffn_gate.astype(BF16)
    wu_b = w_ffn_up.astype(BF16)
    wd_b = w_ffn_down.astype(BF16)
    w1_b = w_conv_pw1.astype(BF16)
    w2_b = w_conv_pw2.astype(BF16)

    new_k, new_v, new_ckv, new_kr = [], [], [], []
    for layer in range(DEPTH):
        gm = g_norm_mix[layer].reshape(1, D)
        gf = g_norm_ffn[layer].reshape(1, D)
        if layer % 2 == 0:
            e = layer // 2
            lw = _attn_weights(e, w_in_attn, g_qk_a_q, g_qk_a_k, g_mla_q_a, w_mla_q_b, g_mla_kv_a,
                               w_mla_kv_b, g_qk_b_q, g_qk_b_k)
            sink = sink_a[e].reshape(A_HEADS)
            (qa, ka4, va4, qb, kb, vlo, vhi, kaf, vaf, ckvf, krf) = _pre_attn_call(
                False, x_ctx, 0, ada, layer, gm, lw, None)
            o_ctx = _attn_ctx_call(sink, qa, ka4, va4, qb, kb, vlo, vhi)
            new_k.append(kaf.reshape(BATCH, SEQ, A_KV, A_DH))
            new_v.append(vaf.reshape(BATCH, SEQ, A_KV, A_DH))
            new_ckv.append(ckvf.reshape(BATCH, SEQ, B_KV_LORA))
            new_kr.append(krf[:, :B_ROPE].reshape(BATCH, SEQ, B_ROPE))
            (qa, ka4, va4, qb, kb, vlo, vhi) = _pre_attn_call(
                True, x_lat, lat_tiles_off * (T_CTX // TM), ada, layer, gm, lw, tabs)
            kr_c = jnp.pad(cache_krope[:, e].reshape(DEC_BATCH * PAST, B_ROPE), ((0, 0), (0, LANE - B_ROPE)))
            kb_c, vlo_c, vhi_c = _cache_kv_call(cache_ckv[:, e].reshape(DEC_BATCH * PAST, B_KV_LORA), kr_c, lw)
            pad_rows = lambda a: jnp.pad(a.reshape(DEC_BATCH, DEC_SEQ, 512), ((0, 0), (WINDOW, WINDOW), (0, 0)))
            ka4c = _pair_variants_host(cache_k_win[:, e].reshape(DEC_BATCH, PAST, LANE)).astype(BF16)
            va4c = _pair_variants_host(cache_v_win[:, e].reshape(DEC_BATCH, PAST, LANE)).astype(BF16)
            cat = lambda a, b, w: jnp.concatenate(
                [a.reshape(DEC_BATCH, PAST, w), b.reshape(DEC_BATCH, DEC_SEQ, w)], axis=1)
            o_lat = _attn_lat_call(sink, qa, pad_rows(ka4), pad_rows(va4), ka4c, va4c, qb,
                                   cat(kb_c, kb, 1024), cat(vlo_c, vlo, 512), cat(vhi_c, vhi, 512))
            x = _post_attn_call(x_ctx, x_lat, lat_tiles_off * (T_CTX // FFN_TM), o_ctx, o_lat, ada, layer, gf,
                                wo_b[e], wg_b[e], wu_b[e], wd_b[e])
        else:
            o_ = layer // 2
            cw = dict(w1=w1_b[o_], b1=b_conv_pw1[o_].reshape(1, 2 * D), wdw=w_conv_dw[o_],
                      bdw=b_conv_dw[o_].reshape(1, D), gln=g_conv_ln[o_].reshape(1, D),
                      bln=b_conv_ln[o_].reshape(1, D), w2=w2_b[o_], b2=b_conv_pw2[o_].reshape(1, D),
                      wr=jnp.pad(w_router[o_], ((0, 0), (0, LANE - N_EXP))))
            x1, h2b, ri, rw, runs = _conv_router_call(x, ada, layer, gm, gf, cw)
            x = _moe_layer(x1, h2b, ri, rw, runs, ada, layer, w_moe_gate, w_moe_up, w_moe_down, o_,
                           split=(layer == DEPTH - 1))
            x_ctx, x_lat, lat_tiles_off = x, x, 1

    y_prompt = x[0].reshape(BATCH, SEQ, D)
    y_sample = x[1].reshape(DEC_BATCH, DEC_SEQ, D)
    return (y_prompt, y_sample, jnp.stack(new_k, axis=1), jnp.stack(new_v, axis=1),
            jnp.stack(new_ckv, axis=1), jnp.stack(new_kr, axis=1))
```
